```python
import math
import jax, jax.numpy as jnp
from jax import lax
import numpy as np

D_MODEL = 1024
BATCH = 2
SEQ = 8192
DEPTH = 4
DEC_BATCH = 32
DEC_SEQ = 1
PAST_LEN = 8192
PAGE_SIZE = 128

HEAD_DIM = 64
N_MIX_HEADS = D_MODEL // HEAD_DIM
NSA_HEADS = N_MIX_HEADS // 2
NSA_KV_HEADS = 2
NSA_GROUP = NSA_HEADS // NSA_KV_HEADS
FOX_HEADS = N_MIX_HEADS - NSA_HEADS
DIL_HEADS = N_MIX_HEADS
CMP_LEN = 32
CMP_STRIDE = 16
CMP_HIDDEN = 4 * HEAD_DIM
SEL_BLOCK = 64
SEL_TOPK = 16
NSA_WINDOW = 512
FORCED_SCORE = 1e9
DIL_BRANCHES = ((128, 1), (512, 4), (2048, 16))
DIL_BUF = 2048
N_BUCKETS = 32
BUCKET_EXACT = 16
BUCKET_MAX_DIST = 2048
LOG_BUCKET_RATIO = math.log(BUCKET_MAX_DIST / BUCKET_EXACT)
D_FF = 4 * D_MODEL
Q_BLOCK = 128
N_A_LAYERS = (DEPTH + 1) // 2
N_C_LAYERS = DEPTH // 2
NORM_EPS = 1e-6
TINY = 1e-30
MIX_WIDTH = N_MIX_HEADS * HEAD_DIM
PROJ_A_SIZES = (NSA_HEADS * HEAD_DIM, 6 * NSA_KV_HEADS * HEAD_DIM, 3 * NSA_HEADS, 3 * FOX_HEADS * HEAD_DIM, FOX_HEADS)
PROJ_A_SPLITS = tuple(sum(PROJ_A_SIZES[:i + 1]) for i in range(len(PROJ_A_SIZES) - 1))
PROJ_A = sum(PROJ_A_SIZES)
PROJ_C = 3 * DIL_HEADS * HEAD_DIM

kernel_name = 'nsa_fox_dilated_hybrid_step'


def rms_norm(x, g):
    xf = x.astype(jnp.float32)
    y = xf * lax.rsqrt(jnp.mean(xf * xf, axis=-1, keepdims=True) + NORM_EPS)
    return (y * g.astype(jnp.float32)).astype(x.dtype)


def adaln_params(c, w, b):
    m = jnp.einsum('bd,de->be', jax.nn.silu(c), w) + b
    return m.reshape(c.shape[0], 6, 1, c.shape[-1])


def sq_relu_mlp(h, w1, w2):
    a = jnp.maximum(jnp.einsum('btd,df->btf', h, w1), 0)
    return jnp.einsum('btf,fd->btd', a * a, w2)


def t5_bucket(dist):
    dist = jnp.maximum(dist, 0)
    df = jnp.maximum(dist, 1).astype(jnp.float32)
    log_b = BUCKET_EXACT + (jnp.log(df / BUCKET_EXACT) / LOG_BUCKET_RATIO * (N_BUCKETS - BUCKET_EXACT)).astype(jnp.int32)
    return jnp.where(dist < BUCKET_EXACT, dist, jnp.clip(log_b, BUCKET_EXACT, N_BUCKETS - 1))


def masked_softmax(logits, mask):
    logits = jnp.where(mask, logits, -jnp.inf)
    m = jnp.max(logits, axis=-1, keepdims=True)
    m = jnp.where(jnp.isfinite(m), m, 0.0)
    e = jnp.exp(logits - m)
    s = jnp.maximum(jnp.sum(e, axis=-1, keepdims=True), TINY)
    return e / s, (jnp.log(s) + m)[..., 0]


def to_blocks(a, nb):
    return jnp.swapaxes(a.reshape(a.shape[0], nb, Q_BLOCK, *a.shape[2:]), 0, 1)


def from_blocks(a):
    a = jnp.swapaxes(a, 0, 1)
    return a.reshape(a.shape[0], -1, *a.shape[3:])


def gather_pages(pool, layer, page_table):
    rows = pool[layer, page_table]
    return rows.reshape(page_table.shape[0], page_table.shape[1] * pool.shape[2], *pool.shape[3:])


def cmp_sel_cover(n_cmp, n_sel):
    c0 = jnp.arange(n_cmp) * CMP_STRIDE
    s0 = jnp.arange(n_sel) * SEL_BLOCK
    return ((c0[:, None] < s0[None, :] + SEL_BLOCK) & (c0[:, None] + CMP_LEN > s0[None, :])).astype(jnp.float32)


def nsa_compress(rows, w1, w2, pe):
    B, L, G, D = rows.shape
    n_cmp = (L - CMP_LEN) // CMP_STRIDE + 1
    idx = jnp.arange(n_cmp)[:, None] * CMP_STRIDE + jnp.arange(CMP_LEN)[None, :]
    blk = rows[:, idx] + pe[None, None, :, None, :]
    blk = jnp.swapaxes(blk, 2, 3).reshape(B, n_cmp, G, CMP_LEN * D)
    return jax.nn.gelu(blk @ w1) @ w2


def nsa_attend(q, t, gates, kc, vc, ks, vs, kw, vw, w_pos, rel_bias):
    f32 = jnp.float32
    B, T, G, R, D = q.shape
    H = G * R
    table = rel_bias[:, :H].astype(f32)
    qf = q.astype(f32) * (D ** -0.5)
    n_cmp = kc.shape[1]
    cmp_end = jnp.arange(n_cmp) * CMP_STRIDE + (CMP_LEN - 1)
    d_c = t[:, None] - cmp_end[None, :]
    b_c = jnp.transpose(table[t5_bucket(d_c)].reshape(T, n_cmp, G, R), (0, 2, 3, 1))
    s_c = jnp.einsum('btgrd,bngd->btgrn', qf, kc.astype(f32)) + b_c
    p_c, _ = masked_softmax(s_c, (d_c >= 0)[None, :, None, None, :])
    o_c = jnp.einsum('btgrn,bngd->btgrd', p_c, vc.astype(f32))
    n_sel = ks.shape[1] // SEL_BLOCK
    imp = jnp.einsum('btgn,ns->btgs', p_c.sum(axis=3), cmp_sel_cover(n_cmp, n_sel))
    blk = jnp.arange(n_sel)[None, :]
    cur = (t // SEL_BLOCK)[:, None]
    forced = (blk == 0) | (blk == cur) | (blk == cur - 1)
    imp = jnp.where(forced[None, :, None, :], FORCED_SCORE, imp)
    imp = jnp.where((blk <= cur)[None, :, None, :], imp, -jnp.inf)
    n_top = min(SEL_TOPK, n_sel)
    top_val, top_idx = lax.top_k(imp, n_top)
    pos_s = (top_idx[..., None] * SEL_BLOCK + jnp.arange(SEL_BLOCK)).reshape(B, T, G, n_top * SEL_BLOCK)
    ok_s = jnp.repeat(top_val > -jnp.inf, SEL_BLOCK, axis=-1) & (pos_s <= t[None, :, None, None])
    bi = jnp.arange(B)[:, None, None, None]
    gi = jnp.arange(G)[None, None, :, None]
    k_sel = ks[bi, pos_s, gi].astype(f32)
    v_sel = vs[bi, pos_s, gi].astype(f32)
    table_g = jnp.transpose(table.reshape(N_BUCKETS, G, R), (1, 0, 2))
    b_s = jnp.moveaxis(table_g[gi, t5_bucket(t[None, :, None, None] - pos_s)], -1, 3)
    s_s = jnp.einsum('btgrd,btgkd->btgrk', qf, k_sel) + b_s
    p_s, _ = masked_softmax(s_s, ok_s[:, :, :, None, :])
    o_s = jnp.einsum('btgrk,btgkd->btgrd', p_s, v_sel)
    d_w = t[:, None] - w_pos[None, :]
    ok_w = (d_w >= 0) & (d_w < NSA_WINDOW) & (w_pos >= 0)[None, :]
    b_w = jnp.transpose(table[t5_bucket(d_w)].reshape(T, w_pos.shape[0], G, R), (0, 2, 3, 1))
    s_w = jnp.einsum('btgrd,blgd->btgrl', qf, kw.astype(f32)) + b_w
    p_w, _ = masked_softmax(s_w, ok_w[None, :, None, None, :])
    o_w = jnp.einsum('btgrl,blgd->btgrd', p_w, vw.astype(f32))
    g = gates.astype(f32)
    o = g[..., 0:1] * o_c + g[..., 1:2] * o_s + g[..., 2:3] * o_w
    return o.reshape(B, T, H * D)


def fox_attend(q, cq, t, k, v, ck, s_pos):
    f32 = jnp.float32
    D = q.shape[-1]
    s = jnp.einsum('bthd,blhd->bhtl', q.astype(f32) * (D ** -0.5), k.astype(f32))
    s = s + jnp.swapaxes(cq, 1, 2)[..., None] - jnp.swapaxes(ck, 1, 2)[:, :, None, :]
    p, _ = masked_softmax(s, (s_pos[None, :] <= t[:, None])[None, None])
    o = jnp.einsum('bhtl,blhd->bthd', p, v.astype(f32))
    return o.reshape(o.shape[0], o.shape[1], -1)


def project_a(h, w_in, gate_b, f_b):
    B, T, _ = h.shape
    z = jnp.einsum('btd,dp->btp', h, w_in)
    q_n, kv_n, g_n, qkv_f, f_f = jnp.split(z, PROJ_A_SPLITS, axis=-1)
    q_n = q_n.reshape(B, T, NSA_KV_HEADS, NSA_GROUP, HEAD_DIM)
    kv_n = kv_n.reshape(B, T, 6, NSA_KV_HEADS, HEAD_DIM)
    gates = jax.nn.sigmoid((g_n.reshape(B, T, NSA_HEADS, 3) + gate_b).astype(jnp.float32))
    gates = gates.reshape(B, T, NSA_KV_HEADS, NSA_GROUP, 3)
    qkv_f = qkv_f.reshape(B, T, 3, FOX_HEADS, HEAD_DIM)
    logf = jax.nn.log_sigmoid((f_f + f_b).astype(jnp.float32))
    return q_n, kv_n, gates, qkv_f[:, :, 0], qkv_f[:, :, 1:3], logf


def even_prompt(h, w_in, gate_b, f_b, cmp_w1, cmp_w2, cmp_pe, w_out, rel_bias):
    B, S, _ = h.shape
    q_n, kv_n, gates, q_f, kv_f, logf = project_a(h, w_in, gate_b, f_b)
    kc = nsa_compress(kv_n[:, :, 0], cmp_w1[0], cmp_w2[0], cmp_pe[0])
    vc = nsa_compress(kv_n[:, :, 1], cmp_w1[1], cmp_w2[1], cmp_pe[1])
    ks, vs = kv_n[:, :, 2], kv_n[:, :, 3]
    kw_pad = jnp.pad(kv_n[:, :, 4:6], ((0, 0), (NSA_WINDOW, 0), (0, 0), (0, 0), (0, 0)))
    k_f, v_f = kv_f[:, :, 0], kv_f[:, :, 1]
    c = lax.cumsum(logf, axis=1)
    nb = S // Q_BLOCK
    s_pos = jnp.arange(S)

    def block(args):
        b, qn_b, g_b, qf_b, c_b = args
        t = b * Q_BLOCK + jnp.arange(Q_BLOCK)
        win = lax.dynamic_slice_in_dim(kw_pad, b * Q_BLOCK, NSA_WINDOW + Q_BLOCK, axis=1)
        w_pos = b * Q_BLOCK - NSA_WINDOW + jnp.arange(NSA_WINDOW + Q_BLOCK)
        o_n = nsa_attend(qn_b, t, g_b, kc, vc, ks, vs, win[:, :, 0], win[:, :, 1], w_pos, rel_bias)
        o_f = fox_attend(qf_b, c_b, t, k_f, v_f, c, s_pos)
        return jnp.concatenate([o_n, o_f], axis=-1).astype(h.dtype)

    out = lax.map(block, (jnp.arange(nb), to_blocks(q_n, nb), to_blocks(gates, nb), to_blocks(q_f, nb), to_blocks(c, nb)))
    y = jnp.einsum('btm,md->btd', from_blocks(out), w_out)
    n_win = min(NSA_WINDOW, S)
    return y, kv_n[:, :, :4], kv_f, logf, kv_n[:, S - n_win:, 4:6]


def even_sample(h, nsa_pool, fox_pool, logf_pool, layer, win_buf, page_table, w_in, gate_b, f_b, cmp_w1, cmp_w2, cmp_pe, w_out, rel_bias):
    Bd, T, _ = h.shape
    q_n, kv_n, gates, q_f, kv_f, logf = project_a(h, w_in, gate_b, f_b)
    past_len = page_table.shape[1] * nsa_pool.shape[2]
    t = past_len + jnp.arange(T)
    nsa_rows = jnp.concatenate([gather_pages(nsa_pool, layer, page_table), kv_n[:, :, :4]], axis=1)
    L = nsa_rows.shape[1]
    kc = nsa_compress(nsa_rows[:, :, 0], cmp_w1[0], cmp_w2[0], cmp_pe[0])
    vc = nsa_compress(nsa_rows[:, :, 1], cmp_w1[1], cmp_w2[1], cmp_pe[1])
    sel = jnp.pad(nsa_rows[:, :, 2:4], ((0, 0), (0, (-L) % SEL_BLOCK), (0, 0), (0, 0), (0, 0)))
    win = jnp.concatenate([win_buf, kv_n[:, :, 4:6]], axis=1)
    n_buf = win_buf.shape[1]
    w_pos = past_len - n_buf + jnp.arange(n_buf + T)
    o_n = nsa_attend(q_n, t, gates, kc, vc, sel[:, :, 0], sel[:, :, 1], win[:, :, 0], win[:, :, 1], w_pos, rel_bias)
    fox_rows = jnp.concatenate([gather_pages(fox_pool, layer, page_table), kv_f], axis=1)
    logf_all = jnp.concatenate([gather_pages(logf_pool, layer, page_table).astype(jnp.float32), logf], axis=1)
    c = lax.cumsum(logf_all, axis=1)
    o_f = fox_attend(q_f, c[:, past_len:], t, fox_rows[:, :, 0], fox_rows[:, :, 1], c, jnp.arange(L))
    y = jnp.einsum('btm,md->btd', jnp.concatenate([o_n, o_f], axis=-1).astype(h.dtype), w_out)
    return y, kv_n[:, :, :4], kv_f, logf, kv_n[:, :, 4:6]


def project_c(h, w_in):
    B, T, _ = h.shape
    qkv = jnp.einsum('btd,dp->btp', h, w_in).reshape(B, T, 3, DIL_HEADS, HEAD_DIM)
    return qkv[:, :, 0], qkv[:, :, 1:3]


def dilated_branch_prompt(q, k, v, window, dilation, rel_bias):
    f32 = jnp.float32
    B, S, H, D = q.shape
    nk = window // dilation
    L = S // dilation
    qb = math.gcd(Q_BLOCK, L)
    nb = L // qb

    def by_residue(a):
        return jnp.swapaxes(a.reshape(B, L, dilation, H, D), 1, 2).reshape(B * dilation, L, H, D)

    def from_residue(a):
        a = a.reshape(B, dilation, L, *a.shape[2:])
        return jnp.swapaxes(a, 1, 2).reshape(B, S, *a.shape[3:])

    pad = ((0, 0), (nk, 0), (0, 0), (0, 0))
    idx = jnp.arange(nb)[:, None] * qb + jnp.arange(nk + qb)[None, :]
    qs = by_residue(q).astype(f32).reshape(B * dilation, nb, qb, H, D) * (D ** -0.5)
    kb = jnp.pad(by_residue(k), pad)[:, idx].astype(f32)
    vb = jnp.pad(by_residue(v), pad)[:, idx].astype(f32)
    off = jnp.arange(qb)[:, None] + nk - jnp.arange(nk + qb)[None, :]
    bias = jnp.transpose(rel_bias[:, :H].astype(f32)[t5_bucket(off * dilation)], (2, 0, 1))
    mask = ((off >= 0) & (off <= nk))[None] & (idx >= nk)[:, None, :]
    s = jnp.einsum('bnqhd,bnkhd->bnhqk', qs, kb) + bias
    p, lse = masked_softmax(s, mask[None, :, None])
    o = jnp.einsum('bnhqk,bnkhd->bnqhd', p, vb)
    o = from_residue(o.reshape(B * dilation, L, H, D))
    lse = from_residue(jnp.swapaxes(lse, 2, 3).reshape(B * dilation, L, H))
    return o, lse


def dilated_branch_sample(q, k_rows, v_rows, n_buf, window, dilation, rel_bias):
    f32 = jnp.float32
    Bd, T, H, D = q.shape
    nk = window // dilation
    dist = jnp.arange(nk + 1) * dilation
    idx = (n_buf + jnp.arange(T))[:, None] - dist[None, :]
    ok = idx >= 0
    idx = jnp.maximum(idx, 0)
    kg = k_rows[:, idx].astype(f32)
    vg = v_rows[:, idx].astype(f32)
    bias = rel_bias[:, :H].astype(f32)[t5_bucket(dist)].T
    s = jnp.einsum('bthd,btmhd->bhtm', q.astype(f32) * (D ** -0.5), kg) + bias[:, None, :]
    p, lse = masked_softmax(s, ok[None, None])
    o = jnp.einsum('bhtm,btmhd->bthd', p, vg)
    return o, jnp.swapaxes(lse, 1, 2)


def combine_by_denominator(outs, lses):
    alpha = jax.nn.softmax(jnp.stack(lses, axis=-1), axis=-1)
    o = jnp.einsum('bthn,nbthd->bthd', alpha, jnp.stack(outs))
    return o.reshape(o.shape[0], o.shape[1], -1)


def odd_prompt(h, w_in, w_out, rel_bias):
    S = h.shape[1]
    q, kv = project_c(h, w_in)
    outs, lses = [], []
    for window, dilation in DIL_BRANCHES:
        o, l = dilated_branch_prompt(q, kv[:, :, 0], kv[:, :, 1], window, dilation, rel_bias)
        outs.append(o)
        lses.append(l)
    y = jnp.einsum('btm,md->btd', combine_by_denominator(outs, lses).astype(h.dtype), w_out)
    return y, kv[:, S - min(DIL_BUF, S):]


def odd_sample(h, buf, w_in, w_out, rel_bias):
    q, kv = project_c(h, w_in)
    rows = jnp.concatenate([buf, kv], axis=1)
    outs, lses = [], []
    for window, dilation in DIL_BRANCHES:
        o, l = dilated_branch_sample(q, rows[:, :, 0], rows[:, :, 1], buf.shape[1], window, dilation, rel_bias)
        outs.append(o)
        lses.append(l)
    y = jnp.einsum('btm,md->btd', combine_by_denominator(outs, lses).astype(h.dtype), w_out)
    return y, kv


def setup_inputs(seed: int = 0) -> dict:
    key = jax.random.key(seed)
    ks = jax.random.split(key, 26)
    f32 = jnp.float32

    def nrm(k, shape, scale=1.0):
        return jax.random.normal(k, shape, f32) * scale

    n_pages = PAST_LEN // PAGE_SIZE
    n_used = DEC_BATCH * n_pages
    n_pool = n_used + (n_used + 3) // 4
    page_table = jax.random.permutation(ks[0], n_pool)[:n_used].astype(jnp.int32).reshape(DEC_BATCH, n_pages)
    n_win = min(NSA_WINDOW, PAST_LEN)
    n_dil = min(DIL_BUF, PAST_LEN)
    return {
        'x_prompt': nrm(ks[1], (BATCH, SEQ, D_MODEL)),
        'x_sample': nrm(ks[2], (DEC_BATCH, DEC_SEQ, D_MODEL)),
        'cache_nsa_kv': nrm(ks[3], (N_A_LAYERS, n_pool, PAGE_SIZE, 4, NSA_KV_HEADS, HEAD_DIM)),
        'cache_fox_kv': nrm(ks[4], (N_A_LAYERS, n_pool, PAGE_SIZE, 2, FOX_HEADS, HEAD_DIM)),
        'cache_fox_logf': jax.nn.log_sigmoid(nrm(ks[5], (N_A_LAYERS, n_pool, PAGE_SIZE, FOX_HEADS)) + 3.0),
        'state_nsa_win_kv': nrm(ks[6], (N_A_LAYERS, DEC_BATCH, n_win, 2, NSA_KV_HEADS, HEAD_DIM)),
        'state_dil_kv': nrm(ks[7], (N_C_LAYERS, DEC_BATCH, n_dil, 2, DIL_HEADS, HEAD_DIM)),
        'page_table': page_table,
        'c_prompt': nrm(ks[8], (BATCH, D_MODEL)),
        'c_sample': nrm(ks[9], (DEC_BATCH, D_MODEL)),
        'rel_bias': nrm(ks[10], (N_BUCKETS, N_MIX_HEADS), 0.5),
        'norm_g': 1.0 + nrm(ks[11], (DEPTH, 4, D_MODEL), 0.05),
        'w_ada': nrm(ks[12], (DEPTH, D_MODEL, 6 * D_MODEL), 0.5 * D_MODEL ** -0.5),
        'b_ada': nrm(ks[13], (DEPTH, 6 * D_MODEL), 0.02),
        'w_in_a': nrm(ks[14], (N_A_LAYERS, D_MODEL, PROJ_A), D_MODEL ** -0.5),
        'nsa_gate_b': nrm(ks[15], (N_A_LAYERS, NSA_HEADS, 3), 0.1),
        'fox_f_b': jax.random.uniform(ks[16], (N_A_LAYERS, FOX_HEADS), f32, 1.0, 6.0),
        'nsa_cmp_w1': nrm(ks[17], (N_A_LAYERS, 2, CMP_LEN * HEAD_DIM, CMP_HIDDEN), (CMP_LEN * HEAD_DIM) ** -0.5),
        'nsa_cmp_w2': nrm(ks[18], (N_A_LAYERS, 2, CMP_HIDDEN, HEAD_DIM), CMP_HIDDEN ** -0.5),
        'nsa_cmp_pe': nrm(ks[19], (N_A_LAYERS, 2, CMP_LEN, HEAD_DIM), 0.1),
        'w_out_a': nrm(ks[20], (N_A_LAYERS, MIX_WIDTH, D_MODEL), MIX_WIDTH ** -0.5),
        'w_in_c': nrm(ks[21], (N_C_LAYERS, D_MODEL, PROJ_C), D_MODEL ** -0.5),
        'w_out_c': nrm(ks[22], (N_C_LAYERS, MIX_WIDTH, D_MODEL), MIX_WIDTH ** -0.5),
        'w_mlp1': nrm(ks[23], (DEPTH, D_MODEL, D_FF), D_MODEL ** -0.5),
        'w_mlp2': nrm(ks[24], (DEPTH, D_FF, D_MODEL), D_FF ** -0.5),
    }


def reference(x_prompt, x_sample, cache_nsa_kv, cache_fox_kv, cache_fox_logf, state_nsa_win_kv, state_dil_kv,
              page_table, c_prompt, c_sample, rel_bias, norm_g, w_ada, b_ada, w_in_a, nsa_gate_b, fox_f_b,
              nsa_cmp_w1, nsa_cmp_w2, nsa_cmp_pe, w_out_a, w_in_c, w_out_c, w_mlp1, w_mlp2):
    xp, xs = x_prompt, x_sample
    nsa_p, nsa_s, fkv_p, fkv_s, lf_p, lf_s, win_p, win_s, dil_p, dil_s = [], [], [], [], [], [], [], [], [], []
    for layer in range(DEPTH):
        mp = adaln_params(c_prompt, w_ada[layer], b_ada[layer])
        ms = adaln_params(c_sample, w_ada[layer], b_ada[layer])
        hp = rms_norm(xp, norm_g[layer, 0]) * (1 + mp[:, 1]) + mp[:, 0]
        hs = rms_norm(xs, norm_g[layer, 0]) * (1 + ms[:, 1]) + ms[:, 0]
        i = layer // 2
        if layer % 2 == 0:
            op, a_p, b_p, c_p, d_p = even_prompt(hp, w_in_a[i], nsa_gate_b[i], fox_f_b[i], nsa_cmp_w1[i],
                                                 nsa_cmp_w2[i], nsa_cmp_pe[i], w_out_a[i], rel_bias)
            os_, a_s, b_s, c_s, d_s = even_sample(hs, cache_nsa_kv, cache_fox_kv, cache_fox_logf, i,
                                                  state_nsa_win_kv[i], page_table, w_in_a[i], nsa_gate_b[i],
                                                  fox_f_b[i], nsa_cmp_w1[i], nsa_cmp_w2[i], nsa_cmp_pe[i],
                                                  w_out_a[i], rel_bias)
            nsa_p.append(a_p)
            nsa_s.append(a_s)
            fkv_p.append(b_p)
            fkv_s.append(b_s)
            lf_p.append(c_p)
            lf_s.append(c_s)
            win_p.append(d_p)
            win_s.append(d_s)
        else:
            op, e_p = odd_prompt(hp, w_in_c[i], w_out_c[i], rel_bias)
            os_, e_s = odd_sample(hs, state_dil_kv[i], w_in_c[i], w_out_c[i], rel_bias)
            dil_p.append(e_p)
            dil_s.append(e_s)
        xp = xp + mp[:, 2] * rms_norm(op, norm_g[layer, 1])
        xs = xs + ms[:, 2] * rms_norm(os_, norm_g[layer, 1])
        hp = rms_norm(xp, norm_g[layer, 2]) * (1 + mp[:, 4]) + mp[:, 3]
        hs = rms_norm(xs, norm_g[layer, 2]) * (1 + ms[:, 4]) + ms[:, 3]
        xp = xp + mp[:, 5] * rms_norm(sq_relu_mlp(hp, w_mlp1[layer], w_mlp2[layer]), norm_g[layer, 3])
        xs = xs + ms[:, 5] * rms_norm(sq_relu_mlp(hs, w_mlp1[layer], w_mlp2[layer]), norm_g[layer, 3])
    return (xp, xs, jnp.stack(nsa_p), jnp.stack(nsa_s), jnp.stack(fkv_p), jnp.stack(fkv_s), jnp.stack(lf_p),
            jnp.stack(lf_s), jnp.stack(win_p), jnp.stack(win_s), jnp.stack(dil_p), jnp.stack(dil_s))
```

```python
import functools
import math

import numpy as np
import jax
import jax.numpy as jnp
from jax import lax
from jax.experimental import pallas as pl
from jax.experimental.pallas import tpu as pltpu

F32 = jnp.float32
BF16 = jnp.bfloat16

HEAD_DIM = 64
NSA_HEADS = 8
NSA_KV_HEADS = 2
NSA_GROUP = NSA_HEADS // NSA_KV_HEADS
FOX_HEADS = 8
DIL_HEADS = 16
CMP_LEN = 32
CMP_STRIDE = 16
CMP_HIDDEN = 4 * HEAD_DIM
SEL_BLOCK = 64
SEL_TOPK = 16
NSA_WINDOW = 512
FORCED_SCORE = 1e9
DIL_BRANCHES = ((128, 1), (512, 4), (2048, 16))
N_BUCKETS = 32
BUCKET_EXACT = 16
BUCKET_MAX_DIST = 2048
NORM_EPS = 1e-6
TINY = 1e-30
PAGE_SIZE = 128

LANES = 128
VMEM_LIMIT_BYTES = 56 * 1024 * 1024

NEG = -1e30
QK_SCALE = HEAD_DIM ** -0.5
TILE = 128
SEL_SHIFT = 6
SMALL_GATES = 3 * NSA_HEADS


def _cparams(*sem):
    return pltpu.CompilerParams(dimension_semantics=sem, vmem_limit_bytes=VMEM_LIMIT_BYTES)


def _bucket_thresholds():
    d = np.arange(0, 2 * BUCKET_MAX_DIST + 1)
    df = np.maximum(d, 1).astype(np.float64)
    ratio = math.log(BUCKET_MAX_DIST / BUCKET_EXACT)
    log_b = BUCKET_EXACT + (np.log(df / BUCKET_EXACT) / ratio * (N_BUCKETS - BUCKET_EXACT)).astype(np.int64)
    bucket = np.where(d < BUCKET_EXACT, d, np.clip(log_b, BUCKET_EXACT, N_BUCKETS - 1))
    return [int(np.argmax(bucket >= b)) for b in range(1, N_BUCKETS)]


BUCKET_THR = _bucket_thresholds()


def _bias_of_distance(d, tab_ref, h):
    val = jnp.full(d.shape, tab_ref[0, h], F32)
    for b in range(1, N_BUCKETS):
        val = jnp.where(d >= BUCKET_THR[b - 1], tab_ref[b, h], val)
    return val


def _toeplitz_kernel(tab_ref, o_ref, *, n_heads, mode):
    idx = pl.program_id(0)
    r = lax.broadcasted_iota(jnp.int32, (TILE, TILE), 0)
    c = lax.broadcasted_iota(jnp.int32, (TILE, TILE), 1)
    d = (idx - 1) * TILE + r - c
    live = idx > 0
    if mode == "causal":
        ok = live & (d >= 0)
        extra = None
    elif mode == "window":
        ok = live & (d >= 0) & (d < NSA_WINDOW)
        extra = None
    else:
        cnt = jnp.zeros((TILE, TILE), F32)
        for window, dil in DIL_BRANCHES:
            cnt = cnt + jnp.where((d >= 0) & (d <= window) & (jnp.bitwise_and(d, dil - 1) == 0), 1.0, 0.0)
        ok = live & (cnt > 0.5)
        extra = jnp.log(jnp.maximum(cnt, 1.0))
    dd = jnp.maximum(d, 0)
    for h in range(n_heads):
        val = _bias_of_distance(dd, tab_ref, h)
        if extra is not None:
            val = val + extra
        o_ref[h, 0] = jnp.where(ok, val, NEG)


def _toeplitz_table(rel_bias, n_heads, n_idx, mode):
    return pl.pallas_call(
        functools.partial(_toeplitz_kernel, n_heads=n_heads, mode=mode),
        grid=(n_idx,),
        in_specs=[pl.BlockSpec(memory_space=pltpu.SMEM)],
        out_specs=pl.BlockSpec((n_heads, 1, TILE, TILE), lambda i: (0, i, 0, 0)),
        out_shape=jax.ShapeDtypeStruct((n_heads, n_idx, TILE, TILE), F32),
        compiler_params=_cparams("arbitrary"),
        name="bias_toeplitz_" + mode,
    )(rel_bias)


def _cmp_bias_kernel(tab_ref, o_ref, *, tq, n_cmp_pad, t_base):
    t = t_base + pl.program_id(0) * tq + lax.broadcasted_iota(jnp.int32, (tq, n_cmp_pad), 0)
    n = lax.broadcasted_iota(jnp.int32, (tq, n_cmp_pad), 1)
    d = jnp.maximum(t - (n * CMP_STRIDE + CMP_LEN - 1), 0)
    for h in range(NSA_HEADS):
        o_ref[h] = _bias_of_distance(d, tab_ref, h)


def _cmp_bias_table(rel_bias, n_rows, tq, n_cmp_pad, t_base):
    return pl.pallas_call(
        functools.partial(_cmp_bias_kernel, tq=tq, n_cmp_pad=n_cmp_pad, t_base=t_base),
        grid=(n_rows // tq,),
        in_specs=[pl.BlockSpec(memory_space=pltpu.SMEM)],
        out_specs=pl.BlockSpec((NSA_HEADS, tq, n_cmp_pad), lambda i: (0, i, 0)),
        out_shape=jax.ShapeDtypeStruct((NSA_HEADS, n_rows, n_cmp_pad), F32),
        compiler_params=_cparams("arbitrary"),
        name="bias_cmp",
    )(rel_bias)


def _ada_kernel(c_ref, w_ref, b_ref, o_ref):
    c = c_ref[...]
    s = (c * jax.nn.sigmoid(c)).astype(BF16)
    o_ref[0] = jnp.dot(s, w_ref[0].astype(BF16), preferred_element_type=F32) + b_ref[0]


def _ada_params(c_all, w_ada, b_ada):
    depth, d, d6 = w_ada.shape
    m = c_all.shape[0]
    return pl.pallas_call(
        _ada_kernel,
        grid=(depth, d6 // d),
        in_specs=[pl.BlockSpec((m, d), lambda l, j: (0, 0)),
                  pl.BlockSpec((1, d, d), lambda l, j: (l, 0, j)),
                  pl.BlockSpec((1, 1, d), lambda l, j: (l, 0, j))],
        out_specs=pl.BlockSpec((1, m, d), lambda l, j: (l, 0, j)),
        out_shape=jax.ShapeDtypeStruct((depth, m, d6), F32),
        compiler_params=_cparams("arbitrary", "arbitrary"),
        name="adaln",
    )(c_all, w_ada, b_ada.reshape(depth, 1, d6))


def _norm_mod(x, g, scale, shift):
    y = x * lax.rsqrt(jnp.mean(x * x, axis=-1, keepdims=True) + NORM_EPS)
    return (y * g) * (1.0 + scale) + shift


def _row_tile(s, want):
    return want if s % want == 0 else s


def _mod_spec(mod, tm):
    if mod.shape[1] == 1:
        return pl.BlockSpec((1, 1, mod.shape[2]), lambda b, i: (b, 0, 0))
    return pl.BlockSpec((1, tm, mod.shape[2]), lambda b, i: (b, i, 0))


def _proj_even_kernel(x_ref, sh_ref, sc_ref, g_ref, w_ref, sb_ref,
                      qn_ref, nsa4_ref, nsabf_ref, win_ref, qf_ref, fkv_ref, fkvbf_ref, small_ref):
    h = _norm_mod(x_ref[0], g_ref[...], sc_ref[0], sh_ref[0]).astype(BF16)
    z = jnp.dot(h, w_ref[...], preferred_element_type=F32)
    qn_ref[0] = (z[:, 0:512] * QK_SCALE).astype(BF16)
    nsa4_ref[0] = z[:, 512:1024]
    nsabf_ref[0] = z[:, 512:1280].astype(BF16)
    win_ref[0] = z[:, 1024:1280]
    qf_ref[0] = (z[:, 1280:1792] * QK_SCALE).astype(BF16)
    fkv_ref[0] = z[:, 1792:2816]
    fkvbf_ref[0] = z[:, 1792:2816].astype(BF16)
    zs = z[:, 2816:2944] + sb_ref[...]
    lane = lax.broadcasted_iota(jnp.int32, zs.shape, 1)
    sig = jax.nn.sigmoid(zs)
    lsg = jnp.minimum(zs, 0.0) - jnp.log1p(jnp.exp(-jnp.abs(zs)))
    small_ref[0] = jnp.where(lane < SMALL_GATES, sig, lsg)


def _proj_even(x, shift, scale, g, w, sb):
    bx, s, d = x.shape
    tm = _row_tile(s, 256)
    n = w.shape[1]
    widths = (512, 512, 768, 256, 512, 1024, 1024, 128)
    dtypes = (BF16, F32, BF16, F32, BF16, F32, BF16, F32)
    return pl.pallas_call(
        _proj_even_kernel,
        grid=(bx, s // tm),
        in_specs=[pl.BlockSpec((1, tm, d), lambda b, i: (b, i, 0)),
                  _mod_spec(shift, tm), _mod_spec(scale, tm),
                  pl.BlockSpec((1, d), lambda b, i: (0, 0)),
                  pl.BlockSpec((d, n), lambda b, i: (0, 0)),
                  pl.BlockSpec((1, LANES), lambda b, i: (0, 0))],
        out_specs=[pl.BlockSpec((1, tm, wd), lambda b, i: (b, i, 0)) for wd in widths],
        out_shape=[jax.ShapeDtypeStruct((bx, s, wd), dt) for wd, dt in zip(widths, dtypes)],
        compiler_params=_cparams("arbitrary", "arbitrary"),
        name="proj_even",
    )(x, shift, scale, g, w, sb)


def _proj_odd_kernel(x_ref, sh_ref, sc_ref, g_ref, w_ref, q_ref, kv_ref, kvbf_ref):
    h = _norm_mod(x_ref[0], g_ref[...], sc_ref[0], sh_ref[0]).astype(BF16)
    z = jnp.dot(h, w_ref[...], preferred_element_type=F32)
    q_ref[0] = (z[:, 0:1024] * QK_SCALE).astype(BF16)
    kv_ref[0] = z[:, 1024:3072]
    kvbf_ref[0] = z[:, 1024:3072].astype(BF16)


def _proj_odd(x, shift, scale, g, w):
    bx, s, d = x.shape
    tm = _row_tile(s, 256)
    n = w.shape[1]
    widths = (1024, 2048, 2048)
    dtypes = (BF16, F32, BF16)
    return pl.pallas_call(
        _proj_odd_kernel,
        grid=(bx, s // tm),
        in_specs=[pl.BlockSpec((1, tm, d), lambda b, i: (b, i, 0)),
                  _mod_spec(shift, tm), _mod_spec(scale, tm),
                  pl.BlockSpec((1, d), lambda b, i: (0, 0)),
                  pl.BlockSpec((d, n), lambda b, i: (0, 0))],
        out_specs=[pl.BlockSpec((1, tm, wd), lambda b, i: (b, i, 0)) for wd in widths],
        out_shape=[jax.ShapeDtypeStruct((bx, s, wd), dt) for wd, dt in zip(widths, dtypes)],
        compiler_params=_cparams("arbitrary", "arbitrary"),
        name="proj_odd",
    )(x, shift, scale, g, w)


def _post_kernel(oa_ref, ob_ref, w_ref, x_ref, gate_ref, g_ref, o_ref):
    half = oa_ref.shape[2]
    y = jnp.dot(oa_ref[0], w_ref[0:half, :], preferred_element_type=F32)
    y = y + jnp.dot(ob_ref[0], w_ref[half:, :], preferred_element_type=F32)
    yn = y * lax.rsqrt(jnp.mean(y * y, axis=-1, keepdims=True) + NORM_EPS) * g_ref[...]
    o_ref[0] = x_ref[0] + gate_ref[0] * yn


def _post(o_a, o_b, w_out, x, gate, g, col_a=0, col_b=0):
    bx, s, d = x.shape
    tm = _row_tile(s, 512)
    half = w_out.shape[0] // 2
    return pl.pallas_call(
        _post_kernel,
        grid=(bx, s // tm),
        in_specs=[pl.BlockSpec((1, tm, half), lambda b, i: (b, i, col_a)),
                  pl.BlockSpec((1, tm, half), lambda b, i: (b, i, col_b)),
                  pl.BlockSpec(w_out.shape, lambda b, i: (0, 0)),
                  pl.BlockSpec((1, tm, d), lambda b, i: (b, i, 0)),
                  _mod_spec(gate, tm),
                  pl.BlockSpec((1, d), lambda b, i: (0, 0))],
        out_specs=pl.BlockSpec((1, tm, d), lambda b, i: (b, i, 0)),
        out_shape=jax.ShapeDtypeStruct((bx, s, d), F32),
        compiler_params=_cparams("arbitrary", "arbitrary"),
        name="post",
    )(o_a, o_b, w_out, x, gate, g)


def _mlp_kernel(x_ref, sh_ref, sc_ref, gate_ref, g2_ref, g3_ref, w1_ref, w2_ref, o_ref, h_ref, acc_ref):
    j = pl.program_id(2)

    @pl.when(j == 0)
    def _():
        h_ref[...] = _norm_mod(x_ref[0], g2_ref[...], sc_ref[0], sh_ref[0]).astype(BF16)
        acc_ref[...] = jnp.zeros_like(acc_ref)

    a = jnp.maximum(jnp.dot(h_ref[...], w1_ref[...], preferred_element_type=F32), 0.0)
    acc_ref[...] += jnp.dot((a * a).astype(BF16), w2_ref[...], preferred_element_type=F32)

    @pl.when(j == pl.num_programs(2) - 1)
    def _():
        y = acc_ref[...]
        yn = y * lax.rsqrt(jnp.mean(y * y, axis=-1, keepdims=True) + NORM_EPS) * g3_ref[...]
        o_ref[0] = x_ref[0] + gate_ref[0] * yn


def _mlp(x, shift, scale, gate, g2, g3, w1, w2):
    bx, s, d = x.shape
    f = w1.shape[1]
    tm = _row_tile(s, 1024)
    tf = 1024

    def mod3(mod):
        if mod.shape[1] == 1:
            return pl.BlockSpec((1, 1, d), lambda b, i, j: (b, 0, 0))
        return pl.BlockSpec((1, tm, d), lambda b, i, j: (b, i, 0))

    return pl.pallas_call(
        _mlp_kernel,
        grid=(bx, s // tm, f // tf),
        in_specs=[pl.BlockSpec((1, tm, d), lambda b, i, j: (b, i, 0)),
                  mod3(shift), mod3(scale), mod3(gate),
                  pl.BlockSpec((1, d), lambda b, i, j: (0, 0)),
                  pl.BlockSpec((1, d), lambda b, i, j: (0, 0)),
                  pl.BlockSpec((d, tf), lambda b, i, j: (0, j)),
                  pl.BlockSpec((tf, d), lambda b, i, j: (j, 0))],
        out_specs=pl.BlockSpec((1, tm, d), lambda b, i, j: (b, i, 0)),
        out_shape=jax.ShapeDtypeStruct((bx, s, d), F32),
        scratch_shapes=[pltpu.VMEM((tm, d), BF16), pltpu.VMEM((tm, d), F32)],
        compiler_params=_cparams("arbitrary", "arbitrary", "arbitrary"),
        name="mlp",
    )(x, shift, scale, gate, g2, g3, w1, w2)


def _cumsum_kernel(x_ref, c_ref, ct_ref, carry_ref, *, tc):
    @pl.when(pl.program_id(1) == 0)
    def _():
        carry_ref[...] = jnp.zeros_like(carry_ref)

    r = lax.broadcasted_iota(jnp.int32, (tc, tc), 0)
    c = lax.broadcasted_iota(jnp.int32, (tc, tc), 1)
    tri = jnp.where(c <= r, 1.0, 0.0).astype(F32)
    cs = jnp.dot(tri, x_ref[0], preferred_element_type=F32, precision=lax.Precision.HIGHEST) + carry_ref[...]
    carry_ref[...] = cs[tc - 1:tc, :]
    c_ref[0] = cs
    ct_ref[0] = cs.T[SMALL_GATES:SMALL_GATES + FOX_HEADS, :]


def _cumsum(small):
    bx, s, _ = small.shape
    tc = _row_tile(s, 256)
    return pl.pallas_call(
        functools.partial(_cumsum_kernel, tc=tc),
        grid=(bx, s // tc),
        in_specs=[pl.BlockSpec((1, tc, LANES), lambda b, i: (b, i, 0))],
        out_specs=[pl.BlockSpec((1, tc, LANES), lambda b, i: (b, i, 0)),
                   pl.BlockSpec((1, FOX_HEADS, tc), lambda b, i: (b, 0, i))],
        out_shape=[jax.ShapeDtypeStruct((bx, s, LANES), F32),
                   jax.ShapeDtypeStruct((bx, FOX_HEADS, s), F32)],
        scratch_shapes=[pltpu.VMEM((1, LANES), F32)],
        compiler_params=_cparams("arbitrary", "arbitrary"),
        name="logf_cumsum",
    )(small)


def _nt_dot(a, b):
    return lax.dot_general(a, b, (((1,), (1,)), ((), ())), preferred_element_type=F32)


def _online_update(state, s, v):
    m, l, acc = state
    m_new = jnp.maximum(m, jnp.max(s, axis=1, keepdims=True))
    alpha = jnp.exp(m - m_new)
    p = jnp.exp(s - m_new)
    l = alpha * l + jnp.sum(p, axis=1, keepdims=True)
    acc = alpha * acc + jnp.dot(p.astype(BF16), v, preferred_element_type=F32)
    return m_new, l, acc


def _init_state(tq):
    return (jnp.full((tq, 1), NEG, F32), jnp.zeros((tq, 1), F32), jnp.zeros((tq, LANES), F32))


def _half_masks(tq):
    lane = lax.broadcasted_iota(jnp.int32, (tq, LANES), 1)
    return lane < HEAD_DIM


def _split_heads(q2, lo):
    zero = jnp.zeros_like(q2)
    return jnp.where(lo, q2, zero), jnp.where(lo, zero, q2)


def _fox_kernel(q_ref, k_ref, v_ref, c_ref, ct_ref, o_ref, *, tq, tk):
    p_idx = pl.program_id(1)
    i = pl.program_id(2)
    lo = _half_masks(tq)
    qa, qb = _split_heads(q_ref[0], lo)
    lane = lax.broadcasted_iota(jnp.int32, (tq, LANES), 1)
    cblk = c_ref[0]
    cqa = jnp.sum(jnp.where(lane == SMALL_GATES + 2 * p_idx, cblk, 0.0), axis=1, keepdims=True)
    cqb = jnp.sum(jnp.where(lane == SMALL_GATES + 2 * p_idx + 1, cblk, 0.0), axis=1, keepdims=True)

    def chunk(c, carry, masked):
        sa_state, sb_state = carry
        off = pl.multiple_of(c * tk, tk)
        k = k_ref[0, pl.ds(off, tk), :]
        v = v_ref[0, pl.ds(off, tk), :]
        cka = ct_ref[0, pl.ds(2 * p_idx, 1), pl.ds(off, tk)]
        ckb = ct_ref[0, pl.ds(2 * p_idx + 1, 1), pl.ds(off, tk)]
        sa = _nt_dot(qa, k) + cqa - cka
        sb = _nt_dot(qb, k) + cqb - ckb
        if masked:
            row = i * tq + lax.broadcasted_iota(jnp.int32, (tq, tk), 0)
            col = off + lax.broadcasted_iota(jnp.int32, (tq, tk), 1)
            ok = col <= row
            sa = jnp.where(ok, sa, NEG)
            sb = jnp.where(ok, sb, NEG)
        return _online_update(sa_state, sa, v), _online_update(sb_state, sb, v)

    n_full = (i * tq) // tk
    carry = lax.fori_loop(0, n_full, lambda c, cr: chunk(c, cr, False), (_init_state(tq), _init_state(tq)))
    (_, la, acca), (_, lb, accb) = chunk(n_full, carry, True)
    o_ref[0] = jnp.where(lo, acca / la, accb / lb).astype(o_ref.dtype)


def _fox_prompt(qf, fkvbf, c, ct):
    bx, s, _ = qf.shape
    tq = TILE
    tk = _row_tile(s, 512)
    n_pairs = FOX_HEADS // 2
    return pl.pallas_call(
        functools.partial(_fox_kernel, tq=tq, tk=tk),
        grid=(bx, n_pairs, s // tq),
        in_specs=[pl.BlockSpec((1, tq, LANES), lambda b, p, i: (b, i, p)),
                  pl.BlockSpec((1, s, LANES), lambda b, p, i: (b, 0, p)),
                  pl.BlockSpec((1, s, LANES), lambda b, p, i: (b, 0, n_pairs + p)),
                  pl.BlockSpec((1, tq, LANES), lambda b, p, i: (b, i, 0)),
                  pl.BlockSpec((1, FOX_HEADS, s), lambda b, p, i: (b, 0, 0))],
        out_specs=pl.BlockSpec((1, tq, LANES), lambda b, p, i: (b, i, p)),
        out_shape=jax.ShapeDtypeStruct((bx, s, FOX_HEADS * HEAD_DIM), BF16),
        compiler_params=_cparams("arbitrary", "arbitrary", "arbitrary"),
        name="fox_prompt",
    )(qf, fkvbf, fkvbf, c, ct)


def _band_kernel(q_ref, k_ref, v_ref, tab_ref, o_ref, *, n_delta):
    i = pl.program_id(2)
    lo = _half_masks(TILE)
    qa, qb = _split_heads(q_ref[0], lo)

    def tile(u, carry):
        sa_state, sb_state = carry
        off = pl.multiple_of((i - u) * TILE, TILE)
        k = k_ref[0, pl.ds(off, TILE), :]
        v = v_ref[0, pl.ds(off, TILE), :]
        sa = _nt_dot(qa, k) + tab_ref[0, u + 1]
        sb = _nt_dot(qb, k) + tab_ref[1, u + 1]
        return _online_update(sa_state, sa, v), _online_update(sb_state, sb, v)

    n_tiles = jnp.minimum(i, n_delta - 1) + 1
    (_, la, acca), (_, lb, accb) = lax.fori_loop(0, n_tiles, tile, (_init_state(TILE), _init_state(TILE)))
    o_ref[0] = jnp.where(lo, acca / la, accb / lb).astype(o_ref.dtype)


def _band_prompt(q, kvbf, table, n_heads):
    bx, s, _ = q.shape
    n_pairs = n_heads // 2
    n_idx = table.shape[1]
    return pl.pallas_call(
        functools.partial(_band_kernel, n_delta=n_idx - 1),
        grid=(bx, n_pairs, s // TILE),
        in_specs=[pl.BlockSpec((1, TILE, LANES), lambda b, p, i: (b, i, p)),
                  pl.BlockSpec((1, s, LANES), lambda b, p, i: (b, 0, p)),
                  pl.BlockSpec((1, s, LANES), lambda b, p, i: (b, 0, n_pairs + p)),
                  pl.BlockSpec((2, n_idx, TILE, TILE), lambda b, p, i: (p, 0, 0, 0))],
        out_specs=pl.BlockSpec((1, TILE, LANES), lambda b, p, i: (b, i, p)),
        out_shape=jax.ShapeDtypeStruct((bx, s, n_heads * HEAD_DIM), BF16),
        compiler_params=_cparams("arbitrary", "arbitrary", "arbitrary"),
        name="dilated_prompt",
    )(q, kvbf, kvbf, table)


def _compress_kernel(pt_ref, page_ref, w1_ref, w2_ref, pe_ref, kc_ref, vc_ref, rows_ref, chunk_ref, *, n_pages):
    j = pl.program_id(1)
    row0 = pl.multiple_of(j * PAGE_SIZE, PAGE_SIZE)
    rows_ref[0, pl.ds(row0, PAGE_SIZE), :] = page_ref[0, :, 0:LANES]
    rows_ref[1, pl.ds(row0, PAGE_SIZE), :] = page_ref[0, :, LANES:2 * LANES]

    @pl.when(j == n_pages - 1)
    def _():
        n_chunks = n_pages * PAGE_SIZE // CMP_STRIDE
        half = CMP_STRIDE * HEAD_DIM
        for kv, out_ref in ((0, kc_ref), (1, vc_ref)):
            w1 = w1_ref[kv]
            pe_a = jnp.broadcast_to(pe_ref[kv, :, 0:half], (8, half)).astype(BF16)
            pe_b = jnp.broadcast_to(pe_ref[kv, :, half:], (8, half)).astype(BF16)
            pe_term = (jnp.dot(pe_a, w1, preferred_element_type=F32)[0:1, 0:CMP_HIDDEN]
                       + jnp.dot(pe_b, w1, preferred_element_type=F32)[0:1, CMP_HIDDEN:])
            for l in range(CMP_STRIDE):
                both = rows_ref[kv, pl.ds(l, n_chunks, stride=CMP_STRIDE), :].astype(BF16)
                for g in range(NSA_KV_HEADS):
                    chunk_ref[g, :, l * HEAD_DIM:(l + 1) * HEAD_DIM] = both[:, g * HEAD_DIM:(g + 1) * HEAD_DIM]
            outs = []
            for g in range(NSA_KV_HEADS):
                uv = jnp.dot(chunk_ref[g], w1, preferred_element_type=F32)
                pre = uv[:, 0:CMP_HIDDEN] + pltpu.roll(uv[:, CMP_HIDDEN:], n_chunks - 1, 0) + pe_term
                hid = jax.nn.gelu(pre).astype(BF16)
                outs.append(jnp.dot(hid, w2_ref[kv], preferred_element_type=F32))
            out_ref[0] = jnp.concatenate(outs, axis=1)


def _compress(pool, page_table, w1cat, w2, pe):
    n_req, n_pages = page_table.shape
    n_chunks = n_pages * PAGE_SIZE // CMP_STRIDE
    width = 2 * NSA_KV_HEADS * HEAD_DIM
    grid_spec = pltpu.PrefetchScalarGridSpec(
        num_scalar_prefetch=1,
        grid=(n_req, n_pages),
        in_specs=[pl.BlockSpec((1, PAGE_SIZE, width), lambda r, j, pt: (pt[r, j], 0, 0)),
                  pl.BlockSpec(w1cat.shape, lambda r, j, pt: (0, 0, 0)),
                  pl.BlockSpec(w2.shape, lambda r, j, pt: (0, 0, 0)),
                  pl.BlockSpec(pe.shape, lambda r, j, pt: (0, 0, 0))],
        out_specs=[pl.BlockSpec((1, n_chunks, LANES), lambda r, j, pt: (r, 0, 0)),
                   pl.BlockSpec((1, n_chunks, LANES), lambda r, j, pt: (r, 0, 0))],
        scratch_shapes=[pltpu.VMEM((2, n_pages * PAGE_SIZE, LANES), F32),
                        pltpu.VMEM((NSA_KV_HEADS, n_chunks, CMP_STRIDE * HEAD_DIM), BF16)],
    )
    return pl.pallas_call(
        functools.partial(_compress_kernel, n_pages=n_pages),
        grid_spec=grid_spec,
        out_shape=[jax.ShapeDtypeStruct((n_req, n_chunks, LANES), F32)] * 2,
        compiler_params=_cparams("arbitrary", "arbitrary"),
        name="nsa_compress",
    )(page_table, pool, w1cat, w2, pe)


def _top_k_mask(imp, n_top):
    lane = lax.broadcasted_iota(jnp.int32, imp.shape, 1)
    width = imp.shape[1]

    def body(_, carry):
        imp, sel = carry
        m = jnp.max(imp, axis=1, keepdims=True)
        first = jnp.min(jnp.where(imp == m, lane, width), axis=1, keepdims=True)
        pick = (lane == first) & (m > -jnp.inf)
        return jnp.where(lane == first, -jnp.inf, imp), jnp.where(pick, 1.0, sel)

    _, sel = lax.fori_loop(0, n_top, body, (imp, jnp.zeros(imp.shape, F32)))
    return sel


def _nsa_kernel(qn_ref, kc_ref, vc_ref, bcmp_ref, kv_ref, tsel_ref, twin_ref, small_ref, o_ref,
                *, tq, tk, n_sel_delta, n_win_tiles):
    i = pl.program_id(1)
    t0 = i * tq
    n_cmp_pad = kc_ref.shape[1]
    n_blk = LANES
    lo = _half_masks(tq)
    lane = lax.broadcasted_iota(jnp.int32, (tq, LANES), 1)
    gates = small_ref[0]
    q_all = qn_ref[0].astype(F32)

    t_c = t0 + lax.broadcasted_iota(jnp.int32, (tq, n_cmp_pad), 0)
    n_c = lax.broadcasted_iota(jnp.int32, (tq, n_cmp_pad), 1)
    ok_c = t_c >= n_c * CMP_STRIDE + (CMP_LEN - 1)
    ci = lax.broadcasted_iota(jnp.int32, (n_cmp_pad, n_blk), 0) * CMP_STRIDE
    sj = lax.broadcasted_iota(jnp.int32, (n_cmp_pad, n_blk), 1) * SEL_BLOCK
    cover = jnp.where((ci < sj + SEL_BLOCK) & (ci + CMP_LEN > sj), 1.0, 0.0).astype(F32)
    t_b = t0 + lax.broadcasted_iota(jnp.int32, (tq, n_blk), 0)
    cur = lax.shift_right_arithmetic(t_b, SEL_SHIFT)
    forced = (lane == 0) | (lane == cur) | (lane == cur - 1)
    blk_of_key = lax.shift_right_arithmetic(lax.broadcasted_iota(jnp.int32, (n_blk, tk), 1), SEL_SHIFT)
    blk_row = lax.broadcasted_iota(jnp.int32, (n_blk, tk), 0)
    blk_delta = blk_row - blk_of_key
    sub = tk // TILE

    pair_out = [None] * (NSA_HEADS // 2)
    for g in range(NSA_KV_HEADS):
        in_g = lo if g == 0 else jnp.logical_not(lo)
        qs = []
        for r in range(NSA_GROUP):
            h = g * NSA_GROUP + r
            blk = q_all[:, LANES * (h // 2):LANES * (h // 2 + 1)]
            if h % 2 != g:
                blk = pltpu.roll(blk, HEAD_DIM, 1)
            qs.append(jnp.where(in_g, blk, 0.0).astype(BF16))

        kcb = kc_ref[0].astype(BF16)
        vcb = vc_ref[0].astype(BF16)
        psum = jnp.zeros((tq, n_cmp_pad), F32)
        o_cmp = []
        for r in range(NSA_GROUP):
            h = g * NSA_GROUP + r
            s = jnp.where(ok_c, _nt_dot(qs[r], kcb) + bcmp_ref[h], NEG)
            m = jnp.max(s, axis=1, keepdims=True)
            e = jnp.where(ok_c, jnp.exp(s - m), 0.0)
            p = e / jnp.maximum(jnp.sum(e, axis=1, keepdims=True), TINY)
            psum = psum + p
            o_cmp.append(jnp.dot(p.astype(BF16), vcb, preferred_element_type=F32))

        imp = jnp.dot(psum, cover, preferred_element_type=F32, precision=lax.Precision.HIGHEST)
        imp = jnp.where(forced, FORCED_SCORE, imp)
        imp = jnp.where(lane <= cur, imp, -jnp.inf)
        not_sel = (1.0 - _top_k_mask(imp, min(SEL_TOPK, n_blk))).astype(BF16)

        def sel_chunk(c, states):
            off = pl.multiple_of(c * tk, tk)
            k = kv_ref[0, pl.ds(off, tk), 2 * LANES:3 * LANES]
            v = kv_ref[0, pl.ds(off, tk), 3 * LANES:4 * LANES]
            expand = jnp.where(blk_delta == c * (tk // SEL_BLOCK), NEG, 0.0).astype(BF16)
            mask_add = jnp.dot(not_sel, expand, preferred_element_type=F32)
            out = []
            for r in range(NSA_GROUP):
                h = g * NSA_GROUP + r
                bias = jnp.concatenate(
                    [tsel_ref[h, jnp.clip(i - (c * sub + u), -1, n_sel_delta - 1) + 1] for u in range(sub)], axis=1)
                out.append(_online_update(states[r], _nt_dot(qs[r], k) + bias + mask_add, v))
            return tuple(out)

        n_chunks = (t0 + tq - 1) // tk + 1
        sel_states = lax.fori_loop(0, n_chunks, sel_chunk, tuple(_init_state(tq) for _ in range(NSA_GROUP)))

        def win_tile(u, states):
            off = pl.multiple_of((i - u) * TILE, TILE)
            k = kv_ref[0, pl.ds(off, TILE), 4 * LANES:5 * LANES]
            v = kv_ref[0, pl.ds(off, TILE), 5 * LANES:6 * LANES]
            out = []
            for r in range(NSA_GROUP):
                h = g * NSA_GROUP + r
                out.append(_online_update(states[r], _nt_dot(qs[r], k) + twin_ref[h, u + 1], v))
            return tuple(out)

        win_states = lax.fori_loop(0, jnp.minimum(i, n_win_tiles - 1) + 1, win_tile,
                                   tuple(_init_state(tq) for _ in range(NSA_GROUP)))

        for r in range(NSA_GROUP):
            h = g * NSA_GROUP + r
            gc, gs, gw = (jnp.sum(jnp.where(lane == 3 * h + b, gates, 0.0), axis=1, keepdims=True) for b in range(3))
            _, l_s, acc_s = sel_states[r]
            _, l_w, acc_w = win_states[r]
            o = gc * o_cmp[r] + gs * (acc_s / l_s) + gw * (acc_w / l_w)
            if h % 2 != g:
                o = pltpu.roll(o, HEAD_DIM, 1)
            prev = pair_out[h // 2]
            keep = lo if h % 2 == 0 else jnp.logical_not(lo)
            pair_out[h // 2] = jnp.where(keep, o, 0.0 if prev is None else prev)

    o_ref[0] = jnp.concatenate(pair_out, axis=1).astype(o_ref.dtype)


def _nsa_prompt(qn, kc, vc, bcmp, nsabf, tsel, twin, small):
    bx, s, _ = qn.shape
    tq = TILE
    tk = _row_tile(s, 512)
    n_cmp_pad = kc.shape[1]
    return pl.pallas_call(
        functools.partial(_nsa_kernel, tq=tq, tk=tk, n_sel_delta=tsel.shape[1] - 1, n_win_tiles=twin.shape[1] - 1),
        grid=(bx, s // tq),
        in_specs=[pl.BlockSpec((1, tq, NSA_HEADS * HEAD_DIM), lambda b, i: (b, i, 0)),
                  pl.BlockSpec((1, n_cmp_pad, LANES), lambda b, i: (b, 0, 0)),
                  pl.BlockSpec((1, n_cmp_pad, LANES), lambda b, i: (b, 0, 0)),
                  pl.BlockSpec((NSA_HEADS, tq, n_cmp_pad), lambda b, i: (0, i, 0)),
                  pl.BlockSpec((1, s, 6 * LANES), lambda b, i: (b, 0, 0)),
                  pl.BlockSpec(tsel.shape, lambda b, i: (0, 0, 0, 0)),
                  pl.BlockSpec(twin.shape, lambda b, i: (0, 0, 0, 0)),
                  pl.BlockSpec((1, tq, LANES), lambda b, i: (b, i, 0))],
        out_specs=pl.BlockSpec((1, tq, NSA_HEADS * HEAD_DIM), lambda b, i: (b, i, 0)),
        out_shape=jax.ShapeDtypeStruct((bx, s, NSA_HEADS * HEAD_DIM), BF16),
        compiler_params=_cparams("arbitrary", "arbitrary"),
        name="nsa_prompt",
    )(qn, kc, vc, bcmp, nsabf, tsel, twin, small)


def _t5_bucket(dist):
    dist = jnp.maximum(dist, 0)
    df = jnp.maximum(dist, 1).astype(F32)
    ratio = math.log(BUCKET_MAX_DIST / BUCKET_EXACT)
    log_b = BUCKET_EXACT + (jnp.log(df / BUCKET_EXACT) / ratio * (N_BUCKETS - BUCKET_EXACT)).astype(jnp.int32)
    return jnp.where(dist < BUCKET_EXACT, dist, jnp.clip(log_b, BUCKET_EXACT, N_BUCKETS - 1))


def _masked_softmax(logits, mask):
    logits = jnp.where(mask, logits, -jnp.inf)
    m = jnp.max(logits, axis=-1, keepdims=True)
    m = jnp.where(jnp.isfinite(m), m, 0.0)
    e = jnp.exp(logits - m)
    s = jnp.maximum(jnp.sum(e, axis=-1, keepdims=True), TINY)
    return e / s, (jnp.log(s) + m)[..., 0]


def _gather_pages(pool, page_table):
    rows = pool[page_table]
    return rows.reshape(page_table.shape[0], page_table.shape[1] * pool.shape[1], *pool.shape[2:])


def _nsa_sample_attend(q, t, gates, kc, vc, ks, vs, kw, vw, w_pos, rel_bias):
    b_, t_, g_, r_, d_ = q.shape
    h_ = g_ * r_
    table = rel_bias[:, :h_].astype(F32)
    n_cmp = kc.shape[1]
    cmp_end = jnp.arange(n_cmp) * CMP_STRIDE + (CMP_LEN - 1)
    d_c = t[:, None] - cmp_end[None, :]
    b_c = jnp.transpose(table[_t5_bucket(d_c)].reshape(t_, n_cmp, g_, r_), (0, 2, 3, 1))
    s_c = jnp.einsum('btgrd,bngd->btgrn', q, kc) + b_c
    p_c, _ = _masked_softmax(s_c, (d_c >= 0)[None, :, None, None, :])
    o_c = jnp.einsum('btgrn,bngd->btgrd', p_c, vc)
    n_sel = ks.shape[1] // SEL_BLOCK
    c0 = jnp.arange(n_cmp) * CMP_STRIDE
    s0 = jnp.arange(n_sel) * SEL_BLOCK
    cover = ((c0[:, None] < s0[None, :] + SEL_BLOCK) & (c0[:, None] + CMP_LEN > s0[None, :])).astype(F32)
    imp = jnp.einsum('btgn,ns->btgs', p_c.sum(axis=3), cover, precision=lax.Precision.HIGHEST)
    blk = jnp.arange(n_sel)[None, :]
    cur = (t // SEL_BLOCK)[:, None]
    forced = (blk == 0) | (blk == cur) | (blk == cur - 1)
    imp = jnp.where(forced[None, :, None, :], FORCED_SCORE, imp)
    imp = jnp.where((blk <= cur)[None, :, None, :], imp, -jnp.inf)
    n_top = min(SEL_TOPK, n_sel)
    top_val, top_idx = lax.top_k(imp, n_top)
    pos_s = (top_idx[..., None] * SEL_BLOCK + jnp.arange(SEL_BLOCK)).reshape(b_, t_, g_, n_top * SEL_BLOCK)
    ok_s = jnp.repeat(top_val > -jnp.inf, SEL_BLOCK, axis=-1) & (pos_s <= t[None, :, None, None])
    bi = jnp.arange(b_)[:, None, None, None]
    gi = jnp.arange(g_)[None, None, :, None]
    k_sel = ks[bi, pos_s, gi]
    v_sel = vs[bi, pos_s, gi]
    table_g = jnp.transpose(table.reshape(N_BUCKETS, g_, r_), (1, 0, 2))
    b_s = jnp.moveaxis(table_g[gi, _t5_bucket(t[None, :, None, None] - pos_s)], -1, 3)
    s_s = jnp.einsum('btgrd,btgkd->btgrk', q, k_sel) + b_s
    p_s, _ = _masked_softmax(s_s, ok_s[:, :, :, None, :])
    o_s = jnp.einsum('btgrk,btgkd->btgrd', p_s, v_sel)
    d_w = t[:, None] - w_pos[None, :]
    ok_w = (d_w >= 0) & (d_w < NSA_WINDOW) & (w_pos >= 0)[None, :]
    b_w = jnp.transpose(table[_t5_bucket(d_w)].reshape(t_, w_pos.shape[0], g_, r_), (0, 2, 3, 1))
    s_w = jnp.einsum('btgrd,blgd->btgrl', q, kw) + b_w
    p_w, _ = _masked_softmax(s_w, ok_w[None, :, None, None, :])
    o_w = jnp.einsum('btgrl,blgd->btgrd', p_w, vw)
    o = gates[..., 0:1] * o_c + gates[..., 1:2] * o_s + gates[..., 2:3] * o_w
    return o.reshape(b_, t_, h_ * d_)


def _even_sample_attention(qn, nsa4, win, qf, fkv, small, kc, vc, cache_nsa, cache_fox, cache_logf, win_buf,
                           page_table, rel_bias):
    bd = qn.shape[1]
    q_n = qn[0].astype(F32).reshape(bd, 1, NSA_KV_HEADS, NSA_GROUP, HEAD_DIM)
    kv_n = nsa4[0].reshape(bd, 1, 4, NSA_KV_HEADS, HEAD_DIM)
    win_new = win[0].reshape(bd, 1, 2, NSA_KV_HEADS, HEAD_DIM)
    gates = small[0, :, :SMALL_GATES].reshape(bd, 1, NSA_KV_HEADS, NSA_GROUP, 3)
    logf = small[0, :, SMALL_GATES:SMALL_GATES + FOX_HEADS].reshape(bd, 1, FOX_HEADS)
    q_f = qf[0].astype(F32).reshape(bd, 1, FOX_HEADS, HEAD_DIM)
    kv_f = fkv[0].reshape(bd, 1, 2, FOX_HEADS, HEAD_DIM)
    past_len = page_table.shape[1] * PAGE_SIZE
    t = past_len + jnp.arange(1)
    nsa_rows = jnp.concatenate([_gather_pages(cache_nsa, page_table)[:, :, 2:4], kv_n[:, :, 2:4]], axis=1)
    length = nsa_rows.shape[1]
    n_cmp = (length - CMP_LEN) // CMP_STRIDE + 1
    sel = jnp.pad(nsa_rows, ((0, 0), (0, (-length) % SEL_BLOCK), (0, 0), (0, 0), (0, 0)))
    win_rows = jnp.concatenate([win_buf, win_new], axis=1)
    n_buf = win_buf.shape[1]
    w_pos = past_len - n_buf + jnp.arange(n_buf + 1)
    kc4 = kc[:, :n_cmp].reshape(bd, n_cmp, NSA_KV_HEADS, HEAD_DIM)
    vc4 = vc[:, :n_cmp].reshape(bd, n_cmp, NSA_KV_HEADS, HEAD_DIM)
    o_n = _nsa_sample_attend(q_n, t, gates, kc4, vc4, sel[:, :, 0], sel[:, :, 1], win_rows[:, :, 0],
                             win_rows[:, :, 1], w_pos, rel_bias)
    fox_rows = jnp.concatenate([_gather_pages(cache_fox, page_table), kv_f], axis=1)
    logf_all = jnp.concatenate([_gather_pages(cache_logf, page_table).astype(F32), logf], axis=1)
    c = lax.cumsum(logf_all, axis=1)
    s = jnp.einsum('bthd,blhd->bhtl', q_f, fox_rows[:, :, 0])
    s = s + jnp.swapaxes(c[:, past_len:], 1, 2)[..., None] - jnp.swapaxes(c, 1, 2)[:, :, None, :]
    p, _ = _masked_softmax(s, (jnp.arange(length)[None, :] <= t[:, None])[None, None])
    o_f = jnp.einsum('bhtl,blhd->bthd', p, fox_rows[:, :, 1]).reshape(bd, 1, -1)
    return o_n.reshape(1, bd, -1).astype(BF16), o_f.reshape(1, bd, -1).astype(BF16)


def _odd_sample_attention(q, kv, buf, rel_bias):
    bd = q.shape[1]
    qh = q[0].astype(F32).reshape(bd, 1, DIL_HEADS, HEAD_DIM)
    rows = jnp.concatenate([buf, kv[0].reshape(bd, 1, 2, DIL_HEADS, HEAD_DIM)], axis=1)
    n_buf = buf.shape[1]
    outs, lses = [], []
    for window, dilation in DIL_BRANCHES:
        nk = window // dilation
        dist = jnp.arange(nk + 1) * dilation
        idx = (n_buf + jnp.arange(1))[:, None] - dist[None, :]
        ok = idx >= 0
        idx = jnp.maximum(idx, 0)
        kg = rows[:, :, 0][:, idx]
        vg = rows[:, :, 1][:, idx]
        bias = rel_bias[:, :DIL_HEADS].astype(F32)[_t5_bucket(dist)].T
        s = jnp.einsum('bthd,btmhd->bhtm', qh, kg) + bias[:, None, :]
        p, lse = _masked_softmax(s, ok[None, None])
        outs.append(jnp.einsum('bhtm,btmhd->bthd', p, vg))
        lses.append(jnp.swapaxes(lse, 1, 2))
    alpha = jax.nn.softmax(jnp.stack(lses, axis=-1), axis=-1)
    o = jnp.einsum('bthn,nbthd->bthd', alpha, jnp.stack(outs)).reshape(1, bd, -1).astype(BF16)
    return o, o


def kernel(x_prompt, x_sample, cache_nsa_kv, cache_fox_kv, cache_fox_logf, state_nsa_win_kv, state_dil_kv, page_table,
           c_prompt, c_sample, rel_bias, norm_g, w_ada, b_ada, w_in_a, nsa_gate_b, fox_f_b, nsa_cmp_w1, nsa_cmp_w2,
           nsa_cmp_pe, w_out_a, w_in_c, w_out_c, w_mlp1, w_mlp2):
    bp, s, d = x_prompt.shape
    bd = x_sample.shape[0]
    depth = w_ada.shape[0]
    n_pages = s // PAGE_SIZE
    n_cmp_pad = s // CMP_STRIDE

    mods = _ada_params(jnp.concatenate([c_prompt, c_sample], axis=0), w_ada, b_ada).reshape(depth, bp + bd, 6, d)
    tab_sel = _toeplitz_table(rel_bias, NSA_HEADS, 15, "causal")
    tab_win = _toeplitz_table(rel_bias, NSA_HEADS, NSA_WINDOW // TILE + 2, "window")
    tab_dil = _toeplitz_table(rel_bias, DIL_HEADS, DIL_BRANCHES[-1][0] // TILE + 2, "dilated")
    bcmp = _cmp_bias_table(rel_bias, s, TILE, n_cmp_pad, 0)

    xp = x_prompt
    xs = x_sample.reshape(1, bd, d)
    prompt_pages = jnp.arange(bp * n_pages, dtype=jnp.int32).reshape(bp, n_pages)
    nsa_p, nsa_s, fkv_p, fkv_s, lf_p, lf_s, win_p, win_s, dil_p, dil_s = [], [], [], [], [], [], [], [], [], []
    for layer in range(depth):
        mp = [mods[layer, :bp, k].reshape(bp, 1, d) for k in range(6)]
        ms = [mods[layer, bp:, k].reshape(1, bd, d) for k in range(6)]
        g = [norm_g[layer, k].reshape(1, d) for k in range(4)]
        i = layer // 2
        if layer % 2 == 0:
            wa = w_in_a[i]
            w_in = jnp.concatenate([wa[:, 0:1280], wa[:, 1304:2840], wa[:, 1280:1304], wa[:, 2840:2848],
                                    jnp.zeros((d, LANES - SMALL_GATES - FOX_HEADS), F32)], axis=1).astype(BF16)
            sb = jnp.concatenate([nsa_gate_b[i].reshape(-1), fox_f_b[i],
                                  jnp.zeros((LANES - SMALL_GATES - FOX_HEADS,), F32)]).reshape(1, LANES)
            half = CMP_STRIDE * HEAD_DIM
            w1cat = jnp.concatenate([nsa_cmp_w1[i][:, :half], nsa_cmp_w1[i][:, half:]], axis=2).astype(BF16)
            w2 = nsa_cmp_w2[i].astype(BF16)
            pe = nsa_cmp_pe[i].reshape(2, 1, CMP_LEN * HEAD_DIM)
            w_out = w_out_a[i].astype(BF16)

            qn, nsa4, nsabf, win, qf, fkv, fkvbf, small = _proj_even(xp, mp[0], mp[1], g[0], w_in, sb)
            c, ct = _cumsum(small)
            o_f = _fox_prompt(qf, fkvbf, c, ct)
            kc, vc = _compress(nsa4.reshape(bp * n_pages, PAGE_SIZE, 512), prompt_pages, w1cat, w2, pe)
            o_n = _nsa_prompt(qn, kc, vc, bcmp, nsabf, tab_sel, tab_win, small)
            op_a, op_b = o_n, o_f
            nsa_p.append(nsa4.reshape(bp, s, 4, NSA_KV_HEADS, HEAD_DIM))
            fkv_p.append(fkv.reshape(bp, s, 2, FOX_HEADS, HEAD_DIM))
            lf_p.append(small[:, :, SMALL_GATES:SMALL_GATES + FOX_HEADS])
            n_win = min(NSA_WINDOW, s)
            win_p.append(win[:, s - n_win:].reshape(bp, n_win, 2, NSA_KV_HEADS, HEAD_DIM))

            qn_s, nsa4_s, _, win_new, qf_s, fkv_s_, _, small_s = _proj_even(xs, ms[0], ms[1], g[0], w_in, sb)
            pool = cache_nsa_kv[i].reshape(cache_nsa_kv.shape[1], PAGE_SIZE, 512)
            kc_s, vc_s = _compress(pool, page_table, w1cat, w2, pe)
            os_a, os_b = _even_sample_attention(qn_s, nsa4_s, win_new, qf_s, fkv_s_, small_s, kc_s, vc_s,
                                                cache_nsa_kv[i], cache_fox_kv[i], cache_fox_logf[i],
                                                state_nsa_win_kv[i], page_table, rel_bias)
            nsa_s.append(nsa4_s.reshape(bd, 1, 4, NSA_KV_HEADS, HEAD_DIM))
            fkv_s.append(fkv_s_.reshape(bd, 1, 2, FOX_HEADS, HEAD_DIM))
            lf_s.append(small_s[0, :, SMALL_GATES:SMALL_GATES + FOX_HEADS].reshape(bd, 1, FOX_HEADS))
            win_s.append(win_new.reshape(bd, 1, 2, NSA_KV_HEADS, HEAD_DIM))
        else:
            w_in = w_in_c[i].astype(BF16)
            w_out = w_out_c[i].astype(BF16)
            q, kv, kvbf = _proj_odd(xp, mp[0], mp[1], g[0], w_in)
            op_a = op_b = _band_prompt(q, kvbf, tab_dil, DIL_HEADS)
            n_dil = min(DIL_BRANCHES[-1][0], s)
            dil_p.append(kv[:, s - n_dil:].reshape(bp, n_dil, 2, DIL_HEADS, HEAD_DIM))
            q_s, kv_s, _ = _proj_odd(xs, ms[0], ms[1], g[0], w_in)
            os_a, os_b = _odd_sample_attention(q_s, kv_s, state_dil_kv[i], rel_bias)
            dil_s.append(kv_s.reshape(bd, 1, 2, DIL_HEADS, HEAD_DIM))
        col_b = 0 if layer % 2 == 0 else 1
        xp = _post(op_a, op_b, w_out, xp, mp[2], g[1], 0, col_b)
        xs = _post(os_a, os_b, w_out, xs, ms[2], g[1], 0, col_b)
        w1 = w_mlp1[layer].astype(BF16)
        w2m = w_mlp2[layer].astype(BF16)
        xp = _mlp(xp, mp[3], mp[4], mp[5], g[2], g[3], w1, w2m)
        xs = _mlp(xs, ms[3], ms[4], ms[5], g[2], g[3], w1, w2m)
    return (xp, xs.reshape(bd, 1, d), jnp.stack(nsa_p), jnp.stack(nsa_s), jnp.stack(fkv_p), jnp.stack(fkv_s),
            jnp.stack(lf_p), jnp.stack(lf_s), jnp.stack(win_p), jnp.stack(win_s), jnp.stack(dil_p), jnp.stack(dil_s))
```

```python
import functools
import math

import numpy as np
import jax
import jax.numpy as jnp
from jax import lax
from jax.experimental import pallas as pl
from jax.experimental.pallas import tpu as pltpu

F32 = jnp.float32
BF16 = jnp.bfloat16

HEAD_DIM = 64
NSA_HEADS = 8
NSA_KV_HEADS = 2
NSA_GROUP = NSA_HEADS // NSA_KV_HEADS
FOX_HEADS = 8
DIL_HEADS = 16
CMP_LEN = 32
CMP_STRIDE = 16
CMP_HIDDEN = 4 * HEAD_DIM
SEL_BLOCK = 64
SEL_TOPK = 16
NSA_WINDOW = 512
FORCED_SCORE = 1e9
DIL_BRANCHES = ((128, 1), (512, 4), (2048, 16))
N_BUCKETS = 32
BUCKET_EXACT = 16
BUCKET_MAX_DIST = 2048
NORM_EPS = 1e-6
TINY = 1e-30
PAGE_SIZE = 128

LANES = 128
VMEM_LIMIT_BYTES = 56 * 1024 * 1024

NEG = -1e30
QK_SCALE = HEAD_DIM ** -0.5
TILE = 128
SEL_SHIFT = 6
SMALL_GATES = 3 * NSA_HEADS


def _cparams(*sem):
    return pltpu.CompilerParams(dimension_semantics=sem, vmem_limit_bytes=VMEM_LIMIT_BYTES)


def _bucket_thresholds():
    d = np.arange(0, 2 * BUCKET_MAX_DIST + 1)
    df = np.maximum(d, 1).astype(np.float64)
    ratio = math.log(BUCKET_MAX_DIST / BUCKET_EXACT)
    log_b = BUCKET_EXACT + (np.log(df / BUCKET_EXACT) / ratio * (N_BUCKETS - BUCKET_EXACT)).astype(np.int64)
    bucket = np.where(d < BUCKET_EXACT, d, np.clip(log_b, BUCKET_EXACT, N_BUCKETS - 1))
    return [int(np.argmax(bucket >= b)) for b in range(1, N_BUCKETS)]


BUCKET_THR = _bucket_thresholds()


def _bias_of_distance(d, tab_ref, h):
    val = jnp.full(d.shape, tab_ref[0, h], F32)
    for b in range(1, N_BUCKETS):
        val = jnp.where(d >= BUCKET_THR[b - 1], tab_ref[b, h], val)
    return val


def _toeplitz_kernel(tab_ref, o_ref, *, n_heads, mode, dil):
    r = lax.broadcasted_iota(jnp.int32, (TILE, TILE), 0)
    c = lax.broadcasted_iota(jnp.int32, (TILE, TILE), 1)
    idx = pl.program_id(0)
    if mode == "branch":
        d = idx * TILE + r - c
        ok = (idx < 2) & (d >= 0) & (d <= TILE)
    else:
        d = (idx - 1) * TILE + r - c
        ok = (idx > 0) & (d >= 0)
        if mode == "window":
            ok = ok & (d < NSA_WINDOW)
    dd = jnp.maximum(d, 0) * dil
    for h in range(n_heads):
        o_ref[h, 0] = jnp.where(ok, _bias_of_distance(dd, tab_ref, h), NEG)


def _toeplitz_table(rel_bias, n_heads, n_idx, mode, dil=1):
    return pl.pallas_call(
        functools.partial(_toeplitz_kernel, n_heads=n_heads, mode=mode, dil=dil),
        grid=(n_idx,),
        in_specs=[pl.BlockSpec(memory_space=pltpu.SMEM)],
        out_specs=pl.BlockSpec((n_heads, 1, TILE, TILE), lambda i: (0, i, 0, 0)),
        out_shape=jax.ShapeDtypeStruct((n_heads, n_idx, TILE, TILE), F32),
        compiler_params=_cparams("arbitrary"),
        name="bias_toeplitz_%s%d" % (mode, dil),
    )(rel_bias)


def _cmp_bias_kernel(tab_ref, o_ref, *, tq, n_cmp_pad, t_base):
    t = t_base + pl.program_id(0) * tq + lax.broadcasted_iota(jnp.int32, (tq, n_cmp_pad), 0)
    n = lax.broadcasted_iota(jnp.int32, (tq, n_cmp_pad), 1)
    d = jnp.maximum(t - (n * CMP_STRIDE + CMP_LEN - 1), 0)
    for h in range(NSA_HEADS):
        o_ref[h] = _bias_of_distance(d, tab_ref, h)


def _cmp_bias_table(rel_bias, n_rows, tq, n_cmp_pad, t_base):
    return pl.pallas_call(
        functools.partial(_cmp_bias_kernel, tq=tq, n_cmp_pad=n_cmp_pad, t_base=t_base),
        grid=(n_rows // tq,),
        in_specs=[pl.BlockSpec(memory_space=pltpu.SMEM)],
        out_specs=pl.BlockSpec((NSA_HEADS, tq, n_cmp_pad), lambda i: (0, i, 0)),
        out_shape=jax.ShapeDtypeStruct((NSA_HEADS, n_rows, n_cmp_pad), F32),
        compiler_params=_cparams("arbitrary"),
        name="bias_cmp",
    )(rel_bias)


def _affine_bias_kernel(tab_ref, o_ref, *, heads, d0, step, limit):
    n = o_ref.shape[1]
    d = d0 + step * lax.broadcasted_iota(jnp.int32, (1, n), 1)
    ok = (d >= 0) & (d < limit)
    dd = jnp.maximum(d, 0)
    for row, h in enumerate(heads):
        if h is None:
            o_ref[row:row + 1, :] = jnp.zeros((1, n), F32)
        else:
            o_ref[row:row + 1, :] = jnp.where(ok, _bias_of_distance(dd, tab_ref, h), NEG)


def _affine_bias(rel_bias, heads, n, d0, step, limit):
    return pl.pallas_call(
        functools.partial(_affine_bias_kernel, heads=tuple(heads), d0=d0, step=step, limit=limit),
        in_specs=[pl.BlockSpec(memory_space=pltpu.SMEM)],
        out_shape=jax.ShapeDtypeStruct((len(heads), n), F32),
        compiler_params=pltpu.CompilerParams(vmem_limit_bytes=VMEM_LIMIT_BYTES),
        name="bias_affine",
    )(rel_bias)


NSA_GROUPED_ROWS = tuple((NSA_GROUP * (row // 8) + row % 8) if row % 8 < NSA_GROUP else None
                         for row in range(8 * NSA_KV_HEADS))


def _ada_kernel(c_ref, w_ref, b_ref, o_ref):
    c = c_ref[...]
    s = (c * jax.nn.sigmoid(c)).astype(BF16)
    o_ref[0] = jnp.dot(s, w_ref[0].astype(BF16), preferred_element_type=F32) + b_ref[0]


def _ada_params(c_all, w_ada, b_ada):
    depth, d, d6 = w_ada.shape
    m = c_all.shape[0]
    return pl.pallas_call(
        _ada_kernel,
        grid=(depth, d6 // d),
        in_specs=[pl.BlockSpec((m, d), lambda l, j: (0, 0)),
                  pl.BlockSpec((1, d, d), lambda l, j: (l, 0, j)),
                  pl.BlockSpec((1, 1, d), lambda l, j: (l, 0, j))],
        out_specs=pl.BlockSpec((1, m, d), lambda l, j: (l, 0, j)),
        out_shape=jax.ShapeDtypeStruct((depth, m, d6), F32),
        compiler_params=_cparams("arbitrary", "arbitrary"),
        name="adaln",
    )(c_all, w_ada, b_ada.reshape(depth, 1, d6))


def _norm_mod(x, g, scale, shift):
    y = x * lax.rsqrt(jnp.mean(x * x, axis=-1, keepdims=True) + NORM_EPS)
    return (y * g) * (1.0 + scale) + shift


def _row_tile(s, want):
    return want if s % want == 0 else s


def _mod_spec(mod, tm):
    if mod.shape[1] == 1:
        return pl.BlockSpec((1, 1, mod.shape[2]), lambda b, i: (b, 0, 0))
    return pl.BlockSpec((1, tm, mod.shape[2]), lambda b, i: (b, i, 0))


def _proj_even_kernel(x_ref, sh_ref, sc_ref, g_ref, w_ref, sb_ref,
                      qn_ref, nsa4_ref, nsabf_ref, win_ref, qf_ref, fkv_ref, fkvbf_ref, small_ref):
    h = _norm_mod(x_ref[0], g_ref[...], sc_ref[0], sh_ref[0]).astype(BF16)
    z = jnp.dot(h, w_ref[...], preferred_element_type=F32)
    qn_ref[0] = (z[:, 0:512] * QK_SCALE).astype(BF16)
    nsa4_ref[0] = z[:, 512:1024]
    nsabf_ref[0] = z[:, 512:1280].astype(BF16)
    win_ref[0] = z[:, 1024:1280]
    qf_ref[0] = (z[:, 1280:1792] * QK_SCALE).astype(BF16)
    fkv_ref[0] = z[:, 1792:2816]
    fkvbf_ref[0] = z[:, 1792:2816].astype(BF16)
    zs = z[:, 2816:2944] + sb_ref[...]
    lane = lax.broadcasted_iota(jnp.int32, zs.shape, 1)
    sig = jax.nn.sigmoid(zs)
    lsg = jnp.minimum(zs, 0.0) - jnp.log1p(jnp.exp(-jnp.abs(zs)))
    small_ref[0] = jnp.where(lane < SMALL_GATES, sig, lsg)


def _proj_even(x, shift, scale, g, w, sb):
    bx, s, d = x.shape
    tm = _row_tile(s, 256)
    n = w.shape[1]
    widths = (512, 512, 768, 256, 512, 1024, 1024, 128)
    dtypes = (BF16, F32, BF16, F32, BF16, F32, BF16, F32)
    return pl.pallas_call(
        _proj_even_kernel,
        grid=(bx, s // tm),
        in_specs=[pl.BlockSpec((1, tm, d), lambda b, i: (b, i, 0)),
                  _mod_spec(shift, tm), _mod_spec(scale, tm),
                  pl.BlockSpec((1, d), lambda b, i: (0, 0)),
                  pl.BlockSpec((d, n), lambda b, i: (0, 0)),
                  pl.BlockSpec((1, LANES), lambda b, i: (0, 0))],
        out_specs=[pl.BlockSpec((1, tm, wd), lambda b, i: (b, i, 0)) for wd in widths],
        out_shape=[jax.ShapeDtypeStruct((bx, s, wd), dt) for wd, dt in zip(widths, dtypes)],
        compiler_params=_cparams("arbitrary", "arbitrary"),
        name="proj_even",
    )(x, shift, scale, g, w, sb)


def _proj_odd_kernel(x_ref, sh_ref, sc_ref, g_ref, w_ref, q_ref, kv_ref, kvbf_ref):
    h = _norm_mod(x_ref[0], g_ref[...], sc_ref[0], sh_ref[0]).astype(BF16)
    z = jnp.dot(h, w_ref[...], preferred_element_type=F32)
    q_ref[0] = (z[:, 0:1024] * QK_SCALE).astype(BF16)
    kv_ref[0] = z[:, 1024:3072]
    kvbf_ref[0] = z[:, 1024:3072].astype(BF16)


def _proj_odd(x, shift, scale, g, w):
    bx, s, d = x.shape
    tm = _row_tile(s, 256)
    n = w.shape[1]
    widths = (1024, 2048, 2048)
    dtypes = (BF16, F32, BF16)
    return pl.pallas_call(
        _proj_odd_kernel,
        grid=(bx, s // tm),
        in_specs=[pl.BlockSpec((1, tm, d), lambda b, i: (b, i, 0)),
                  _mod_spec(shift, tm), _mod_spec(scale, tm),
                  pl.BlockSpec((1, d), lambda b, i: (0, 0)),
                  pl.BlockSpec((d, n), lambda b, i: (0, 0))],
        out_specs=[pl.BlockSpec((1, tm, wd), lambda b, i: (b, i, 0)) for wd in widths],
        out_shape=[jax.ShapeDtypeStruct((bx, s, wd), dt) for wd, dt in zip(widths, dtypes)],
        compiler_params=_cparams("arbitrary", "arbitrary"),
        name="proj_odd",
    )(x, shift, scale, g, w)


def _post_kernel(oa_ref, ob_ref, w_ref, x_ref, gate_ref, g_ref, o_ref):
    half = oa_ref.shape[2]
    y = jnp.dot(oa_ref[0], w_ref[0:half, :], preferred_element_type=F32)
    y = y + jnp.dot(ob_ref[0], w_ref[half:, :], preferred_element_type=F32)
    yn = y * lax.rsqrt(jnp.mean(y * y, axis=-1, keepdims=True) + NORM_EPS) * g_ref[...]
    o_ref[0] = x_ref[0] + gate_ref[0] * yn


def _post(o_a, o_b, w_out, x, gate, g, col_a=0, col_b=0):
    bx, s, d = x.shape
    tm = _row_tile(s, 512)
    half = w_out.shape[0] // 2
    return pl.pallas_call(
        _post_kernel,
        grid=(bx, s // tm),
        in_specs=[pl.BlockSpec((1, tm, half), lambda b, i: (b, i, col_a)),
                  pl.BlockSpec((1, tm, half), lambda b, i: (b, i, col_b)),
                  pl.BlockSpec(w_out.shape, lambda b, i: (0, 0)),
                  pl.BlockSpec((1, tm, d), lambda b, i: (b, i, 0)),
                  _mod_spec(gate, tm),
                  pl.BlockSpec((1, d), lambda b, i: (0, 0))],
        out_specs=pl.BlockSpec((1, tm, d), lambda b, i: (b, i, 0)),
        out_shape=jax.ShapeDtypeStruct((bx, s, d), F32),
        compiler_params=_cparams("arbitrary", "arbitrary"),
        name="post",
    )(o_a, o_b, w_out, x, gate, g)


def _mlp_kernel(x_ref, sh_ref, sc_ref, gate_ref, g2_ref, g3_ref, w1_ref, w2_ref, o_ref, h_ref, acc_ref):
    j = pl.program_id(2)

    @pl.when(j == 0)
    def _():
        h_ref[...] = _norm_mod(x_ref[0], g2_ref[...], sc_ref[0], sh_ref[0]).astype(BF16)
        acc_ref[...] = jnp.zeros_like(acc_ref)

    a = jnp.maximum(jnp.dot(h_ref[...], w1_ref[...], preferred_element_type=F32), 0.0)
    acc_ref[...] += jnp.dot((a * a).astype(BF16), w2_ref[...], preferred_element_type=F32)

    @pl.when(j == pl.num_programs(2) - 1)
    def _():
        y = acc_ref[...]
        yn = y * lax.rsqrt(jnp.mean(y * y, axis=-1, keepdims=True) + NORM_EPS) * g3_ref[...]
        o_ref[0] = x_ref[0] + gate_ref[0] * yn


def _mlp(x, shift, scale, gate, g2, g3, w1, w2):
    bx, s, d = x.shape
    f = w1.shape[1]
    tm = _row_tile(s, 1024)
    tf = 1024

    def mod3(mod):
        if mod.shape[1] == 1:
            return pl.BlockSpec((1, 1, d), lambda b, i, j: (b, 0, 0))
        return pl.BlockSpec((1, tm, d), lambda b, i, j: (b, i, 0))

    return pl.pallas_call(
        _mlp_kernel,
        grid=(bx, s // tm, f // tf),
        in_specs=[pl.BlockSpec((1, tm, d), lambda b, i, j: (b, i, 0)),
                  mod3(shift), mod3(scale), mod3(gate),
                  pl.BlockSpec((1, d), lambda b, i, j: (0, 0)),
                  pl.BlockSpec((1, d), lambda b, i, j: (0, 0)),
                  pl.BlockSpec((d, tf), lambda b, i, j: (0, j)),
                  pl.BlockSpec((tf, d), lambda b, i, j: (j, 0))],
        out_specs=pl.BlockSpec((1, tm, d), lambda b, i, j: (b, i, 0)),
        out_shape=jax.ShapeDtypeStruct((bx, s, d), F32),
        scratch_shapes=[pltpu.VMEM((tm, d), BF16), pltpu.VMEM((tm, d), F32)],
        compiler_params=_cparams("arbitrary", "arbitrary", "arbitrary"),
        name="mlp",
    )(x, shift, scale, gate, g2, g3, w1, w2)


def _cumsum_kernel(x_ref, c_ref, ct_ref, carry_ref, *, tc):
    @pl.when(pl.program_id(1) == 0)
    def _():
        carry_ref[...] = jnp.zeros_like(carry_ref)

    r = lax.broadcasted_iota(jnp.int32, (tc, tc), 0)
    c = lax.broadcasted_iota(jnp.int32, (tc, tc), 1)
    tri = jnp.where(c <= r, 1.0, 0.0).astype(F32)
    cs = jnp.dot(tri, x_ref[0], preferred_element_type=F32, precision=lax.Precision.HIGHEST) + carry_ref[...]
    carry_ref[...] = cs[tc - 1:tc, :]
    c_ref[0] = cs
    ct_ref[0] = cs.T[SMALL_GATES:SMALL_GATES + FOX_HEADS, :]


def _cumsum(small):
    bx, s, _ = small.shape
    tc = _row_tile(s, 256)
    return pl.pallas_call(
        functools.partial(_cumsum_kernel, tc=tc),
        grid=(bx, s // tc),
        in_specs=[pl.BlockSpec((1, tc, LANES), lambda b, i: (b, i, 0))],
        out_specs=[pl.BlockSpec((1, tc, LANES), lambda b, i: (b, i, 0)),
                   pl.BlockSpec((1, FOX_HEADS, tc), lambda b, i: (b, 0, i))],
        out_shape=[jax.ShapeDtypeStruct((bx, s, LANES), F32),
                   jax.ShapeDtypeStruct((bx, FOX_HEADS, s), F32)],
        scratch_shapes=[pltpu.VMEM((1, LANES), F32)],
        compiler_params=_cparams("arbitrary", "arbitrary"),
        name="logf_cumsum",
    )(small)


def _nt_dot(a, b):
    return lax.dot_general(a, b, (((1,), (1,)), ((), ())), preferred_element_type=F32)


def _online_update(state, s, v):
    m, l, acc = state
    m_new = jnp.maximum(m, jnp.max(s, axis=1, keepdims=True))
    alpha = jnp.exp(m - m_new)
    p = jnp.exp(s - m_new)
    l = alpha * l + jnp.sum(p, axis=1, keepdims=True)
    acc = alpha * acc + jnp.dot(p.astype(BF16), v, preferred_element_type=F32)
    return m_new, l, acc


def _init_state(tq):
    return (jnp.full((tq, 1), NEG, F32), jnp.zeros((tq, 1), F32), jnp.zeros((tq, LANES), F32))


def _half_masks(tq):
    lane = lax.broadcasted_iota(jnp.int32, (tq, LANES), 1)
    return lane < HEAD_DIM


def _split_heads(q2, lo):
    zero = jnp.zeros_like(q2)
    return jnp.where(lo, q2, zero), jnp.where(lo, zero, q2)


def _fox_kernel(q_ref, k_ref, v_ref, c_ref, ct_ref, o_ref, *, tq, tk, n_pairs):
    grp = pl.program_id(1)
    i = pl.program_id(2)
    lo = _half_masks(tq)
    lane = lax.broadcasted_iota(jnp.int32, (tq, LANES), 1)
    cblk = c_ref[0]
    qs, cqs, heads = [], [], []
    for p in range(n_pairs):
        qs.extend(_split_heads(q_ref[0, :, p * LANES:(p + 1) * LANES], lo))
        for e in range(2):
            head = (grp * n_pairs + p) * 2 + e
            heads.append(head)
            cqs.append(jnp.sum(jnp.where(lane == SMALL_GATES + head, cblk, 0.0), axis=1, keepdims=True))

    def chunk(c, states, masked):
        off = pl.multiple_of(c * tk, tk)
        if masked:
            row = i * tq + lax.broadcasted_iota(jnp.int32, (tq, tk), 0)
            col = off + lax.broadcasted_iota(jnp.int32, (tq, tk), 1)
            ok = col <= row
        out = []
        for n in range(2 * n_pairs):
            p = n // 2
            k = k_ref[0, pl.ds(off, tk), p * LANES:(p + 1) * LANES]
            v = v_ref[0, pl.ds(off, tk), p * LANES:(p + 1) * LANES]
            ck = ct_ref[0, pl.ds(heads[n], 1), pl.ds(off, tk)]
            s = _nt_dot(qs[n], k) + cqs[n] - ck
            if masked:
                s = jnp.where(ok, s, NEG)
            out.append(_online_update(states[n], s, v))
        return tuple(out)

    n_full = (i * tq) // tk
    init = tuple(_init_state(tq) for _ in range(2 * n_pairs))
    states = lax.fori_loop(0, n_full, lambda c, st: chunk(c, st, False), init)
    states = chunk(n_full, states, True)
    outs = []
    for p in range(n_pairs):
        (_, la, acca), (_, lb, accb) = states[2 * p], states[2 * p + 1]
        outs.append(jnp.where(lo, acca / la, accb / lb))
    o_ref[0] = jnp.concatenate(outs, axis=1).astype(o_ref.dtype)


def _fox_prompt(qf, fkvbf, c, ct):
    bx, s, _ = qf.shape
    tq = _row_tile(s, 256)
    tk = _row_tile(s, 512)
    n_pairs = 2
    n_grp = FOX_HEADS // 2 // n_pairs
    wd = n_pairs * LANES
    return pl.pallas_call(
        functools.partial(_fox_kernel, tq=tq, tk=tk, n_pairs=n_pairs),
        grid=(bx, n_grp, s // tq),
        in_specs=[pl.BlockSpec((1, tq, wd), lambda b, p, i: (b, i, p)),
                  pl.BlockSpec((1, s, wd), lambda b, p, i: (b, 0, p)),
                  pl.BlockSpec((1, s, wd), lambda b, p, i: (b, 0, n_grp + p)),
                  pl.BlockSpec((1, tq, LANES), lambda b, p, i: (b, i, 0)),
                  pl.BlockSpec((1, FOX_HEADS, s), lambda b, p, i: (b, 0, 0))],
        out_specs=pl.BlockSpec((1, tq, wd), lambda b, p, i: (b, i, p)),
        out_shape=jax.ShapeDtypeStruct((bx, s, FOX_HEADS * HEAD_DIM), BF16),
        compiler_params=_cparams("arbitrary", "arbitrary", "arbitrary"),
        name="fox_prompt",
    )(qf, fkvbf, fkvbf, c, ct)


def _dil_branch_kernel(q_ref, kc_ref, kp_ref, vc_ref, vp_ref, tab_ref, o_ref, lse_ref, *, tq):
    i = pl.program_id(2)
    lo = _half_masks(TILE)
    lane = lax.broadcasted_iota(jnp.int32, (TILE, LANES), 1)
    first_prev = jnp.where(i == 0, 2, 1)
    for a in range(tq // TILE):
        rows = slice(a * TILE, (a + 1) * TILE)
        lse_tile = jnp.zeros((TILE, LANES), F32)
        outs = []
        for p in range(DIL_HEADS // 2):
            cols = slice(p * LANES, (p + 1) * LANES)
            if a == 0:
                k_prev, v_prev, prev_idx = kp_ref[0, :, cols], vp_ref[0, :, cols], first_prev
            else:
                prev = slice((a - 1) * TILE, a * TILE)
                k_prev, v_prev, prev_idx = kc_ref[0, prev, cols], vc_ref[0, prev, cols], 1
            k2 = jnp.concatenate([k_prev, kc_ref[0, rows, cols]], axis=0)
            v2 = jnp.concatenate([v_prev, vc_ref[0, rows, cols]], axis=0)
            pair = []
            for e, qh in enumerate(_split_heads(q_ref[0, rows, cols], lo)):
                h = 2 * p + e
                bias = jnp.concatenate([tab_ref[h, prev_idx], tab_ref[h, 0]], axis=1)
                s = _nt_dot(qh, k2) + bias
                m = jnp.max(s, axis=1, keepdims=True)
                e_s = jnp.exp(s - m)
                l = jnp.sum(e_s, axis=1, keepdims=True)
                pair.append(jnp.dot(e_s.astype(BF16), v2, preferred_element_type=F32) / l)
                lse_tile = jnp.where(lane == h, jnp.log(l) + m, lse_tile)
            outs.append(jnp.where(lo, pair[0], pair[1]))
        o_ref[0, rows, :] = jnp.concatenate(outs, axis=1).astype(o_ref.dtype)
        lse_ref[0, rows, :] = lse_tile


def _dil_branch_prompt(q, kvbf, table, dil):
    bx, s, width = q.shape
    n_rows = s // dil
    tq = _row_tile(n_rows, 256)
    sub = tq // TILE
    qv = q.reshape(bx, n_rows, dil * width)
    kvv = kvbf.reshape(bx, n_rows, dil * 2 * width)
    o, lse = pl.pallas_call(
        functools.partial(_dil_branch_kernel, tq=tq),
        grid=(bx, dil, n_rows // tq),
        in_specs=[pl.BlockSpec((1, tq, width), lambda b, r, i: (b, i, r)),
                  pl.BlockSpec((1, tq, width), lambda b, r, i: (b, i, 2 * r)),
                  pl.BlockSpec((1, TILE, width), lambda b, r, i: (b, jnp.maximum(sub * i - 1, 0), 2 * r)),
                  pl.BlockSpec((1, tq, width), lambda b, r, i: (b, i, 2 * r + 1)),
                  pl.BlockSpec((1, TILE, width), lambda b, r, i: (b, jnp.maximum(sub * i - 1, 0), 2 * r + 1)),
                  pl.BlockSpec(table.shape, lambda b, r, i: (0, 0, 0, 0))],
        out_specs=[pl.BlockSpec((1, tq, width), lambda b, r, i: (b, i, r)),
                   pl.BlockSpec((1, tq, LANES), lambda b, r, i: (b, i, r))],
        out_shape=[jax.ShapeDtypeStruct((bx, n_rows, dil * width), BF16),
                   jax.ShapeDtypeStruct((bx, n_rows, dil * LANES), F32)],
        compiler_params=_cparams("arbitrary", "arbitrary", "arbitrary"),
        name="dilated_branch_prompt",
    )(qv, kvv, kvv, kvv, kvv, table)
    return o.reshape(bx, s, width), lse.reshape(bx, s, LANES)


def _post_dil_kernel(o1_ref, o2_ref, o3_ref, l1_ref, l2_ref, l3_ref, w_ref, x_ref, gate_ref, g_ref, o_ref):
    lses = [r[0] for r in (l1_ref, l2_ref, l3_ref)]
    m = jnp.maximum(jnp.maximum(lses[0], lses[1]), lses[2])
    es = [jnp.exp(l - m) for l in lses]
    tot = es[0] + es[1] + es[2]
    width = o1_ref.shape[2]
    head_of_col = lax.shift_right_arithmetic(lax.broadcasted_iota(jnp.int32, (LANES, width), 1), SEL_SHIFT)
    expand = jnp.where(head_of_col == lax.broadcasted_iota(jnp.int32, (LANES, width), 0), 1.0, 0.0).astype(BF16)
    mix = jnp.zeros((o1_ref.shape[1], width), F32)
    for e, o_ref_j in zip(es, (o1_ref, o2_ref, o3_ref)):
        alpha = jnp.dot((e / tot).astype(BF16), expand, preferred_element_type=F32)
        mix = mix + alpha * o_ref_j[0].astype(F32)
    y = jnp.dot(mix.astype(BF16), w_ref[...], preferred_element_type=F32)
    yn = y * lax.rsqrt(jnp.mean(y * y, axis=-1, keepdims=True) + NORM_EPS) * g_ref[...]
    o_ref[0] = x_ref[0] + gate_ref[0] * yn


def _post_dil(outs, lses, w_out, x, gate, g):
    bx, s, d = x.shape
    tm = _row_tile(s, 256)
    width = w_out.shape[0]
    return pl.pallas_call(
        _post_dil_kernel,
        grid=(bx, s // tm),
        in_specs=([pl.BlockSpec((1, tm, width), lambda b, i: (b, i, 0))] * 3
                  + [pl.BlockSpec((1, tm, LANES), lambda b, i: (b, i, 0))] * 3
                  + [pl.BlockSpec(w_out.shape, lambda b, i: (0, 0)),
                     pl.BlockSpec((1, tm, d), lambda b, i: (b, i, 0)),
                     _mod_spec(gate, tm),
                     pl.BlockSpec((1, d), lambda b, i: (0, 0))]),
        out_specs=pl.BlockSpec((1, tm, d), lambda b, i: (b, i, 0)),
        out_shape=jax.ShapeDtypeStruct((bx, s, d), F32),
        compiler_params=_cparams("arbitrary", "arbitrary"),
        name="post_dilated",
    )(*outs, *lses, w_out, x, gate, g)


def _compress_kernel(pt_ref, page_ref, w1_ref, w2_ref, pe_ref, kc_ref, vc_ref, rows_ref, chunk_ref, *, n_pages):
    j = pl.program_id(1)
    row0 = pl.multiple_of(j * PAGE_SIZE, PAGE_SIZE)
    rows_ref[0, pl.ds(row0, PAGE_SIZE), :] = page_ref[0, :, 0:LANES]
    rows_ref[1, pl.ds(row0, PAGE_SIZE), :] = page_ref[0, :, LANES:2 * LANES]

    @pl.when(j == n_pages - 1)
    def _():
        n_chunks = n_pages * PAGE_SIZE // CMP_STRIDE
        half = CMP_STRIDE * HEAD_DIM
        for kv, out_ref in ((0, kc_ref), (1, vc_ref)):
            w1 = w1_ref[kv]
            pe_a = jnp.broadcast_to(pe_ref[kv, :, 0:half], (8, half)).astype(BF16)
            pe_b = jnp.broadcast_to(pe_ref[kv, :, half:], (8, half)).astype(BF16)
            pe_term = (jnp.dot(pe_a, w1, preferred_element_type=F32)[0:1, 0:CMP_HIDDEN]
                       + jnp.dot(pe_b, w1, preferred_element_type=F32)[0:1, CMP_HIDDEN:])
            for l in range(CMP_STRIDE):
                both = rows_ref[kv, pl.ds(l, n_chunks, stride=CMP_STRIDE), :].astype(BF16)
                for g in range(NSA_KV_HEADS):
                    chunk_ref[g, :, l * HEAD_DIM:(l + 1) * HEAD_DIM] = both[:, g * HEAD_DIM:(g + 1) * HEAD_DIM]
            outs = []
            for g in range(NSA_KV_HEADS):
                uv = jnp.dot(chunk_ref[g], w1, preferred_element_type=F32)
                pre = uv[:, 0:CMP_HIDDEN] + pltpu.roll(uv[:, CMP_HIDDEN:], n_chunks - 1, 0) + pe_term
                hid = jax.nn.gelu(pre).astype(BF16)
                outs.append(jnp.dot(hid, w2_ref[kv], preferred_element_type=F32))
            out_ref[0] = jnp.concatenate(outs, axis=1)


def _compress(pool, page_table, w1cat, w2, pe):
    n_req, n_pages = page_table.shape
    n_chunks = n_pages * PAGE_SIZE // CMP_STRIDE
    width = 2 * NSA_KV_HEADS * HEAD_DIM
    grid_spec = pltpu.PrefetchScalarGridSpec(
        num_scalar_prefetch=1,
        grid=(n_req, n_pages),
        in_specs=[pl.BlockSpec((1, PAGE_SIZE, width), lambda r, j, pt: (pt[r, j], 0, 0)),
                  pl.BlockSpec(w1cat.shape, lambda r, j, pt: (0, 0, 0)),
                  pl.BlockSpec(w2.shape, lambda r, j, pt: (0, 0, 0)),
                  pl.BlockSpec(pe.shape, lambda r, j, pt: (0, 0, 0))],
        out_specs=[pl.BlockSpec((1, n_chunks, LANES), lambda r, j, pt: (r, 0, 0)),
                   pl.BlockSpec((1, n_chunks, LANES), lambda r, j, pt: (r, 0, 0))],
        scratch_shapes=[pltpu.VMEM((2, n_pages * PAGE_SIZE, LANES), F32),
                        pltpu.VMEM((NSA_KV_HEADS, n_chunks, CMP_STRIDE * HEAD_DIM), BF16)],
    )
    return pl.pallas_call(
        functools.partial(_compress_kernel, n_pages=n_pages),
        grid_spec=grid_spec,
        out_shape=[jax.ShapeDtypeStruct((n_req, n_chunks, LANES), F32)] * 2,
        compiler_params=_cparams("arbitrary", "arbitrary"),
        name="nsa_compress",
    )(page_table, pool, w1cat, w2, pe)


def _top_k_mask(imp, n_top):
    lane = lax.broadcasted_iota(jnp.int32, imp.shape, 1)
    width = imp.shape[1]

    def body(_, carry):
        imp, sel = carry
        m = jnp.max(imp, axis=1, keepdims=True)
        first = jnp.min(jnp.where(imp == m, lane, width), axis=1, keepdims=True)
        pick = (lane == first) & (m > -jnp.inf)
        return jnp.where(lane == first, -jnp.inf, imp), jnp.where(pick, 1.0, sel)

    _, sel = lax.fori_loop(0, n_top, body, (imp, jnp.zeros(imp.shape, F32)))
    return sel


def _nsa_kernel(qn_ref, kc_ref, vc_ref, bcmp_ref, sel_ref, win_ref, tsel_ref, twin_ref, small_ref, o_ref,
                *, tq, tk, n_sel_delta, n_win_tiles):
    i = pl.program_id(1)
    t0 = i * tq
    n_cmp_pad = kc_ref.shape[1]
    n_blk = LANES
    lo = _half_masks(tq)
    lane = lax.broadcasted_iota(jnp.int32, (tq, LANES), 1)
    gates = small_ref[0]
    q_all = qn_ref[0].astype(F32)

    t_c = t0 + lax.broadcasted_iota(jnp.int32, (tq, n_cmp_pad), 0)
    n_c = lax.broadcasted_iota(jnp.int32, (tq, n_cmp_pad), 1)
    ok_c = t_c >= n_c * CMP_STRIDE + (CMP_LEN - 1)
    ci = lax.broadcasted_iota(jnp.int32, (n_cmp_pad, n_blk), 0) * CMP_STRIDE
    sj = lax.broadcasted_iota(jnp.int32, (n_cmp_pad, n_blk), 1) * SEL_BLOCK
    cover = jnp.where((ci < sj + SEL_BLOCK) & (ci + CMP_LEN > sj), 1.0, 0.0).astype(F32)
    t_b = t0 + lax.broadcasted_iota(jnp.int32, (tq, n_blk), 0)
    cur = lax.shift_right_arithmetic(t_b, SEL_SHIFT)
    forced = (lane == 0) | (lane == cur) | (lane == cur - 1)
    blk_of_key = lax.shift_right_arithmetic(lax.broadcasted_iota(jnp.int32, (n_blk, tk), 1), SEL_SHIFT)
    blk_row = lax.broadcasted_iota(jnp.int32, (n_blk, tk), 0)
    blk_delta = blk_row - blk_of_key
    sub = tk // TILE

    n_sub = tq // TILE
    group_of = [h // NSA_GROUP for h in range(NSA_HEADS)]

    qs = []
    for h in range(NSA_HEADS):
        g = group_of[h]
        blk = q_all[:, LANES * (h // 2):LANES * (h // 2 + 1)]
        if h % 2 != g:
            blk = pltpu.roll(blk, HEAD_DIM, 1)
        qs.append(jnp.where(lo if g == 0 else jnp.logical_not(lo), blk, 0.0).astype(BF16))

    kcb = kc_ref[0].astype(BF16)
    vcb = vc_ref[0].astype(BF16)
    o_cmp, imps = [], []
    for g in range(NSA_KV_HEADS):
        psum = jnp.zeros((tq, n_cmp_pad), F32)
        for r in range(NSA_GROUP):
            h = g * NSA_GROUP + r
            s = jnp.where(ok_c, _nt_dot(qs[h], kcb) + bcmp_ref[h], NEG)
            m = jnp.max(s, axis=1, keepdims=True)
            e = jnp.where(ok_c, jnp.exp(s - m), 0.0)
            p = e / jnp.maximum(jnp.sum(e, axis=1, keepdims=True), TINY)
            psum = psum + p
            o_cmp.append(jnp.dot(p.astype(BF16), vcb, preferred_element_type=F32))
        imp = jnp.dot(psum, cover, preferred_element_type=F32, precision=lax.Precision.HIGHEST)
        imp = jnp.where(forced, FORCED_SCORE, imp)
        imps.append(jnp.where(lane <= cur, imp, -jnp.inf))

    sel = _top_k_mask(jnp.concatenate(imps, axis=0), min(SEL_TOPK, n_blk))
    not_sel = [(1.0 - sel[g * tq:(g + 1) * tq]).astype(BF16) for g in range(NSA_KV_HEADS)]

    def sel_chunk(c, states):
        off = pl.multiple_of(c * tk, tk)
        k = sel_ref[0, pl.ds(off, tk), 0:LANES]
        v = sel_ref[0, pl.ds(off, tk), LANES:2 * LANES]
        expand = jnp.where(blk_delta == c * (tk // SEL_BLOCK), NEG, 0.0).astype(BF16)
        mask_add = [jnp.dot(ns, expand, preferred_element_type=F32) for ns in not_sel]
        out = []
        for h in range(NSA_HEADS):
            bias = jnp.concatenate([jnp.concatenate(
                [tsel_ref[h, jnp.clip(n_sub * i + a - (c * sub + u), -1, n_sel_delta - 1) + 1] for u in range(sub)],
                axis=1) for a in range(n_sub)], axis=0)
            out.append(_online_update(states[h], _nt_dot(qs[h], k) + bias + mask_add[group_of[h]], v))
        return tuple(out)

    n_chunks = (t0 + tq - 1) // tk + 1
    sel_states = lax.fori_loop(0, n_chunks, sel_chunk, tuple(_init_state(tq) for _ in range(NSA_HEADS)))

    o_win = [[] for _ in range(NSA_HEADS)]
    for a in range(n_sub):
        j0 = n_sub * i + a
        ks, vs, idxs = [], [], []
        for u in range(n_win_tiles - 1, -1, -1):
            off = pl.multiple_of(jnp.maximum(j0 - u, 0) * TILE, TILE)
            ks.append(win_ref[0, pl.ds(off, TILE), 0:LANES])
            vs.append(win_ref[0, pl.ds(off, TILE), LANES:2 * LANES])
            idxs.append(jnp.where(j0 - u >= 0, u + 1, 0))
        k = jnp.concatenate(ks, axis=0)
        v = jnp.concatenate(vs, axis=0)
        for h in range(NSA_HEADS):
            bias = jnp.concatenate([twin_ref[h, ix] for ix in idxs], axis=1)
            s = _nt_dot(qs[h][a * TILE:(a + 1) * TILE], k) + bias
            e = jnp.exp(s - jnp.max(s, axis=1, keepdims=True))
            l = jnp.sum(e, axis=1, keepdims=True)
            o_win[h].append(jnp.dot(e.astype(BF16), v, preferred_element_type=F32) / l)

    pair_out = [None] * (NSA_HEADS // 2)
    for h in range(NSA_HEADS):
        gc, gs, gw = (jnp.sum(jnp.where(lane == 3 * h + b, gates, 0.0), axis=1, keepdims=True) for b in range(3))
        _, l_s, acc_s = sel_states[h]
        o = gc * o_cmp[h] + gs * (acc_s / l_s) + gw * jnp.concatenate(o_win[h], axis=0)
        if h % 2 != group_of[h]:
            o = pltpu.roll(o, HEAD_DIM, 1)
        prev = pair_out[h // 2]
        keep = lo if h % 2 == 0 else jnp.logical_not(lo)
        pair_out[h // 2] = jnp.where(keep, o, 0.0 if prev is None else prev)

    o_ref[0] = jnp.concatenate(pair_out, axis=1).astype(o_ref.dtype)


def _nsa_prompt(qn, kc, vc, bcmp, nsabf, tsel, twin, small):
    bx, s, _ = qn.shape
    tq = _row_tile(s, 256)
    tk = _row_tile(s, 512)
    n_cmp_pad = kc.shape[1]
    once = pl.Buffered(1)
    return pl.pallas_call(
        functools.partial(_nsa_kernel, tq=tq, tk=tk, n_sel_delta=tsel.shape[1] - 1, n_win_tiles=twin.shape[1] - 1),
        grid=(bx, s // tq),
        in_specs=[pl.BlockSpec((1, tq, NSA_HEADS * HEAD_DIM), lambda b, i: (b, i, 0)),
                  pl.BlockSpec((1, n_cmp_pad, LANES), lambda b, i: (b, 0, 0)),
                  pl.BlockSpec((1, n_cmp_pad, LANES), lambda b, i: (b, 0, 0)),
                  pl.BlockSpec((NSA_HEADS, tq, n_cmp_pad), lambda b, i: (0, i, 0)),
                  pl.BlockSpec((1, s, 2 * LANES), lambda b, i: (b, 0, 1)),
                  pl.BlockSpec((1, s, 2 * LANES), lambda b, i: (b, 0, 2)),
                  pl.BlockSpec(tsel.shape, lambda b, i: (0, 0, 0, 0), pipeline_mode=once),
                  pl.BlockSpec(twin.shape, lambda b, i: (0, 0, 0, 0), pipeline_mode=once),
                  pl.BlockSpec((1, tq, LANES), lambda b, i: (b, i, 0))],
        out_specs=pl.BlockSpec((1, tq, NSA_HEADS * HEAD_DIM), lambda b, i: (b, i, 0)),
        out_shape=jax.ShapeDtypeStruct((bx, s, NSA_HEADS * HEAD_DIM), BF16),
        compiler_params=_cparams("arbitrary", "arbitrary"),
        name="nsa_prompt",
    )(qn, kc, vc, bcmp, nsabf, nsabf, tsel, twin, small)


def _block_diag_rows(row, n_heads):
    shape = (n_heads, row.shape[1])
    r = lax.broadcasted_iota(jnp.int32, shape, 0)
    c = lax.shift_right_arithmetic(lax.broadcasted_iota(jnp.int32, shape, 1), SEL_SHIFT)
    return jnp.where(r == c, jnp.broadcast_to(row, shape), 0.0)


def _diag_heads(o):
    r = lax.broadcasted_iota(jnp.int32, o.shape, 0)
    c = lax.shift_right_arithmetic(lax.broadcasted_iota(jnp.int32, o.shape, 1), SEL_SHIFT)
    return jnp.sum(jnp.where(r == c, o, 0.0), axis=0, keepdims=True)


def _bf16_round(x):
    return x.astype(BF16).astype(F32)


def _fox_sample_kernel(pt_ref, page_ref, lft_ref, q_ref, new_ref, small_ref, o_ref,
                       m_ref, l_ref, acc_ref, carry_ref, *, n_pages):
    j = pl.program_id(1)
    width = FOX_HEADS * HEAD_DIM
    qmat = _block_diag_rows(q_ref[0].astype(F32), FOX_HEADS)
    row8 = lax.broadcasted_iota(jnp.int32, (FOX_HEADS, LANES), 0)
    lane = lax.broadcasted_iota(jnp.int32, (FOX_HEADS, LANES), 1)

    @pl.when(j == 0)
    def _():
        k_new = _bf16_round(new_ref[0][:, 0:width])
        v_new = _bf16_round(new_ref[0][:, width:])
        s_new = jnp.sum(qmat * k_new, axis=1, keepdims=True)
        m_ref[...] = jnp.broadcast_to(s_new, m_ref.shape)
        l_ref[...] = jnp.ones_like(l_ref)
        acc_ref[...] = jnp.broadcast_to(v_new, acc_ref.shape)
        small = jnp.broadcast_to(small_ref[0], (FOX_HEADS, LANES))
        lf_new = jnp.sum(jnp.where(lane == SMALL_GATES + row8, small, 0.0), axis=1, keepdims=True)
        carry_ref[...] = jnp.broadcast_to(lf_new, carry_ref.shape)

    kb = page_ref[0][:, 0:width].astype(BF16)
    vb = page_ref[0][:, width:].astype(BF16)
    lf = lft_ref[0]
    u = lax.broadcasted_iota(jnp.int32, (PAGE_SIZE, PAGE_SIZE), 0)
    c = lax.broadcasted_iota(jnp.int32, (PAGE_SIZE, PAGE_SIZE), 1)
    later = jnp.where(u > c, 1.0, 0.0).astype(F32)
    carry = carry_ref[:, 0:1]
    bias = carry + jnp.dot(lf, later, preferred_element_type=F32, precision=lax.Precision.HIGHEST)
    s = _nt_dot(qmat.astype(BF16), kb) + bias
    m_old = m_ref[:, 0:1]
    m_new = jnp.maximum(m_old, jnp.max(s, axis=1, keepdims=True))
    alpha = jnp.exp(m_old - m_new)
    p = jnp.exp(s - m_new)
    l_ref[...] = jnp.broadcast_to(alpha * l_ref[:, 0:1] + jnp.sum(p, axis=1, keepdims=True), l_ref.shape)
    acc_ref[...] = alpha * acc_ref[...] + jnp.dot(p.astype(BF16), vb, preferred_element_type=F32)
    m_ref[...] = jnp.broadcast_to(m_new, m_ref.shape)
    carry_ref[...] = jnp.broadcast_to(carry + jnp.sum(lf, axis=1, keepdims=True), carry_ref.shape)

    @pl.when(j == n_pages - 1)
    def _():
        o_ref[0] = _diag_heads(acc_ref[...] / l_ref[:, 0:1]).astype(o_ref.dtype)


def _fox_sample(pool, logf_t, page_table, q, new_kv, small):
    n_req, n_pages = page_table.shape
    width = FOX_HEADS * HEAD_DIM
    grid_spec = pltpu.PrefetchScalarGridSpec(
        num_scalar_prefetch=1,
        grid=(n_req, n_pages),
        in_specs=[pl.BlockSpec((1, PAGE_SIZE, 2 * width), lambda r, j, pt: (pt[r, n_pages - 1 - j], 0, 0)),
                  pl.BlockSpec((1, FOX_HEADS, PAGE_SIZE), lambda r, j, pt: (pt[r, n_pages - 1 - j], 0, 0)),
                  pl.BlockSpec((1, 1, width), lambda r, j, pt: (r, 0, 0)),
                  pl.BlockSpec((1, 1, 2 * width), lambda r, j, pt: (r, 0, 0)),
                  pl.BlockSpec((1, 1, LANES), lambda r, j, pt: (r, 0, 0))],
        out_specs=pl.BlockSpec((1, 1, width), lambda r, j, pt: (r, 0, 0)),
        scratch_shapes=[pltpu.VMEM((FOX_HEADS, LANES), F32), pltpu.VMEM((FOX_HEADS, LANES), F32),
                        pltpu.VMEM((FOX_HEADS, width), F32), pltpu.VMEM((FOX_HEADS, LANES), F32)],
    )
    return pl.pallas_call(
        functools.partial(_fox_sample_kernel, n_pages=n_pages),
        grid_spec=grid_spec,
        out_shape=jax.ShapeDtypeStruct((n_req, 1, width), BF16),
        compiler_params=_cparams("arbitrary", "arbitrary"),
        name="fox_sample",
    )(page_table, pool, logf_t, q, new_kv, small)


def _dil_sample_kernel(q_ref, new_ref, b1_ref, b2_ref, b3_ref, tab_ref, o_ref):
    width = DIL_HEADS * HEAD_DIM
    qmat = _block_diag_rows(q_ref[0].astype(F32), DIL_HEADS)
    qb = qmat.astype(BF16)
    k_new = _bf16_round(new_ref[0][:, 0:width])
    v_new = _bf16_round(new_ref[0][:, width:])
    s_new0 = jnp.sum(qmat * k_new, axis=1, keepdims=True)
    outs, lses = [], []
    for br, buf in enumerate((b1_ref, b2_ref, b3_ref)):
        kb = buf[0][:, 0:width].astype(BF16)
        vb = buf[0][:, width:].astype(BF16)
        bias = tab_ref[br]
        s = _nt_dot(qb, kb) + bias[:, 0:TILE]
        s_new = s_new0 + bias[:, TILE:TILE + 1]
        m = jnp.maximum(jnp.max(s, axis=1, keepdims=True), s_new)
        p = jnp.exp(s - m)
        p_new = jnp.exp(s_new - m)
        l = jnp.sum(p, axis=1, keepdims=True) + p_new
        outs.append((jnp.dot(p.astype(BF16), vb, preferred_element_type=F32) + _bf16_round(p_new) * v_new) / l)
        lses.append(jnp.log(l) + m)
    top = jnp.maximum(jnp.maximum(lses[0], lses[1]), lses[2])
    es = [jnp.exp(l - top) for l in lses]
    tot = es[0] + es[1] + es[2]
    mix = sum(_bf16_round(e / tot) * _bf16_round(o) for e, o in zip(es, outs))
    o_ref[0] = _diag_heads(mix).astype(o_ref.dtype)


def _dil_sample(q, new_kv, buf, table):
    n_req, n_buf, row_w = buf.shape
    width = DIL_HEADS * HEAD_DIM
    views, specs = [], []
    for _, dil in DIL_BRANCHES:
        n_rows = n_buf // dil
        assert n_rows % TILE == 0
        views.append(buf.reshape(n_req, n_rows, dil * row_w))
        specs.append(pl.BlockSpec((1, TILE, row_w), functools.partial(lambda r, blk: (r, blk, 0), blk=n_rows // TILE - 1)))
    return pl.pallas_call(
        _dil_sample_kernel,
        grid=(n_req,),
        in_specs=[pl.BlockSpec((1, 1, width), lambda r: (r, 0, 0)),
                  pl.BlockSpec((1, 1, row_w), lambda r: (r, 0, 0))] + specs
                 + [pl.BlockSpec(table.shape, lambda r: (0, 0, 0))],
        out_specs=pl.BlockSpec((1, 1, width), lambda r: (r, 0, 0)),
        out_shape=jax.ShapeDtypeStruct((n_req, 1, width), BF16),
        compiler_params=_cparams("arbitrary"),
        name="dilated_sample",
    )(q, new_kv, *views, table)


def _group_queries(q_row):
    row8 = lax.broadcasted_iota(jnp.int32, (8, LANES), 0)
    lo = lax.broadcasted_iota(jnp.int32, (8, LANES), 1) < HEAD_DIM
    out = []
    for g in range(NSA_KV_HEADS):
        qg = jnp.zeros((8, LANES), F32)
        for r in range(NSA_GROUP):
            h = g * NSA_GROUP + r
            blk = jnp.broadcast_to(q_row[:, LANES * (h // 2):LANES * (h // 2 + 1)], (8, LANES))
            if h % 2 != g:
                blk = pltpu.roll(blk, HEAD_DIM, 1)
            qg = jnp.where((row8 == r) & (lo if g == 0 else jnp.logical_not(lo)), blk, qg)
        out.append(qg)
    return out


def _nsa_select_kernel(q_ref, kc_ref, vc_ref, bcmp_ref, oc_ref, idx_ref, *, n_blk_pad, cur):
    n_cmp_pad = kc_ref.shape[1]
    kcb = kc_ref[0].astype(BF16)
    vcb = vc_ref[0].astype(BF16)
    row8 = lax.broadcasted_iota(jnp.int32, (8, n_cmp_pad), 0)
    ci = lax.broadcasted_iota(jnp.int32, (n_cmp_pad, n_blk_pad), 0) * CMP_STRIDE
    sj = lax.broadcasted_iota(jnp.int32, (n_cmp_pad, n_blk_pad), 1) * SEL_BLOCK
    cover = jnp.where((ci < sj + SEL_BLOCK) & (ci + CMP_LEN > sj), 1.0, 0.0).astype(F32)
    blk = lax.broadcasted_iota(jnp.int32, (8, n_blk_pad), 1)
    forced = (blk == 0) | (blk == cur) | (blk == cur - 1)
    lane = lax.broadcasted_iota(jnp.int32, (8, LANES), 1)
    for g, qg in enumerate(_group_queries(q_ref[0].astype(F32))):
        bias = bcmp_ref[8 * g:8 * g + 8, :]
        ok = (bias > 0.5 * NEG) & (row8 < NSA_GROUP)
        s = jnp.where(ok, _nt_dot(qg.astype(BF16), kcb) + bias, NEG)
        m = jnp.max(s, axis=1, keepdims=True)
        e = jnp.where(ok, jnp.exp(s - m), 0.0)
        p = e / jnp.maximum(jnp.sum(e, axis=1, keepdims=True), TINY)
        oc_ref[0, g] = jnp.dot(p.astype(BF16), vcb, preferred_element_type=F32)
        psum = jnp.broadcast_to(jnp.sum(p, axis=0, keepdims=True), (8, n_cmp_pad))
        imp = jnp.dot(psum, cover, preferred_element_type=F32, precision=lax.Precision.HIGHEST)
        imp = jnp.where(forced, FORCED_SCORE, imp)
        imp = jnp.where(blk <= cur, imp, -jnp.inf)

        def body(it, carry):
            imp, idx = carry
            top = jnp.max(imp, axis=1, keepdims=True)
            first = jnp.min(jnp.where(imp == top, blk, n_blk_pad), axis=1, keepdims=True)
            idx = jnp.where(lane == it, jnp.where(top > -jnp.inf, first, -1), idx)
            return jnp.where(blk == first, -jnp.inf, imp), idx

        _, idx = lax.fori_loop(0, SEL_TOPK, body, (imp, jnp.full((8, LANES), -1, jnp.int32)))
        idx_ref[0, g] = idx


def _nsa_select(q, kc, vc, bcmp_row, n_blk_pad, cur):
    n_req = q.shape[0]
    n_cmp_pad = kc.shape[1]
    return pl.pallas_call(
        functools.partial(_nsa_select_kernel, n_blk_pad=n_blk_pad, cur=cur),
        grid=(n_req,),
        in_specs=[pl.BlockSpec((1, 1, NSA_HEADS * HEAD_DIM), lambda r: (r, 0, 0)),
                  pl.BlockSpec((1, n_cmp_pad, LANES), lambda r: (r, 0, 0)),
                  pl.BlockSpec((1, n_cmp_pad, LANES), lambda r: (r, 0, 0)),
                  pl.BlockSpec(bcmp_row.shape, lambda r: (0, 0))],
        out_specs=[pl.BlockSpec((1, NSA_KV_HEADS, 8, LANES), lambda r: (r, 0, 0, 0)),
                   pl.BlockSpec((1, NSA_KV_HEADS, 8, LANES), lambda r: (r, 0, 0, 0))],
        out_shape=[jax.ShapeDtypeStruct((n_req, NSA_KV_HEADS, 8, LANES), F32),
                   jax.ShapeDtypeStruct((n_req, NSA_KV_HEADS, 8, LANES), jnp.int32)],
        compiler_params=_cparams("arbitrary"),
        name="nsa_sample_select",
    )(q, kc, vc, bcmp_row)


def _nsa_attend_kernel(idx_ref, pt_ref, q_ref, blk0_ref, blk1_ref, fsel_ref, wbuf_ref, fwin_ref, nsel_ref, nwin_ref,
                       oc_ref, small_ref, o_ref, m_ref, l_ref, acc_ref, ow_ref, *, n_past_blk, new_lane, n_win):
    r_idx = pl.program_id(0)
    k_idx = pl.program_id(1)
    qgs = _group_queries(q_ref[0].astype(F32))
    row8 = lax.broadcasted_iota(jnp.int32, (8, LANES), 0)

    @pl.when(k_idx == 0)
    def _():
        ksel_new = _bf16_round(nsel_ref[0][:, 2 * LANES:3 * LANES])
        vsel_new = _bf16_round(nsel_ref[0][:, 3 * LANES:4 * LANES])
        kwin_new = _bf16_round(nwin_ref[0][:, 0:LANES])
        vwin_new = _bf16_round(nwin_ref[0][:, LANES:2 * LANES])
        kw = wbuf_ref[0][:, 0:LANES].astype(BF16)
        vw = wbuf_ref[0][:, LANES:2 * LANES].astype(BF16)
        for g, qg in enumerate(qgs):
            rows = slice(8 * g, 8 * g + 8)
            s_new = jnp.sum(qg * ksel_new, axis=1, keepdims=True) + fsel_ref[n_past_blk, rows, new_lane:new_lane + 1]
            m_ref[g] = jnp.broadcast_to(s_new, (8, LANES))
            l_ref[g] = jnp.ones((8, LANES), F32)
            acc_ref[g] = jnp.broadcast_to(vsel_new, (8, LANES))
            s = _nt_dot(qg.astype(BF16), kw) + fwin_ref[rows, 0:n_win]
            s_wn = jnp.sum(qg * kwin_new, axis=1, keepdims=True) + fwin_ref[rows, n_win:n_win + 1]
            m = jnp.maximum(jnp.max(s, axis=1, keepdims=True), s_wn)
            p = jnp.exp(s - m)
            p_new = jnp.exp(s_wn - m)
            l = jnp.sum(p, axis=1, keepdims=True) + p_new
            ow_ref[g] = (jnp.dot(p.astype(BF16), vw, preferred_element_type=F32) + _bf16_round(p_new) * vwin_new) / l

    for g, (qg, blk_ref) in enumerate(zip(qgs, (blk0_ref, blk1_ref))):
        b = idx_ref[r_idx, g * SEL_TOPK + k_idx]
        valid = (b >= 0) & (b < n_past_blk)
        kb = blk_ref[0][:, 0:LANES].astype(BF16)
        vb = blk_ref[0][:, LANES:2 * LANES].astype(BF16)
        bias = fsel_ref[jnp.clip(b, 0, n_past_blk - 1), 8 * g:8 * g + 8, 0:SEL_BLOCK]
        s = jnp.where(valid, _nt_dot(qg.astype(BF16), kb) + bias, NEG)
        m_old = m_ref[g][:, 0:1]
        m_new = jnp.maximum(m_old, jnp.max(s, axis=1, keepdims=True))
        alpha = jnp.exp(m_old - m_new)
        p = jnp.exp(s - m_new)
        l_ref[g] = jnp.broadcast_to(alpha * l_ref[g][:, 0:1] + jnp.sum(p, axis=1, keepdims=True), (8, LANES))
        acc_ref[g] = alpha * acc_ref[g] + jnp.dot(p.astype(BF16), vb, preferred_element_type=F32)
        m_ref[g] = jnp.broadcast_to(m_new, (8, LANES))

    @pl.when(k_idx == SEL_TOPK - 1)
    def _():
        gates = small_ref[0]
        for g in range(NSA_KV_HEADS):
            gate = []
            for b in range(3):
                col = jnp.zeros((8, 1), F32)
                for r in range(NSA_GROUP):
                    lane_i = 3 * (g * NSA_GROUP + r) + b
                    col = jnp.where(row8[:, 0:1] == r, gates[:, lane_i:lane_i + 1], col)
                gate.append(col)
            o_ref[0, g] = gate[0] * oc_ref[0, g] + gate[1] * (acc_ref[g] / l_ref[g][:, 0:1]) + gate[2] * ow_ref[g]


def _nsa_attend(idx, page_table, q, pool, fsel, win_buf, fwin, new_sel, new_win, o_cmp, small, n_past_blk, new_lane):
    n_req = q.shape[0]
    n_win = win_buf.shape[1]

    def blk_map(g):
        def index(r, k, idx_ref, pt_ref):
            b = idx_ref[r, g * SEL_TOPK + k]
            b = jnp.where((b >= 0) & (b < n_past_blk), b, 0)
            return pt_ref[r, lax.shift_right_arithmetic(b, 1)] * 2 + jnp.bitwise_and(b, 1), 0, 1
        return index

    const = lambda *shape: (lambda r, k, idx_ref, pt_ref: shape)
    per_req3 = lambda r, k, idx_ref, pt_ref: (r, 0, 0)
    per_req4 = lambda r, k, idx_ref, pt_ref: (r, 0, 0, 0)
    grid_spec = pltpu.PrefetchScalarGridSpec(
        num_scalar_prefetch=2,
        grid=(n_req, SEL_TOPK),
        in_specs=[pl.BlockSpec((1, 1, NSA_HEADS * HEAD_DIM), per_req3),
                  pl.BlockSpec((1, SEL_BLOCK, 2 * LANES), blk_map(0)),
                  pl.BlockSpec((1, SEL_BLOCK, 2 * LANES), blk_map(1)),
                  pl.BlockSpec(fsel.shape, const(0, 0, 0)),
                  pl.BlockSpec((1, n_win, 2 * LANES), per_req3),
                  pl.BlockSpec(fwin.shape, const(0, 0)),
                  pl.BlockSpec((1, 1, 4 * LANES), per_req3),
                  pl.BlockSpec((1, 1, 2 * LANES), per_req3),
                  pl.BlockSpec((1, NSA_KV_HEADS, 8, LANES), per_req4),
                  pl.BlockSpec((1, 1, LANES), per_req3)],
        out_specs=pl.BlockSpec((1, NSA_KV_HEADS, 8, LANES), per_req4),
        scratch_shapes=[pltpu.VMEM((NSA_KV_HEADS, 8, LANES), F32)] * 4,
    )
    return pl.pallas_call(
        functools.partial(_nsa_attend_kernel, n_past_blk=n_past_blk, new_lane=new_lane, n_win=n_win),
        grid_spec=grid_spec,
        out_shape=jax.ShapeDtypeStruct((n_req, NSA_KV_HEADS, 8, LANES), F32),
        compiler_params=_cparams("arbitrary", "arbitrary"),
        name="nsa_sample_attend",
    )(idx, page_table, q, pool, pool, fsel, win_buf, fwin, new_sel, new_win, o_cmp, small)


def kernel(x_prompt, x_sample, cache_nsa_kv, cache_fox_kv, cache_fox_logf, state_nsa_win_kv, state_dil_kv, page_table,
           c_prompt, c_sample, rel_bias, norm_g, w_ada, b_ada, w_in_a, nsa_gate_b, fox_f_b, nsa_cmp_w1, nsa_cmp_w2,
           nsa_cmp_pe, w_out_a, w_in_c, w_out_c, w_mlp1, w_mlp2):
    bp, s, d = x_prompt.shape
    bd = x_sample.shape[0]
    depth = w_ada.shape[0]
    n_pages = s // PAGE_SIZE
    n_cmp_pad = s // CMP_STRIDE

    mods = _ada_params(jnp.concatenate([c_prompt, c_sample], axis=0), w_ada, b_ada).reshape(depth, bp + bd, 6, d)
    tab_sel = _toeplitz_table(rel_bias, NSA_HEADS, 15, "causal")
    tab_win = _toeplitz_table(rel_bias, NSA_HEADS, NSA_WINDOW // TILE + 2, "window")
    assert all(window // dil == TILE for window, dil in DIL_BRANCHES)
    tab_dil = [_toeplitz_table(rel_bias, DIL_HEADS, 3, "branch", dil) for _, dil in DIL_BRANCHES]
    bcmp = _cmp_bias_table(rel_bias, s, TILE, n_cmp_pad, 0)

    past_len = page_table.shape[1] * PAGE_SIZE
    assert past_len % SEL_BLOCK == 0
    n_past_blk = past_len // SEL_BLOCK
    n_blk_pad = -(-(n_past_blk + 1) // LANES) * LANES
    far = 1 << 30
    bcmp_s = _affine_bias(rel_bias, NSA_GROUPED_ROWS, past_len // CMP_STRIDE, past_len - (CMP_LEN - 1), -CMP_STRIDE, far)
    fsel = _affine_bias(rel_bias, NSA_GROUPED_ROWS, (n_past_blk + 1) * SEL_BLOCK, past_len, -1, far)
    fsel = jnp.pad(jnp.swapaxes(fsel.reshape(len(NSA_GROUPED_ROWS), n_past_blk + 1, SEL_BLOCK), 0, 1),
                   ((0, 0), (0, 0), (0, LANES - SEL_BLOCK)))
    n_win_buf = state_nsa_win_kv.shape[2]
    fwin = _affine_bias(rel_bias, NSA_GROUPED_ROWS, n_win_buf + LANES, n_win_buf, -1, NSA_WINDOW)
    tab_dil_s = jnp.stack([_affine_bias(rel_bias, range(DIL_HEADS), 2 * TILE, TILE * dil, -dil, far)
                           for _, dil in DIL_BRANCHES])

    xp = x_prompt
    xs = x_sample.reshape(1, bd, d)
    per_req = lambda a: a.reshape(bd, 1, a.shape[-1])
    prompt_pages = jnp.arange(bp * n_pages, dtype=jnp.int32).reshape(bp, n_pages)
    nsa_p, nsa_s, fkv_p, fkv_s, lf_p, lf_s, win_p, win_s, dil_p, dil_s = [], [], [], [], [], [], [], [], [], []
    for layer in range(depth):
        mp = [mods[layer, :bp, k].reshape(bp, 1, d) for k in range(6)]
        ms = [mods[layer, bp:, k].reshape(1, bd, d) for k in range(6)]
        g = [norm_g[layer, k].reshape(1, d) for k in range(4)]
        i = layer // 2
        if layer % 2 == 0:
            wa = w_in_a[i]
            w_in = jnp.concatenate([wa[:, 0:1280], wa[:, 1304:2840], wa[:, 1280:1304], wa[:, 2840:2848],
                                    jnp.zeros((d, LANES - SMALL_GATES - FOX_HEADS), F32)], axis=1).astype(BF16)
            sb = jnp.concatenate([nsa_gate_b[i].reshape(-1), fox_f_b[i],
                                  jnp.zeros((LANES - SMALL_GATES - FOX_HEADS,), F32)]).reshape(1, LANES)
            half = CMP_STRIDE * HEAD_DIM
            w1cat = jnp.concatenate([nsa_cmp_w1[i][:, :half], nsa_cmp_w1[i][:, half:]], axis=2).astype(BF16)
            w2 = nsa_cmp_w2[i].astype(BF16)
            pe = nsa_cmp_pe[i].reshape(2, 1, CMP_LEN * HEAD_DIM)
            w_out = w_out_a[i].astype(BF16)

            qn, nsa4, nsabf, win, qf, fkv, fkvbf, small = _proj_even(xp, mp[0], mp[1], g[0], w_in, sb)
            c, ct = _cumsum(small)
            o_f = _fox_prompt(qf, fkvbf, c, ct)
            kc, vc = _compress(nsa4.reshape(bp * n_pages, PAGE_SIZE, 512), prompt_pages, w1cat, w2, pe)
            o_n = _nsa_prompt(qn, kc, vc, bcmp, nsabf, tab_sel, tab_win, small)
            op_a, op_b = o_n, o_f
            nsa_p.append(nsa4.reshape(bp, s, 4, NSA_KV_HEADS, HEAD_DIM))
            fkv_p.append(fkv.reshape(bp, s, 2, FOX_HEADS, HEAD_DIM))
            lf_p.append(small[:, :, SMALL_GATES:SMALL_GATES + FOX_HEADS])
            n_win = min(NSA_WINDOW, s)
            win_p.append(win[:, s - n_win:].reshape(bp, n_win, 2, NSA_KV_HEADS, HEAD_DIM))

            qn_s, nsa4_s, _, win_new, qf_s, fkv_s_, _, small_s = _proj_even(xs, ms[0], ms[1], g[0], w_in, sb)
            pool = cache_nsa_kv[i].reshape(cache_nsa_kv.shape[1], PAGE_SIZE, 512)
            kc_s, vc_s = _compress(pool, page_table, w1cat, w2, pe)
            o_cmp, idx = _nsa_select(per_req(qn_s), kc_s, vc_s, bcmp_s, n_blk_pad, n_past_blk)
            idx = idx[:, :, 0, :SEL_TOPK].reshape(bd, NSA_KV_HEADS * SEL_TOPK)
            o_nsa = _nsa_attend(idx, page_table, per_req(qn_s), pool.reshape(2 * pool.shape[0], SEL_BLOCK, 512), fsel,
                                state_nsa_win_kv[i].reshape(bd, n_win_buf, 2 * LANES), fwin, per_req(nsa4_s),
                                per_req(win_new), o_cmp, per_req(small_s), n_past_blk, 0)
            os_a = jnp.concatenate([o_nsa[:, g, :NSA_GROUP, g * HEAD_DIM:(g + 1) * HEAD_DIM]
                                    for g in range(NSA_KV_HEADS)], axis=1).reshape(1, bd, -1).astype(BF16)
            n_pool = cache_fox_kv.shape[1]
            os_b = _fox_sample(cache_fox_kv[i].reshape(n_pool, PAGE_SIZE, 2 * FOX_HEADS * HEAD_DIM),
                               jnp.swapaxes(cache_fox_logf[i], 1, 2), page_table, per_req(qf_s), per_req(fkv_s_),
                               per_req(small_s)).reshape(1, bd, -1)
            nsa_s.append(nsa4_s.reshape(bd, 1, 4, NSA_KV_HEADS, HEAD_DIM))
            fkv_s.append(fkv_s_.reshape(bd, 1, 2, FOX_HEADS, HEAD_DIM))
            lf_s.append(small_s[0, :, SMALL_GATES:SMALL_GATES + FOX_HEADS].reshape(bd, 1, FOX_HEADS))
            win_s.append(win_new.reshape(bd, 1, 2, NSA_KV_HEADS, HEAD_DIM))
        else:
            w_in = w_in_c[i].astype(BF16)
            w_out = w_out_c[i].astype(BF16)
            q, kv, kvbf = _proj_odd(xp, mp[0], mp[1], g[0], w_in)
            branches = [_dil_branch_prompt(q, kvbf, tab, dil) for tab, (_, dil) in zip(tab_dil, DIL_BRANCHES)]
            n_dil = min(DIL_BRANCHES[-1][0], s)
            dil_p.append(kv[:, s - n_dil:].reshape(bp, n_dil, 2, DIL_HEADS, HEAD_DIM))
            q_s, kv_s, _ = _proj_odd(xs, ms[0], ms[1], g[0], w_in)
            os_a = os_b = _dil_sample(per_req(q_s), per_req(kv_s),
                                      state_dil_kv[i].reshape(bd, state_dil_kv.shape[2], 2 * DIL_HEADS * HEAD_DIM),
                                      tab_dil_s).reshape(1, bd, -1)
            dil_s.append(kv_s.reshape(bd, 1, 2, DIL_HEADS, HEAD_DIM))
        if layer % 2 == 0:
            xp = _post(op_a, op_b, w_out, xp, mp[2], g[1])
            xs = _post(os_a, os_b, w_out, xs, ms[2], g[1])
        else:
            xp = _post_dil([o for o, _ in branches], [l for _, l in branches], w_out, xp, mp[2], g[1])
            xs = _post(os_a, os_b, w_out, xs, ms[2], g[1], 0, 1)
        w1 = w_mlp1[layer].astype(BF16)
        w2m = w_mlp2[layer].astype(BF16)
        xp = _mlp(xp, mp[3], mp[4], mp[5], g[2], g[3], w1, w2m)
        xs = _mlp(xs, ms[3], ms[4], ms[5], g[2], g[3], w1, w2m)
    return (xp, xs.reshape(bd, 1, d), jnp.stack(nsa_p), jnp.stack(nsa_s), jnp.stack(fkv_p), jnp.stack(fkv_s),
            jnp.stack(lf_p), jnp.stack(lf_s), jnp.stack(win_p), jnp.stack(win_s), jnp.stack(dil_p), jnp.stack(dil_s))
```

```python
import functools
import math

import numpy as np
import jax
import jax.numpy as jnp
from jax import lax
from jax.experimental import pallas as pl
from jax.experimental.pallas import tpu as pltpu

F32 = jnp.float32
BF16 = jnp.bfloat16

HEAD_DIM = 64
NSA_HEADS = 8
NSA_KV_HEADS = 2
NSA_GROUP = NSA_HEADS // NSA_KV_HEADS
FOX_HEADS = 8
DIL_HEADS = 16
CMP_LEN = 32
CMP_STRIDE = 16
CMP_HIDDEN = 4 * HEAD_DIM
SEL_BLOCK = 64
SEL_TOPK = 16
NSA_WINDOW = 512
FORCED_SCORE = 1e9
DIL_BRANCHES = ((128, 1), (512, 4), (2048, 16))
N_BUCKETS = 32
BUCKET_EXACT = 16
BUCKET_MAX_DIST = 2048
NORM_EPS = 1e-6
TINY = 1e-30
PAGE_SIZE = 128

LANES = 128
VMEM_LIMIT_BYTES = 56 * 1024 * 1024

NEG = -1e30
QK_SCALE = HEAD_DIM ** -0.5
TILE = 128
SEL_SHIFT = 6
SMALL_GATES = 3 * NSA_HEADS


def _cparams(*sem):
    return pltpu.CompilerParams(dimension_semantics=sem, vmem_limit_bytes=VMEM_LIMIT_BYTES)


def _bucket_thresholds():
    d = np.arange(0, 2 * BUCKET_MAX_DIST + 1)
    df = np.maximum(d, 1).astype(np.float64)
    ratio = math.log(BUCKET_MAX_DIST / BUCKET_EXACT)
    log_b = BUCKET_EXACT + (np.log(df / BUCKET_EXACT) / ratio * (N_BUCKETS - BUCKET_EXACT)).astype(np.int64)
    bucket = np.where(d < BUCKET_EXACT, d, np.clip(log_b, BUCKET_EXACT, N_BUCKETS - 1))
    return [int(np.argmax(bucket >= b)) for b in range(1, N_BUCKETS)]


BUCKET_THR = _bucket_thresholds()


def _bias_of_distance(d, tab_ref, h):
    val = jnp.full(d.shape, tab_ref[0, h], F32)
    for b in range(1, N_BUCKETS):
        val = jnp.where(d >= BUCKET_THR[b - 1], tab_ref[b, h], val)
    return val


def _toeplitz_kernel(tab_ref, o_ref, *, n_heads, mode, dil):
    r = lax.broadcasted_iota(jnp.int32, (TILE, TILE), 0)
    c = lax.broadcasted_iota(jnp.int32, (TILE, TILE), 1)
    idx = pl.program_id(0)
    if mode == "branch":
        d = idx * TILE + r - c
        ok = (idx < 2) & (d >= 0) & (d <= TILE)
    else:
        d = (idx - 1) * TILE + r - c
        ok = (idx > 0) & (d >= 0)
        if mode == "window":
            ok = ok & (d < NSA_WINDOW)
    dd = jnp.maximum(d, 0) * dil
    for h in range(n_heads):
        o_ref[h, 0] = jnp.where(ok, _bias_of_distance(dd, tab_ref, h), NEG)


def _toeplitz_table(rel_bias, n_heads, n_idx, mode, dil=1):
    return pl.pallas_call(
        functools.partial(_toeplitz_kernel, n_heads=n_heads, mode=mode, dil=dil),
        grid=(n_idx,),
        in_specs=[pl.BlockSpec(memory_space=pltpu.SMEM)],
        out_specs=pl.BlockSpec((n_heads, 1, TILE, TILE), lambda i: (0, i, 0, 0)),
        out_shape=jax.ShapeDtypeStruct((n_heads, n_idx, TILE, TILE), F32),
        compiler_params=_cparams("arbitrary"),
        name="bias_toeplitz_%s%d" % (mode, dil),
    )(rel_bias)


def _cmp_bias_kernel(tab_ref, o_ref, *, tq, n_cmp_pad, t_base):
    t = t_base + pl.program_id(0) * tq + lax.broadcasted_iota(jnp.int32, (tq, n_cmp_pad), 0)
    n = lax.broadcasted_iota(jnp.int32, (tq, n_cmp_pad), 1)
    d = jnp.maximum(t - (n * CMP_STRIDE + CMP_LEN - 1), 0)
    for h in range(NSA_HEADS):
        o_ref[h] = _bias_of_distance(d, tab_ref, h)


def _cmp_bias_table(rel_bias, n_rows, tq, n_cmp_pad, t_base):
    return pl.pallas_call(
        functools.partial(_cmp_bias_kernel, tq=tq, n_cmp_pad=n_cmp_pad, t_base=t_base),
        grid=(n_rows // tq,),
        in_specs=[pl.BlockSpec(memory_space=pltpu.SMEM)],
        out_specs=pl.BlockSpec((NSA_HEADS, tq, n_cmp_pad), lambda i: (0, i, 0)),
        out_shape=jax.ShapeDtypeStruct((NSA_HEADS, n_rows, n_cmp_pad), F32),
        compiler_params=_cparams("arbitrary"),
        name="bias_cmp",
    )(rel_bias)


def _affine_bias_kernel(tab_ref, o_ref, *, heads, d0, step, limit):
    n = o_ref.shape[1]
    d = d0 + step * lax.broadcasted_iota(jnp.int32, (1, n), 1)
    ok = (d >= 0) & (d < limit)
    dd = jnp.maximum(d, 0)
    for row, h in enumerate(heads):
        if h is None:
            o_ref[row:row + 1, :] = jnp.zeros((1, n), F32)
        else:
            o_ref[row:row + 1, :] = jnp.where(ok, _bias_of_distance(dd, tab_ref, h), NEG)


def _affine_bias(rel_bias, heads, n, d0, step, limit):
    return pl.pallas_call(
        functools.partial(_affine_bias_kernel, heads=tuple(heads), d0=d0, step=step, limit=limit),
        in_specs=[pl.BlockSpec(memory_space=pltpu.SMEM)],
        out_shape=jax.ShapeDtypeStruct((len(heads), n), F32),
        compiler_params=pltpu.CompilerParams(vmem_limit_bytes=VMEM_LIMIT_BYTES),
        name="bias_affine",
    )(rel_bias)


NSA_GROUPED_ROWS = tuple((NSA_GROUP * (row // 8) + row % 8) if row % 8 < NSA_GROUP else None
                         for row in range(8 * NSA_KV_HEADS))


def _ada_kernel(c_ref, w_ref, b_ref, o_ref):
    c = c_ref[...]
    s = (c * jax.nn.sigmoid(c)).astype(BF16)
    o_ref[0] = jnp.dot(s, w_ref[0].astype(BF16), preferred_element_type=F32) + b_ref[0]


def _ada_params(c_all, w_ada, b_ada):
    depth, d, d6 = w_ada.shape
    m = c_all.shape[0]
    return pl.pallas_call(
        _ada_kernel,
        grid=(depth, d6 // d),
        in_specs=[pl.BlockSpec((m, d), lambda l, j: (0, 0)),
                  pl.BlockSpec((1, d, d), lambda l, j: (l, 0, j)),
                  pl.BlockSpec((1, 1, d), lambda l, j: (l, 0, j))],
        out_specs=pl.BlockSpec((1, m, d), lambda l, j: (l, 0, j)),
        out_shape=jax.ShapeDtypeStruct((depth, m, d6), F32),
        compiler_params=_cparams("arbitrary", "arbitrary"),
        name="adaln",
    )(c_all, w_ada, b_ada.reshape(depth, 1, d6))


def _norm_mod(x, g, scale, shift):
    y = x * lax.rsqrt(jnp.mean(x * x, axis=-1, keepdims=True) + NORM_EPS)
    return (y * g) * (1.0 + scale) + shift


def _row_tile(s, want):
    return want if s % want == 0 else s


def _mod_spec(mod, tm):
    if mod.shape[1] == 1:
        return pl.BlockSpec((1, 1, mod.shape[2]), lambda b, i: (b, 0, 0))
    return pl.BlockSpec((1, tm, mod.shape[2]), lambda b, i: (b, i, 0))


def _proj_even_kernel(x_ref, sh_ref, sc_ref, g_ref, w_ref, sb_ref,
                      qn_ref, nsa4_ref, nsabf_ref, win_ref, qf_ref, fkv_ref, fkvbf_ref, small_ref):
    h = _norm_mod(x_ref[0], g_ref[...], sc_ref[0], sh_ref[0]).astype(BF16)
    z = jnp.dot(h, w_ref[...], preferred_element_type=F32)
    qn_ref[0] = (z[:, 0:512] * QK_SCALE).astype(BF16)
    nsa4_ref[0] = z[:, 512:1024]
    nsabf_ref[0] = z[:, 512:1280].astype(BF16)
    win_ref[0] = z[:, 1024:1280]
    qf_ref[0] = (z[:, 1280:1792] * QK_SCALE).astype(BF16)
    fkv_ref[0] = z[:, 1792:2816]
    fkvbf_ref[0] = z[:, 1792:2816].astype(BF16)
    zs = z[:, 2816:2944] + sb_ref[...]
    lane = lax.broadcasted_iota(jnp.int32, zs.shape, 1)
    sig = jax.nn.sigmoid(zs)
    lsg = jnp.minimum(zs, 0.0) - jnp.log1p(jnp.exp(-jnp.abs(zs)))
    small_ref[0] = jnp.where(lane < SMALL_GATES, sig, lsg)


def _proj_even(x, shift, scale, g, w, sb):
    bx, s, d = x.shape
    tm = _row_tile(s, 256)
    n = w.shape[1]
    widths = (512, 512, 768, 256, 512, 1024, 1024, 128)
    dtypes = (BF16, F32, BF16, F32, BF16, F32, BF16, F32)
    return pl.pallas_call(
        _proj_even_kernel,
        grid=(bx, s // tm),
        in_specs=[pl.BlockSpec((1, tm, d), lambda b, i: (b, i, 0)),
                  _mod_spec(shift, tm), _mod_spec(scale, tm),
                  pl.BlockSpec((1, d), lambda b, i: (0, 0)),
                  pl.BlockSpec((d, n), lambda b, i: (0, 0)),
                  pl.BlockSpec((1, LANES), lambda b, i: (0, 0))],
        out_specs=[pl.BlockSpec((1, tm, wd), lambda b, i: (b, i, 0)) for wd in widths],
        out_shape=[jax.ShapeDtypeStruct((bx, s, wd), dt) for wd, dt in zip(widths, dtypes)],
        compiler_params=_cparams("arbitrary", "arbitrary"),
        name="proj_even",
    )(x, shift, scale, g, w, sb)


def _proj_odd_kernel(x_ref, sh_ref, sc_ref, g_ref, w_ref, q_ref, kv_ref, kvbf_ref):
    h = _norm_mod(x_ref[0], g_ref[...], sc_ref[0], sh_ref[0]).astype(BF16)
    z = jnp.dot(h, w_ref[...], preferred_element_type=F32)
    q_ref[0] = (z[:, 0:1024] * QK_SCALE).astype(BF16)
    kv_ref[0] = z[:, 1024:3072]
    kvbf_ref[0] = z[:, 1024:3072].astype(BF16)


def _proj_odd(x, shift, scale, g, w):
    bx, s, d = x.shape
    tm = _row_tile(s, 256)
    n = w.shape[1]
    widths = (1024, 2048, 2048)
    dtypes = (BF16, F32, BF16)
    return pl.pallas_call(
        _proj_odd_kernel,
        grid=(bx, s // tm),
        in_specs=[pl.BlockSpec((1, tm, d), lambda b, i: (b, i, 0)),
                  _mod_spec(shift, tm), _mod_spec(scale, tm),
                  pl.BlockSpec((1, d), lambda b, i: (0, 0)),
                  pl.BlockSpec((d, n), lambda b, i: (0, 0))],
        out_specs=[pl.BlockSpec((1, tm, wd), lambda b, i: (b, i, 0)) for wd in widths],
        out_shape=[jax.ShapeDtypeStruct((bx, s, wd), dt) for wd, dt in zip(widths, dtypes)],
        compiler_params=_cparams("arbitrary", "arbitrary"),
        name="proj_odd",
    )(x, shift, scale, g, w)


def _post_kernel(oa_ref, ob_ref, w_ref, x_ref, gate_ref, g_ref, o_ref):
    half = oa_ref.shape[2]
    y = jnp.dot(oa_ref[0], w_ref[0:half, :], preferred_element_type=F32)
    y = y + jnp.dot(ob_ref[0], w_ref[half:, :], preferred_element_type=F32)
    yn = y * lax.rsqrt(jnp.mean(y * y, axis=-1, keepdims=True) + NORM_EPS) * g_ref[...]
    o_ref[0] = x_ref[0] + gate_ref[0] * yn


def _post(o_a, o_b, w_out, x, gate, g, col_a=0, col_b=0):
    bx, s, d = x.shape
    tm = _row_tile(s, 512)
    half = w_out.shape[0] // 2
    return pl.pallas_call(
        _post_kernel,
        grid=(bx, s // tm),
        in_specs=[pl.BlockSpec((1, tm, half), lambda b, i: (b, i, col_a)),
                  pl.BlockSpec((1, tm, half), lambda b, i: (b, i, col_b)),
                  pl.BlockSpec(w_out.shape, lambda b, i: (0, 0)),
                  pl.BlockSpec((1, tm, d), lambda b, i: (b, i, 0)),
                  _mod_spec(gate, tm),
                  pl.BlockSpec((1, d), lambda b, i: (0, 0))],
        out_specs=pl.BlockSpec((1, tm, d), lambda b, i: (b, i, 0)),
        out_shape=jax.ShapeDtypeStruct((bx, s, d), F32),
        compiler_params=_cparams("arbitrary", "arbitrary"),
        name="post",
    )(o_a, o_b, w_out, x, gate, g)


def _mlp_kernel(x_ref, sh_ref, sc_ref, gate_ref, g2_ref, g3_ref, w1_ref, w2_ref, o_ref, h_ref, acc_ref):
    j = pl.program_id(2)

    @pl.when(j == 0)
    def _():
        h_ref[...] = _norm_mod(x_ref[0], g2_ref[...], sc_ref[0], sh_ref[0]).astype(BF16)
        acc_ref[...] = jnp.zeros_like(acc_ref)

    a = jnp.maximum(jnp.dot(h_ref[...], w1_ref[...], preferred_element_type=F32), 0.0)
    acc_ref[...] += jnp.dot((a * a).astype(BF16), w2_ref[...], preferred_element_type=F32)

    @pl.when(j == pl.num_programs(2) - 1)
    def _():
        y = acc_ref[...]
        yn = y * lax.rsqrt(jnp.mean(y * y, axis=-1, keepdims=True) + NORM_EPS) * g3_ref[...]
        o_ref[0] = x_ref[0] + gate_ref[0] * yn


def _mlp(x, shift, scale, gate, g2, g3, w1, w2):
    bx, s, d = x.shape
    f = w1.shape[1]
    tm = _row_tile(s, 1024)
    tf = 1024

    def mod3(mod):
        if mod.shape[1] == 1:
            return pl.BlockSpec((1, 1, d), lambda b, i, j: (b, 0, 0))
        return pl.BlockSpec((1, tm, d), lambda b, i, j: (b, i, 0))

    return pl.pallas_call(
        _mlp_kernel,
        grid=(bx, s // tm, f // tf),
        in_specs=[pl.BlockSpec((1, tm, d), lambda b, i, j: (b, i, 0)),
                  mod3(shift), mod3(scale), mod3(gate),
                  pl.BlockSpec((1, d), lambda b, i, j: (0, 0)),
                  pl.BlockSpec((1, d), lambda b, i, j: (0, 0)),
                  pl.BlockSpec((d, tf), lambda b, i, j: (0, j)),
                  pl.BlockSpec((tf, d), lambda b, i, j: (j, 0))],
        out_specs=pl.BlockSpec((1, tm, d), lambda b, i, j: (b, i, 0)),
        out_shape=jax.ShapeDtypeStruct((bx, s, d), F32),
        scratch_shapes=[pltpu.VMEM((tm, d), BF16), pltpu.VMEM((tm, d), F32)],
        compiler_params=_cparams("arbitrary", "arbitrary", "arbitrary"),
        name="mlp",
    )(x, shift, scale, gate, g2, g3, w1, w2)


def _cumsum_kernel(x_ref, c_ref, ct_ref, carry_ref, *, tc):
    @pl.when(pl.program_id(1) == 0)
    def _():
        carry_ref[...] = jnp.zeros_like(carry_ref)

    r = lax.broadcasted_iota(jnp.int32, (tc, tc), 0)
    c = lax.broadcasted_iota(jnp.int32, (tc, tc), 1)
    tri = jnp.where(c <= r, 1.0, 0.0).astype(F32)
    cs = jnp.dot(tri, x_ref[0], preferred_element_type=F32, precision=lax.Precision.HIGHEST) + carry_ref[...]
    carry_ref[...] = cs[tc - 1:tc, :]
    c_ref[0] = cs
    ct_ref[0] = cs.T[SMALL_GATES:SMALL_GATES + FOX_HEADS, :]


def _cumsum(small):
    bx, s, _ = small.shape
    tc = _row_tile(s, 256)
    return pl.pallas_call(
        functools.partial(_cumsum_kernel, tc=tc),
        grid=(bx, s // tc),
        in_specs=[pl.BlockSpec((1, tc, LANES), lambda b, i: (b, i, 0))],
        out_specs=[pl.BlockSpec((1, tc, LANES), lambda b, i: (b, i, 0)),
                   pl.BlockSpec((1, FOX_HEADS, tc), lambda b, i: (b, 0, i))],
        out_shape=[jax.ShapeDtypeStruct((bx, s, LANES), F32),
                   jax.ShapeDtypeStruct((bx, FOX_HEADS, s), F32)],
        scratch_shapes=[pltpu.VMEM((1, LANES), F32)],
        compiler_params=_cparams("arbitrary", "arbitrary"),
        name="logf_cumsum",
    )(small)


def _nt_dot(a, b):
    return lax.dot_general(a, b, (((1,), (1,)), ((), ())), preferred_element_type=F32)


def _online_update(state, s, v):
    m, l, acc = state
    m_new = jnp.maximum(m, jnp.max(s, axis=1, keepdims=True))
    alpha = jnp.exp(m - m_new)
    p = jnp.exp(s - m_new)
    l = alpha * l + jnp.sum(p, axis=1, keepdims=True)
    acc = alpha * acc + jnp.dot(p.astype(BF16), v, preferred_element_type=F32)
    return m_new, l, acc


def _init_state(tq):
    return (jnp.full((tq, 1), NEG, F32), jnp.zeros((tq, 1), F32), jnp.zeros((tq, LANES), F32))


def _half_masks(tq):
    lane = lax.broadcasted_iota(jnp.int32, (tq, LANES), 1)
    return lane < HEAD_DIM


def _split_heads(q2, lo):
    zero = jnp.zeros_like(q2)
    return jnp.where(lo, q2, zero), jnp.where(lo, zero, q2)


def _fox_kernel(q_ref, k_ref, v_ref, c_ref, ct_ref, o_ref, *, tq, tk, n_pairs):
    grp = pl.program_id(1)
    i = pl.program_id(2)
    lo = _half_masks(tq)
    lane = lax.broadcasted_iota(jnp.int32, (tq, LANES), 1)
    cblk = c_ref[0]
    qs, cqs, heads = [], [], []
    for p in range(n_pairs):
        qs.extend(_split_heads(q_ref[0, :, p * LANES:(p + 1) * LANES], lo))
        for e in range(2):
            head = (grp * n_pairs + p) * 2 + e
            heads.append(head)
            cqs.append(jnp.sum(jnp.where(lane == SMALL_GATES + head, cblk, 0.0), axis=1, keepdims=True))

    def chunk(c, states, masked):
        off = pl.multiple_of(c * tk, tk)
        if masked:
            row = i * tq + lax.broadcasted_iota(jnp.int32, (tq, tk), 0)
            col = off + lax.broadcasted_iota(jnp.int32, (tq, tk), 1)
            ok = col <= row
        out = []
        for n in range(2 * n_pairs):
            p = n // 2
            k = k_ref[0, pl.ds(off, tk), p * LANES:(p + 1) * LANES]
            v = v_ref[0, pl.ds(off, tk), p * LANES:(p + 1) * LANES]
            ck = ct_ref[0, pl.ds(heads[n], 1), pl.ds(off, tk)]
            s = _nt_dot(qs[n], k) + cqs[n] - ck
            if masked:
                s = jnp.where(ok, s, NEG)
            out.append(_online_update(states[n], s, v))
        return tuple(out)

    n_full = (i * tq) // tk
    init = tuple(_init_state(tq) for _ in range(2 * n_pairs))
    states = lax.fori_loop(0, n_full, lambda c, st: chunk(c, st, False), init)
    states = chunk(n_full, states, True)
    outs = []
    for p in range(n_pairs):
        (_, la, acca), (_, lb, accb) = states[2 * p], states[2 * p + 1]
        outs.append(jnp.where(lo, acca / la, accb / lb))
    o_ref[0] = jnp.concatenate(outs, axis=1).astype(o_ref.dtype)


def _fox_prompt(qf, fkvbf, c, ct):
    bx, s, _ = qf.shape
    tq = _row_tile(s, 256)
    tk = _row_tile(s, 512)
    n_pairs = 2
    n_grp = FOX_HEADS // 2 // n_pairs
    wd = n_pairs * LANES
    return pl.pallas_call(
        functools.partial(_fox_kernel, tq=tq, tk=tk, n_pairs=n_pairs),
        grid=(bx, n_grp, s // tq),
        in_specs=[pl.BlockSpec((1, tq, wd), lambda b, p, i: (b, i, p)),
                  pl.BlockSpec((1, s, wd), lambda b, p, i: (b, 0, p)),
                  pl.BlockSpec((1, s, wd), lambda b, p, i: (b, 0, n_grp + p)),
                  pl.BlockSpec((1, tq, LANES), lambda b, p, i: (b, i, 0)),
                  pl.BlockSpec((1, FOX_HEADS, s), lambda b, p, i: (b, 0, 0))],
        out_specs=pl.BlockSpec((1, tq, wd), lambda b, p, i: (b, i, p)),
        out_shape=jax.ShapeDtypeStruct((bx, s, FOX_HEADS * HEAD_DIM), BF16),
        compiler_params=_cparams("arbitrary", "arbitrary", "arbitrary"),
        name="fox_prompt",
    )(qf, fkvbf, fkvbf, c, ct)


def _dil_branch_kernel(q_ref, kc_ref, kp_ref, vc_ref, vp_ref, tab_ref, o_ref, lse_ref, *, tq):
    i = pl.program_id(2)
    lo = _half_masks(TILE)
    lane = lax.broadcasted_iota(jnp.int32, (TILE, LANES), 1)
    first_prev = jnp.where(i == 0, 2, 1)
    for a in range(tq // TILE):
        rows = slice(a * TILE, (a + 1) * TILE)
        lse_tile = jnp.zeros((TILE, LANES), F32)
        outs = []
        for p in range(DIL_HEADS // 2):
            cols = slice(p * LANES, (p + 1) * LANES)
            if a == 0:
                k_prev, v_prev, prev_idx = kp_ref[0, :, cols], vp_ref[0, :, cols], first_prev
            else:
                prev = slice((a - 1) * TILE, a * TILE)
                k_prev, v_prev, prev_idx = kc_ref[0, prev, cols], vc_ref[0, prev, cols], 1
            k2 = jnp.concatenate([k_prev, kc_ref[0, rows, cols]], axis=0)
            v2 = jnp.concatenate([v_prev, vc_ref[0, rows, cols]], axis=0)
            pair = []
            for e, qh in enumerate(_split_heads(q_ref[0, rows, cols], lo)):
                h = 2 * p + e
                bias = jnp.concatenate([tab_ref[h, prev_idx], tab_ref[h, 0]], axis=1)
                s = _nt_dot(qh, k2) + bias
                m = jnp.max(s, axis=1, keepdims=True)
                e_s = jnp.exp(s - m)
                l = jnp.sum(e_s, axis=1, keepdims=True)
                pair.append(jnp.dot(e_s.astype(BF16), v2, preferred_element_type=F32) / l)
                lse_tile = jnp.where(lane == h, jnp.log(l) + m, lse_tile)
            outs.append(jnp.where(lo, pair[0], pair[1]))
        o_ref[0, rows, :] = jnp.concatenate(outs, axis=1).astype(o_ref.dtype)
        lse_ref[0, rows, :] = lse_tile


def _dil_branch_prompt(q, kvbf, table, dil):
    bx, s, width = q.shape
    n_rows = s // dil
    tq = _row_tile(n_rows, 256)
    sub = tq // TILE
    qv = q.reshape(bx, n_rows, dil * width)
    kvv = kvbf.reshape(bx, n_rows, dil * 2 * width)
    o, lse = pl.pallas_call(
        functools.partial(_dil_branch_kernel, tq=tq),
        grid=(bx, dil, n_rows // tq),
        in_specs=[pl.BlockSpec((1, tq, width), lambda b, r, i: (b, i, r)),
                  pl.BlockSpec((1, tq, width), lambda b, r, i: (b, i, 2 * r)),
                  pl.BlockSpec((1, TILE, width), lambda b, r, i: (b, jnp.maximum(sub * i - 1, 0), 2 * r)),
                  pl.BlockSpec((1, tq, width), lambda b, r, i: (b, i, 2 * r + 1)),
                  pl.BlockSpec((1, TILE, width), lambda b, r, i: (b, jnp.maximum(sub * i - 1, 0), 2 * r + 1)),
                  pl.BlockSpec(table.shape, lambda b, r, i: (0, 0, 0, 0))],
        out_specs=[pl.BlockSpec((1, tq, width), lambda b, r, i: (b, i, r)),
                   pl.BlockSpec((1, tq, LANES), lambda b, r, i: (b, i, r))],
        out_shape=[jax.ShapeDtypeStruct((bx, n_rows, dil * width), BF16),
                   jax.ShapeDtypeStruct((bx, n_rows, dil * LANES), F32)],
        compiler_params=_cparams("arbitrary", "arbitrary", "arbitrary"),
        name="dilated_branch_prompt",
    )(qv, kvv, kvv, kvv, kvv, table)
    return o.reshape(bx, s, width), lse.reshape(bx, s, LANES)


def _post_dil_kernel(o1_ref, o2_ref, o3_ref, l1_ref, l2_ref, l3_ref, w_ref, x_ref, gate_ref, g_ref, o_ref):
    lses = [r[0] for r in (l1_ref, l2_ref, l3_ref)]
    m = jnp.maximum(jnp.maximum(lses[0], lses[1]), lses[2])
    es = [jnp.exp(l - m) for l in lses]
    tot = es[0] + es[1] + es[2]
    width = o1_ref.shape[2]
    head_of_col = lax.shift_right_arithmetic(lax.broadcasted_iota(jnp.int32, (LANES, width), 1), SEL_SHIFT)
    expand = jnp.where(head_of_col == lax.broadcasted_iota(jnp.int32, (LANES, width), 0), 1.0, 0.0).astype(BF16)
    mix = jnp.zeros((o1_ref.shape[1], width), F32)
    for e, o_ref_j in zip(es, (o1_ref, o2_ref, o3_ref)):
        alpha = jnp.dot((e / tot).astype(BF16), expand, preferred_element_type=F32)
        mix = mix + alpha * o_ref_j[0].astype(F32)
    y = jnp.dot(mix.astype(BF16), w_ref[...], preferred_element_type=F32)
    yn = y * lax.rsqrt(jnp.mean(y * y, axis=-1, keepdims=True) + NORM_EPS) * g_ref[...]
    o_ref[0] = x_ref[0] + gate_ref[0] * yn


def _post_dil(outs, lses, w_out, x, gate, g):
    bx, s, d = x.shape
    tm = _row_tile(s, 256)
    width = w_out.shape[0]
    return pl.pallas_call(
        _post_dil_kernel,
        grid=(bx, s // tm),
        in_specs=([pl.BlockSpec((1, tm, width), lambda b, i: (b, i, 0))] * 3
                  + [pl.BlockSpec((1, tm, LANES), lambda b, i: (b, i, 0))] * 3
                  + [pl.BlockSpec(w_out.shape, lambda b, i: (0, 0)),
                     pl.BlockSpec((1, tm, d), lambda b, i: (b, i, 0)),
                     _mod_spec(gate, tm),
                     pl.BlockSpec((1, d), lambda b, i: (0, 0))]),
        out_specs=pl.BlockSpec((1, tm, d), lambda b, i: (b, i, 0)),
        out_shape=jax.ShapeDtypeStruct((bx, s, d), F32),
        compiler_params=_cparams("arbitrary", "arbitrary"),
        name="post_dilated",
    )(*outs, *lses, w_out, x, gate, g)


def _compress_kernel(pt_ref, page_ref, w1_ref, w2_ref, pe_ref, kc_ref, vc_ref, rows_ref, chunk_ref, *, n_pages):
    j = pl.program_id(1)
    row0 = pl.multiple_of(j * PAGE_SIZE, PAGE_SIZE)
    rows_ref[0, pl.ds(row0, PAGE_SIZE), :] = page_ref[0, :, 0:LANES]
    rows_ref[1, pl.ds(row0, PAGE_SIZE), :] = page_ref[0, :, LANES:2 * LANES]

    @pl.when(j == n_pages - 1)
    def _():
        n_chunks = n_pages * PAGE_SIZE // CMP_STRIDE
        half = CMP_STRIDE * HEAD_DIM
        for kv, out_ref in ((0, kc_ref), (1, vc_ref)):
            w1 = w1_ref[kv]
            pe_a = jnp.broadcast_to(pe_ref[kv, :, 0:half], (8, half)).astype(BF16)
            pe_b = jnp.broadcast_to(pe_ref[kv, :, half:], (8, half)).astype(BF16)
            pe_term = (jnp.dot(pe_a, w1, preferred_element_type=F32)[0:1, 0:CMP_HIDDEN]
                       + jnp.dot(pe_b, w1, preferred_element_type=F32)[0:1, CMP_HIDDEN:])
            for l in range(CMP_STRIDE):
                both = rows_ref[kv, pl.ds(l, n_chunks, stride=CMP_STRIDE), :].astype(BF16)
                for g in range(NSA_KV_HEADS):
                    chunk_ref[g, :, l * HEAD_DIM:(l + 1) * HEAD_DIM] = both[:, g * HEAD_DIM:(g + 1) * HEAD_DIM]
            outs = []
            for g in range(NSA_KV_HEADS):
                uv = jnp.dot(chunk_ref[g], w1, preferred_element_type=F32)
                pre = uv[:, 0:CMP_HIDDEN] + pltpu.roll(uv[:, CMP_HIDDEN:], n_chunks - 1, 0) + pe_term
                hid = jax.nn.gelu(pre).astype(BF16)
                outs.append(jnp.dot(hid, w2_ref[kv], preferred_element_type=F32))
            out_ref[0] = jnp.concatenate(outs, axis=1)


def _compress(pool, page_table, w1cat, w2, pe):
    n_req, n_pages = page_table.shape
    n_chunks = n_pages * PAGE_SIZE // CMP_STRIDE
    width = 2 * NSA_KV_HEADS * HEAD_DIM
    grid_spec = pltpu.PrefetchScalarGridSpec(
        num_scalar_prefetch=1,
        grid=(n_req, n_pages),
        in_specs=[pl.BlockSpec((1, PAGE_SIZE, width), lambda r, j, pt: (pt[r, j], 0, 0)),
                  pl.BlockSpec(w1cat.shape, lambda r, j, pt: (0, 0, 0)),
                  pl.BlockSpec(w2.shape, lambda r, j, pt: (0, 0, 0)),
                  pl.BlockSpec(pe.shape, lambda r, j, pt: (0, 0, 0))],
        out_specs=[pl.BlockSpec((1, n_chunks, LANES), lambda r, j, pt: (r, 0, 0)),
                   pl.BlockSpec((1, n_chunks, LANES), lambda r, j, pt: (r, 0, 0))],
        scratch_shapes=[pltpu.VMEM((2, n_pages * PAGE_SIZE, LANES), F32),
                        pltpu.VMEM((NSA_KV_HEADS, n_chunks, CMP_STRIDE * HEAD_DIM), BF16)],
    )
    return pl.pallas_call(
        functools.partial(_compress_kernel, n_pages=n_pages),
        grid_spec=grid_spec,
        out_shape=[jax.ShapeDtypeStruct((n_req, n_chunks, LANES), F32)] * 2,
        compiler_params=_cparams("arbitrary", "arbitrary"),
        name="nsa_compress",
    )(page_table, pool, w1cat, w2, pe)


def _top_k_mask(imp, n_top):
    lane = lax.broadcasted_iota(jnp.int32, imp.shape, 1)
    width = imp.shape[1]

    def body(_, carry):
        imp, sel = carry
        m = jnp.max(imp, axis=1, keepdims=True)
        first = jnp.min(jnp.where(imp == m, lane, width), axis=1, keepdims=True)
        pick = (lane == first) & (m > -jnp.inf)
        return jnp.where(lane == first, -jnp.inf, imp), jnp.where(pick, 1.0, sel)

    _, sel = lax.fori_loop(0, n_top, body, (imp, jnp.zeros(imp.shape, F32)))
    return sel


def _nsa_kernel(qn_ref, kc_ref, vc_ref, bcmp_ref, sel_ref, win_ref, tsel_ref, twin_ref, small_ref, o_ref,
                *, tq, tk, n_sel_delta, n_win_tiles):
    i = pl.program_id(1)
    t0 = i * tq
    n_cmp_pad = kc_ref.shape[1]
    n_blk = LANES
    lo = _half_masks(tq)
    lane = lax.broadcasted_iota(jnp.int32, (tq, LANES), 1)
    gates = small_ref[0]
    q_all = qn_ref[0].astype(F32)

    t_c = t0 + lax.broadcasted_iota(jnp.int32, (tq, n_cmp_pad), 0)
    n_c = lax.broadcasted_iota(jnp.int32, (tq, n_cmp_pad), 1)
    ok_c = t_c >= n_c * CMP_STRIDE + (CMP_LEN - 1)
    ci = lax.broadcasted_iota(jnp.int32, (n_cmp_pad, n_blk), 0) * CMP_STRIDE
    sj = lax.broadcasted_iota(jnp.int32, (n_cmp_pad, n_blk), 1) * SEL_BLOCK
    cover = jnp.where((ci < sj + SEL_BLOCK) & (ci + CMP_LEN > sj), 1.0, 0.0).astype(F32)
    t_b = t0 + lax.broadcasted_iota(jnp.int32, (tq, n_blk), 0)
    cur = lax.shift_right_arithmetic(t_b, SEL_SHIFT)
    forced = (lane == 0) | (lane == cur) | (lane == cur - 1)
    blk_of_key = lax.shift_right_arithmetic(lax.broadcasted_iota(jnp.int32, (n_blk, tk), 1), SEL_SHIFT)
    blk_row = lax.broadcasted_iota(jnp.int32, (n_blk, tk), 0)
    blk_delta = blk_row - blk_of_key
    sub = tk // TILE

    n_sub = tq // TILE
    group_of = [h // NSA_GROUP for h in range(NSA_HEADS)]

    qs = []
    for h in range(NSA_HEADS):
        g = group_of[h]
        blk = q_all[:, LANES * (h // 2):LANES * (h // 2 + 1)]
        if h % 2 != g:
            blk = pltpu.roll(blk, HEAD_DIM, 1)
        qs.append(jnp.where(lo if g == 0 else jnp.logical_not(lo), blk, 0.0).astype(BF16))

    kcb = kc_ref[0].astype(BF16)
    vcb = vc_ref[0].astype(BF16)
    o_cmp, imps = [], []
    for g in range(NSA_KV_HEADS):
        psum = jnp.zeros((tq, n_cmp_pad), F32)
        for r in range(NSA_GROUP):
            h = g * NSA_GROUP + r
            s = jnp.where(ok_c, _nt_dot(qs[h], kcb) + bcmp_ref[h], NEG)
            m = jnp.max(s, axis=1, keepdims=True)
            e = jnp.where(ok_c, jnp.exp(s - m), 0.0)
            p = e / jnp.maximum(jnp.sum(e, axis=1, keepdims=True), TINY)
            psum = psum + p
            o_cmp.append(jnp.dot(p.astype(BF16), vcb, preferred_element_type=F32))
        imp = jnp.dot(psum, cover, preferred_element_type=F32, precision=lax.Precision.HIGHEST)
        imp = jnp.where(forced, FORCED_SCORE, imp)
        imps.append(jnp.where(lane <= cur, imp, -jnp.inf))

    sel = _top_k_mask(jnp.concatenate(imps, axis=0), min(SEL_TOPK, n_blk))
    not_sel = [(1.0 - sel[g * tq:(g + 1) * tq]).astype(BF16) for g in range(NSA_KV_HEADS)]

    def sel_chunk(c, states):
        off = pl.multiple_of(c * tk, tk)
        k = sel_ref[0, pl.ds(off, tk), 0:LANES]
        v = sel_ref[0, pl.ds(off, tk), LANES:2 * LANES]
        expand = jnp.where(blk_delta == c * (tk // SEL_BLOCK), NEG, 0.0).astype(BF16)
        mask_add = [jnp.dot(ns, expand, preferred_element_type=F32) for ns in not_sel]
        out = []
        for h in range(NSA_HEADS):
            bias = jnp.concatenate([jnp.concatenate(
                [tsel_ref[h, jnp.clip(n_sub * i + a - (c * sub + u), -1, n_sel_delta - 1) + 1] for u in range(sub)],
                axis=1) for a in range(n_sub)], axis=0)
            out.append(_online_update(states[h], _nt_dot(qs[h], k) + bias + mask_add[group_of[h]], v))
        return tuple(out)

    n_chunks = (t0 + tq - 1) // tk + 1
    sel_states = lax.fori_loop(0, n_chunks, sel_chunk, tuple(_init_state(tq) for _ in range(NSA_HEADS)))

    o_win = [[] for _ in range(NSA_HEADS)]
    for a in range(n_sub):
        j0 = n_sub * i + a
        ks, vs, idxs = [], [], []
        for u in range(n_win_tiles - 1, -1, -1):
            off = pl.multiple_of(jnp.maximum(j0 - u, 0) * TILE, TILE)
            ks.append(win_ref[0, pl.ds(off, TILE), 0:LANES])
            vs.append(win_ref[0, pl.ds(off, TILE), LANES:2 * LANES])
            idxs.append(jnp.where(j0 - u >= 0, u + 1, 0))
        k = jnp.concatenate(ks, axis=0)
        v = jnp.concatenate(vs, axis=0)
        for h in range(NSA_HEADS):
            bias = jnp.concatenate([twin_ref[h, ix] for ix in idxs], axis=1)
            s = _nt_dot(qs[h][a * TILE:(a + 1) * TILE], k) + bias
            e = jnp.exp(s - jnp.max(s, axis=1, keepdims=True))
            l = jnp.sum(e, axis=1, keepdims=True)
            o_win[h].append(jnp.dot(e.astype(BF16), v, preferred_element_type=F32) / l)

    pair_out = [None] * (NSA_HEADS // 2)
    for h in range(NSA_HEADS):
        gc, gs, gw = (jnp.sum(jnp.where(lane == 3 * h + b, gates, 0.0), axis=1, keepdims=True) for b in range(3))
        _, l_s, acc_s = sel_states[h]
        o = gc * o_cmp[h] + gs * (acc_s / l_s) + gw * jnp.concatenate(o_win[h], axis=0)
        if h % 2 != group_of[h]:
            o = pltpu.roll(o, HEAD_DIM, 1)
        prev = pair_out[h // 2]
        keep = lo if h % 2 == 0 else jnp.logical_not(lo)
        pair_out[h // 2] = jnp.where(keep, o, 0.0 if prev is None else prev)

    o_ref[0] = jnp.concatenate(pair_out, axis=1).astype(o_ref.dtype)


def _nsa_prompt(qn, kc, vc, bcmp, nsabf, tsel, twin, small):
    bx, s, _ = qn.shape
    tq = _row_tile(s, 256)
    tk = _row_tile(s, 512)
    n_cmp_pad = kc.shape[1]
    once = pl.Buffered(1)
    return pl.pallas_call(
        functools.partial(_nsa_kernel, tq=tq, tk=tk, n_sel_delta=tsel.shape[1] - 1, n_win_tiles=twin.shape[1] - 1),
        grid=(bx, s // tq),
        in_specs=[pl.BlockSpec((1, tq, NSA_HEADS * HEAD_DIM), lambda b, i: (b, i, 0)),
                  pl.BlockSpec((1, n_cmp_pad, LANES), lambda b, i: (b, 0, 0)),
                  pl.BlockSpec((1, n_cmp_pad, LANES), lambda b, i: (b, 0, 0)),
                  pl.BlockSpec((NSA_HEADS, tq, n_cmp_pad), lambda b, i: (0, i, 0)),
                  pl.BlockSpec((1, s, 2 * LANES), lambda b, i: (b, 0, 1)),
                  pl.BlockSpec((1, s, 2 * LANES), lambda b, i: (b, 0, 2)),
                  pl.BlockSpec(tsel.shape, lambda b, i: (0, 0, 0, 0), pipeline_mode=once),
                  pl.BlockSpec(twin.shape, lambda b, i: (0, 0, 0, 0), pipeline_mode=once),
                  pl.BlockSpec((1, tq, LANES), lambda b, i: (b, i, 0))],
        out_specs=pl.BlockSpec((1, tq, NSA_HEADS * HEAD_DIM), lambda b, i: (b, i, 0)),
        out_shape=jax.ShapeDtypeStruct((bx, s, NSA_HEADS * HEAD_DIM), BF16),
        compiler_params=_cparams("arbitrary", "arbitrary"),
        name="nsa_prompt",
    )(qn, kc, vc, bcmp, nsabf, nsabf, tsel, twin, small)


def _bf16_round(x):
    return x.astype(BF16).astype(F32)


def _dot3(z, w):
    hi = z.astype(BF16)
    rest = z - hi.astype(F32)
    mid = rest.astype(BF16)
    lo = (rest - mid.astype(F32)).astype(BF16)
    return sum(jnp.dot(part, w, preferred_element_type=F32) for part in (hi, mid, lo))


def _lane_sum_bcast(z):
    return _dot3(z, jnp.ones((LANES, LANES), BF16))


def _pad_lanes(x):
    return jnp.concatenate([x, jnp.zeros_like(x)], axis=-1)


def _diag_scatter(g, n_rows):
    shape = (n_rows,) + g.shape
    row = lax.broadcasted_iota(jnp.int32, shape, 0)
    lane = lax.broadcasted_iota(jnp.int32, shape, 2)
    return jnp.where(row == lane, jnp.broadcast_to(g[None], shape), 0.0)


def _fox_sample_kernel(pt_ref, page_ref, lft_ref, q_ref, new_ref, lfn_ref, o_ref,
                       m_ref, l_ref, acc_ref, carry_ref, *, n_pages):
    j = pl.program_id(1)
    q = q_ref[0]

    @pl.when(j == 0)
    def _():
        m_ref[...] = _lane_sum_bcast(_pad_lanes(q * _bf16_round(new_ref[0, 0])))
        l_ref[...] = jnp.ones_like(l_ref)
        acc_ref[...] = _bf16_round(new_ref[0, 1])
        carry_ref[...] = lfn_ref[0]

    k = _bf16_round(page_ref[0, 0, :, 0])
    v = _bf16_round(page_ref[0, 0, :, 1])
    lf = lft_ref[0]
    u = lax.broadcasted_iota(jnp.int32, (PAGE_SIZE, PAGE_SIZE), 0)
    c = lax.broadcasted_iota(jnp.int32, (PAGE_SIZE, PAGE_SIZE), 1)
    later = jnp.where(u > c, 1.0, 0.0).astype(BF16)
    carry = carry_ref[...]
    bias = carry + _dot3(lf, later)
    z = _pad_lanes(k * q[None]) + _diag_scatter(bias, PAGE_SIZE)
    s = _lane_sum_bcast(z.reshape(PAGE_SIZE * FOX_HEADS, LANES)).reshape(PAGE_SIZE, FOX_HEADS, LANES)
    m_old = m_ref[...]
    m_new = jnp.maximum(m_old, jnp.max(s, axis=0))
    alpha = jnp.exp(m_old - m_new)
    p = jnp.exp(s - m_new[None])
    l_ref[...] = alpha * l_ref[...] + jnp.sum(p, axis=0)
    acc_ref[...] = alpha[:, 0:HEAD_DIM] * acc_ref[...] + jnp.sum(_bf16_round(p)[:, :, 0:HEAD_DIM] * v, axis=0)
    m_ref[...] = m_new
    carry_ref[...] = carry + _lane_sum_bcast(lf)

    @pl.when(j == n_pages - 1)
    def _():
        o_ref[0] = acc_ref[...] / l_ref[:, 0:HEAD_DIM]


def _fox_sample(cache, layer, logf_t, page_table, q, new_kv, lf_new):
    n_req, n_pages = page_table.shape
    page = lambda r, j, pt: pt[r, n_pages - 1 - j]
    grid_spec = pltpu.PrefetchScalarGridSpec(
        num_scalar_prefetch=1,
        grid=(n_req, n_pages),
        in_specs=[pl.BlockSpec((1, 1, PAGE_SIZE, 2, FOX_HEADS, HEAD_DIM),
                               lambda r, j, pt: (layer, page(r, j, pt), 0, 0, 0, 0)),
                  pl.BlockSpec((1, FOX_HEADS, PAGE_SIZE), lambda r, j, pt: (page(r, j, pt), 0, 0)),
                  pl.BlockSpec((1, FOX_HEADS, HEAD_DIM), lambda r, j, pt: (r, 0, 0)),
                  pl.BlockSpec((1, 2, FOX_HEADS, HEAD_DIM), lambda r, j, pt: (r, 0, 0, 0)),
                  pl.BlockSpec((1, FOX_HEADS, LANES), lambda r, j, pt: (r, 0, 0))],
        out_specs=pl.BlockSpec((1, FOX_HEADS, HEAD_DIM), lambda r, j, pt: (r, 0, 0)),
        scratch_shapes=[pltpu.VMEM((FOX_HEADS, LANES), F32), pltpu.VMEM((FOX_HEADS, LANES), F32),
                        pltpu.VMEM((FOX_HEADS, HEAD_DIM), F32), pltpu.VMEM((FOX_HEADS, LANES), F32)],
    )
    return pl.pallas_call(
        functools.partial(_fox_sample_kernel, n_pages=n_pages),
        grid_spec=grid_spec,
        out_shape=jax.ShapeDtypeStruct((n_req, FOX_HEADS, HEAD_DIM), F32),
        compiler_params=_cparams("arbitrary", "arbitrary"),
        name="fox_sample",
    )(page_table, cache, logf_t, q, new_kv, lf_new)


def _dil_sample_kernel(q_ref, new_ref, b1_ref, b2_ref, b3_ref, tab_ref, o_ref):
    q = q_ref[0]
    k_new = _bf16_round(new_ref[0, 0])
    v_new = _bf16_round(new_ref[0, 1])
    qk_new = _pad_lanes(q * k_new)
    lane = lax.broadcasted_iota(jnp.int32, (DIL_HEADS, LANES), 1)
    outs, lses = [], []
    for br, buf in enumerate((b1_ref, b2_ref, b3_ref)):
        k = _bf16_round(buf[0, 0, :, 0, 0])
        v = _bf16_round(buf[0, 0, :, 0, 1])
        z = _pad_lanes(k * q[None]) + _diag_scatter(tab_ref[br, :, 0:LANES], TILE)
        s = _lane_sum_bcast(z.reshape(TILE * DIL_HEADS, LANES)).reshape(TILE, DIL_HEADS, LANES)
        s_new = _lane_sum_bcast(qk_new + jnp.where(lane == 0, tab_ref[br, :, LANES:2 * LANES], 0.0))
        m = jnp.maximum(jnp.max(s, axis=0), s_new)
        p = jnp.exp(s - m[None])
        p_new = jnp.exp(s_new - m)
        l = jnp.sum(p, axis=0) + p_new
        o = jnp.sum(_bf16_round(p)[:, :, 0:HEAD_DIM] * v, axis=0) + _bf16_round(p_new)[:, 0:HEAD_DIM] * v_new
        outs.append(o / l[:, 0:HEAD_DIM])
        lses.append(jnp.log(l) + m)
    top = jnp.maximum(jnp.maximum(lses[0], lses[1]), lses[2])
    es = [jnp.exp(x - top) for x in lses]
    tot = es[0] + es[1] + es[2]
    o_ref[0] = sum(_bf16_round(e / tot)[:, 0:HEAD_DIM] * _bf16_round(o) for e, o in zip(es, outs))


def _dil_sample(state, layer, q, new_kv, table):
    n_layers, n_req, n_buf = state.shape[:3]
    tile = (2, DIL_HEADS, HEAD_DIM)
    views, specs = [], []
    for _, dil in DIL_BRANCHES:
        n_rows = n_buf // dil
        assert n_rows % TILE == 0
        views.append(state.reshape(n_layers, n_req, n_rows, dil, *tile))
        specs.append(pl.BlockSpec((1, 1, TILE, 1) + tile,
                                  functools.partial(lambda r, blk: (layer, r, blk, 0, 0, 0, 0), blk=n_rows // TILE - 1)))
    return pl.pallas_call(
        _dil_sample_kernel,
        grid=(n_req,),
        in_specs=[pl.BlockSpec((1, DIL_HEADS, HEAD_DIM), lambda r: (r, 0, 0)),
                  pl.BlockSpec((1,) + tile, lambda r: (r, 0, 0, 0))] + specs
                 + [pl.BlockSpec(table.shape, lambda r: (0, 0, 0))],
        out_specs=pl.BlockSpec((1, DIL_HEADS, HEAD_DIM), lambda r: (r, 0, 0)),
        out_shape=jax.ShapeDtypeStruct((n_req, DIL_HEADS, HEAD_DIM), F32),
        compiler_params=_cparams("arbitrary"),
        name="dilated_sample",
    )(q, new_kv, *views, table)


def _group_queries(q_row):
    row8 = lax.broadcasted_iota(jnp.int32, (8, LANES), 0)
    lo = lax.broadcasted_iota(jnp.int32, (8, LANES), 1) < HEAD_DIM
    out = []
    for g in range(NSA_KV_HEADS):
        qg = jnp.zeros((8, LANES), F32)
        for r in range(NSA_GROUP):
            h = g * NSA_GROUP + r
            blk = jnp.broadcast_to(q_row[:, LANES * (h // 2):LANES * (h // 2 + 1)], (8, LANES))
            if h % 2 != g:
                blk = pltpu.roll(blk, HEAD_DIM, 1)
            qg = jnp.where((row8 == r) & (lo if g == 0 else jnp.logical_not(lo)), blk, qg)
        out.append(qg)
    return out


def _nsa_select_kernel(q_ref, kc_ref, vc_ref, bcmp_ref, oc_ref, idx_ref, *, n_blk_pad, cur):
    n_cmp_pad = kc_ref.shape[1]
    kcb = kc_ref[0].astype(BF16)
    vcb = vc_ref[0].astype(BF16)
    row8 = lax.broadcasted_iota(jnp.int32, (8, n_cmp_pad), 0)
    ci = lax.broadcasted_iota(jnp.int32, (n_cmp_pad, n_blk_pad), 0) * CMP_STRIDE
    sj = lax.broadcasted_iota(jnp.int32, (n_cmp_pad, n_blk_pad), 1) * SEL_BLOCK
    cover = jnp.where((ci < sj + SEL_BLOCK) & (ci + CMP_LEN > sj), 1.0, 0.0).astype(F32)
    blk = lax.broadcasted_iota(jnp.int32, (8, n_blk_pad), 1)
    forced = (blk == 0) | (blk == cur) | (blk == cur - 1)
    lane = lax.broadcasted_iota(jnp.int32, (8, LANES), 1)
    for g, qg in enumerate(_group_queries(q_ref[0].astype(F32))):
        bias = bcmp_ref[8 * g:8 * g + 8, :]
        ok = (bias > 0.5 * NEG) & (row8 < NSA_GROUP)
        s = jnp.where(ok, _nt_dot(qg.astype(BF16), kcb) + bias, NEG)
        m = jnp.max(s, axis=1, keepdims=True)
        e = jnp.where(ok, jnp.exp(s - m), 0.0)
        p = e / jnp.maximum(jnp.sum(e, axis=1, keepdims=True), TINY)
        oc_ref[0, g] = jnp.dot(p.astype(BF16), vcb, preferred_element_type=F32)
        psum = jnp.broadcast_to(jnp.sum(p, axis=0, keepdims=True), (8, n_cmp_pad))
        imp = jnp.dot(psum, cover, preferred_element_type=F32, precision=lax.Precision.HIGHEST)
        imp = jnp.where(forced, FORCED_SCORE, imp)
        imp = jnp.where(blk <= cur, imp, -jnp.inf)

        def body(it, carry):
            imp, idx = carry
            top = jnp.max(imp, axis=1, keepdims=True)
            first = jnp.min(jnp.where(imp == top, blk, n_blk_pad), axis=1, keepdims=True)
            idx = jnp.where(lane == it, jnp.where(top > -jnp.inf, first, -1), idx)
            return jnp.where(blk == first, -jnp.inf, imp), idx

        _, idx = lax.fori_loop(0, SEL_TOPK, body, (imp, jnp.full((8, LANES), -1, jnp.int32)))
        idx_ref[0, g] = idx


def _nsa_select(q, kc, vc, bcmp_row, n_blk_pad, cur):
    n_req = q.shape[0]
    n_cmp_pad = kc.shape[1]
    return pl.pallas_call(
        functools.partial(_nsa_select_kernel, n_blk_pad=n_blk_pad, cur=cur),
        grid=(n_req,),
        in_specs=[pl.BlockSpec((1, 1, NSA_HEADS * HEAD_DIM), lambda r: (r, 0, 0)),
                  pl.BlockSpec((1, n_cmp_pad, LANES), lambda r: (r, 0, 0)),
                  pl.BlockSpec((1, n_cmp_pad, LANES), lambda r: (r, 0, 0)),
                  pl.BlockSpec(bcmp_row.shape, lambda r: (0, 0))],
        out_specs=[pl.BlockSpec((1, NSA_KV_HEADS, 8, LANES), lambda r: (r, 0, 0, 0)),
                   pl.BlockSpec((1, NSA_KV_HEADS, 8, LANES), lambda r: (r, 0, 0, 0))],
        out_shape=[jax.ShapeDtypeStruct((n_req, NSA_KV_HEADS, 8, LANES), F32),
                   jax.ShapeDtypeStruct((n_req, NSA_KV_HEADS, 8, LANES), jnp.int32)],
        compiler_params=_cparams("arbitrary"),
        name="nsa_sample_select",
    )(q, kc, vc, bcmp_row)


def _nsa_attend_kernel(idx_ref, pt_ref, q_ref, blk0_ref, blk1_ref, fsel_ref, wbuf_ref, fwin_ref, nsel_ref, nwin_ref,
                       oc_ref, small_ref, o_ref, m_ref, l_ref, acc_ref, ow_ref, *, n_past_blk, new_lane, n_win):
    r_idx = pl.program_id(0)
    k_idx = pl.program_id(1)
    qgs = _group_queries(q_ref[0].astype(F32))
    row8 = lax.broadcasted_iota(jnp.int32, (8, LANES), 0)

    @pl.when(k_idx == 0)
    def _():
        ksel_new = _bf16_round(nsel_ref[0][:, 2 * LANES:3 * LANES])
        vsel_new = _bf16_round(nsel_ref[0][:, 3 * LANES:4 * LANES])
        kwin_new = _bf16_round(nwin_ref[0][:, 0:LANES])
        vwin_new = _bf16_round(nwin_ref[0][:, LANES:2 * LANES])
        kw = wbuf_ref[0][:, 0:LANES].astype(BF16)
        vw = wbuf_ref[0][:, LANES:2 * LANES].astype(BF16)
        for g, qg in enumerate(qgs):
            rows = slice(8 * g, 8 * g + 8)
            s_new = jnp.sum(qg * ksel_new, axis=1, keepdims=True) + fsel_ref[n_past_blk, rows, new_lane:new_lane + 1]
            m_ref[g] = jnp.broadcast_to(s_new, (8, LANES))
            l_ref[g] = jnp.ones((8, LANES), F32)
            acc_ref[g] = jnp.broadcast_to(vsel_new, (8, LANES))
            s = _nt_dot(qg.astype(BF16), kw) + fwin_ref[rows, 0:n_win]
            s_wn = jnp.sum(qg * kwin_new, axis=1, keepdims=True) + fwin_ref[rows, n_win:n_win + 1]
            m = jnp.maximum(jnp.max(s, axis=1, keepdims=True), s_wn)
            p = jnp.exp(s - m)
            p_new = jnp.exp(s_wn - m)
            l = jnp.sum(p, axis=1, keepdims=True) + p_new
            ow_ref[g] = (jnp.dot(p.astype(BF16), vw, preferred_element_type=F32) + _bf16_round(p_new) * vwin_new) / l

    for g, (qg, blk_ref) in enumerate(zip(qgs, (blk0_ref, blk1_ref))):
        b = idx_ref[r_idx, g * SEL_TOPK + k_idx]
        valid = (b >= 0) & (b < n_past_blk)
        kb = blk_ref[0][:, 0:LANES].astype(BF16)
        vb = blk_ref[0][:, LANES:2 * LANES].astype(BF16)
        bias = fsel_ref[jnp.clip(b, 0, n_past_blk - 1), 8 * g:8 * g + 8, 0:SEL_BLOCK]
        s = jnp.where(valid, _nt_dot(qg.astype(BF16), kb) + bias, NEG)
        m_old = m_ref[g][:, 0:1]
        m_new = jnp.maximum(m_old, jnp.max(s, axis=1, keepdims=True))
        alpha = jnp.exp(m_old - m_new)
        p = jnp.exp(s - m_new)
        l_ref[g] = jnp.broadcast_to(alpha * l_ref[g][:, 0:1] + jnp.sum(p, axis=1, keepdims=True), (8, LANES))
        acc_ref[g] = alpha * acc_ref[g] + jnp.dot(p.astype(BF16), vb, preferred_element_type=F32)
        m_ref[g] = jnp.broadcast_to(m_new, (8, LANES))

    @pl.when(k_idx == SEL_TOPK - 1)
    def _():
        gates = small_ref[0]
        for g in range(NSA_KV_HEADS):
            gate = []
            for b in range(3):
                col = jnp.zeros((8, 1), F32)
                for r in range(NSA_GROUP):
                    lane_i = 3 * (g * NSA_GROUP + r) + b
                    col = jnp.where(row8[:, 0:1] == r, gates[:, lane_i:lane_i + 1], col)
                gate.append(col)
            o_ref[0, g] = gate[0] * oc_ref[0, g] + gate[1] * (acc_ref[g] / l_ref[g][:, 0:1]) + gate[2] * ow_ref[g]


def _nsa_attend(idx, page_table, q, pool, fsel, win_buf, fwin, new_sel, new_win, o_cmp, small, n_past_blk, new_lane):
    n_req = q.shape[0]
    n_win = win_buf.shape[1]

    def blk_map(g):
        def index(r, k, idx_ref, pt_ref):
            b = idx_ref[r, g * SEL_TOPK + k]
            b = jnp.where((b >= 0) & (b < n_past_blk), b, 0)
            return pt_ref[r, lax.shift_right_arithmetic(b, 1)], jnp.bitwise_and(b, 1), 1
        return index

    const = lambda *shape: (lambda r, k, idx_ref, pt_ref: shape)
    per_req3 = lambda r, k, idx_ref, pt_ref: (r, 0, 0)
    per_req4 = lambda r, k, idx_ref, pt_ref: (r, 0, 0, 0)
    grid_spec = pltpu.PrefetchScalarGridSpec(
        num_scalar_prefetch=2,
        grid=(n_req, SEL_TOPK),
        in_specs=[pl.BlockSpec((1, 1, NSA_HEADS * HEAD_DIM), per_req3),
                  pl.BlockSpec((1, SEL_BLOCK, 2 * LANES), blk_map(0)),
                  pl.BlockSpec((1, SEL_BLOCK, 2 * LANES), blk_map(1)),
                  pl.BlockSpec(fsel.shape, const(0, 0, 0)),
                  pl.BlockSpec((1, n_win, 2 * LANES), per_req3),
                  pl.BlockSpec(fwin.shape, const(0, 0)),
                  pl.BlockSpec((1, 1, 4 * LANES), per_req3),
                  pl.BlockSpec((1, 1, 2 * LANES), per_req3),
                  pl.BlockSpec((1, NSA_KV_HEADS, 8, LANES), per_req4),
                  pl.BlockSpec((1, 1, LANES), per_req3)],
        out_specs=pl.BlockSpec((1, NSA_KV_HEADS, 8, LANES), per_req4),
        scratch_shapes=[pltpu.VMEM((NSA_KV_HEADS, 8, LANES), F32)] * 4,
    )
    return pl.pallas_call(
        functools.partial(_nsa_attend_kernel, n_past_blk=n_past_blk, new_lane=new_lane, n_win=n_win),
        grid_spec=grid_spec,
        out_shape=jax.ShapeDtypeStruct((n_req, NSA_KV_HEADS, 8, LANES), F32),
        compiler_params=_cparams("arbitrary", "arbitrary"),
        name="nsa_sample_attend",
    )(idx, page_table, q, pool, pool, fsel, win_buf, fwin, new_sel, new_win, o_cmp, small)


def kernel(x_prompt, x_sample, cache_nsa_kv, cache_fox_kv, cache_fox_logf, state_nsa_win_kv, state_dil_kv, page_table,
           c_prompt, c_sample, rel_bias, norm_g, w_ada, b_ada, w_in_a, nsa_gate_b, fox_f_b, nsa_cmp_w1, nsa_cmp_w2,
           nsa_cmp_pe, w_out_a, w_in_c, w_out_c, w_mlp1, w_mlp2):
    bp, s, d = x_prompt.shape
    bd = x_sample.shape[0]
    depth = w_ada.shape[0]
    n_pages = s // PAGE_SIZE
    n_cmp_pad = s // CMP_STRIDE

    mods = _ada_params(jnp.concatenate([c_prompt, c_sample], axis=0), w_ada, b_ada).reshape(depth, bp + bd, 6, d)
    tab_sel = _toeplitz_table(rel_bias, NSA_HEADS, 15, "causal")
    tab_win = _toeplitz_table(rel_bias, NSA_HEADS, NSA_WINDOW // TILE + 2, "window")
    assert all(window // dil == TILE for window, dil in DIL_BRANCHES)
    tab_dil = [_toeplitz_table(rel_bias, DIL_HEADS, 3, "branch", dil) for _, dil in DIL_BRANCHES]
    bcmp = _cmp_bias_table(rel_bias, s, TILE, n_cmp_pad, 0)

    past_len = page_table.shape[1] * PAGE_SIZE
    assert past_len % SEL_BLOCK == 0
    n_past_blk = past_len // SEL_BLOCK
    n_blk_pad = -(-(n_past_blk + 1) // LANES) * LANES
    far = 1 << 30
    bcmp_s = _affine_bias(rel_bias, NSA_GROUPED_ROWS, past_len // CMP_STRIDE, past_len - (CMP_LEN - 1), -CMP_STRIDE, far)
    fsel = _affine_bias(rel_bias, NSA_GROUPED_ROWS, (n_past_blk + 1) * SEL_BLOCK, past_len, -1, far)
    fsel = jnp.pad(jnp.swapaxes(fsel.reshape(len(NSA_GROUPED_ROWS), n_past_blk + 1, SEL_BLOCK), 0, 1),
                   ((0, 0), (0, 0), (0, LANES - SEL_BLOCK)))
    n_win_buf = state_nsa_win_kv.shape[2]
    fwin = _affine_bias(rel_bias, NSA_GROUPED_ROWS, n_win_buf + LANES, n_win_buf, -1, NSA_WINDOW)
    tab_dil_s = jnp.stack([_affine_bias(rel_bias, range(DIL_HEADS), 2 * TILE, TILE * dil, -dil, far)
                           for _, dil in DIL_BRANCHES])

    xp = x_prompt
    xs = x_sample.reshape(1, bd, d)
    per_req = lambda a: a.reshape(bd, 1, a.shape[-1])
    prompt_pages = jnp.arange(bp * n_pages, dtype=jnp.int32).reshape(bp, n_pages)
    nsa_p, nsa_s, fkv_p, fkv_s, lf_p, lf_s, win_p, win_s, dil_p, dil_s = [], [], [], [], [], [], [], [], [], []
    for layer in range(depth):
        mp = [mods[layer, :bp, k].reshape(bp, 1, d) for k in range(6)]
        ms = [mods[layer, bp:, k].reshape(1, bd, d) for k in range(6)]
        g = [norm_g[layer, k].reshape(1, d) for k in range(4)]
        i = layer // 2
        if layer % 2 == 0:
            wa = w_in_a[i]
            w_in = jnp.concatenate([wa[:, 0:1280], wa[:, 1304:2840], wa[:, 1280:1304], wa[:, 2840:2848],
                                    jnp.zeros((d, LANES - SMALL_GATES - FOX_HEADS), F32)], axis=1).astype(BF16)
            sb = jnp.concatenate([nsa_gate_b[i].reshape(-1), fox_f_b[i],
                                  jnp.zeros((LANES - SMALL_GATES - FOX_HEADS,), F32)]).reshape(1, LANES)
            half = CMP_STRIDE * HEAD_DIM
            w1cat = jnp.concatenate([nsa_cmp_w1[i][:, :half], nsa_cmp_w1[i][:, half:]], axis=2).astype(BF16)
            w2 = nsa_cmp_w2[i].astype(BF16)
            pe = nsa_cmp_pe[i].reshape(2, 1, CMP_LEN * HEAD_DIM)
            w_out = w_out_a[i].astype(BF16)

            qn, nsa4, nsabf, win, qf, fkv, fkvbf, small = _proj_even(xp, mp[0], mp[1], g[0], w_in, sb)
            c, ct = _cumsum(small)
            o_f = _fox_prompt(qf, fkvbf, c, ct)
            kc, vc = _compress(nsa4.reshape(bp * n_pages, PAGE_SIZE, 512), prompt_pages, w1cat, w2, pe)
            o_n = _nsa_prompt(qn, kc, vc, bcmp, nsabf, tab_sel, tab_win, small)
            op_a, op_b = o_n, o_f
            nsa_p.append(nsa4.reshape(bp, s, 4, NSA_KV_HEADS, HEAD_DIM))
            fkv_p.append(fkv.reshape(bp, s, 2, FOX_HEADS, HEAD_DIM))
            lf_p.append(small[:, :, SMALL_GATES:SMALL_GATES + FOX_HEADS])
            n_win = min(NSA_WINDOW, s)
            win_p.append(win[:, s - n_win:].reshape(bp, n_win, 2, NSA_KV_HEADS, HEAD_DIM))

            qn_s, nsa4_s, _, win_new, qf_s, fkv_s_, _, small_s = _proj_even(xs, ms[0], ms[1], g[0], w_in, sb)
            pool = cache_nsa_kv[i].reshape(cache_nsa_kv.shape[1], PAGE_SIZE, 512)
            kc_s, vc_s = _compress(pool, page_table, w1cat, w2, pe)
            o_cmp, idx = _nsa_select(per_req(qn_s), kc_s, vc_s, bcmp_s, n_blk_pad, n_past_blk)
            idx = idx[:, :, 0, :SEL_TOPK].reshape(bd, NSA_KV_HEADS * SEL_TOPK)
            o_nsa = _nsa_attend(idx, page_table, per_req(qn_s), pool, fsel,
                                state_nsa_win_kv[i].reshape(bd, n_win_buf, 2 * LANES), fwin, per_req(nsa4_s),
                                per_req(win_new), o_cmp, per_req(small_s), n_past_blk, 0)
            os_a = jnp.concatenate([o_nsa[:, g, :NSA_GROUP, g * HEAD_DIM:(g + 1) * HEAD_DIM]
                                    for g in range(NSA_KV_HEADS)], axis=1).reshape(1, bd, -1).astype(BF16)
            lf_new = small_s[0, :, SMALL_GATES:SMALL_GATES + FOX_HEADS]
            os_b = _fox_sample(cache_fox_kv, i, jnp.swapaxes(cache_fox_logf[i], 1, 2), page_table,
                               qf_s.astype(F32).reshape(bd, FOX_HEADS, HEAD_DIM),
                               fkv_s_.reshape(bd, 2, FOX_HEADS, HEAD_DIM),
                               jnp.broadcast_to(lf_new[:, :, None], (bd, FOX_HEADS, LANES))).reshape(1, bd, -1).astype(BF16)
            nsa_s.append(nsa4_s.reshape(bd, 1, 4, NSA_KV_HEADS, HEAD_DIM))
            fkv_s.append(fkv_s_.reshape(bd, 1, 2, FOX_HEADS, HEAD_DIM))
            lf_s.append(small_s[0, :, SMALL_GATES:SMALL_GATES + FOX_HEADS].reshape(bd, 1, FOX_HEADS))
            win_s.append(win_new.reshape(bd, 1, 2, NSA_KV_HEADS, HEAD_DIM))
        else:
            w_in = w_in_c[i].astype(BF16)
            w_out = w_out_c[i].astype(BF16)
            q, kv, kvbf = _proj_odd(xp, mp[0], mp[1], g[0], w_in)
            branches = [_dil_branch_prompt(q, kvbf, tab, dil) for tab, (_, dil) in zip(tab_dil, DIL_BRANCHES)]
            n_dil = min(DIL_BRANCHES[-1][0], s)
            dil_p.append(kv[:, s - n_dil:].reshape(bp, n_dil, 2, DIL_HEADS, HEAD_DIM))
            q_s, kv_s, _ = _proj_odd(xs, ms[0], ms[1], g[0], w_in)
            os_a = os_b = _dil_sample(state_dil_kv, i, q_s.astype(F32).reshape(bd, DIL_HEADS, HEAD_DIM),
                                      kv_s.reshape(bd, 2, DIL_HEADS, HEAD_DIM),
                                      tab_dil_s).reshape(1, bd, -1).astype(BF16)
            dil_s.append(kv_s.reshape(bd, 1, 2, DIL_HEADS, HEAD_DIM))
        if layer % 2 == 0:
            xp = _post(op_a, op_b, w_out, xp, mp[2], g[1])
            xs = _post(os_a, os_b, w_out, xs, ms[2], g[1])
        else:
            xp = _post_dil([o for o, _ in branches], [l for _, l in branches], w_out, xp, mp[2], g[1])
            xs = _post(os_a, os_b, w_out, xs, ms[2], g[1], 0, 1)
        w1 = w_mlp1[layer].astype(BF16)
        w2m = w_mlp2[layer].astype(BF16)
        xp = _mlp(xp, mp[3], mp[4], mp[5], g[2], g[3], w1, w2m)
        xs = _mlp(xs, ms[3], ms[4], ms[5], g[2], g[3], w1, w2m)
    return (xp, xs.reshape(bd, 1, d), jnp.stack(nsa_p), jnp.stack(nsa_s), jnp.stack(fkv_p), jnp.stack(fkv_s),
            jnp.stack(lf_p), jnp.stack(lf_s), jnp.stack(win_p), jnp.stack(win_s), jnp.stack(dil_p), jnp.stack(dil_s))
```

```python
import functools
import math

import numpy as np
import jax
import jax.numpy as jnp
from jax import lax
from jax.experimental import pallas as pl
from jax.experimental.pallas import tpu as pltpu

F32 = jnp.float32
BF16 = jnp.bfloat16

HEAD_DIM = 64
NSA_HEADS = 8
NSA_KV_HEADS = 2
NSA_GROUP = NSA_HEADS // NSA_KV_HEADS
FOX_HEADS = 8
DIL_HEADS = 16
CMP_LEN = 32
CMP_STRIDE = 16
CMP_HIDDEN = 4 * HEAD_DIM
SEL_BLOCK = 64
SEL_TOPK = 16
NSA_WINDOW = 512
FORCED_SCORE = 1e9
DIL_BRANCHES = ((128, 1), (512, 4), (2048, 16))
N_BUCKETS = 32
BUCKET_EXACT = 16
BUCKET_MAX_DIST = 2048
NORM_EPS = 1e-6
TINY = 1e-30
PAGE_SIZE = 128

LANES = 128
VMEM_LIMIT_BYTES = 56 * 1024 * 1024

NEG = -1e30
QK_SCALE = HEAD_DIM ** -0.5
TILE = 128
SEL_SHIFT = 6
SMALL_GATES = 3 * NSA_HEADS


def _cparams(*sem):
    return pltpu.CompilerParams(dimension_semantics=sem, vmem_limit_bytes=VMEM_LIMIT_BYTES)


def _bucket_thresholds():
    d = np.arange(0, 2 * BUCKET_MAX_DIST + 1)
    df = np.maximum(d, 1).astype(np.float64)
    ratio = math.log(BUCKET_MAX_DIST / BUCKET_EXACT)
    log_b = BUCKET_EXACT + (np.log(df / BUCKET_EXACT) / ratio * (N_BUCKETS - BUCKET_EXACT)).astype(np.int64)
    bucket = np.where(d < BUCKET_EXACT, d, np.clip(log_b, BUCKET_EXACT, N_BUCKETS - 1))
    return [int(np.argmax(bucket >= b)) for b in range(1, N_BUCKETS)]


BUCKET_THR = _bucket_thresholds()


def _bias_of_distance(d, tab_ref, h):
    val = jnp.full(d.shape, tab_ref[0, h], F32)
    for b in range(1, N_BUCKETS):
        val = jnp.where(d >= BUCKET_THR[b - 1], tab_ref[b, h], val)
    return val


def _toeplitz_kernel(tab_ref, o_ref, *, n_heads, mode, dil):
    r = lax.broadcasted_iota(jnp.int32, (TILE, TILE), 0)
    c = lax.broadcasted_iota(jnp.int32, (TILE, TILE), 1)
    idx = pl.program_id(0)
    if mode == "branch":
        d = idx * TILE + r - c
        ok = (idx < 2) & (d >= 0) & (d <= TILE)
    else:
        d = (idx - 1) * TILE + r - c
        ok = (idx > 0) & (d >= 0)
        if mode == "window":
            ok = ok & (d < NSA_WINDOW)
    dd = jnp.maximum(d, 0) * dil
    for h in range(n_heads):
        o_ref[h, 0] = jnp.where(ok, _bias_of_distance(dd, tab_ref, h), NEG)


def _toeplitz_table(rel_bias, n_heads, n_idx, mode, dil=1):
    return pl.pallas_call(
        functools.partial(_toeplitz_kernel, n_heads=n_heads, mode=mode, dil=dil),
        grid=(n_idx,),
        in_specs=[pl.BlockSpec(memory_space=pltpu.SMEM)],
        out_specs=pl.BlockSpec((n_heads, 1, TILE, TILE), lambda i: (0, i, 0, 0)),
        out_shape=jax.ShapeDtypeStruct((n_heads, n_idx, TILE, TILE), F32),
        compiler_params=_cparams("arbitrary"),
        name="bias_toeplitz_%s%d" % (mode, dil),
    )(rel_bias)


def _cmp_bias_kernel(tab_ref, o_ref, *, tq, n_cmp_pad, t_base):
    t = t_base + pl.program_id(0) * tq + lax.broadcasted_iota(jnp.int32, (tq, n_cmp_pad), 0)
    n = lax.broadcasted_iota(jnp.int32, (tq, n_cmp_pad), 1)
    d = jnp.maximum(t - (n * CMP_STRIDE + CMP_LEN - 1), 0)
    for h in range(NSA_HEADS):
        o_ref[h] = _bias_of_distance(d, tab_ref, h)


def _cmp_bias_table(rel_bias, n_rows, tq, n_cmp_pad, t_base):
    return pl.pallas_call(
        functools.partial(_cmp_bias_kernel, tq=tq, n_cmp_pad=n_cmp_pad, t_base=t_base),
        grid=(n_rows // tq,),
        in_specs=[pl.BlockSpec(memory_space=pltpu.SMEM)],
        out_specs=pl.BlockSpec((NSA_HEADS, tq, n_cmp_pad), lambda i: (0, i, 0)),
        out_shape=jax.ShapeDtypeStruct((NSA_HEADS, n_rows, n_cmp_pad), F32),
        compiler_params=_cparams("arbitrary"),
        name="bias_cmp",
    )(rel_bias)


def _affine_bias_kernel(tab_ref, o_ref, *, heads, d0, step, limit, union):
    n = o_ref.shape[1]
    d = d0 + step * lax.broadcasted_iota(jnp.int32, (1, n), 1)
    ok = (d >= 0) & (d < limit)
    extra = jnp.zeros((1, n), F32)
    if union:
        cnt = jnp.zeros((1, n), F32)
        for window, dil in DIL_BRANCHES:
            cnt = cnt + jnp.where((d >= 0) & (d <= window) & (jnp.bitwise_and(d, dil - 1) == 0), 1.0, 0.0)
        ok = ok & (cnt > 0.5)
        extra = jnp.log(jnp.maximum(cnt, 1.0))
    dd = jnp.maximum(d, 0)
    for row, h in enumerate(heads):
        if h is None:
            o_ref[row:row + 1, :] = jnp.zeros((1, n), F32)
        else:
            o_ref[row:row + 1, :] = jnp.where(ok, _bias_of_distance(dd, tab_ref, h) + extra, NEG)


def _affine_bias(rel_bias, heads, n, d0, step, limit, union=False):
    return pl.pallas_call(
        functools.partial(_affine_bias_kernel, heads=tuple(heads), d0=d0, step=step, limit=limit, union=union),
        in_specs=[pl.BlockSpec(memory_space=pltpu.SMEM)],
        out_shape=jax.ShapeDtypeStruct((len(heads), n), F32),
        compiler_params=pltpu.CompilerParams(vmem_limit_bytes=VMEM_LIMIT_BYTES),
        name="bias_affine",
    )(rel_bias)


NSA_GROUPED_ROWS = tuple((NSA_GROUP * (row // 8) + row % 8) if row % 8 < NSA_GROUP else None
                         for row in range(8 * NSA_KV_HEADS))


def _ada_kernel(c_ref, w_ref, b_ref, o_ref):
    c = c_ref[...]
    s = (c * jax.nn.sigmoid(c)).astype(BF16)
    o_ref[0] = jnp.dot(s, w_ref[0].astype(BF16), preferred_element_type=F32) + b_ref[0]


def _ada_params(c_all, w_ada, b_ada):
    depth, d, d6 = w_ada.shape
    m = c_all.shape[0]
    return pl.pallas_call(
        _ada_kernel,
        grid=(depth, d6 // d),
        in_specs=[pl.BlockSpec((m, d), lambda l, j: (0, 0)),
                  pl.BlockSpec((1, d, d), lambda l, j: (l, 0, j)),
                  pl.BlockSpec((1, 1, d), lambda l, j: (l, 0, j))],
        out_specs=pl.BlockSpec((1, m, d), lambda l, j: (l, 0, j)),
        out_shape=jax.ShapeDtypeStruct((depth, m, d6), F32),
        compiler_params=_cparams("arbitrary", "arbitrary"),
        name="adaln",
    )(c_all, w_ada, b_ada.reshape(depth, 1, d6))


def _norm_mod(x, g, scale, shift):
    y = x * lax.rsqrt(jnp.mean(x * x, axis=-1, keepdims=True) + NORM_EPS)
    return (y * g) * (1.0 + scale) + shift


def _row_tile(s, want):
    return want if s % want == 0 else s


def _mod_spec(mod, tm):
    if mod.shape[1] == 1:
        return pl.BlockSpec((1, 1, mod.shape[2]), lambda b, i: (b, 0, 0))
    return pl.BlockSpec((1, tm, mod.shape[2]), lambda b, i: (b, i, 0))


def _proj_even_kernel(x_ref, sh_ref, sc_ref, g_ref, w_ref, sb_ref,
                      qn_ref, nsa4_ref, nsabf_ref, win_ref, qf_ref, fkv_ref, fkvbf_ref, small_ref):
    h = _norm_mod(x_ref[0], g_ref[...], sc_ref[0], sh_ref[0]).astype(BF16)
    z = jnp.dot(h, w_ref[...], preferred_element_type=F32)
    qn_ref[0] = (z[:, 0:512] * QK_SCALE).astype(BF16)
    nsa4_ref[0] = z[:, 512:1024]
    nsabf_ref[0] = z[:, 512:1280].astype(BF16)
    win_ref[0] = z[:, 1024:1280]
    qf_ref[0] = (z[:, 1280:1792] * QK_SCALE).astype(BF16)
    fkv_ref[0] = z[:, 1792:2816]
    fkvbf_ref[0] = z[:, 1792:2816].astype(BF16)
    zs = z[:, 2816:2944] + sb_ref[...]
    lane = lax.broadcasted_iota(jnp.int32, zs.shape, 1)
    sig = jax.nn.sigmoid(zs)
    lsg = jnp.minimum(zs, 0.0) - jnp.log1p(jnp.exp(-jnp.abs(zs)))
    small_ref[0] = jnp.where(lane < SMALL_GATES, sig, lsg)


def _proj_even(x, shift, scale, g, w, sb):
    bx, s, d = x.shape
    tm = _row_tile(s, 256)
    n = w.shape[1]
    widths = (512, 512, 768, 256, 512, 1024, 1024, 128)
    dtypes = (BF16, F32, BF16, F32, BF16, F32, BF16, F32)
    return pl.pallas_call(
        _proj_even_kernel,
        grid=(bx, s // tm),
        in_specs=[pl.BlockSpec((1, tm, d), lambda b, i: (b, i, 0)),
                  _mod_spec(shift, tm), _mod_spec(scale, tm),
                  pl.BlockSpec((1, d), lambda b, i: (0, 0)),
                  pl.BlockSpec((d, n), lambda b, i: (0, 0)),
                  pl.BlockSpec((1, LANES), lambda b, i: (0, 0))],
        out_specs=[pl.BlockSpec((1, tm, wd), lambda b, i: (b, i, 0)) for wd in widths],
        out_shape=[jax.ShapeDtypeStruct((bx, s, wd), dt) for wd, dt in zip(widths, dtypes)],
        compiler_params=_cparams("arbitrary", "arbitrary"),
        name="proj_even",
    )(x, shift, scale, g, w, sb)


def _proj_odd_kernel(x_ref, sh_ref, sc_ref, g_ref, w_ref, q_ref, kv_ref, kvbf_ref):
    h = _norm_mod(x_ref[0], g_ref[...], sc_ref[0], sh_ref[0]).astype(BF16)
    z = jnp.dot(h, w_ref[...], preferred_element_type=F32)
    q_ref[0] = (z[:, 0:1024] * QK_SCALE).astype(BF16)
    kv_ref[0] = z[:, 1024:3072]
    kvbf_ref[0] = z[:, 1024:3072].astype(BF16)


def _proj_odd(x, shift, scale, g, w):
    bx, s, d = x.shape
    tm = _row_tile(s, 256)
    n = w.shape[1]
    widths = (1024, 2048, 2048)
    dtypes = (BF16, F32, BF16)
    return pl.pallas_call(
        _proj_odd_kernel,
        grid=(bx, s // tm),
        in_specs=[pl.BlockSpec((1, tm, d), lambda b, i: (b, i, 0)),
                  _mod_spec(shift, tm), _mod_spec(scale, tm),
                  pl.BlockSpec((1, d), lambda b, i: (0, 0)),
                  pl.BlockSpec((d, n), lambda b, i: (0, 0))],
        out_specs=[pl.BlockSpec((1, tm, wd), lambda b, i: (b, i, 0)) for wd in widths],
        out_shape=[jax.ShapeDtypeStruct((bx, s, wd), dt) for wd, dt in zip(widths, dtypes)],
        compiler_params=_cparams("arbitrary", "arbitrary"),
        name="proj_odd",
    )(x, shift, scale, g, w)


def _post_kernel(oa_ref, ob_ref, w_ref, x_ref, gate_ref, g_ref, o_ref):
    half = oa_ref.shape[2]
    y = jnp.dot(oa_ref[0], w_ref[0:half, :], preferred_element_type=F32)
    y = y + jnp.dot(ob_ref[0], w_ref[half:, :], preferred_element_type=F32)
    yn = y * lax.rsqrt(jnp.mean(y * y, axis=-1, keepdims=True) + NORM_EPS) * g_ref[...]
    o_ref[0] = x_ref[0] + gate_ref[0] * yn


def _post(o_a, o_b, w_out, x, gate, g, col_a=0, col_b=0):
    bx, s, d = x.shape
    tm = _row_tile(s, 512)
    half = w_out.shape[0] // 2
    return pl.pallas_call(
        _post_kernel,
        grid=(bx, s // tm),
        in_specs=[pl.BlockSpec((1, tm, half), lambda b, i: (b, i, col_a)),
                  pl.BlockSpec((1, tm, half), lambda b, i: (b, i, col_b)),
                  pl.BlockSpec(w_out.shape, lambda b, i: (0, 0)),
                  pl.BlockSpec((1, tm, d), lambda b, i: (b, i, 0)),
                  _mod_spec(gate, tm),
                  pl.BlockSpec((1, d), lambda b, i: (0, 0))],
        out_specs=pl.BlockSpec((1, tm, d), lambda b, i: (b, i, 0)),
        out_shape=jax.ShapeDtypeStruct((bx, s, d), F32),
        compiler_params=_cparams("arbitrary", "arbitrary"),
        name="post",
    )(o_a, o_b, w_out, x, gate, g)


def _mlp_kernel(x_ref, sh_ref, sc_ref, gate_ref, g2_ref, g3_ref, w1_ref, w2_ref, o_ref, h_ref, acc_ref):
    j = pl.program_id(2)

    @pl.when(j == 0)
    def _():
        h_ref[...] = _norm_mod(x_ref[0], g2_ref[...], sc_ref[0], sh_ref[0]).astype(BF16)
        acc_ref[...] = jnp.zeros_like(acc_ref)

    a = jnp.maximum(jnp.dot(h_ref[...], w1_ref[...], preferred_element_type=F32), 0.0)
    acc_ref[...] += jnp.dot((a * a).astype(BF16), w2_ref[...], preferred_element_type=F32)

    @pl.when(j == pl.num_programs(2) - 1)
    def _():
        y = acc_ref[...]
        yn = y * lax.rsqrt(jnp.mean(y * y, axis=-1, keepdims=True) + NORM_EPS) * g3_ref[...]
        o_ref[0] = x_ref[0] + gate_ref[0] * yn


def _mlp(x, shift, scale, gate, g2, g3, w1, w2):
    bx, s, d = x.shape
    f = w1.shape[1]
    tm = _row_tile(s, 1024)
    tf = 1024

    def mod3(mod):
        if mod.shape[1] == 1:
            return pl.BlockSpec((1, 1, d), lambda b, i, j: (b, 0, 0))
        return pl.BlockSpec((1, tm, d), lambda b, i, j: (b, i, 0))

    return pl.pallas_call(
        _mlp_kernel,
        grid=(bx, s // tm, f // tf),
        in_specs=[pl.BlockSpec((1, tm, d), lambda b, i, j: (b, i, 0)),
                  mod3(shift), mod3(scale), mod3(gate),
                  pl.BlockSpec((1, d), lambda b, i, j: (0, 0)),
                  pl.BlockSpec((1, d), lambda b, i, j: (0, 0)),
                  pl.BlockSpec((d, tf), lambda b, i, j: (0, j)),
                  pl.BlockSpec((tf, d), lambda b, i, j: (j, 0))],
        out_specs=pl.BlockSpec((1, tm, d), lambda b, i, j: (b, i, 0)),
        out_shape=jax.ShapeDtypeStruct((bx, s, d), F32),
        scratch_shapes=[pltpu.VMEM((tm, d), BF16), pltpu.VMEM((tm, d), F32)],
        compiler_params=_cparams("arbitrary", "arbitrary", "arbitrary"),
        name="mlp",
    )(x, shift, scale, gate, g2, g3, w1, w2)


def _cumsum_kernel(x_ref, c_ref, ct_ref, carry_ref, *, tc):
    @pl.when(pl.program_id(1) == 0)
    def _():
        carry_ref[...] = jnp.zeros_like(carry_ref)

    r = lax.broadcasted_iota(jnp.int32, (tc, tc), 0)
    c = lax.broadcasted_iota(jnp.int32, (tc, tc), 1)
    tri = jnp.where(c <= r, 1.0, 0.0).astype(F32)
    cs = jnp.dot(tri, x_ref[0], preferred_element_type=F32, precision=lax.Precision.HIGHEST) + carry_ref[...]
    carry_ref[...] = cs[tc - 1:tc, :]
    c_ref[0] = cs
    ct_ref[0] = cs.T[SMALL_GATES:SMALL_GATES + FOX_HEADS, :]


def _cumsum(small):
    bx, s, _ = small.shape
    tc = _row_tile(s, 256)
    return pl.pallas_call(
        functools.partial(_cumsum_kernel, tc=tc),
        grid=(bx, s // tc),
        in_specs=[pl.BlockSpec((1, tc, LANES), lambda b, i: (b, i, 0))],
        out_specs=[pl.BlockSpec((1, tc, LANES), lambda b, i: (b, i, 0)),
                   pl.BlockSpec((1, FOX_HEADS, tc), lambda b, i: (b, 0, i))],
        out_shape=[jax.ShapeDtypeStruct((bx, s, LANES), F32),
                   jax.ShapeDtypeStruct((bx, FOX_HEADS, s), F32)],
        scratch_shapes=[pltpu.VMEM((1, LANES), F32)],
        compiler_params=_cparams("arbitrary", "arbitrary"),
        name="logf_cumsum",
    )(small)


def _nt_dot(a, b):
    return lax.dot_general(a, b, (((1,), (1,)), ((), ())), preferred_element_type=F32)


def _online_update(state, s, v):
    m, l, acc = state
    m_new = jnp.maximum(m, jnp.max(s, axis=1, keepdims=True))
    alpha = jnp.exp(m - m_new)
    p = jnp.exp(s - m_new)
    l = alpha * l + jnp.sum(p, axis=1, keepdims=True)
    acc = alpha * acc + jnp.dot(p.astype(BF16), v, preferred_element_type=F32)
    return m_new, l, acc


def _init_state(tq):
    return (jnp.full((tq, 1), NEG, F32), jnp.zeros((tq, 1), F32), jnp.zeros((tq, LANES), F32))


def _half_masks(tq):
    lane = lax.broadcasted_iota(jnp.int32, (tq, LANES), 1)
    return lane < HEAD_DIM


def _split_heads(q2, lo):
    zero = jnp.zeros_like(q2)
    return jnp.where(lo, q2, zero), jnp.where(lo, zero, q2)


def _fox_kernel(q_ref, k_ref, v_ref, c_ref, ct_ref, o_ref, *, tq, tk, n_pairs):
    grp = pl.program_id(1)
    i = pl.program_id(2)
    lo = _half_masks(tq)
    lane = lax.broadcasted_iota(jnp.int32, (tq, LANES), 1)
    cblk = c_ref[0]
    qs, cqs, heads = [], [], []
    for p in range(n_pairs):
        qs.extend(_split_heads(q_ref[0, :, p * LANES:(p + 1) * LANES], lo))
        for e in range(2):
            head = (grp * n_pairs + p) * 2 + e
            heads.append(head)
            cqs.append(jnp.sum(jnp.where(lane == SMALL_GATES + head, cblk, 0.0), axis=1, keepdims=True))

    def chunk(c, states, masked):
        off = pl.multiple_of(c * tk, tk)
        if masked:
            row = i * tq + lax.broadcasted_iota(jnp.int32, (tq, tk), 0)
            col = off + lax.broadcasted_iota(jnp.int32, (tq, tk), 1)
            ok = col <= row
        out = []
        for n in range(2 * n_pairs):
            p = n // 2
            k = k_ref[0, pl.ds(off, tk), p * LANES:(p + 1) * LANES]
            v = v_ref[0, pl.ds(off, tk), p * LANES:(p + 1) * LANES]
            ck = ct_ref[0, pl.ds(heads[n], 1), pl.ds(off, tk)]
            s = _nt_dot(qs[n], k) + cqs[n] - ck
            if masked:
                s = jnp.where(ok, s, NEG)
            out.append(_online_update(states[n], s, v))
        return tuple(out)

    n_full = (i * tq) // tk
    init = tuple(_init_state(tq) for _ in range(2 * n_pairs))
    states = lax.fori_loop(0, n_full, lambda c, st: chunk(c, st, False), init)
    states = chunk(n_full, states, True)
    outs = []
    for p in range(n_pairs):
        (_, la, acca), (_, lb, accb) = states[2 * p], states[2 * p + 1]
        outs.append(jnp.where(lo, acca / la, accb / lb))
    o_ref[0] = jnp.concatenate(outs, axis=1).astype(o_ref.dtype)


def _fox_prompt(qf, fkvbf, c, ct):
    bx, s, _ = qf.shape
    tq = _row_tile(s, 512)
    tk = _row_tile(s, 512)
    n_pairs = 1
    n_grp = FOX_HEADS // 2 // n_pairs
    wd = n_pairs * LANES
    return pl.pallas_call(
        functools.partial(_fox_kernel, tq=tq, tk=tk, n_pairs=n_pairs),
        grid=(bx, n_grp, s // tq),
        in_specs=[pl.BlockSpec((1, tq, wd), lambda b, p, i: (b, i, p)),
                  pl.BlockSpec((1, s, wd), lambda b, p, i: (b, 0, p)),
                  pl.BlockSpec((1, s, wd), lambda b, p, i: (b, 0, n_grp + p)),
                  pl.BlockSpec((1, tq, LANES), lambda b, p, i: (b, i, 0)),
                  pl.BlockSpec((1, FOX_HEADS, s), lambda b, p, i: (b, 0, 0))],
        out_specs=pl.BlockSpec((1, tq, wd), lambda b, p, i: (b, i, p)),
        out_shape=jax.ShapeDtypeStruct((bx, s, FOX_HEADS * HEAD_DIM), BF16),
        compiler_params=_cparams("arbitrary", "arbitrary", "arbitrary"),
        name="fox_prompt",
    )(qf, fkvbf, fkvbf, c, ct)


def _dil_branch_kernel(q_ref, kc_ref, kp_ref, vc_ref, vp_ref, tab_ref, o_ref, lse_ref, *, tq):
    i = pl.program_id(2)
    lo = _half_masks(TILE)
    lane = lax.broadcasted_iota(jnp.int32, (TILE, LANES), 1)
    first_prev = jnp.where(i == 0, 2, 1)
    for a in range(tq // TILE):
        rows = slice(a * TILE, (a + 1) * TILE)
        lse_tile = jnp.zeros((TILE, LANES), F32)
        outs = []
        for p in range(DIL_HEADS // 2):
            cols = slice(p * LANES, (p + 1) * LANES)
            if a == 0:
                k_prev, v_prev, prev_idx = kp_ref[0, :, cols], vp_ref[0, :, cols], first_prev
            else:
                prev = slice((a - 1) * TILE, a * TILE)
                k_prev, v_prev, prev_idx = kc_ref[0, prev, cols], vc_ref[0, prev, cols], 1
            k2 = jnp.concatenate([k_prev, kc_ref[0, rows, cols]], axis=0)
            v2 = jnp.concatenate([v_prev, vc_ref[0, rows, cols]], axis=0)
            pair = []
            for e, qh in enumerate(_split_heads(q_ref[0, rows, cols], lo)):
                h = 2 * p + e
                bias = jnp.concatenate([tab_ref[h, prev_idx], tab_ref[h, 0]], axis=1)
                s = _nt_dot(qh, k2) + bias
                m = jnp.max(s, axis=1, keepdims=True)
                e_s = jnp.exp(s - m)
                l = jnp.sum(e_s, axis=1, keepdims=True)
                pair.append(jnp.dot(e_s.astype(BF16), v2, preferred_element_type=F32) / l)
                lse_tile = jnp.where(lane == h, jnp.log(l) + m, lse_tile)
            outs.append(jnp.where(lo, pair[0], pair[1]))
        o_ref[0, rows, :] = jnp.concatenate(outs, axis=1).astype(o_ref.dtype)
        lse_ref[0, rows, :] = lse_tile


def _dil_branch_prompt(q, kvbf, table, dil):
    bx, s, width = q.shape
    n_rows = s // dil
    tq = _row_tile(n_rows, 256)
    sub = tq // TILE
    qv = q.reshape(bx, n_rows, dil * width)
    kvv = kvbf.reshape(bx, n_rows, dil * 2 * width)
    o, lse = pl.pallas_call(
        functools.partial(_dil_branch_kernel, tq=tq),
        grid=(bx, dil, n_rows // tq),
        in_specs=[pl.BlockSpec((1, tq, width), lambda b, r, i: (b, i, r)),
                  pl.BlockSpec((1, tq, width), lambda b, r, i: (b, i, 2 * r)),
                  pl.BlockSpec((1, TILE, width), lambda b, r, i: (b, jnp.maximum(sub * i - 1, 0), 2 * r)),
                  pl.BlockSpec((1, tq, width), lambda b, r, i: (b, i, 2 * r + 1)),
                  pl.BlockSpec((1, TILE, width), lambda b, r, i: (b, jnp.maximum(sub * i - 1, 0), 2 * r + 1)),
                  pl.BlockSpec(table.shape, lambda b, r, i: (0, 0, 0, 0))],
        out_specs=[pl.BlockSpec((1, tq, width), lambda b, r, i: (b, i, r)),
                   pl.BlockSpec((1, tq, LANES), lambda b, r, i: (b, i, r))],
        out_shape=[jax.ShapeDtypeStruct((bx, n_rows, dil * width), BF16),
                   jax.ShapeDtypeStruct((bx, n_rows, dil * LANES), F32)],
        compiler_params=_cparams("arbitrary", "arbitrary", "arbitrary"),
        name="dilated_branch_prompt",
    )(qv, kvv, kvv, kvv, kvv, table)
    return o.reshape(bx, s, width), lse.reshape(bx, s, LANES)


def _post_dil_kernel(o1_ref, o2_ref, o3_ref, l1_ref, l2_ref, l3_ref, w_ref, x_ref, gate_ref, g_ref, o_ref):
    lses = [r[0] for r in (l1_ref, l2_ref, l3_ref)]
    m = jnp.maximum(jnp.maximum(lses[0], lses[1]), lses[2])
    es = [jnp.exp(l - m) for l in lses]
    tot = es[0] + es[1] + es[2]
    width = o1_ref.shape[2]
    head_of_col = lax.shift_right_arithmetic(lax.broadcasted_iota(jnp.int32, (LANES, width), 1), SEL_SHIFT)
    expand = jnp.where(head_of_col == lax.broadcasted_iota(jnp.int32, (LANES, width), 0), 1.0, 0.0).astype(BF16)
    mix = jnp.zeros((o1_ref.shape[1], width), F32)
    for e, o_ref_j in zip(es, (o1_ref, o2_ref, o3_ref)):
        alpha = jnp.dot((e / tot).astype(BF16), expand, preferred_element_type=F32)
        mix = mix + alpha * o_ref_j[0].astype(F32)
    y = jnp.dot(mix.astype(BF16), w_ref[...], preferred_element_type=F32)
    yn = y * lax.rsqrt(jnp.mean(y * y, axis=-1, keepdims=True) + NORM_EPS) * g_ref[...]
    o_ref[0] = x_ref[0] + gate_ref[0] * yn


def _post_dil(outs, lses, w_out, x, gate, g):
    bx, s, d = x.shape
    tm = _row_tile(s, 256)
    width = w_out.shape[0]
    return pl.pallas_call(
        _post_dil_kernel,
        grid=(bx, s // tm),
        in_specs=([pl.BlockSpec((1, tm, width), lambda b, i: (b, i, 0))] * 3
                  + [pl.BlockSpec((1, tm, LANES), lambda b, i: (b, i, 0))] * 3
                  + [pl.BlockSpec(w_out.shape, lambda b, i: (0, 0)),
                     pl.BlockSpec((1, tm, d), lambda b, i: (b, i, 0)),
                     _mod_spec(gate, tm),
                     pl.BlockSpec((1, d), lambda b, i: (0, 0))]),
        out_specs=pl.BlockSpec((1, tm, d), lambda b, i: (b, i, 0)),
        out_shape=jax.ShapeDtypeStruct((bx, s, d), F32),
        compiler_params=_cparams("arbitrary", "arbitrary"),
        name="post_dilated",
    )(*outs, *lses, w_out, x, gate, g)


def _compress_kernel(pt_ref, page_ref, w1_ref, w2_ref, pe_ref, kc_ref, vc_ref, rows_ref, chunk_ref, *, n_pages,
                     feature_major):
    j = pl.program_id(1)
    row0 = pl.multiple_of(j * PAGE_SIZE, PAGE_SIZE)
    if feature_major:
        for kv in range(2):
            rows_ref[kv, pl.ds(row0, PAGE_SIZE), :] = jnp.concatenate(
                [page_ref[0, 0, kv, g].T for g in range(NSA_KV_HEADS)], axis=1)
    else:
        rows_ref[0, pl.ds(row0, PAGE_SIZE), :] = page_ref[0, :, 0:LANES]
        rows_ref[1, pl.ds(row0, PAGE_SIZE), :] = page_ref[0, :, LANES:2 * LANES]

    @pl.when(j == n_pages - 1)
    def _():
        n_chunks = n_pages * PAGE_SIZE // CMP_STRIDE
        half = CMP_STRIDE * HEAD_DIM
        for kv, out_ref in ((0, kc_ref), (1, vc_ref)):
            w1 = w1_ref[kv]
            pe_a = jnp.broadcast_to(pe_ref[kv, :, 0:half], (8, half)).astype(BF16)
            pe_b = jnp.broadcast_to(pe_ref[kv, :, half:], (8, half)).astype(BF16)
            pe_term = (jnp.dot(pe_a, w1, preferred_element_type=F32)[0:1, 0:CMP_HIDDEN]
                       + jnp.dot(pe_b, w1, preferred_element_type=F32)[0:1, CMP_HIDDEN:])
            for l in range(CMP_STRIDE):
                both = rows_ref[kv, pl.ds(l, n_chunks, stride=CMP_STRIDE), :].astype(BF16)
                for g in range(NSA_KV_HEADS):
                    chunk_ref[g, :, l * HEAD_DIM:(l + 1) * HEAD_DIM] = both[:, g * HEAD_DIM:(g + 1) * HEAD_DIM]
            outs = []
            for g in range(NSA_KV_HEADS):
                uv = jnp.dot(chunk_ref[g], w1, preferred_element_type=F32)
                pre = uv[:, 0:CMP_HIDDEN] + pltpu.roll(uv[:, CMP_HIDDEN:], n_chunks - 1, 0) + pe_term
                hid = jax.nn.gelu(pre).astype(BF16)
                outs.append(jnp.dot(hid, w2_ref[kv], preferred_element_type=F32))
            out_ref[0] = jnp.concatenate(outs, axis=1)


def _compress(pool, page_table, w1cat, w2, pe, layer=None):
    n_req, n_pages = page_table.shape
    n_chunks = n_pages * PAGE_SIZE // CMP_STRIDE
    width = 2 * NSA_KV_HEADS * HEAD_DIM
    if layer is None:
        page_spec = pl.BlockSpec((1, PAGE_SIZE, width), lambda r, j, pt: (pt[r, j], 0, 0))
    else:
        page_spec = pl.BlockSpec((1, 1, 2, NSA_KV_HEADS, HEAD_DIM, PAGE_SIZE),
                                 lambda r, j, pt: (layer, pt[r, j], 0, 0, 0, 0))
    grid_spec = pltpu.PrefetchScalarGridSpec(
        num_scalar_prefetch=1,
        grid=(n_req, n_pages),
        in_specs=[page_spec,
                  pl.BlockSpec(w1cat.shape, lambda r, j, pt: (0, 0, 0)),
                  pl.BlockSpec(w2.shape, lambda r, j, pt: (0, 0, 0)),
                  pl.BlockSpec(pe.shape, lambda r, j, pt: (0, 0, 0))],
        out_specs=[pl.BlockSpec((1, n_chunks, LANES), lambda r, j, pt: (r, 0, 0)),
                   pl.BlockSpec((1, n_chunks, LANES), lambda r, j, pt: (r, 0, 0))],
        scratch_shapes=[pltpu.VMEM((2, n_pages * PAGE_SIZE, LANES), F32),
                        pltpu.VMEM((NSA_KV_HEADS, n_chunks, CMP_STRIDE * HEAD_DIM), BF16)],
    )
    return pl.pallas_call(
        functools.partial(_compress_kernel, n_pages=n_pages, feature_major=layer is not None),
        grid_spec=grid_spec,
        out_shape=[jax.ShapeDtypeStruct((n_req, n_chunks, LANES), F32)] * 2,
        compiler_params=_cparams("arbitrary", "arbitrary"),
        name="nsa_compress",
    )(page_table, pool, w1cat, w2, pe)


def _top_k_mask(imp, n_top):
    lane = lax.broadcasted_iota(jnp.int32, imp.shape, 1)
    width = imp.shape[1]

    def body(_, carry):
        imp, sel = carry
        m = jnp.max(imp, axis=1, keepdims=True)
        first = jnp.min(jnp.where(imp == m, lane, width), axis=1, keepdims=True)
        pick = (lane == first) & (m > -jnp.inf)
        return jnp.where(lane == first, -jnp.inf, imp), jnp.where(pick, 1.0, sel)

    _, sel = lax.fori_loop(0, n_top, body, (imp, jnp.zeros(imp.shape, F32)))
    return sel


def _nsa_kernel(qn_ref, kc_ref, vc_ref, bcmp_ref, sel_ref, win_ref, tsel_ref, twin_ref, small_ref, o_ref,
                *, tq, tk, n_sel_delta, n_win_tiles):
    i = pl.program_id(1)
    t0 = i * tq
    n_cmp_pad = kc_ref.shape[1]
    n_blk = LANES
    lo = _half_masks(tq)
    lane = lax.broadcasted_iota(jnp.int32, (tq, LANES), 1)
    gates = small_ref[0]
    q_all = qn_ref[0].astype(F32)

    t_c = t0 + lax.broadcasted_iota(jnp.int32, (tq, n_cmp_pad), 0)
    n_c = lax.broadcasted_iota(jnp.int32, (tq, n_cmp_pad), 1)
    ok_c = t_c >= n_c * CMP_STRIDE + (CMP_LEN - 1)
    ci = lax.broadcasted_iota(jnp.int32, (n_cmp_pad, n_blk), 0) * CMP_STRIDE
    sj = lax.broadcasted_iota(jnp.int32, (n_cmp_pad, n_blk), 1) * SEL_BLOCK
    cover = jnp.where((ci < sj + SEL_BLOCK) & (ci + CMP_LEN > sj), 1.0, 0.0).astype(F32)
    t_b = t0 + lax.broadcasted_iota(jnp.int32, (tq, n_blk), 0)
    cur = lax.shift_right_arithmetic(t_b, SEL_SHIFT)
    forced = (lane == 0) | (lane == cur) | (lane == cur - 1)
    blk_of_key = lax.shift_right_arithmetic(lax.broadcasted_iota(jnp.int32, (n_blk, tk), 1), SEL_SHIFT)
    blk_row = lax.broadcasted_iota(jnp.int32, (n_blk, tk), 0)
    blk_delta = blk_row - blk_of_key
    sub = tk // TILE

    n_sub = tq // TILE
    group_of = [h // NSA_GROUP for h in range(NSA_HEADS)]

    qs = []
    for h in range(NSA_HEADS):
        g = group_of[h]
        blk = q_all[:, LANES * (h // 2):LANES * (h // 2 + 1)]
        if h % 2 != g:
            blk = pltpu.roll(blk, HEAD_DIM, 1)
        qs.append(jnp.where(lo if g == 0 else jnp.logical_not(lo), blk, 0.0).astype(BF16))

    kcb = kc_ref[0].astype(BF16)
    vcb = vc_ref[0].astype(BF16)
    o_cmp, imps = [], []
    for g in range(NSA_KV_HEADS):
        psum = jnp.zeros((tq, n_cmp_pad), F32)
        for r in range(NSA_GROUP):
            h = g * NSA_GROUP + r
            s = jnp.where(ok_c, _nt_dot(qs[h], kcb) + bcmp_ref[h], NEG)
            m = jnp.max(s, axis=1, keepdims=True)
            e = jnp.where(ok_c, jnp.exp(s - m), 0.0)
            p = e / jnp.maximum(jnp.sum(e, axis=1, keepdims=True), TINY)
            psum = psum + p
            o_cmp.append(jnp.dot(p.astype(BF16), vcb, preferred_element_type=F32))
        imp = jnp.dot(psum, cover, preferred_element_type=F32, precision=lax.Precision.HIGHEST)
        imp = jnp.where(forced, FORCED_SCORE, imp)
        imps.append(jnp.where(lane <= cur, imp, -jnp.inf))

    sel = _top_k_mask(jnp.concatenate(imps, axis=0), min(SEL_TOPK, n_blk))
    not_sel = [(1.0 - sel[g * tq:(g + 1) * tq]).astype(BF16) for g in range(NSA_KV_HEADS)]

    def sel_chunk(c, states):
        off = pl.multiple_of(c * tk, tk)
        k = sel_ref[0, pl.ds(off, tk), 0:LANES]
        v = sel_ref[0, pl.ds(off, tk), LANES:2 * LANES]
        expand = jnp.where(blk_delta == c * (tk // SEL_BLOCK), NEG, 0.0).astype(BF16)
        mask_add = [jnp.dot(ns, expand, preferred_element_type=F32) for ns in not_sel]
        out = []
        for h in range(NSA_HEADS):
            bias = jnp.concatenate([jnp.concatenate(
                [tsel_ref[h, jnp.clip(n_sub * i + a - (c * sub + u), -1, n_sel_delta - 1) + 1] for u in range(sub)],
                axis=1) for a in range(n_sub)], axis=0)
            out.append(_online_update(states[h], _nt_dot(qs[h], k) + bias + mask_add[group_of[h]], v))
        return tuple(out)

    n_chunks = (t0 + tq - 1) // tk + 1
    sel_states = lax.fori_loop(0, n_chunks, sel_chunk, tuple(_init_state(tq) for _ in range(NSA_HEADS)))

    o_win = [[] for _ in range(NSA_HEADS)]
    for a in range(n_sub):
        j0 = n_sub * i + a
        ks, vs, idxs = [], [], []
        for u in range(n_win_tiles - 1, -1, -1):
            off = pl.multiple_of(jnp.maximum(j0 - u, 0) * TILE, TILE)
            ks.append(win_ref[0, pl.ds(off, TILE), 0:LANES])
            vs.append(win_ref[0, pl.ds(off, TILE), LANES:2 * LANES])
            idxs.append(jnp.where(j0 - u >= 0, u + 1, 0))
        k = jnp.concatenate(ks, axis=0)
        v = jnp.concatenate(vs, axis=0)
        for h in range(NSA_HEADS):
            bias = jnp.concatenate([twin_ref[h, ix] for ix in idxs], axis=1)
            s = _nt_dot(qs[h][a * TILE:(a + 1) * TILE], k) + bias
            e = jnp.exp(s - jnp.max(s, axis=1, keepdims=True))
            l = jnp.sum(e, axis=1, keepdims=True)
            o_win[h].append(jnp.dot(e.astype(BF16), v, preferred_element_type=F32) / l)

    pair_out = [None] * (NSA_HEADS // 2)
    for h in range(NSA_HEADS):
        gc, gs, gw = (jnp.sum(jnp.where(lane == 3 * h + b, gates, 0.0), axis=1, keepdims=True) for b in range(3))
        _, l_s, acc_s = sel_states[h]
        o = gc * o_cmp[h] + gs * (acc_s / l_s) + gw * jnp.concatenate(o_win[h], axis=0)
        if h % 2 != group_of[h]:
            o = pltpu.roll(o, HEAD_DIM, 1)
        prev = pair_out[h // 2]
        keep = lo if h % 2 == 0 else jnp.logical_not(lo)
        pair_out[h // 2] = jnp.where(keep, o, 0.0 if prev is None else prev)

    o_ref[0] = jnp.concatenate(pair_out, axis=1).astype(o_ref.dtype)


def _nsa_prompt(qn, kc, vc, bcmp, nsabf, tsel, twin, small):
    bx, s, _ = qn.shape
    tq = _row_tile(s, 256)
    tk = _row_tile(s, 512)
    n_cmp_pad = kc.shape[1]
    once = pl.Buffered(1)
    return pl.pallas_call(
        functools.partial(_nsa_kernel, tq=tq, tk=tk, n_sel_delta=tsel.shape[1] - 1, n_win_tiles=twin.shape[1] - 1),
        grid=(bx, s // tq),
        in_specs=[pl.BlockSpec((1, tq, NSA_HEADS * HEAD_DIM), lambda b, i: (b, i, 0)),
                  pl.BlockSpec((1, n_cmp_pad, LANES), lambda b, i: (b, 0, 0)),
                  pl.BlockSpec((1, n_cmp_pad, LANES), lambda b, i: (b, 0, 0)),
                  pl.BlockSpec((NSA_HEADS, tq, n_cmp_pad), lambda b, i: (0, i, 0)),
                  pl.BlockSpec((1, s, 2 * LANES), lambda b, i: (b, 0, 1)),
                  pl.BlockSpec((1, s, 2 * LANES), lambda b, i: (b, 0, 2)),
                  pl.BlockSpec(tsel.shape, lambda b, i: (0, 0, 0, 0), pipeline_mode=once),
                  pl.BlockSpec(twin.shape, lambda b, i: (0, 0, 0, 0), pipeline_mode=once),
                  pl.BlockSpec((1, tq, LANES), lambda b, i: (b, i, 0))],
        out_specs=pl.BlockSpec((1, tq, NSA_HEADS * HEAD_DIM), lambda b, i: (b, i, 0)),
        out_shape=jax.ShapeDtypeStruct((bx, s, NSA_HEADS * HEAD_DIM), BF16),
        compiler_params=_cparams("arbitrary", "arbitrary"),
        name="nsa_prompt",
    )(qn, kc, vc, bcmp, nsabf, nsabf, tsel, twin, small)


def _bf16_round(x):
    return x.astype(BF16).astype(F32)


def _dot3(z, w):
    hi = z.astype(BF16)
    rest = z - hi.astype(F32)
    mid = rest.astype(BF16)
    lo = (rest - mid.astype(F32)).astype(BF16)
    return sum(jnp.dot(part, w, preferred_element_type=F32) for part in (hi, mid, lo))


def _feature_major(cache):
    n = cache.ndim
    return jnp.transpose(cache, tuple(range(n - 4)) + (n - 3, n - 2, n - 1, n - 4))


def _nt_dot_bf16(p, vt):
    return lax.dot_general(p.astype(BF16), vt, (((1,), (1,)), ((), ())), preferred_element_type=F32)


def _fox_sample_kernel(pt_ref, *refs, n_steps, pages_per_step):
    page_refs, lf_refs = refs[:pages_per_step], refs[pages_per_step:2 * pages_per_step]
    q_ref, new_ref, lfn_ref, o_ref, m_ref, l_ref, acc_ref, carry_ref = refs[2 * pages_per_step:]
    j = pl.program_id(1)
    width = FOX_HEADS * HEAD_DIM
    qbd = q_ref[0]

    @pl.when(j == 0)
    def _():
        s_new = jnp.sum(qbd.astype(F32) * _bf16_round(new_ref[0, 0:1]), axis=1, keepdims=True)
        m_ref[...] = jnp.broadcast_to(s_new, m_ref.shape)
        l_ref[...] = jnp.ones_like(l_ref)
        acc_ref[...] = jnp.broadcast_to(_bf16_round(new_ref[0, 1:2]), acc_ref.shape)
        carry_ref[...] = lfn_ref[0]

    u = lax.broadcasted_iota(jnp.int32, (PAGE_SIZE, PAGE_SIZE), 0)
    c = lax.broadcasted_iota(jnp.int32, (PAGE_SIZE, PAGE_SIZE), 1)
    later = jnp.where(u > c, 1.0, 0.0).astype(BF16)
    for page_ref, lf_ref in zip(page_refs, lf_refs):
        kt = page_ref[0, 0, 0].reshape(width, PAGE_SIZE).astype(BF16)
        vt = page_ref[0, 0, 1].reshape(width, PAGE_SIZE).astype(BF16)
        lf = lf_ref[0, 0]
        carry = carry_ref[:, 0:1]
        s = jnp.dot(qbd, kt, preferred_element_type=F32) + (carry + _dot3(lf, later))
        m_old = m_ref[:, 0:1]
        m_new = jnp.maximum(m_old, jnp.max(s, axis=1, keepdims=True))
        alpha = jnp.exp(m_old - m_new)
        p = jnp.exp(s - m_new)
        l_ref[...] = jnp.broadcast_to(alpha * l_ref[:, 0:1] + jnp.sum(p, axis=1, keepdims=True), l_ref.shape)
        acc_ref[...] = alpha * acc_ref[...] + _nt_dot_bf16(p, vt)
        m_ref[...] = jnp.broadcast_to(m_new, m_ref.shape)
        carry_ref[...] = jnp.broadcast_to(carry + jnp.sum(lf, axis=1, keepdims=True), carry_ref.shape)

    @pl.when(j == n_steps - 1)
    def _():
        o_ref[0] = acc_ref[...] / l_ref[:, 0:1]


def _block_diag_queries(q, n_heads):
    width = n_heads * HEAD_DIM
    keep = (jnp.arange(width) // HEAD_DIM)[None, :] == jnp.arange(n_heads)[:, None]
    return jnp.where(keep[None], q[:, None, :], jnp.zeros((), q.dtype))


def _diag_blocks(o, n_heads):
    n = o.shape[0]
    o5 = o.reshape(n, n_heads, n_heads, HEAD_DIM)
    return jnp.stack([o5[:, h, h] for h in range(n_heads)], axis=1).reshape(n, n_heads * HEAD_DIM)


def _fox_sample(cache_t, layer, logf_t, page_table, qbd, new_kv, lf_new):
    n_req, n_pages = page_table.shape
    pps = 4 if n_pages % 4 == 0 else 1
    n_steps = n_pages // pps
    width = FOX_HEADS * HEAD_DIM

    def page_spec(shape, p):
        zeros = (0,) * (len(shape) - 2)
        return pl.BlockSpec(shape, lambda r, j, pt: (layer, pt[r, n_pages - 1 - (j * pps + p)]) + zeros)

    grid_spec = pltpu.PrefetchScalarGridSpec(
        num_scalar_prefetch=1,
        grid=(n_req, n_steps),
        in_specs=([page_spec((1, 1, 2, FOX_HEADS, HEAD_DIM, PAGE_SIZE), p) for p in range(pps)]
                  + [page_spec((1, 1, FOX_HEADS, PAGE_SIZE), p) for p in range(pps)]
                  + [pl.BlockSpec((1, FOX_HEADS, width), lambda r, j, pt: (r, 0, 0)),
                     pl.BlockSpec((1, 2, width), lambda r, j, pt: (r, 0, 0)),
                     pl.BlockSpec((1, FOX_HEADS, LANES), lambda r, j, pt: (r, 0, 0))]),
        out_specs=pl.BlockSpec((1, FOX_HEADS, width), lambda r, j, pt: (r, 0, 0)),
        scratch_shapes=[pltpu.VMEM((FOX_HEADS, LANES), F32), pltpu.VMEM((FOX_HEADS, LANES), F32),
                        pltpu.VMEM((FOX_HEADS, width), F32), pltpu.VMEM((FOX_HEADS, LANES), F32)],
    )
    return pl.pallas_call(
        functools.partial(_fox_sample_kernel, n_steps=n_steps, pages_per_step=pps),
        grid_spec=grid_spec,
        out_shape=jax.ShapeDtypeStruct((n_req, FOX_HEADS, width), F32),
        compiler_params=_cparams("arbitrary", "arbitrary"),
        name="fox_sample",
    )(page_table, *([cache_t] * pps), *([logf_t] * pps), qbd, new_kv, lf_new)


def _dil_sample_kernel(q_ref, new_ref, kt_ref, vt_ref, tab_ref, o_ref, *, n_buf):
    heads = q_ref.shape[2]
    qbd = q_ref[0, 0]
    kt = kt_ref[0, 0, 0].reshape(heads * HEAD_DIM, n_buf).astype(BF16)
    vt = vt_ref[0, 0, 0].reshape(heads * HEAD_DIM, n_buf).astype(BF16)
    tab = tab_ref[0]
    s = jnp.dot(qbd, kt, preferred_element_type=F32) + tab[:, 0:n_buf]
    s_new = (jnp.sum(qbd.astype(F32) * _bf16_round(new_ref[0, 0, 0:1]), axis=1, keepdims=True)
             + tab[:, n_buf:n_buf + 1])
    m = jnp.maximum(jnp.max(s, axis=1, keepdims=True), s_new)
    p = jnp.exp(s - m)
    p_new = jnp.exp(s_new - m)
    l = jnp.sum(p, axis=1, keepdims=True) + p_new
    o_ref[0, 0] = (_nt_dot_bf16(p, vt) + _bf16_round(p_new) * _bf16_round(new_ref[0, 0, 1:2])) / l


def _dil_sample(state_t, layer, qbd, new_kv, table):
    n_req = state_t.shape[1]
    n_buf = state_t.shape[-1]
    half = DIL_HEADS // 2
    width = half * HEAD_DIM
    return pl.pallas_call(
        functools.partial(_dil_sample_kernel, n_buf=n_buf),
        grid=(n_req, 2),
        in_specs=[pl.BlockSpec((1, 1, half, width), lambda r, hh: (r, hh, 0, 0)),
                  pl.BlockSpec((1, 1, 2, width), lambda r, hh: (r, hh, 0, 0)),
                  pl.BlockSpec((1, 1, 1, half, HEAD_DIM, n_buf), lambda r, hh: (layer, r, 0, hh, 0, 0)),
                  pl.BlockSpec((1, 1, 1, half, HEAD_DIM, n_buf), lambda r, hh: (layer, r, 1, hh, 0, 0)),
                  pl.BlockSpec((1, half, table.shape[2]), lambda r, hh: (hh, 0, 0))],
        out_specs=pl.BlockSpec((1, 1, half, width), lambda r, hh: (r, hh, 0, 0)),
        out_shape=jax.ShapeDtypeStruct((n_req, 2, half, width), F32),
        compiler_params=_cparams("arbitrary", "arbitrary"),
        name="dilated_sample",
    )(qbd, new_kv, state_t, state_t, table)


def _group_queries(q_row):
    row8 = lax.broadcasted_iota(jnp.int32, (8, LANES), 0)
    lo = lax.broadcasted_iota(jnp.int32, (8, LANES), 1) < HEAD_DIM
    out = []
    for g in range(NSA_KV_HEADS):
        qg = jnp.zeros((8, LANES), F32)
        for r in range(NSA_GROUP):
            h = g * NSA_GROUP + r
            blk = jnp.broadcast_to(q_row[:, LANES * (h // 2):LANES * (h // 2 + 1)], (8, LANES))
            if h % 2 != g:
                blk = pltpu.roll(blk, HEAD_DIM, 1)
            qg = jnp.where((row8 == r) & (lo if g == 0 else jnp.logical_not(lo)), blk, qg)
        out.append(qg)
    return out


def _nsa_select_kernel(q_ref, kc_ref, vc_ref, bcmp_ref, oc_ref, idx_ref, *, n_blk_pad, cur):
    n_cmp_pad = kc_ref.shape[1]
    kcb = kc_ref[0].astype(BF16)
    vcb = vc_ref[0].astype(BF16)
    row8 = lax.broadcasted_iota(jnp.int32, (8, n_cmp_pad), 0)
    ci = lax.broadcasted_iota(jnp.int32, (n_cmp_pad, n_blk_pad), 0) * CMP_STRIDE
    sj = lax.broadcasted_iota(jnp.int32, (n_cmp_pad, n_blk_pad), 1) * SEL_BLOCK
    cover = jnp.where((ci < sj + SEL_BLOCK) & (ci + CMP_LEN > sj), 1.0, 0.0).astype(F32)
    blk = lax.broadcasted_iota(jnp.int32, (8, n_blk_pad), 1)
    forced = (blk == 0) | (blk == cur) | (blk == cur - 1)
    lane = lax.broadcasted_iota(jnp.int32, (8, LANES), 1)
    for g, qg in enumerate(_group_queries(q_ref[0].astype(F32))):
        bias = bcmp_ref[8 * g:8 * g + 8, :]
        ok = (bias > 0.5 * NEG) & (row8 < NSA_GROUP)
        s = jnp.where(ok, _nt_dot(qg.astype(BF16), kcb) + bias, NEG)
        m = jnp.max(s, axis=1, keepdims=True)
        e = jnp.where(ok, jnp.exp(s - m), 0.0)
        p = e / jnp.maximum(jnp.sum(e, axis=1, keepdims=True), TINY)
        oc_ref[0, g] = jnp.dot(p.astype(BF16), vcb, preferred_element_type=F32)
        psum = jnp.broadcast_to(jnp.sum(p, axis=0, keepdims=True), (8, n_cmp_pad))
        imp = jnp.dot(psum, cover, preferred_element_type=F32, precision=lax.Precision.HIGHEST)
        imp = jnp.where(forced, FORCED_SCORE, imp)
        imp = jnp.where(blk <= cur, imp, -jnp.inf)

        def body(it, carry):
            imp, idx = carry
            top = jnp.max(imp, axis=1, keepdims=True)
            first = jnp.min(jnp.where(imp == top, blk, n_blk_pad), axis=1, keepdims=True)
            idx = jnp.where(lane == it, jnp.where(top > -jnp.inf, first, -1), idx)
            return jnp.where(blk == first, -jnp.inf, imp), idx

        _, idx = lax.fori_loop(0, SEL_TOPK, body, (imp, jnp.full((8, LANES), -1, jnp.int32)))
        idx_ref[0, g] = idx


def _nsa_select(q, kc, vc, bcmp_row, n_blk_pad, cur):
    n_req = q.shape[0]
    n_cmp_pad = kc.shape[1]
    return pl.pallas_call(
        functools.partial(_nsa_select_kernel, n_blk_pad=n_blk_pad, cur=cur),
        grid=(n_req,),
        in_specs=[pl.BlockSpec((1, 1, NSA_HEADS * HEAD_DIM), lambda r: (r, 0, 0)),
                  pl.BlockSpec((1, n_cmp_pad, LANES), lambda r: (r, 0, 0)),
                  pl.BlockSpec((1, n_cmp_pad, LANES), lambda r: (r, 0, 0)),
                  pl.BlockSpec(bcmp_row.shape, lambda r: (0, 0))],
        out_specs=[pl.BlockSpec((1, NSA_KV_HEADS, 8, LANES), lambda r: (r, 0, 0, 0)),
                   pl.BlockSpec((1, NSA_KV_HEADS, 8, LANES), lambda r: (r, 0, 0, 0))],
        out_shape=[jax.ShapeDtypeStruct((n_req, NSA_KV_HEADS, 8, LANES), F32),
                   jax.ShapeDtypeStruct((n_req, NSA_KV_HEADS, 8, LANES), jnp.int32)],
        compiler_params=_cparams("arbitrary"),
        name="nsa_sample_select",
    )(q, kc, vc, bcmp_row)


def _nsa_attend_kernel(idx_ref, pt_ref, q_ref, blk0_ref, blk1_ref, fsel_ref, wbuf_ref, fwin_ref, nsel_ref, nwin_ref,
                       oc_ref, small_ref, o_ref, m_ref, l_ref, acc_ref, ow_ref, *, n_past_blk, new_lane, n_win):
    r_idx = pl.program_id(0)
    k_idx = pl.program_id(1)
    qgs = [qg[:, g * HEAD_DIM:(g + 1) * HEAD_DIM] for g, qg in enumerate(_group_queries(q_ref[0].astype(F32)))]
    row8 = lax.broadcasted_iota(jnp.int32, (8, LANES), 0)
    lane = lax.broadcasted_iota(jnp.int32, (8, LANES), 1)
    n_pages_past = n_past_blk // 2

    @pl.when(k_idx == 0)
    def _():
        for g, qg in enumerate(qgs):
            grp = slice(g * HEAD_DIM, (g + 1) * HEAD_DIM)
            ksel_new = _bf16_round(nsel_ref[0][:, 2 * LANES:3 * LANES][:, grp])
            vsel_new = _bf16_round(nsel_ref[0][:, 3 * LANES:4 * LANES][:, grp])
            kwin_new = _bf16_round(nwin_ref[0][:, 0:LANES][:, grp])
            vwin_new = _bf16_round(nwin_ref[0][:, LANES:2 * LANES][:, grp])
            rows = slice(8 * g, 8 * g + 8)
            s_new = jnp.sum(qg * ksel_new, axis=1, keepdims=True) + fsel_ref[n_pages_past, rows, new_lane:new_lane + 1]
            m_ref[g] = jnp.broadcast_to(s_new, (8, LANES))
            l_ref[g] = jnp.ones((8, LANES), F32)
            acc_ref[g] = jnp.broadcast_to(vsel_new, (8, HEAD_DIM))
            kw_t = wbuf_ref[0, 0, 0, g].astype(BF16)
            vw_t = wbuf_ref[0, 0, 1, g].astype(BF16)
            s = jnp.dot(qg.astype(BF16), kw_t, preferred_element_type=F32) + fwin_ref[rows, 0:n_win]
            s_wn = jnp.sum(qg * kwin_new, axis=1, keepdims=True) + fwin_ref[rows, n_win:n_win + 1]
            m = jnp.maximum(jnp.max(s, axis=1, keepdims=True), s_wn)
            p = jnp.exp(s - m)
            p_new = jnp.exp(s_wn - m)
            l = jnp.sum(p, axis=1, keepdims=True) + p_new
            ow_ref[g] = (_nt_dot_bf16(p, vw_t) + _bf16_round(p_new) * vwin_new) / l

    for g, (qg, blk_ref) in enumerate(zip(qgs, (blk0_ref, blk1_ref))):
        b = idx_ref[r_idx, g * SEL_TOPK + k_idx]
        valid = (b >= 0) & (b < n_past_blk)
        kt = blk_ref[0, 0, 0, g].astype(BF16)
        vt = blk_ref[0, 0, 1, g].astype(BF16)
        page = jnp.clip(lax.shift_right_arithmetic(b, 1), 0, n_pages_past - 1)
        bias = fsel_ref[page, 8 * g:8 * g + 8, :]
        in_blk = lax.shift_right_arithmetic(lane, SEL_SHIFT) == jnp.bitwise_and(b, 1)
        s = jnp.where(valid & in_blk, jnp.dot(qg.astype(BF16), kt, preferred_element_type=F32) + bias, NEG)
        m_old = m_ref[g][:, 0:1]
        m_new = jnp.maximum(m_old, jnp.max(s, axis=1, keepdims=True))
        alpha = jnp.exp(m_old - m_new)
        p = jnp.exp(s - m_new)
        l_ref[g] = jnp.broadcast_to(alpha * l_ref[g][:, 0:1] + jnp.sum(p, axis=1, keepdims=True), (8, LANES))
        acc_ref[g] = alpha * acc_ref[g] + _nt_dot_bf16(p, vt)
        m_ref[g] = jnp.broadcast_to(m_new, (8, LANES))

    @pl.when(k_idx == SEL_TOPK - 1)
    def _():
        gates = small_ref[0]
        for g in range(NSA_KV_HEADS):
            gate = []
            for b in range(3):
                col = jnp.zeros((8, 1), F32)
                for r in range(NSA_GROUP):
                    lane_i = 3 * (g * NSA_GROUP + r) + b
                    col = jnp.where(row8[:, 0:1] == r, gates[:, lane_i:lane_i + 1], col)
                gate.append(col)
            o_cmp = oc_ref[0, g][:, g * HEAD_DIM:(g + 1) * HEAD_DIM]
            o_ref[0, g] = gate[0] * o_cmp + gate[1] * (acc_ref[g] / l_ref[g][:, 0:1]) + gate[2] * ow_ref[g]


def _nsa_attend(idx, page_table, q, cache_t, layer, fsel, win_t, fwin, new_sel, new_win, o_cmp, small, n_past_blk,
                new_lane):
    n_req = q.shape[0]
    n_win = win_t.shape[-1]

    def blk_map(g):
        def index(r, k, idx_ref, pt_ref):
            b = idx_ref[r, g * SEL_TOPK + k]
            b = jnp.where((b >= 0) & (b < n_past_blk), b, 0)
            return layer, pt_ref[r, lax.shift_right_arithmetic(b, 1)], 1, 0, 0, 0
        return index

    const = lambda *shape: (lambda r, k, idx_ref, pt_ref: shape)
    per_req3 = lambda r, k, idx_ref, pt_ref: (r, 0, 0)
    per_req4 = lambda r, k, idx_ref, pt_ref: (r, 0, 0, 0)
    page_blk = (1, 1, 2, NSA_KV_HEADS, HEAD_DIM, PAGE_SIZE)
    grid_spec = pltpu.PrefetchScalarGridSpec(
        num_scalar_prefetch=2,
        grid=(n_req, SEL_TOPK),
        in_specs=[pl.BlockSpec((1, 1, NSA_HEADS * HEAD_DIM), per_req3),
                  pl.BlockSpec(page_blk, blk_map(0)),
                  pl.BlockSpec(page_blk, blk_map(1)),
                  pl.BlockSpec(fsel.shape, const(0, 0, 0)),
                  pl.BlockSpec((1, 1, 2, NSA_KV_HEADS, HEAD_DIM, n_win),
                               lambda r, k, idx_ref, pt_ref: (layer, r, 0, 0, 0, 0)),
                  pl.BlockSpec(fwin.shape, const(0, 0)),
                  pl.BlockSpec((1, 1, 4 * LANES), per_req3),
                  pl.BlockSpec((1, 1, 2 * LANES), per_req3),
                  pl.BlockSpec((1, NSA_KV_HEADS, 8, LANES), per_req4),
                  pl.BlockSpec((1, 1, LANES), per_req3)],
        out_specs=pl.BlockSpec((1, NSA_KV_HEADS, 8, HEAD_DIM), per_req4),
        scratch_shapes=[pltpu.VMEM((NSA_KV_HEADS, 8, LANES), F32), pltpu.VMEM((NSA_KV_HEADS, 8, LANES), F32),
                        pltpu.VMEM((NSA_KV_HEADS, 8, HEAD_DIM), F32), pltpu.VMEM((NSA_KV_HEADS, 8, HEAD_DIM), F32)],
    )
    return pl.pallas_call(
        functools.partial(_nsa_attend_kernel, n_past_blk=n_past_blk, new_lane=new_lane, n_win=n_win),
        grid_spec=grid_spec,
        out_shape=jax.ShapeDtypeStruct((n_req, NSA_KV_HEADS, 8, HEAD_DIM), F32),
        compiler_params=_cparams("arbitrary", "arbitrary"),
        name="nsa_sample_attend",
    )(idx, page_table, q, cache_t, cache_t, fsel, win_t, fwin, new_sel, new_win, o_cmp, small)


def kernel(x_prompt, x_sample, cache_nsa_kv, cache_fox_kv, cache_fox_logf, state_nsa_win_kv, state_dil_kv, page_table,
           c_prompt, c_sample, rel_bias, norm_g, w_ada, b_ada, w_in_a, nsa_gate_b, fox_f_b, nsa_cmp_w1, nsa_cmp_w2,
           nsa_cmp_pe, w_out_a, w_in_c, w_out_c, w_mlp1, w_mlp2):
    bp, s, d = x_prompt.shape
    bd = x_sample.shape[0]
    depth = w_ada.shape[0]
    n_pages = s // PAGE_SIZE
    n_cmp_pad = s // CMP_STRIDE

    mods = _ada_params(jnp.concatenate([c_prompt, c_sample], axis=0), w_ada, b_ada).reshape(depth, bp + bd, 6, d)
    tab_sel = _toeplitz_table(rel_bias, NSA_HEADS, 15, "causal")
    tab_win = _toeplitz_table(rel_bias, NSA_HEADS, NSA_WINDOW // TILE + 2, "window")
    assert all(window // dil == TILE for window, dil in DIL_BRANCHES)
    tab_dil = [_toeplitz_table(rel_bias, DIL_HEADS, 3, "branch", dil) for _, dil in DIL_BRANCHES]
    bcmp = _cmp_bias_table(rel_bias, s, TILE, n_cmp_pad, 0)

    past_len = page_table.shape[1] * PAGE_SIZE
    assert past_len % SEL_BLOCK == 0
    n_past_blk = past_len // SEL_BLOCK
    n_blk_pad = -(-(n_past_blk + 1) // LANES) * LANES
    far = 1 << 30
    bcmp_s = _affine_bias(rel_bias, NSA_GROUPED_ROWS, past_len // CMP_STRIDE, past_len - (CMP_LEN - 1), -CMP_STRIDE, far)
    assert past_len % PAGE_SIZE == 0
    n_pages_s = past_len // PAGE_SIZE
    fsel = _affine_bias(rel_bias, NSA_GROUPED_ROWS, (n_pages_s + 1) * PAGE_SIZE, past_len, -1, far)
    fsel = jnp.swapaxes(fsel.reshape(len(NSA_GROUPED_ROWS), n_pages_s + 1, PAGE_SIZE), 0, 1)
    n_win_buf = state_nsa_win_kv.shape[2]
    fwin = _affine_bias(rel_bias, NSA_GROUPED_ROWS, n_win_buf + LANES, n_win_buf, -1, NSA_WINDOW)
    n_dil_buf = state_dil_kv.shape[2]
    tab_dil_s = _affine_bias(rel_bias, range(DIL_HEADS), n_dil_buf + LANES, n_dil_buf, -1, far, union=True)
    tab_dil_s = tab_dil_s.reshape(2, DIL_HEADS // 2, n_dil_buf + LANES)
    cache_nsa_t = _feature_major(cache_nsa_kv)
    cache_fox_t = _feature_major(cache_fox_kv)
    logf_t = jnp.swapaxes(cache_fox_logf, 2, 3)
    win_t = _feature_major(state_nsa_win_kv)
    dil_t = _feature_major(state_dil_kv)

    xp = x_prompt
    xs = x_sample.reshape(1, bd, d)
    per_req = lambda a: a.reshape(bd, 1, a.shape[-1])
    prompt_pages = jnp.arange(bp * n_pages, dtype=jnp.int32).reshape(bp, n_pages)
    nsa_p, nsa_s, fkv_p, fkv_s, lf_p, lf_s, win_p, win_s, dil_p, dil_s = [], [], [], [], [], [], [], [], [], []
    for layer in range(depth):
        mp = [mods[layer, :bp, k].reshape(bp, 1, d) for k in range(6)]
        ms = [mods[layer, bp:, k].reshape(1, bd, d) for k in range(6)]
        g = [norm_g[layer, k].reshape(1, d) for k in range(4)]
        i = layer // 2
        if layer % 2 == 0:
            wa = w_in_a[i]
            w_in = jnp.concatenate([wa[:, 0:1280], wa[:, 1304:2840], wa[:, 1280:1304], wa[:, 2840:2848],
                                    jnp.zeros((d, LANES - SMALL_GATES - FOX_HEADS), F32)], axis=1).astype(BF16)
            sb = jnp.concatenate([nsa_gate_b[i].reshape(-1), fox_f_b[i],
                                  jnp.zeros((LANES - SMALL_GATES - FOX_HEADS,), F32)]).reshape(1, LANES)
            half = CMP_STRIDE * HEAD_DIM
            w1cat = jnp.concatenate([nsa_cmp_w1[i][:, :half], nsa_cmp_w1[i][:, half:]], axis=2).astype(BF16)
            w2 = nsa_cmp_w2[i].astype(BF16)
            pe = nsa_cmp_pe[i].reshape(2, 1, CMP_LEN * HEAD_DIM)
            w_out = w_out_a[i].astype(BF16)

            qn, nsa4, nsabf, win, qf, fkv, fkvbf, small = _proj_even(xp, mp[0], mp[1], g[0], w_in, sb)
            c, ct = _cumsum(small)
            o_f = _fox_prompt(qf, fkvbf, c, ct)
            kc, vc = _compress(nsa4.reshape(bp * n_pages, PAGE_SIZE, 512), prompt_pages, w1cat, w2, pe)
            o_n = _nsa_prompt(qn, kc, vc, bcmp, nsabf, tab_sel, tab_win, small)
            op_a, op_b = o_n, o_f
            nsa_p.append(nsa4.reshape(bp, s, 4, NSA_KV_HEADS, HEAD_DIM))
            fkv_p.append(fkv.reshape(bp, s, 2, FOX_HEADS, HEAD_DIM))
            lf_p.append(small[:, :, SMALL_GATES:SMALL_GATES + FOX_HEADS])
            n_win = min(NSA_WINDOW, s)
            win_p.append(win[:, s - n_win:].reshape(bp, n_win, 2, NSA_KV_HEADS, HEAD_DIM))

            qn_s, nsa4_s, _, win_new, qf_s, fkv_s_, _, small_s = _proj_even(xs, ms[0], ms[1], g[0], w_in, sb)
            kc_s, vc_s = _compress(cache_nsa_t, page_table, w1cat, w2, pe, layer=i)
            o_cmp, idx = _nsa_select(per_req(qn_s), kc_s, vc_s, bcmp_s, n_blk_pad, n_past_blk)
            idx = idx[:, :, 0, :SEL_TOPK].reshape(bd, NSA_KV_HEADS * SEL_TOPK)
            o_nsa = _nsa_attend(idx, page_table, per_req(qn_s), cache_nsa_t, i, fsel, win_t, fwin, per_req(nsa4_s),
                                per_req(win_new), o_cmp, per_req(small_s), n_past_blk, 0)
            os_a = o_nsa[:, :, :NSA_GROUP].reshape(1, bd, NSA_HEADS * HEAD_DIM).astype(BF16)
            lf_new = small_s[0, :, SMALL_GATES:SMALL_GATES + FOX_HEADS]
            o_fox = _fox_sample(cache_fox_t, i, logf_t, page_table, _block_diag_queries(qf_s[0], FOX_HEADS),
                                fkv_s_.reshape(bd, 2, FOX_HEADS * HEAD_DIM),
                                jnp.broadcast_to(lf_new[:, :, None], (bd, FOX_HEADS, LANES)))
            os_b = _diag_blocks(o_fox, FOX_HEADS).reshape(1, bd, -1).astype(BF16)
            nsa_s.append(nsa4_s.reshape(bd, 1, 4, NSA_KV_HEADS, HEAD_DIM))
            fkv_s.append(fkv_s_.reshape(bd, 1, 2, FOX_HEADS, HEAD_DIM))
            lf_s.append(small_s[0, :, SMALL_GATES:SMALL_GATES + FOX_HEADS].reshape(bd, 1, FOX_HEADS))
            win_s.append(win_new.reshape(bd, 1, 2, NSA_KV_HEADS, HEAD_DIM))
        else:
            w_in = w_in_c[i].astype(BF16)
            w_out = w_out_c[i].astype(BF16)
            q, kv, kvbf = _proj_odd(xp, mp[0], mp[1], g[0], w_in)
            branches = [_dil_branch_prompt(q, kvbf, tab, dil) for tab, (_, dil) in zip(tab_dil, DIL_BRANCHES)]
            n_dil = min(DIL_BRANCHES[-1][0], s)
            dil_p.append(kv[:, s - n_dil:].reshape(bp, n_dil, 2, DIL_HEADS, HEAD_DIM))
            q_s, kv_s, _ = _proj_odd(xs, ms[0], ms[1], g[0], w_in)
            half = DIL_HEADS // 2
            qbd = _block_diag_queries(q_s.reshape(bd * 2, half * HEAD_DIM), half).reshape(bd, 2, half, half * HEAD_DIM)
            new_kv = jnp.swapaxes(kv_s.reshape(bd, 2, 2, half * HEAD_DIM), 1, 2)
            o_dil = _dil_sample(dil_t, i, qbd, new_kv, tab_dil_s)
            os_a = os_b = _diag_blocks(o_dil.reshape(bd * 2, half, half * HEAD_DIM), half).reshape(1, bd, -1).astype(BF16)
            dil_s.append(kv_s.reshape(bd, 1, 2, DIL_HEADS, HEAD_DIM))
        if layer % 2 == 0:
            xp = _post(op_a, op_b, w_out, xp, mp[2], g[1])
            xs = _post(os_a, os_b, w_out, xs, ms[2], g[1])
        else:
            xp = _post_dil([o for o, _ in branches], [l for _, l in branches], w_out, xp, mp[2], g[1])
            xs = _post(os_a, os_b, w_out, xs, ms[2], g[1], 0, 1)
        w1 = w_mlp1[layer].astype(BF16)
        w2m = w_mlp2[layer].astype(BF16)
        xp = _mlp(xp, mp[3], mp[4], mp[5], g[2], g[3], w1, w2m)
        xs = _mlp(xs, ms[3], ms[4], ms[5], g[2], g[3], w1, w2m)
    return (xp, xs.reshape(bd, 1, d), jnp.stack(nsa_p), jnp.stack(nsa_s), jnp.stack(fkv_p), jnp.stack(fkv_s),
            jnp.stack(lf_p), jnp.stack(lf_s), jnp.stack(win_p), jnp.stack(win_s), jnp.stack(dil_p), jnp.stack(dil_s))
```

```python
import functools
import math

import numpy as np
import jax
import jax.numpy as jnp
from jax import lax
from jax.experimental import pallas as pl
from jax.experimental.pallas import tpu as pltpu

F32 = jnp.float32
BF16 = jnp.bfloat16

HEAD_DIM = 64
NSA_HEADS = 8
NSA_KV_HEADS = 2
NSA_GROUP = NSA_HEADS // NSA_KV_HEADS
FOX_HEADS = 8
DIL_HEADS = 16
CMP_LEN = 32
CMP_STRIDE = 16
CMP_HIDDEN = 4 * HEAD_DIM
SEL_BLOCK = 64
SEL_TOPK = 16
NSA_WINDOW = 512
FORCED_SCORE = 1e9
DIL_BRANCHES = ((128, 1), (512, 4), (2048, 16))
N_BUCKETS = 32
BUCKET_EXACT = 16
BUCKET_MAX_DIST = 2048
NORM_EPS = 1e-6
TINY = 1e-30
PAGE_SIZE = 128

LANES = 128
VMEM_LIMIT_BYTES = 56 * 1024 * 1024

NEG = -1e30
QK_SCALE = HEAD_DIM ** -0.5
TILE = 128
SEL_SHIFT = 6
SMALL_GATES = 3 * NSA_HEADS


def _cparams(*sem):
    return pltpu.CompilerParams(dimension_semantics=sem, vmem_limit_bytes=VMEM_LIMIT_BYTES)


def _bucket_thresholds():
    d = np.arange(0, 2 * BUCKET_MAX_DIST + 1)
    df = np.maximum(d, 1).astype(np.float64)
    ratio = math.log(BUCKET_MAX_DIST / BUCKET_EXACT)
    log_b = BUCKET_EXACT + (np.log(df / BUCKET_EXACT) / ratio * (N_BUCKETS - BUCKET_EXACT)).astype(np.int64)
    bucket = np.where(d < BUCKET_EXACT, d, np.clip(log_b, BUCKET_EXACT, N_BUCKETS - 1))
    return [int(np.argmax(bucket >= b)) for b in range(1, N_BUCKETS)]


BUCKET_THR = _bucket_thresholds()


def _bias_of_distance(d, tab_ref, h):
    val = jnp.full(d.shape, tab_ref[0, h], F32)
    for b in range(1, N_BUCKETS):
        val = jnp.where(d >= BUCKET_THR[b - 1], tab_ref[b, h], val)
    return val


def _toeplitz_kernel(tab_ref, o_ref, *, n_heads, mode, dil):
    r = lax.broadcasted_iota(jnp.int32, (TILE, TILE), 0)
    c = lax.broadcasted_iota(jnp.int32, (TILE, TILE), 1)
    idx = pl.program_id(0)
    if mode == "branch":
        d = idx * TILE + r - c
        ok = (idx < 2) & (d >= 0) & (d <= TILE)
    else:
        d = (idx - 1) * TILE + r - c
        ok = (idx > 0) & (d >= 0)
        if mode == "window":
            ok = ok & (d < NSA_WINDOW)
    dd = jnp.maximum(d, 0) * dil
    for h in range(n_heads):
        o_ref[h, 0] = jnp.where(ok, _bias_of_distance(dd, tab_ref, h), NEG)


def _toeplitz_table(rel_bias, n_heads, n_idx, mode, dil=1):
    return pl.pallas_call(
        functools.partial(_toeplitz_kernel, n_heads=n_heads, mode=mode, dil=dil),
        grid=(n_idx,),
        in_specs=[pl.BlockSpec(memory_space=pltpu.SMEM)],
        out_specs=pl.BlockSpec((n_heads, 1, TILE, TILE), lambda i: (0, i, 0, 0)),
        out_shape=jax.ShapeDtypeStruct((n_heads, n_idx, TILE, TILE), F32),
        compiler_params=_cparams("arbitrary"),
        name="bias_toeplitz_%s%d" % (mode, dil),
    )(rel_bias)


def _cmp_bias_kernel(tab_ref, o_ref, *, tq, n_cmp_pad, t_base):
    t = t_base + pl.program_id(0) * tq + lax.broadcasted_iota(jnp.int32, (tq, n_cmp_pad), 0)
    n = lax.broadcasted_iota(jnp.int32, (tq, n_cmp_pad), 1)
    d = t - (n * CMP_STRIDE + CMP_LEN - 1)
    dd = jnp.maximum(d, 0)
    for h in range(NSA_HEADS):
        o_ref[h] = jnp.where(d >= 0, _bias_of_distance(dd, tab_ref, h), NEG)


def _cmp_bias_table(rel_bias, n_rows, tq, n_cmp_pad, t_base):
    return pl.pallas_call(
        functools.partial(_cmp_bias_kernel, tq=tq, n_cmp_pad=n_cmp_pad, t_base=t_base),
        grid=(n_rows // tq,),
        in_specs=[pl.BlockSpec(memory_space=pltpu.SMEM)],
        out_specs=pl.BlockSpec((NSA_HEADS, tq, n_cmp_pad), lambda i: (0, i, 0)),
        out_shape=jax.ShapeDtypeStruct((NSA_HEADS, n_rows, n_cmp_pad), F32),
        compiler_params=_cparams("arbitrary"),
        name="bias_cmp",
    )(rel_bias)


def _affine_bias_kernel(tab_ref, o_ref, *, heads, d0, step, limit, union):
    n = o_ref.shape[1]
    d = d0 + step * lax.broadcasted_iota(jnp.int32, (1, n), 1)
    ok = (d >= 0) & (d < limit)
    extra = jnp.zeros((1, n), F32)
    if union:
        cnt = jnp.zeros((1, n), F32)
        for window, dil in DIL_BRANCHES:
            cnt = cnt + jnp.where((d >= 0) & (d <= window) & (jnp.bitwise_and(d, dil - 1) == 0), 1.0, 0.0)
        ok = ok & (cnt > 0.5)
        extra = jnp.log(jnp.maximum(cnt, 1.0))
    dd = jnp.maximum(d, 0)
    for row, h in enumerate(heads):
        if h is None:
            o_ref[row:row + 1, :] = jnp.zeros((1, n), F32)
        else:
            o_ref[row:row + 1, :] = jnp.where(ok, _bias_of_distance(dd, tab_ref, h) + extra, NEG)


def _affine_bias(rel_bias, heads, n, d0, step, limit, union=False):
    return pl.pallas_call(
        functools.partial(_affine_bias_kernel, heads=tuple(heads), d0=d0, step=step, limit=limit, union=union),
        in_specs=[pl.BlockSpec(memory_space=pltpu.SMEM)],
        out_shape=jax.ShapeDtypeStruct((len(heads), n), F32),
        compiler_params=pltpu.CompilerParams(vmem_limit_bytes=VMEM_LIMIT_BYTES),
        name="bias_affine",
    )(rel_bias)


NSA_GROUPED_ROWS = tuple((NSA_GROUP * (row // 8) + row % 8) if row % 8 < NSA_GROUP else None
                         for row in range(8 * NSA_KV_HEADS))


def _ada_kernel(c_ref, w_ref, b_ref, o_ref):
    c = c_ref[...]
    s = (c * jax.nn.sigmoid(c)).astype(BF16)
    o_ref[0] = jnp.dot(s, w_ref[0].astype(BF16), preferred_element_type=F32) + b_ref[0]


def _ada_params(c_all, w_ada, b_ada):
    depth, d, d6 = w_ada.shape
    m = c_all.shape[0]
    return pl.pallas_call(
        _ada_kernel,
        grid=(depth, d6 // d),
        in_specs=[pl.BlockSpec((m, d), lambda l, j: (0, 0)),
                  pl.BlockSpec((1, d, d), lambda l, j: (l, 0, j)),
                  pl.BlockSpec((1, 1, d), lambda l, j: (l, 0, j))],
        out_specs=pl.BlockSpec((1, m, d), lambda l, j: (l, 0, j)),
        out_shape=jax.ShapeDtypeStruct((depth, m, d6), F32),
        compiler_params=_cparams("arbitrary", "arbitrary"),
        name="adaln",
    )(c_all, w_ada, b_ada.reshape(depth, 1, d6))


def _norm_mod(x, g, scale, shift):
    y = x * lax.rsqrt(jnp.mean(x * x, axis=-1, keepdims=True) + NORM_EPS)
    return (y * g) * (1.0 + scale) + shift


def _row_tile(s, want):
    return want if s % want == 0 else s


def _mod_spec(mod, tm):
    if mod.shape[1] == 1:
        return pl.BlockSpec((1, 1, mod.shape[2]), lambda b, i: (b, 0, 0))
    return pl.BlockSpec((1, tm, mod.shape[2]), lambda b, i: (b, i, 0))


def _proj_even_kernel(x_ref, sh_ref, sc_ref, g_ref, w_ref, sb_ref,
                      qn_ref, nsa4_ref, nsabf_ref, win_ref, qf_ref, fkv_ref, fkvbf_ref, small_ref):
    h = _norm_mod(x_ref[0], g_ref[...], sc_ref[0], sh_ref[0]).astype(BF16)
    z = jnp.dot(h, w_ref[...], preferred_element_type=F32)
    qn_ref[0] = (z[:, 0:512] * QK_SCALE).astype(BF16)
    nsa4_ref[0] = z[:, 512:1024]
    nsabf_ref[0] = z[:, 512:1280].astype(BF16)
    win_ref[0] = z[:, 1024:1280]
    qf_ref[0] = (z[:, 1280:1792] * QK_SCALE).astype(BF16)
    fkv_ref[0] = z[:, 1792:2816]
    fkvbf_ref[0] = z[:, 1792:2816].astype(BF16)
    zs = z[:, 2816:2944] + sb_ref[...]
    lane = lax.broadcasted_iota(jnp.int32, zs.shape, 1)
    sig = jax.nn.sigmoid(zs)
    lsg = jnp.minimum(zs, 0.0) - jnp.log1p(jnp.exp(-jnp.abs(zs)))
    small_ref[0] = jnp.where(lane < SMALL_GATES, sig, lsg)


def _proj_even(x, shift, scale, g, w, sb):
    bx, s, d = x.shape
    tm = _row_tile(s, 256)
    n = w.shape[1]
    widths = (512, 512, 768, 256, 512, 1024, 1024, 128)
    dtypes = (BF16, F32, BF16, F32, BF16, F32, BF16, F32)
    return pl.pallas_call(
        _proj_even_kernel,
        grid=(bx, s // tm),
        in_specs=[pl.BlockSpec((1, tm, d), lambda b, i: (b, i, 0)),
                  _mod_spec(shift, tm), _mod_spec(scale, tm),
                  pl.BlockSpec((1, d), lambda b, i: (0, 0)),
                  pl.BlockSpec((d, n), lambda b, i: (0, 0)),
                  pl.BlockSpec((1, LANES), lambda b, i: (0, 0))],
        out_specs=[pl.BlockSpec((1, tm, wd), lambda b, i: (b, i, 0)) for wd in widths],
        out_shape=[jax.ShapeDtypeStruct((bx, s, wd), dt) for wd, dt in zip(widths, dtypes)],
        compiler_params=_cparams("arbitrary", "arbitrary"),
        name="proj_even",
    )(x, shift, scale, g, w, sb)


def _proj_odd_kernel(x_ref, sh_ref, sc_ref, g_ref, w_ref, q_ref, kv_ref, kvbf_ref):
    h = _norm_mod(x_ref[0], g_ref[...], sc_ref[0], sh_ref[0]).astype(BF16)
    z = jnp.dot(h, w_ref[...], preferred_element_type=F32)
    q_ref[0] = (z[:, 0:1024] * QK_SCALE).astype(BF16)
    kv_ref[0] = z[:, 1024:3072]
    kvbf_ref[0] = z[:, 1024:3072].astype(BF16)


def _proj_odd(x, shift, scale, g, w):
    bx, s, d = x.shape
    tm = _row_tile(s, 256)
    n = w.shape[1]
    widths = (1024, 2048, 2048)
    dtypes = (BF16, F32, BF16)
    return pl.pallas_call(
        _proj_odd_kernel,
        grid=(bx, s // tm),
        in_specs=[pl.BlockSpec((1, tm, d), lambda b, i: (b, i, 0)),
                  _mod_spec(shift, tm), _mod_spec(scale, tm),
                  pl.BlockSpec((1, d), lambda b, i: (0, 0)),
                  pl.BlockSpec((d, n), lambda b, i: (0, 0))],
        out_specs=[pl.BlockSpec((1, tm, wd), lambda b, i: (b, i, 0)) for wd in widths],
        out_shape=[jax.ShapeDtypeStruct((bx, s, wd), dt) for wd, dt in zip(widths, dtypes)],
        compiler_params=_cparams("arbitrary", "arbitrary"),
        name="proj_odd",
    )(x, shift, scale, g, w)


def _post_kernel(oa_ref, ob_ref, w_ref, x_ref, gate_ref, g_ref, o_ref):
    half = oa_ref.shape[2]
    y = jnp.dot(oa_ref[0], w_ref[0:half, :], preferred_element_type=F32)
    y = y + jnp.dot(ob_ref[0], w_ref[half:, :], preferred_element_type=F32)
    yn = y * lax.rsqrt(jnp.mean(y * y, axis=-1, keepdims=True) + NORM_EPS) * g_ref[...]
    o_ref[0] = x_ref[0] + gate_ref[0] * yn


def _post(o_a, o_b, w_out, x, gate, g, col_a=0, col_b=0):
    bx, s, d = x.shape
    tm = _row_tile(s, 512)
    half = w_out.shape[0] // 2
    return pl.pallas_call(
        _post_kernel,
        grid=(bx, s // tm),
        in_specs=[pl.BlockSpec((1, tm, half), lambda b, i: (b, i, col_a)),
                  pl.BlockSpec((1, tm, half), lambda b, i: (b, i, col_b)),
                  pl.BlockSpec(w_out.shape, lambda b, i: (0, 0)),
                  pl.BlockSpec((1, tm, d), lambda b, i: (b, i, 0)),
                  _mod_spec(gate, tm),
                  pl.BlockSpec((1, d), lambda b, i: (0, 0))],
        out_specs=pl.BlockSpec((1, tm, d), lambda b, i: (b, i, 0)),
        out_shape=jax.ShapeDtypeStruct((bx, s, d), F32),
        compiler_params=_cparams("arbitrary", "arbitrary"),
        name="post",
    )(o_a, o_b, w_out, x, gate, g)


def _mlp_kernel(x_ref, sh_ref, sc_ref, gate_ref, g2_ref, g3_ref, w1_ref, w2_ref, o_ref, h_ref, acc_ref):
    j = pl.program_id(2)

    @pl.when(j == 0)
    def _():
        h_ref[...] = _norm_mod(x_ref[0], g2_ref[...], sc_ref[0], sh_ref[0]).astype(BF16)
        acc_ref[...] = jnp.zeros_like(acc_ref)

    a = jnp.maximum(jnp.dot(h_ref[...], w1_ref[...], preferred_element_type=F32), 0.0)
    acc_ref[...] += jnp.dot((a * a).astype(BF16), w2_ref[...], preferred_element_type=F32)

    @pl.when(j == pl.num_programs(2) - 1)
    def _():
        y = acc_ref[...]
        yn = y * lax.rsqrt(jnp.mean(y * y, axis=-1, keepdims=True) + NORM_EPS) * g3_ref[...]
        o_ref[0] = x_ref[0] + gate_ref[0] * yn


def _mlp(x, shift, scale, gate, g2, g3, w1, w2):
    bx, s, d = x.shape
    f = w1.shape[1]
    tm = _row_tile(s, 1024)
    tf = 1024

    def mod3(mod):
        if mod.shape[1] == 1:
            return pl.BlockSpec((1, 1, d), lambda b, i, j: (b, 0, 0))
        return pl.BlockSpec((1, tm, d), lambda b, i, j: (b, i, 0))

    return pl.pallas_call(
        _mlp_kernel,
        grid=(bx, s // tm, f // tf),
        in_specs=[pl.BlockSpec((1, tm, d), lambda b, i, j: (b, i, 0)),
                  mod3(shift), mod3(scale), mod3(gate),
                  pl.BlockSpec((1, d), lambda b, i, j: (0, 0)),
                  pl.BlockSpec((1, d), lambda b, i, j: (0, 0)),
                  pl.BlockSpec((d, tf), lambda b, i, j: (0, j)),
                  pl.BlockSpec((tf, d), lambda b, i, j: (j, 0))],
        out_specs=pl.BlockSpec((1, tm, d), lambda b, i, j: (b, i, 0)),
        out_shape=jax.ShapeDtypeStruct((bx, s, d), F32),
        scratch_shapes=[pltpu.VMEM((tm, d), BF16), pltpu.VMEM((tm, d), F32)],
        compiler_params=_cparams("arbitrary", "arbitrary", "arbitrary"),
        name="mlp",
    )(x, shift, scale, gate, g2, g3, w1, w2)


def _cumsum_kernel(x_ref, c_ref, ct_ref, carry_ref, *, tc):
    @pl.when(pl.program_id(1) == 0)
    def _():
        carry_ref[...] = jnp.zeros_like(carry_ref)

    r = lax.broadcasted_iota(jnp.int32, (tc, tc), 0)
    c = lax.broadcasted_iota(jnp.int32, (tc, tc), 1)
    tri = jnp.where(c <= r, 1.0, 0.0).astype(F32)
    cs = jnp.dot(tri, x_ref[0], preferred_element_type=F32, precision=lax.Precision.HIGHEST) + carry_ref[...]
    carry_ref[...] = cs[tc - 1:tc, :]
    c_ref[0] = cs
    ct_ref[0] = cs.T[SMALL_GATES:SMALL_GATES + FOX_HEADS, :]


def _cumsum(small):
    bx, s, _ = small.shape
    tc = _row_tile(s, 256)
    return pl.pallas_call(
        functools.partial(_cumsum_kernel, tc=tc),
        grid=(bx, s // tc),
        in_specs=[pl.BlockSpec((1, tc, LANES), lambda b, i: (b, i, 0))],
        out_specs=[pl.BlockSpec((1, tc, LANES), lambda b, i: (b, i, 0)),
                   pl.BlockSpec((1, FOX_HEADS, tc), lambda b, i: (b, 0, i))],
        out_shape=[jax.ShapeDtypeStruct((bx, s, LANES), F32),
                   jax.ShapeDtypeStruct((bx, FOX_HEADS, s), F32)],
        scratch_shapes=[pltpu.VMEM((1, LANES), F32)],
        compiler_params=_cparams("arbitrary", "arbitrary"),
        name="logf_cumsum",
    )(small)


def _nt_dot(a, b):
    return lax.dot_general(a, b, (((1,), (1,)), ((), ())), preferred_element_type=F32)


def _online_update(state, s, v):
    m, l, acc = state
    m_new = jnp.maximum(m, jnp.max(s, axis=1, keepdims=True))
    alpha = jnp.exp(m - m_new)
    p = jnp.exp(s - m_new)
    l = alpha * l + jnp.sum(p, axis=1, keepdims=True)
    acc = alpha * acc + jnp.dot(p.astype(BF16), v, preferred_element_type=F32)
    return m_new, l, acc


def _init_state(tq):
    return (jnp.full((tq, 1), NEG, F32), jnp.zeros((tq, 1), F32), jnp.zeros((tq, LANES), F32))


def _half_masks(tq):
    lane = lax.broadcasted_iota(jnp.int32, (tq, LANES), 1)
    return lane < HEAD_DIM


def _split_heads(q2, lo):
    zero = jnp.zeros_like(q2)
    return jnp.where(lo, q2, zero), jnp.where(lo, zero, q2)


def _fox_kernel(q_ref, k_ref, v_ref, c_ref, ct_ref, o_ref, *, tq, tk, n_pairs):
    grp = pl.program_id(1)
    i = pl.program_id(2)
    lo = _half_masks(tq)
    lane = lax.broadcasted_iota(jnp.int32, (tq, LANES), 1)
    cblk = c_ref[0]
    qs, cqs, heads = [], [], []
    for p in range(n_pairs):
        qs.extend(_split_heads(q_ref[0, :, p * LANES:(p + 1) * LANES], lo))
        for e in range(2):
            head = (grp * n_pairs + p) * 2 + e
            heads.append(head)
            cqs.append(jnp.sum(jnp.where(lane == SMALL_GATES + head, cblk, 0.0), axis=1, keepdims=True))

    def chunk(c, states, masked):
        off = pl.multiple_of(c * tk, tk)
        if masked:
            row = i * tq + lax.broadcasted_iota(jnp.int32, (tq, tk), 0)
            col = off + lax.broadcasted_iota(jnp.int32, (tq, tk), 1)
            ok = col <= row
        out = []
        for n in range(2 * n_pairs):
            p = n // 2
            k = k_ref[0, pl.ds(off, tk), p * LANES:(p + 1) * LANES]
            v = v_ref[0, pl.ds(off, tk), p * LANES:(p + 1) * LANES]
            ck = ct_ref[0, pl.ds(heads[n], 1), pl.ds(off, tk)]
            s = _nt_dot(qs[n], k) + cqs[n] - ck
            if masked:
                s = jnp.where(ok, s, NEG)
            out.append(_online_update(states[n], s, v))
        return tuple(out)

    n_full = (i * tq) // tk
    init = tuple(_init_state(tq) for _ in range(2 * n_pairs))
    states = lax.fori_loop(0, n_full, lambda c, st: chunk(c, st, False), init)
    states = chunk(n_full, states, True)
    outs = []
    for p in range(n_pairs):
        (_, la, acca), (_, lb, accb) = states[2 * p], states[2 * p + 1]
        outs.append(jnp.where(lo, acca / la, accb / lb))
    o_ref[0] = jnp.concatenate(outs, axis=1).astype(o_ref.dtype)


def _fox_prompt(qf, fkvbf, c, ct):
    bx, s, _ = qf.shape
    tq = _row_tile(s, 512)
    tk = _row_tile(s, 512)
    n_pairs = 1
    n_grp = FOX_HEADS // 2 // n_pairs
    wd = n_pairs * LANES
    return pl.pallas_call(
        functools.partial(_fox_kernel, tq=tq, tk=tk, n_pairs=n_pairs),
        grid=(bx, n_grp, s // tq),
        in_specs=[pl.BlockSpec((1, tq, wd), lambda b, p, i: (b, i, p)),
                  pl.BlockSpec((1, s, wd), lambda b, p, i: (b, 0, p)),
                  pl.BlockSpec((1, s, wd), lambda b, p, i: (b, 0, n_grp + p)),
                  pl.BlockSpec((1, tq, LANES), lambda b, p, i: (b, i, 0)),
                  pl.BlockSpec((1, FOX_HEADS, s), lambda b, p, i: (b, 0, 0))],
        out_specs=pl.BlockSpec((1, tq, wd), lambda b, p, i: (b, i, p)),
        out_shape=jax.ShapeDtypeStruct((bx, s, FOX_HEADS * HEAD_DIM), BF16),
        compiler_params=_cparams("arbitrary", "arbitrary", "arbitrary"),
        name="fox_prompt",
    )(qf, fkvbf, fkvbf, c, ct)


def _dil_branch_kernel(q_ref, kc_ref, kp_ref, vc_ref, vp_ref, tab_ref, o_ref, lse_ref, *, tq):
    i = pl.program_id(2)
    lo = _half_masks(TILE)
    lane = lax.broadcasted_iota(jnp.int32, (TILE, LANES), 1)
    first_prev = jnp.where(i == 0, 2, 1)
    for a in range(tq // TILE):
        rows = slice(a * TILE, (a + 1) * TILE)
        lse_tile = jnp.zeros((TILE, LANES), F32)
        outs = []
        for p in range(DIL_HEADS // 2):
            cols = slice(p * LANES, (p + 1) * LANES)
            if a == 0:
                k_prev, v_prev, prev_idx = kp_ref[0, :, cols], vp_ref[0, :, cols], first_prev
            else:
                prev = slice((a - 1) * TILE, a * TILE)
                k_prev, v_prev, prev_idx = kc_ref[0, prev, cols], vc_ref[0, prev, cols], 1
            k2 = jnp.concatenate([k_prev, kc_ref[0, rows, cols]], axis=0)
            v2 = jnp.concatenate([v_prev, vc_ref[0, rows, cols]], axis=0)
            pair = []
            for e, qh in enumerate(_split_heads(q_ref[0, rows, cols], lo)):
                h = 2 * p + e
                bias = jnp.concatenate([tab_ref[h, prev_idx], tab_ref[h, 0]], axis=1)
                s = _nt_dot(qh, k2) + bias
                m = jnp.max(s, axis=1, keepdims=True)
                e_s = jnp.exp(s - m)
                l = jnp.sum(e_s, axis=1, keepdims=True)
                pair.append(jnp.dot(e_s.astype(BF16), v2, preferred_element_type=F32) / l)
                lse_tile = jnp.where(lane == h, jnp.log(l) + m, lse_tile)
            outs.append(jnp.where(lo, pair[0], pair[1]))
        o_ref[0, rows, :] = jnp.concatenate(outs, axis=1).astype(o_ref.dtype)
        lse_ref[0, rows, :] = lse_tile


def _dil_branch_prompt(q, kvbf, table, dil):
    bx, s, width = q.shape
    n_rows = s // dil
    tq = _row_tile(n_rows, 256)
    sub = tq // TILE
    qv = q.reshape(bx, n_rows, dil * width)
    kvv = kvbf.reshape(bx, n_rows, dil * 2 * width)
    o, lse = pl.pallas_call(
        functools.partial(_dil_branch_kernel, tq=tq),
        grid=(bx, dil, n_rows // tq),
        in_specs=[pl.BlockSpec((1, tq, width), lambda b, r, i: (b, i, r)),
                  pl.BlockSpec((1, tq, width), lambda b, r, i: (b, i, 2 * r)),
                  pl.BlockSpec((1, TILE, width), lambda b, r, i: (b, jnp.maximum(sub * i - 1, 0), 2 * r)),
                  pl.BlockSpec((1, tq, width), lambda b, r, i: (b, i, 2 * r + 1)),
                  pl.BlockSpec((1, TILE, width), lambda b, r, i: (b, jnp.maximum(sub * i - 1, 0), 2 * r + 1)),
                  pl.BlockSpec(table.shape, lambda b, r, i: (0, 0, 0, 0))],
        out_specs=[pl.BlockSpec((1, tq, width), lambda b, r, i: (b, i, r)),
                   pl.BlockSpec((1, tq, LANES), lambda b, r, i: (b, i, r))],
        out_shape=[jax.ShapeDtypeStruct((bx, n_rows, dil * width), BF16),
                   jax.ShapeDtypeStruct((bx, n_rows, dil * LANES), F32)],
        compiler_params=_cparams("arbitrary", "arbitrary", "arbitrary"),
        name="dilated_branch_prompt",
    )(qv, kvv, kvv, kvv, kvv, table)
    return o.reshape(bx, s, width), lse.reshape(bx, s, LANES)


def _post_dil_kernel(o1_ref, o2_ref, o3_ref, l1_ref, l2_ref, l3_ref, w_ref, x_ref, gate_ref, g_ref, o_ref):
    lses = [r[0] for r in (l1_ref, l2_ref, l3_ref)]
    m = jnp.maximum(jnp.maximum(lses[0], lses[1]), lses[2])
    es = [jnp.exp(l - m) for l in lses]
    tot = es[0] + es[1] + es[2]
    width = o1_ref.shape[2]
    head_of_col = lax.shift_right_arithmetic(lax.broadcasted_iota(jnp.int32, (LANES, width), 1), SEL_SHIFT)
    expand = jnp.where(head_of_col == lax.broadcasted_iota(jnp.int32, (LANES, width), 0), 1.0, 0.0).astype(BF16)
    mix = jnp.zeros((o1_ref.shape[1], width), F32)
    for e, o_ref_j in zip(es, (o1_ref, o2_ref, o3_ref)):
        alpha = jnp.dot((e / tot).astype(BF16), expand, preferred_element_type=F32)
        mix = mix + alpha * o_ref_j[0].astype(F32)
    y = jnp.dot(mix.astype(BF16), w_ref[...], preferred_element_type=F32)
    yn = y * lax.rsqrt(jnp.mean(y * y, axis=-1, keepdims=True) + NORM_EPS) * g_ref[...]
    o_ref[0] = x_ref[0] + gate_ref[0] * yn


def _post_dil(outs, lses, w_out, x, gate, g):
    bx, s, d = x.shape
    tm = _row_tile(s, 256)
    width = w_out.shape[0]
    return pl.pallas_call(
        _post_dil_kernel,
        grid=(bx, s // tm),
        in_specs=([pl.BlockSpec((1, tm, width), lambda b, i: (b, i, 0))] * 3
                  + [pl.BlockSpec((1, tm, LANES), lambda b, i: (b, i, 0))] * 3
                  + [pl.BlockSpec(w_out.shape, lambda b, i: (0, 0)),
                     pl.BlockSpec((1, tm, d), lambda b, i: (b, i, 0)),
                     _mod_spec(gate, tm),
                     pl.BlockSpec((1, d), lambda b, i: (0, 0))]),
        out_specs=pl.BlockSpec((1, tm, d), lambda b, i: (b, i, 0)),
        out_shape=jax.ShapeDtypeStruct((bx, s, d), F32),
        compiler_params=_cparams("arbitrary", "arbitrary"),
        name="post_dilated",
    )(*outs, *lses, w_out, x, gate, g)


def _compress_kernel(pt_ref, *refs, n_pages, pages_per_step, feature_major):
    page_refs = refs[:pages_per_step]
    w1_ref, w2_ref, pe_ref, kc_ref, vc_ref, rows_ref, chunk_ref = refs[pages_per_step:]
    j = pl.program_id(1)
    for p, page_ref in enumerate(page_refs):
        row0 = pl.multiple_of((j * pages_per_step + p) * PAGE_SIZE, PAGE_SIZE)
        if feature_major:
            for kv in range(2):
                rows_ref[kv, pl.ds(row0, PAGE_SIZE), :] = jnp.concatenate(
                    [page_ref[0, 0, kv, g].T for g in range(NSA_KV_HEADS)], axis=1)
        else:
            rows_ref[0, pl.ds(row0, PAGE_SIZE), :] = page_ref[0, :, 0:LANES]
            rows_ref[1, pl.ds(row0, PAGE_SIZE), :] = page_ref[0, :, LANES:2 * LANES]

    @pl.when(j == n_pages // pages_per_step - 1)
    def _():
        n_chunks = n_pages * PAGE_SIZE // CMP_STRIDE
        half = CMP_STRIDE * HEAD_DIM
        for kv, out_ref in ((0, kc_ref), (1, vc_ref)):
            w1 = w1_ref[kv]
            pe_a = jnp.broadcast_to(pe_ref[kv, :, 0:half], (8, half)).astype(BF16)
            pe_b = jnp.broadcast_to(pe_ref[kv, :, half:], (8, half)).astype(BF16)
            pe_term = (jnp.dot(pe_a, w1, preferred_element_type=F32)[0:1, 0:CMP_HIDDEN]
                       + jnp.dot(pe_b, w1, preferred_element_type=F32)[0:1, CMP_HIDDEN:])
            for l in range(CMP_STRIDE):
                both = rows_ref[kv, pl.ds(l, n_chunks, stride=CMP_STRIDE), :].astype(BF16)
                for g in range(NSA_KV_HEADS):
                    chunk_ref[g, :, l * HEAD_DIM:(l + 1) * HEAD_DIM] = both[:, g * HEAD_DIM:(g + 1) * HEAD_DIM]
            outs = []
            for g in range(NSA_KV_HEADS):
                uv = jnp.dot(chunk_ref[g], w1, preferred_element_type=F32)
                pre = uv[:, 0:CMP_HIDDEN] + pltpu.roll(uv[:, CMP_HIDDEN:], n_chunks - 1, 0) + pe_term
                hid = jax.nn.gelu(pre).astype(BF16)
                outs.append(jnp.dot(hid, w2_ref[kv], preferred_element_type=F32))
            out_ref[0] = jnp.concatenate(outs, axis=1)


def _compress(pool, page_table, w1cat, w2, pe, layer=None):
    n_req, n_pages = page_table.shape
    n_chunks = n_pages * PAGE_SIZE // CMP_STRIDE
    width = 2 * NSA_KV_HEADS * HEAD_DIM
    pps = 8 if n_pages % 8 == 0 else 1

    def page_spec(p):
        if layer is None:
            return pl.BlockSpec((1, PAGE_SIZE, width), lambda r, j, pt: (pt[r, j * pps + p], 0, 0))
        return pl.BlockSpec((1, 1, 2, NSA_KV_HEADS, HEAD_DIM, PAGE_SIZE),
                            lambda r, j, pt: (layer, pt[r, j * pps + p], 0, 0, 0, 0))

    grid_spec = pltpu.PrefetchScalarGridSpec(
        num_scalar_prefetch=1,
        grid=(n_req, n_pages // pps),
        in_specs=[page_spec(p) for p in range(pps)] + [
                  pl.BlockSpec(w1cat.shape, lambda r, j, pt: (0, 0, 0)),
                  pl.BlockSpec(w2.shape, lambda r, j, pt: (0, 0, 0)),
                  pl.BlockSpec(pe.shape, lambda r, j, pt: (0, 0, 0))],
        out_specs=[pl.BlockSpec((1, n_chunks, LANES), lambda r, j, pt: (r, 0, 0)),
                   pl.BlockSpec((1, n_chunks, LANES), lambda r, j, pt: (r, 0, 0))],
        scratch_shapes=[pltpu.VMEM((2, n_pages * PAGE_SIZE, LANES), F32),
                        pltpu.VMEM((NSA_KV_HEADS, n_chunks, CMP_STRIDE * HEAD_DIM), BF16)],
    )
    return pl.pallas_call(
        functools.partial(_compress_kernel, n_pages=n_pages, pages_per_step=pps, feature_major=layer is not None),
        grid_spec=grid_spec,
        out_shape=[jax.ShapeDtypeStruct((n_req, n_chunks, LANES), F32)] * 2,
        compiler_params=_cparams("arbitrary", "arbitrary"),
        name="nsa_compress",
    )(page_table, *([pool] * pps), w1cat, w2, pe)


def _top_k_mask(imp, n_top):
    cand = lax.broadcasted_iota(jnp.int32, imp.shape, 0)
    height = imp.shape[0]

    def body(_, carry):
        imp, sel = carry
        m = jnp.max(imp, axis=0, keepdims=True)
        first = jnp.min(jnp.where(imp == m, cand, height), axis=0, keepdims=True)
        pick = (cand == first) & (m > -jnp.inf)
        return jnp.where(cand == first, -jnp.inf, imp), jnp.where(pick, 1.0, sel)

    _, sel = lax.fori_loop(0, n_top, body, (imp, jnp.zeros(imp.shape, F32)))
    return sel


def _nsa_kernel(qn_ref, kc_ref, vc_ref, bcmp_ref, sel_ref, win_ref, tsel_ref, twin_ref, small_ref, o_ref,
                *, tq, tk, n_sel_delta, n_win_tiles):
    i = pl.program_id(1)
    t0 = i * tq
    n_cmp_pad = kc_ref.shape[1]
    n_blk = LANES
    lo = _half_masks(tq)
    lane = lax.broadcasted_iota(jnp.int32, (tq, LANES), 1)
    gates = small_ref[0]
    q_all = qn_ref[0].astype(F32)

    sj = lax.broadcasted_iota(jnp.int32, (n_blk, n_cmp_pad), 0) * SEL_BLOCK
    ci = lax.broadcasted_iota(jnp.int32, (n_blk, n_cmp_pad), 1) * CMP_STRIDE
    cover_t = jnp.where((ci < sj + SEL_BLOCK) & (ci + CMP_LEN > sj), 1.0, 0.0).astype(F32)
    blk_t = lax.broadcasted_iota(jnp.int32, (n_blk, tq), 0)
    cur_t = lax.shift_right_arithmetic(t0 + lax.broadcasted_iota(jnp.int32, (n_blk, tq), 1), SEL_SHIFT)
    forced_t = (blk_t == 0) | (blk_t == cur_t) | (blk_t == cur_t - 1)
    blk_of_key = lax.shift_right_arithmetic(lax.broadcasted_iota(jnp.int32, (n_blk, tk), 1), SEL_SHIFT)
    blk_row = lax.broadcasted_iota(jnp.int32, (n_blk, tk), 0)
    blk_delta = blk_row - blk_of_key
    sub = tk // TILE

    n_sub = tq // TILE
    group_of = [h // NSA_GROUP for h in range(NSA_HEADS)]

    qs = []
    for h in range(NSA_HEADS):
        g = group_of[h]
        blk = q_all[:, LANES * (h // 2):LANES * (h // 2 + 1)]
        if h % 2 != g:
            blk = pltpu.roll(blk, HEAD_DIM, 1)
        qs.append(jnp.where(lo if g == 0 else jnp.logical_not(lo), blk, 0.0).astype(BF16))

    kcb = kc_ref[0].astype(BF16)
    vcb = vc_ref[0].astype(BF16)
    o_cmp, imps = [], []
    for g in range(NSA_KV_HEADS):
        psum = jnp.zeros((tq, n_cmp_pad), F32)
        for r in range(NSA_GROUP):
            h = g * NSA_GROUP + r
            s = _nt_dot(qs[h], kcb) + bcmp_ref[h]
            m = jnp.max(s, axis=1, keepdims=True)
            e = jnp.exp(s - m)
            scale = jnp.where(m > 0.5 * NEG, 1.0 / jnp.maximum(jnp.sum(e, axis=1, keepdims=True), TINY), 0.0)
            p = e * scale
            psum = psum + p
            o_cmp.append(jnp.dot(p.astype(BF16), vcb, preferred_element_type=F32))
        imp = lax.dot_general(cover_t, psum, (((1,), (1,)), ((), ())), preferred_element_type=F32,
                              precision=lax.Precision.HIGHEST)
        imp = jnp.where(forced_t, FORCED_SCORE, imp)
        imps.append(jnp.where(blk_t <= cur_t, imp, -jnp.inf))

    sel_t = _top_k_mask(jnp.concatenate(imps, axis=1), min(SEL_TOPK, n_blk))
    not_sel = [(1.0 - sel_t[:, g * tq:(g + 1) * tq]).T.astype(BF16) for g in range(NSA_KV_HEADS)]

    def sel_chunk(c, states):
        off = pl.multiple_of(c * tk, tk)
        k = sel_ref[0, pl.ds(off, tk), 0:LANES]
        v = sel_ref[0, pl.ds(off, tk), LANES:2 * LANES]
        expand = jnp.where(blk_delta == c * (tk // SEL_BLOCK), NEG, 0.0).astype(BF16)
        mask_add = [jnp.dot(ns, expand, preferred_element_type=F32) for ns in not_sel]
        out = []
        for h in range(NSA_HEADS):
            bias = jnp.concatenate([jnp.concatenate(
                [tsel_ref[h, jnp.clip(n_sub * i + a - (c * sub + u), -1, n_sel_delta - 1) + 1] for u in range(sub)],
                axis=1) for a in range(n_sub)], axis=0)
            out.append(_online_update(states[h], _nt_dot(qs[h], k) + bias + mask_add[group_of[h]], v))
        return tuple(out)

    n_chunks = (t0 + tq - 1) // tk + 1
    sel_states = lax.fori_loop(0, n_chunks, sel_chunk, tuple(_init_state(tq) for _ in range(NSA_HEADS)))

    o_win = [[] for _ in range(NSA_HEADS)]
    for a in range(n_sub):
        j0 = n_sub * i + a
        ks, vs, idxs = [], [], []
        for u in range(n_win_tiles - 1, -1, -1):
            off = pl.multiple_of(jnp.maximum(j0 - u, 0) * TILE, TILE)
            ks.append(win_ref[0, pl.ds(off, TILE), 0:LANES])
            vs.append(win_ref[0, pl.ds(off, TILE), LANES:2 * LANES])
            idxs.append(jnp.where(j0 - u >= 0, u + 1, 0))
        k = jnp.concatenate(ks, axis=0)
        v = jnp.concatenate(vs, axis=0)
        for h in range(NSA_HEADS):
            bias = jnp.concatenate([twin_ref[h, ix] for ix in idxs], axis=1)
            s = _nt_dot(qs[h][a * TILE:(a + 1) * TILE], k) + bias
            e = jnp.exp(s - jnp.max(s, axis=1, keepdims=True))
            l = jnp.sum(e, axis=1, keepdims=True)
            o_win[h].append(jnp.dot(e.astype(BF16), v, preferred_element_type=F32) / l)

    pair_out = [None] * (NSA_HEADS // 2)
    for h in range(NSA_HEADS):
        gc, gs, gw = (jnp.sum(jnp.where(lane == 3 * h + b, gates, 0.0), axis=1, keepdims=True) for b in range(3))
        _, l_s, acc_s = sel_states[h]
        o = gc * o_cmp[h] + gs * (acc_s / l_s) + gw * jnp.concatenate(o_win[h], axis=0)
        if h % 2 != group_of[h]:
            o = pltpu.roll(o, HEAD_DIM, 1)
        prev = pair_out[h // 2]
        keep = lo if h % 2 == 0 else jnp.logical_not(lo)
        pair_out[h // 2] = jnp.where(keep, o, 0.0 if prev is None else prev)

    o_ref[0] = jnp.concatenate(pair_out, axis=1).astype(o_ref.dtype)


def _nsa_prompt(qn, kc, vc, bcmp, nsabf, tsel, twin, small):
    bx, s, _ = qn.shape
    tq = _row_tile(s, 256)
    tk = _row_tile(s, 1024)
    n_cmp_pad = kc.shape[1]
    once = pl.Buffered(1)
    return pl.pallas_call(
        functools.partial(_nsa_kernel, tq=tq, tk=tk, n_sel_delta=tsel.shape[1] - 1, n_win_tiles=twin.shape[1] - 1),
        grid=(bx, s // tq),
        in_specs=[pl.BlockSpec((1, tq, NSA_HEADS * HEAD_DIM), lambda b, i: (b, i, 0)),
                  pl.BlockSpec((1, n_cmp_pad, LANES), lambda b, i: (b, 0, 0)),
                  pl.BlockSpec((1, n_cmp_pad, LANES), lambda b, i: (b, 0, 0)),
                  pl.BlockSpec((NSA_HEADS, tq, n_cmp_pad), lambda b, i: (0, i, 0)),
                  pl.BlockSpec((1, s, 2 * LANES), lambda b, i: (b, 0, 1)),
                  pl.BlockSpec((1, s, 2 * LANES), lambda b, i: (b, 0, 2)),
                  pl.BlockSpec(tsel.shape, lambda b, i: (0, 0, 0, 0), pipeline_mode=once),
                  pl.BlockSpec(twin.shape, lambda b, i: (0, 0, 0, 0), pipeline_mode=once),
                  pl.BlockSpec((1, tq, LANES), lambda b, i: (b, i, 0))],
        out_specs=pl.BlockSpec((1, tq, NSA_HEADS * HEAD_DIM), lambda b, i: (b, i, 0)),
        out_shape=jax.ShapeDtypeStruct((bx, s, NSA_HEADS * HEAD_DIM), BF16),
        compiler_params=_cparams("arbitrary", "arbitrary"),
        name="nsa_prompt",
    )(qn, kc, vc, bcmp, nsabf, nsabf, tsel, twin, small)


def _bf16_round(x):
    return x.astype(BF16).astype(F32)


def _dot3(z, w):
    hi = z.astype(BF16)
    rest = z - hi.astype(F32)
    mid = rest.astype(BF16)
    lo = (rest - mid.astype(F32)).astype(BF16)
    return sum(jnp.dot(part, w, preferred_element_type=F32) for part in (hi, mid, lo))


def _feature_major(cache):
    n = cache.ndim
    return jnp.transpose(cache, tuple(range(n - 4)) + (n - 3, n - 2, n - 1, n - 4))


def _nt_dot_bf16(p, vt):
    return lax.dot_general(p.astype(BF16), vt, (((1,), (1,)), ((), ())), preferred_element_type=F32)


def _fox_sample_kernel(pt_ref, *refs, n_steps, pages_per_step):
    page_refs, lf_refs = refs[:pages_per_step], refs[pages_per_step:2 * pages_per_step]
    q_ref, new_ref, lfn_ref, o_ref, m_ref, l_ref, acc_ref, carry_ref = refs[2 * pages_per_step:]
    j = pl.program_id(1)
    width = FOX_HEADS * HEAD_DIM
    qbd = q_ref[0]

    @pl.when(j == 0)
    def _():
        s_new = jnp.sum(qbd.astype(F32) * _bf16_round(new_ref[0, 0:1]), axis=1, keepdims=True)
        m_ref[...] = jnp.broadcast_to(s_new, m_ref.shape)
        l_ref[...] = jnp.ones_like(l_ref)
        acc_ref[...] = jnp.broadcast_to(_bf16_round(new_ref[0, 1:2]), acc_ref.shape)
        carry_ref[...] = lfn_ref[0]

    u = lax.broadcasted_iota(jnp.int32, (PAGE_SIZE, PAGE_SIZE), 0)
    c = lax.broadcasted_iota(jnp.int32, (PAGE_SIZE, PAGE_SIZE), 1)
    later = jnp.where(u > c, 1.0, 0.0).astype(BF16)
    carry = carry_ref[:, 0:1]
    biases = []
    for lf_ref in lf_refs:
        lf = lf_ref[0, 0]
        biases.append(carry + _dot3(lf, later))
        carry = carry + jnp.sum(lf, axis=1, keepdims=True)
    kt = jnp.concatenate([r[0, 0, 0].reshape(width, PAGE_SIZE).astype(BF16) for r in page_refs], axis=1)
    vt = jnp.concatenate([r[0, 0, 1].reshape(width, PAGE_SIZE).astype(BF16) for r in page_refs], axis=1)
    s = jnp.dot(qbd, kt, preferred_element_type=F32) + jnp.concatenate(biases, axis=1)
    m_old = m_ref[:, 0:1]
    m_new = jnp.maximum(m_old, jnp.max(s, axis=1, keepdims=True))
    alpha = jnp.exp(m_old - m_new)
    p = jnp.exp(s - m_new)
    l_ref[...] = jnp.broadcast_to(alpha * l_ref[:, 0:1] + jnp.sum(p, axis=1, keepdims=True), l_ref.shape)
    acc_ref[...] = alpha * acc_ref[...] + _nt_dot_bf16(p, vt)
    m_ref[...] = jnp.broadcast_to(m_new, m_ref.shape)
    carry_ref[...] = jnp.broadcast_to(carry, carry_ref.shape)

    @pl.when(j == n_steps - 1)
    def _():
        o_ref[0] = acc_ref[...] / l_ref[:, 0:1]


def _block_diag_queries(q, n_heads):
    width = n_heads * HEAD_DIM
    keep = (jnp.arange(width) // HEAD_DIM)[None, :] == jnp.arange(n_heads)[:, None]
    return jnp.where(keep[None], q[:, None, :], jnp.zeros((), q.dtype))


def _diag_blocks(o, n_heads):
    n = o.shape[0]
    o5 = o.reshape(n, n_heads, n_heads, HEAD_DIM)
    return jnp.stack([o5[:, h, h] for h in range(n_heads)], axis=1).reshape(n, n_heads * HEAD_DIM)


def _fox_sample(cache_t, layer, logf_t, page_table, qbd, new_kv, lf_new):
    n_req, n_pages = page_table.shape
    pps = 8 if n_pages % 8 == 0 else 1
    n_steps = n_pages // pps
    width = FOX_HEADS * HEAD_DIM

    def page_spec(shape, p):
        zeros = (0,) * (len(shape) - 2)
        return pl.BlockSpec(shape, lambda r, j, pt: (layer, pt[r, n_pages - 1 - (j * pps + p)]) + zeros)

    grid_spec = pltpu.PrefetchScalarGridSpec(
        num_scalar_prefetch=1,
        grid=(n_req, n_steps),
        in_specs=([page_spec((1, 1, 2, FOX_HEADS, HEAD_DIM, PAGE_SIZE), p) for p in range(pps)]
                  + [page_spec((1, 1, FOX_HEADS, PAGE_SIZE), p) for p in range(pps)]
                  + [pl.BlockSpec((1, FOX_HEADS, width), lambda r, j, pt: (r, 0, 0)),
                     pl.BlockSpec((1, 2, width), lambda r, j, pt: (r, 0, 0)),
                     pl.BlockSpec((1, FOX_HEADS, LANES), lambda r, j, pt: (r, 0, 0))]),
        out_specs=pl.BlockSpec((1, FOX_HEADS, width), lambda r, j, pt: (r, 0, 0)),
        scratch_shapes=[pltpu.VMEM((FOX_HEADS, LANES), F32), pltpu.VMEM((FOX_HEADS, LANES), F32),
                        pltpu.VMEM((FOX_HEADS, width), F32), pltpu.VMEM((FOX_HEADS, LANES), F32)],
    )
    return pl.pallas_call(
        functools.partial(_fox_sample_kernel, n_steps=n_steps, pages_per_step=pps),
        grid_spec=grid_spec,
        out_shape=jax.ShapeDtypeStruct((n_req, FOX_HEADS, width), F32),
        compiler_params=_cparams("arbitrary", "arbitrary"),
        name="fox_sample",
    )(page_table, *([cache_t] * pps), *([logf_t] * pps), qbd, new_kv, lf_new)


def _dil_sample_kernel(q_ref, new_ref, kt_ref, vt_ref, tab_ref, o_ref, *, n_buf):
    heads = q_ref.shape[2]
    qbd = q_ref[0, 0]
    kt = kt_ref[0, 0, 0].reshape(heads * HEAD_DIM, n_buf).astype(BF16)
    vt = vt_ref[0, 0, 0].reshape(heads * HEAD_DIM, n_buf).astype(BF16)
    tab = tab_ref[0]
    s = jnp.dot(qbd, kt, preferred_element_type=F32) + tab[:, 0:n_buf]
    s_new = (jnp.sum(qbd.astype(F32) * _bf16_round(new_ref[0, 0, 0:1]), axis=1, keepdims=True)
             + tab[:, n_buf:n_buf + 1])
    m = jnp.maximum(jnp.max(s, axis=1, keepdims=True), s_new)
    p = jnp.exp(s - m)
    p_new = jnp.exp(s_new - m)
    l = jnp.sum(p, axis=1, keepdims=True) + p_new
    o_ref[0, 0] = (_nt_dot_bf16(p, vt) + _bf16_round(p_new) * _bf16_round(new_ref[0, 0, 1:2])) / l


def _dil_sample(state_t, layer, qbd, new_kv, table):
    n_req = state_t.shape[1]
    n_buf = state_t.shape[-1]
    half = DIL_HEADS // 2
    width = half * HEAD_DIM
    return pl.pallas_call(
        functools.partial(_dil_sample_kernel, n_buf=n_buf),
        grid=(n_req, 2),
        in_specs=[pl.BlockSpec((1, 1, half, width), lambda r, hh: (r, hh, 0, 0)),
                  pl.BlockSpec((1, 1, 2, width), lambda r, hh: (r, hh, 0, 0)),
                  pl.BlockSpec((1, 1, 1, half, HEAD_DIM, n_buf), lambda r, hh: (layer, r, 0, hh, 0, 0)),
                  pl.BlockSpec((1, 1, 1, half, HEAD_DIM, n_buf), lambda r, hh: (layer, r, 1, hh, 0, 0)),
                  pl.BlockSpec((1, half, table.shape[2]), lambda r, hh: (hh, 0, 0))],
        out_specs=pl.BlockSpec((1, 1, half, width), lambda r, hh: (r, hh, 0, 0)),
        out_shape=jax.ShapeDtypeStruct((n_req, 2, half, width), F32),
        compiler_params=_cparams("arbitrary", "arbitrary"),
        name="dilated_sample",
    )(qbd, new_kv, state_t, state_t, table)


def _group_queries(q_row):
    row8 = lax.broadcasted_iota(jnp.int32, (8, LANES), 0)
    lo = lax.broadcasted_iota(jnp.int32, (8, LANES), 1) < HEAD_DIM
    out = []
    for g in range(NSA_KV_HEADS):
        qg = jnp.zeros((8, LANES), F32)
        for r in range(NSA_GROUP):
            h = g * NSA_GROUP + r
            blk = jnp.broadcast_to(q_row[:, LANES * (h // 2):LANES * (h // 2 + 1)], (8, LANES))
            if h % 2 != g:
                blk = pltpu.roll(blk, HEAD_DIM, 1)
            qg = jnp.where((row8 == r) & (lo if g == 0 else jnp.logical_not(lo)), blk, qg)
        out.append(qg)
    return out


def _nsa_select_kernel(q_ref, kc_ref, vc_ref, bcmp_ref, oc_ref, idx_ref, *, n_blk_pad, cur):
    n_cmp_pad = kc_ref.shape[1]
    kcb = kc_ref[0].astype(BF16)
    vcb = vc_ref[0].astype(BF16)
    row8 = lax.broadcasted_iota(jnp.int32, (8, n_cmp_pad), 0)
    ci = lax.broadcasted_iota(jnp.int32, (n_cmp_pad, n_blk_pad), 0) * CMP_STRIDE
    sj = lax.broadcasted_iota(jnp.int32, (n_cmp_pad, n_blk_pad), 1) * SEL_BLOCK
    cover = jnp.where((ci < sj + SEL_BLOCK) & (ci + CMP_LEN > sj), 1.0, 0.0).astype(F32)
    blk = lax.broadcasted_iota(jnp.int32, (8, n_blk_pad), 1)
    forced = (blk == 0) | (blk == cur) | (blk == cur - 1)
    lane = lax.broadcasted_iota(jnp.int32, (8, LANES), 1)
    for g, qg in enumerate(_group_queries(q_ref[0].astype(F32))):
        bias = bcmp_ref[8 * g:8 * g + 8, :]
        ok = (bias > 0.5 * NEG) & (row8 < NSA_GROUP)
        s = jnp.where(ok, _nt_dot(qg.astype(BF16), kcb) + bias, NEG)
        m = jnp.max(s, axis=1, keepdims=True)
        e = jnp.where(ok, jnp.exp(s - m), 0.0)
        p = e / jnp.maximum(jnp.sum(e, axis=1, keepdims=True), TINY)
        oc_ref[0, g] = jnp.dot(p.astype(BF16), vcb, preferred_element_type=F32)
        psum = jnp.broadcast_to(jnp.sum(p, axis=0, keepdims=True), (8, n_cmp_pad))
        imp = jnp.dot(psum, cover, preferred_element_type=F32, precision=lax.Precision.HIGHEST)
        imp = jnp.where(forced, FORCED_SCORE, imp)
        imp = jnp.where(blk <= cur, imp, -jnp.inf)

        def body(it, carry):
            imp, idx = carry
            top = jnp.max(imp, axis=1, keepdims=True)
            first = jnp.min(jnp.where(imp == top, blk, n_blk_pad), axis=1, keepdims=True)
            idx = jnp.where(lane == it, jnp.where(top > -jnp.inf, first, -1), idx)
            return jnp.where(blk == first, -jnp.inf, imp), idx

        _, idx = lax.fori_loop(0, SEL_TOPK, body, (imp, jnp.full((8, LANES), -1, jnp.int32)))
        idx_ref[0, g] = idx


def _nsa_select(q, kc, vc, bcmp_row, n_blk_pad, cur):
    n_req = q.shape[0]
    n_cmp_pad = kc.shape[1]
    return pl.pallas_call(
        functools.partial(_nsa_select_kernel, n_blk_pad=n_blk_pad, cur=cur),
        grid=(n_req,),
        in_specs=[pl.BlockSpec((1, 1, NSA_HEADS * HEAD_DIM), lambda r: (r, 0, 0)),
                  pl.BlockSpec((1, n_cmp_pad, LANES), lambda r: (r, 0, 0)),
                  pl.BlockSpec((1, n_cmp_pad, LANES), lambda r: (r, 0, 0)),
                  pl.BlockSpec(bcmp_row.shape, lambda r: (0, 0))],
        out_specs=[pl.BlockSpec((1, NSA_KV_HEADS, 8, LANES), lambda r: (r, 0, 0, 0)),
                   pl.BlockSpec((1, NSA_KV_HEADS, 8, LANES), lambda r: (r, 0, 0, 0))],
        out_shape=[jax.ShapeDtypeStruct((n_req, NSA_KV_HEADS, 8, LANES), F32),
                   jax.ShapeDtypeStruct((n_req, NSA_KV_HEADS, 8, LANES), jnp.int32)],
        compiler_params=_cparams("arbitrary"),
        name="nsa_sample_select",
    )(q, kc, vc, bcmp_row)


def _nsa_attend_kernel(idx_ref, pt_ref, q_ref, *refs, n_past_blk, new_lane, n_win):
    n_picks = NSA_KV_HEADS * SEL_TOPK
    blk_refs = refs[:n_picks]
    fsel_ref, wbuf_ref, fwin_ref, nsel_ref, nwin_ref, oc_ref, small_ref, o_ref = refs[n_picks:]
    r_idx = pl.program_id(0)
    qgs = [qg[:, g * HEAD_DIM:(g + 1) * HEAD_DIM] for g, qg in enumerate(_group_queries(q_ref[0].astype(F32)))]
    row8 = lax.broadcasted_iota(jnp.int32, (8, LANES), 0)
    lane = lax.broadcasted_iota(jnp.int32, (8, LANES), 1)
    n_pages_past = n_past_blk // 2
    gates = small_ref[0]

    for g, qg in enumerate(qgs):
        grp = slice(g * HEAD_DIM, (g + 1) * HEAD_DIM)
        rows = slice(8 * g, 8 * g + 8)
        qb = qg.astype(BF16)
        ksel_new = _bf16_round(nsel_ref[0][:, 2 * LANES:3 * LANES][:, grp])
        vsel_new = _bf16_round(nsel_ref[0][:, 3 * LANES:4 * LANES][:, grp])
        kwin_new = _bf16_round(nwin_ref[0][:, 0:LANES][:, grp])
        vwin_new = _bf16_round(nwin_ref[0][:, LANES:2 * LANES][:, grp])

        kts, vts, biases = [], [], []
        for k in range(SEL_TOPK):
            blk_ref = blk_refs[g * SEL_TOPK + k]
            b = idx_ref[r_idx, g * SEL_TOPK + k]
            valid = (b >= 0) & (b < n_past_blk)
            page = jnp.clip(lax.shift_right_arithmetic(b, 1), 0, n_pages_past - 1)
            in_blk = lax.shift_right_arithmetic(lane, SEL_SHIFT) == jnp.bitwise_and(b, 1)
            kts.append(blk_ref[0, 0, 0, g].astype(BF16))
            vts.append(blk_ref[0, 0, 1, g].astype(BF16))
            biases.append(jnp.where(valid & in_blk, fsel_ref[page, rows, :], NEG))
        s = jnp.dot(qb, jnp.concatenate(kts, axis=1), preferred_element_type=F32) + jnp.concatenate(biases, axis=1)
        s_new = jnp.sum(qg * ksel_new, axis=1, keepdims=True) + fsel_ref[n_pages_past, rows, new_lane:new_lane + 1]
        m = jnp.maximum(jnp.max(s, axis=1, keepdims=True), s_new)
        p = jnp.exp(s - m)
        p_new = jnp.exp(s_new - m)
        l = jnp.sum(p, axis=1, keepdims=True) + p_new
        o_sel = (_nt_dot_bf16(p, jnp.concatenate(vts, axis=1)) + _bf16_round(p_new) * vsel_new) / l

        kw_t = wbuf_ref[0, 0, 0, g].astype(BF16)
        vw_t = wbuf_ref[0, 0, 1, g].astype(BF16)
        s = jnp.dot(qb, kw_t, preferred_element_type=F32) + fwin_ref[rows, 0:n_win]
        s_wn = jnp.sum(qg * kwin_new, axis=1, keepdims=True) + fwin_ref[rows, n_win:n_win + 1]
        m = jnp.maximum(jnp.max(s, axis=1, keepdims=True), s_wn)
        p = jnp.exp(s - m)
        p_new = jnp.exp(s_wn - m)
        l = jnp.sum(p, axis=1, keepdims=True) + p_new
        o_win = (_nt_dot_bf16(p, vw_t) + _bf16_round(p_new) * vwin_new) / l

        gate = []
        for b in range(3):
            col = jnp.zeros((8, 1), F32)
            for r in range(NSA_GROUP):
                lane_i = 3 * (g * NSA_GROUP + r) + b
                col = jnp.where(row8[:, 0:1] == r, gates[:, lane_i:lane_i + 1], col)
            gate.append(col)
        o_cmp = oc_ref[0, g][:, g * HEAD_DIM:(g + 1) * HEAD_DIM]
        o_ref[0, g] = gate[0] * o_cmp + gate[1] * o_sel + gate[2] * o_win


def _nsa_attend(idx, page_table, q, cache_t, layer, fsel, win_t, fwin, new_sel, new_win, o_cmp, small, n_past_blk,
                new_lane):
    n_req = q.shape[0]
    n_win = win_t.shape[-1]

    def blk_map(pick):
        def index(r, idx_ref, pt_ref):
            b = idx_ref[r, pick]
            b = jnp.where((b >= 0) & (b < n_past_blk), b, 0)
            return layer, pt_ref[r, lax.shift_right_arithmetic(b, 1)], 1, 0, 0, 0
        return index

    n_picks = NSA_KV_HEADS * SEL_TOPK
    const = lambda *shape: (lambda r, idx_ref, pt_ref: shape)
    per_req3 = lambda r, idx_ref, pt_ref: (r, 0, 0)
    per_req4 = lambda r, idx_ref, pt_ref: (r, 0, 0, 0)
    page_blk = (1, 1, 2, NSA_KV_HEADS, HEAD_DIM, PAGE_SIZE)
    grid_spec = pltpu.PrefetchScalarGridSpec(
        num_scalar_prefetch=2,
        grid=(n_req,),
        in_specs=([pl.BlockSpec((1, 1, NSA_HEADS * HEAD_DIM), per_req3)]
                  + [pl.BlockSpec(page_blk, blk_map(pick)) for pick in range(n_picks)]
                  + [pl.BlockSpec(fsel.shape, const(0, 0, 0)),
                     pl.BlockSpec((1, 1, 2, NSA_KV_HEADS, HEAD_DIM, n_win),
                                  lambda r, idx_ref, pt_ref: (layer, r, 0, 0, 0, 0)),
                     pl.BlockSpec(fwin.shape, const(0, 0)),
                     pl.BlockSpec((1, 1, 4 * LANES), per_req3),
                     pl.BlockSpec((1, 1, 2 * LANES), per_req3),
                     pl.BlockSpec((1, NSA_KV_HEADS, 8, LANES), per_req4),
                     pl.BlockSpec((1, 1, LANES), per_req3)]),
        out_specs=pl.BlockSpec((1, NSA_KV_HEADS, 8, HEAD_DIM), per_req4),
    )
    return pl.pallas_call(
        functools.partial(_nsa_attend_kernel, n_past_blk=n_past_blk, new_lane=new_lane, n_win=n_win),
        grid_spec=grid_spec,
        out_shape=jax.ShapeDtypeStruct((n_req, NSA_KV_HEADS, 8, HEAD_DIM), F32),
        compiler_params=_cparams("arbitrary"),
        name="nsa_sample_attend",
    )(idx, page_table, q, *([cache_t] * n_picks), fsel, win_t, fwin, new_sel, new_win, o_cmp, small)


def kernel(x_prompt, x_sample, cache_nsa_kv, cache_fox_kv, cache_fox_logf, state_nsa_win_kv, state_dil_kv, page_table,
           c_prompt, c_sample, rel_bias, norm_g, w_ada, b_ada, w_in_a, nsa_gate_b, fox_f_b, nsa_cmp_w1, nsa_cmp_w2,
           nsa_cmp_pe, w_out_a, w_in_c, w_out_c, w_mlp1, w_mlp2):
    bp, s, d = x_prompt.shape
    bd = x_sample.shape[0]
    depth = w_ada.shape[0]
    n_pages = s // PAGE_SIZE
    n_cmp_pad = s // CMP_STRIDE

    mods = _ada_params(jnp.concatenate([c_prompt, c_sample], axis=0), w_ada, b_ada).reshape(depth, bp + bd, 6, d)
    tab_sel = _toeplitz_table(rel_bias, NSA_HEADS, 15, "causal")
    tab_win = _toeplitz_table(rel_bias, NSA_HEADS, NSA_WINDOW // TILE + 2, "window")
    assert all(window // dil == TILE for window, dil in DIL_BRANCHES)
    tab_dil = [_toeplitz_table(rel_bias, DIL_HEADS, 3, "branch", dil) for _, dil in DIL_BRANCHES]
    bcmp = _cmp_bias_table(rel_bias, s, TILE, n_cmp_pad, 0)

    past_len = page_table.shape[1] * PAGE_SIZE
    assert past_len % SEL_BLOCK == 0
    n_past_blk = past_len // SEL_BLOCK
    n_blk_pad = -(-(n_past_blk + 1) // LANES) * LANES
    far = 1 << 30
    bcmp_s = _affine_bias(rel_bias, NSA_GROUPED_ROWS, past_len // CMP_STRIDE, past_len - (CMP_LEN - 1), -CMP_STRIDE, far)
    assert past_len % PAGE_SIZE == 0
    n_pages_s = past_len // PAGE_SIZE
    fsel = _affine_bias(rel_bias, NSA_GROUPED_ROWS, (n_pages_s + 1) * PAGE_SIZE, past_len, -1, far)
    fsel = jnp.swapaxes(fsel.reshape(len(NSA_GROUPED_ROWS), n_pages_s + 1, PAGE_SIZE), 0, 1)
    n_win_buf = state_nsa_win_kv.shape[2]
    fwin = _affine_bias(rel_bias, NSA_GROUPED_ROWS, n_win_buf + LANES, n_win_buf, -1, NSA_WINDOW)
    n_dil_buf = state_dil_kv.shape[2]
    tab_dil_s = _affine_bias(rel_bias, range(DIL_HEADS), n_dil_buf + LANES, n_dil_buf, -1, far, union=True)
    tab_dil_s = tab_dil_s.reshape(2, DIL_HEADS // 2, n_dil_buf + LANES)
    cache_nsa_t = _feature_major(cache_nsa_kv)
    cache_fox_t = _feature_major(cache_fox_kv)
    logf_t = jnp.swapaxes(cache_fox_logf, 2, 3)
    win_t = _feature_major(state_nsa_win_kv)
    dil_t = _feature_major(state_dil_kv)

    xp = x_prompt
    xs = x_sample.reshape(1, bd, d)
    per_req = lambda a: a.reshape(bd, 1, a.shape[-1])
    prompt_pages = jnp.arange(bp * n_pages, dtype=jnp.int32).reshape(bp, n_pages)
    nsa_p, nsa_s, fkv_p, fkv_s, lf_p, lf_s, win_p, win_s, dil_p, dil_s = [], [], [], [], [], [], [], [], [], []
    for layer in range(depth):
        mp = [mods[layer, :bp, k].reshape(bp, 1, d) for k in range(6)]
        ms = [mods[layer, bp:, k].reshape(1, bd, d) for k in range(6)]
        g = [norm_g[layer, k].reshape(1, d) for k in range(4)]
        i = layer // 2
        if layer % 2 == 0:
            wa = w_in_a[i]
            w_in = jnp.concatenate([wa[:, 0:1280], wa[:, 1304:2840], wa[:, 1280:1304], wa[:, 2840:2848],
                                    jnp.zeros((d, LANES - SMALL_GATES - FOX_HEADS), F32)], axis=1).astype(BF16)
            sb = jnp.concatenate([nsa_gate_b[i].reshape(-1), fox_f_b[i],
                                  jnp.zeros((LANES - SMALL_GATES - FOX_HEADS,), F32)]).reshape(1, LANES)
            half = CMP_STRIDE * HEAD_DIM
            w1cat = jnp.concatenate([nsa_cmp_w1[i][:, :half], nsa_cmp_w1[i][:, half:]], axis=2).astype(BF16)
            w2 = nsa_cmp_w2[i].astype(BF16)
            pe = nsa_cmp_pe[i].reshape(2, 1, CMP_LEN * HEAD_DIM)
            w_out = w_out_a[i].astype(BF16)

            qn, nsa4, nsabf, win, qf, fkv, fkvbf, small = _proj_even(xp, mp[0], mp[1], g[0], w_in, sb)
            c, ct = _cumsum(small)
            o_f = _fox_prompt(qf, fkvbf, c, ct)
            kc, vc = _compress(nsa4.reshape(bp * n_pages, PAGE_SIZE, 512), prompt_pages, w1cat, w2, pe)
            o_n = _nsa_prompt(qn, kc, vc, bcmp, nsabf, tab_sel, tab_win, small)
            op_a, op_b = o_n, o_f
            nsa_p.append(nsa4.reshape(bp, s, 4, NSA_KV_HEADS, HEAD_DIM))
            fkv_p.append(fkv.reshape(bp, s, 2, FOX_HEADS, HEAD_DIM))
            lf_p.append(small[:, :, SMALL_GATES:SMALL_GATES + FOX_HEADS])
            n_win = min(NSA_WINDOW, s)
            win_p.append(win[:, s - n_win:].reshape(bp, n_win, 2, NSA_KV_HEADS, HEAD_DIM))

            qn_s, nsa4_s, _, win_new, qf_s, fkv_s_, _, small_s = _proj_even(xs, ms[0], ms[1], g[0], w_in, sb)
            kc_s, vc_s = _compress(cache_nsa_t, page_table, w1cat, w2, pe, layer=i)
            o_cmp, idx = _nsa_select(per_req(qn_s), kc_s, vc_s, bcmp_s, n_blk_pad, n_past_blk)
            idx = idx[:, :, 0, :SEL_TOPK].reshape(bd, NSA_KV_HEADS * SEL_TOPK)
            o_nsa = _nsa_attend(idx, page_table, per_req(qn_s), cache_nsa_t, i, fsel, win_t, fwin, per_req(nsa4_s),
                                per_req(win_new), o_cmp, per_req(small_s), n_past_blk, 0)
            os_a = o_nsa[:, :, :NSA_GROUP].reshape(1, bd, NSA_HEADS * HEAD_DIM).astype(BF16)
            lf_new = small_s[0, :, SMALL_GATES:SMALL_GATES + FOX_HEADS]
            o_fox = _fox_sample(cache_fox_t, i, logf_t, page_table, _block_diag_queries(qf_s[0], FOX_HEADS),
                                fkv_s_.reshape(bd, 2, FOX_HEADS * HEAD_DIM),
                                jnp.broadcast_to(lf_new[:, :, None], (bd, FOX_HEADS, LANES)))
            os_b = _diag_blocks(o_fox, FOX_HEADS).reshape(1, bd, -1).astype(BF16)
            nsa_s.append(nsa4_s.reshape(bd, 1, 4, NSA_KV_HEADS, HEAD_DIM))
            fkv_s.append(fkv_s_.reshape(bd, 1, 2, FOX_HEADS, HEAD_DIM))
            lf_s.append(small_s[0, :, SMALL_GATES:SMALL_GATES + FOX_HEADS].reshape(bd, 1, FOX_HEADS))
            win_s.append(win_new.reshape(bd, 1, 2, NSA_KV_HEADS, HEAD_DIM))
        else:
            w_in = w_in_c[i].astype(BF16)
            w_out = w_out_c[i].astype(BF16)
            q, kv, kvbf = _proj_odd(xp, mp[0], mp[1], g[0], w_in)
            branches = [_dil_branch_prompt(q, kvbf, tab, dil) for tab, (_, dil) in zip(tab_dil, DIL_BRANCHES)]
            n_dil = min(DIL_BRANCHES[-1][0], s)
            dil_p.append(kv[:, s - n_dil:].reshape(bp, n_dil, 2, DIL_HEADS, HEAD_DIM))
            q_s, kv_s, _ = _proj_odd(xs, ms[0], ms[1], g[0], w_in)
            half = DIL_HEADS // 2
            qbd = _block_diag_queries(q_s.reshape(bd * 2, half * HEAD_DIM), half).reshape(bd, 2, half, half * HEAD_DIM)
            new_kv = jnp.swapaxes(kv_s.reshape(bd, 2, 2, half * HEAD_DIM), 1, 2)
            o_dil = _dil_sample(dil_t, i, qbd, new_kv, tab_dil_s)
            os_a = os_b = _diag_blocks(o_dil.reshape(bd * 2, half, half * HEAD_DIM), half).reshape(1, bd, -1).astype(BF16)
            dil_s.append(kv_s.reshape(bd, 1, 2, DIL_HEADS, HEAD_DIM))
        if layer % 2 == 0:
            xp = _post(op_a, op_b, w_out, xp, mp[2], g[1])
            xs = _post(os_a, os_b, w_out, xs, ms[2], g[1])
        else:
            xp = _post_dil([o for o, _ in branches], [l for _, l in branches], w_out, xp, mp[2], g[1])
            xs = _post(os_a, os_b, w_out, xs, ms[2], g[1], 0, 1)
        w1 = w_mlp1[layer].astype(BF16)
        w2m = w_mlp2[layer].astype(BF16)
        xp = _mlp(xp, mp[3], mp[4], mp[5], g[2], g[3], w1, w2m)
        xs = _mlp(xs, ms[3], ms[4], ms[5], g[2], g[3], w1, w2m)
    return (xp, xs.reshape(bd, 1, d), jnp.stack(nsa_p), jnp.stack(nsa_s), jnp.stack(fkv_p), jnp.stack(fkv_s),
            jnp.stack(lf_p), jnp.stack(lf_s), jnp.stack(win_p), jnp.stack(win_s), jnp.stack(dil_p), jnp.stack(dil_s))
```

```python
import functools
import math

import numpy as np
import jax
import jax.numpy as jnp
from jax import lax
from jax.experimental import pallas as pl
from jax.experimental.pallas import tpu as pltpu

F32 = jnp.float32
BF16 = jnp.bfloat16

HEAD_DIM = 64
NSA_HEADS = 8
NSA_KV_HEADS = 2
NSA_GROUP = NSA_HEADS // NSA_KV_HEADS
FOX_HEADS = 8
DIL_HEADS = 16
CMP_LEN = 32
CMP_STRIDE = 16
CMP_HIDDEN = 4 * HEAD_DIM
SEL_BLOCK = 64
SEL_TOPK = 16
NSA_WINDOW = 512
FORCED_SCORE = 1e9
DIL_BRANCHES = ((128, 1), (512, 4), (2048, 16))
N_BUCKETS = 32
BUCKET_EXACT = 16
BUCKET_MAX_DIST = 2048
NORM_EPS = 1e-6
TINY = 1e-30
PAGE_SIZE = 128

LANES = 128
VMEM_LIMIT_BYTES = 56 * 1024 * 1024

NEG = -1e30
QK_SCALE = HEAD_DIM ** -0.5
TILE = 128
SEL_SHIFT = 6
SMALL_GATES = 3 * NSA_HEADS


def _cparams(*sem):
    return pltpu.CompilerParams(dimension_semantics=sem, vmem_limit_bytes=VMEM_LIMIT_BYTES)


def _bucket_thresholds():
    d = np.arange(0, 2 * BUCKET_MAX_DIST + 1)
    df = np.maximum(d, 1).astype(np.float64)
    ratio = math.log(BUCKET_MAX_DIST / BUCKET_EXACT)
    log_b = BUCKET_EXACT + (np.log(df / BUCKET_EXACT) / ratio * (N_BUCKETS - BUCKET_EXACT)).astype(np.int64)
    bucket = np.where(d < BUCKET_EXACT, d, np.clip(log_b, BUCKET_EXACT, N_BUCKETS - 1))
    return [int(np.argmax(bucket >= b)) for b in range(1, N_BUCKETS)]


BUCKET_THR = _bucket_thresholds()


def _bias_of_distance(d, tab_ref, h):
    val = jnp.full(d.shape, tab_ref[0, h], F32)
    for b in range(1, N_BUCKETS):
        val = jnp.where(d >= BUCKET_THR[b - 1], tab_ref[b, h], val)
    return val


def _toeplitz_kernel(tab_ref, o_ref, *, n_heads, mode, dil):
    r = lax.broadcasted_iota(jnp.int32, (TILE, TILE), 0)
    c = lax.broadcasted_iota(jnp.int32, (TILE, TILE), 1)
    idx = pl.program_id(0)
    if mode == "branch":
        d = idx * TILE + r - c
        ok = (idx < 2) & (d >= 0) & (d <= TILE)
    else:
        d = (idx - 1) * TILE + r - c
        ok = (idx > 0) & (d >= 0)
        if mode == "window":
            ok = ok & (d < NSA_WINDOW)
    dd = jnp.maximum(d, 0) * dil
    for h in range(n_heads):
        o_ref[h, 0] = jnp.where(ok, _bias_of_distance(dd, tab_ref, h), NEG)


def _toeplitz_table(rel_bias, n_heads, n_idx, mode, dil=1):
    return pl.pallas_call(
        functools.partial(_toeplitz_kernel, n_heads=n_heads, mode=mode, dil=dil),
        grid=(n_idx,),
        in_specs=[pl.BlockSpec(memory_space=pltpu.SMEM)],
        out_specs=pl.BlockSpec((n_heads, 1, TILE, TILE), lambda i: (0, i, 0, 0)),
        out_shape=jax.ShapeDtypeStruct((n_heads, n_idx, TILE, TILE), F32),
        compiler_params=_cparams("arbitrary"),
        name="bias_toeplitz_%s%d" % (mode, dil),
    )(rel_bias)


def _cmp_bias_kernel(tab_ref, o_ref, *, tq, n_cmp_pad, t_base):
    t_lo = t_base + pl.program_id(0) * tq
    last = N_BUCKETS - 1
    for c in range(n_cmp_pad // LANES):
        cols = slice(c * LANES, (c + 1) * LANES)
        end_lo = c * LANES * CMP_STRIDE + CMP_LEN - 1
        end_hi = end_lo + (LANES - 1) * CMP_STRIDE
        all_future = t_lo + tq - 1 < end_lo
        all_far = t_lo - end_hi >= BUCKET_THR[last - 1]

        @pl.when(all_future)
        def _():
            for h in range(NSA_HEADS):
                o_ref[h, :, cols] = jnp.full((tq, LANES), NEG, F32)

        @pl.when(all_far)
        def _():
            for h in range(NSA_HEADS):
                o_ref[h, :, cols] = jnp.full((tq, LANES), tab_ref[last, h], F32)

        @pl.when(jnp.logical_not(all_future | all_far))
        def _():
            t = t_lo + lax.broadcasted_iota(jnp.int32, (tq, LANES), 0)
            n = c * LANES + lax.broadcasted_iota(jnp.int32, (tq, LANES), 1)
            d = t - (n * CMP_STRIDE + CMP_LEN - 1)
            dd = jnp.maximum(d, 0)
            for h in range(NSA_HEADS):
                o_ref[h, :, cols] = jnp.where(d >= 0, _bias_of_distance(dd, tab_ref, h), NEG)


def _cmp_bias_table(rel_bias, n_rows, tq, n_cmp_pad, t_base):
    assert n_cmp_pad % LANES == 0
    return pl.pallas_call(
        functools.partial(_cmp_bias_kernel, tq=tq, n_cmp_pad=n_cmp_pad, t_base=t_base),
        grid=(n_rows // tq,),
        in_specs=[pl.BlockSpec(memory_space=pltpu.SMEM)],
        out_specs=pl.BlockSpec((NSA_HEADS, tq, n_cmp_pad), lambda i: (0, i, 0)),
        out_shape=jax.ShapeDtypeStruct((NSA_HEADS, n_rows, n_cmp_pad), F32),
        compiler_params=_cparams("arbitrary"),
        name="bias_cmp",
    )(rel_bias)


def _affine_bias_kernel(tab_ref, o_ref, *, heads, d0, step, limit, union):
    n = o_ref.shape[1]
    d = d0 + step * lax.broadcasted_iota(jnp.int32, (1, n), 1)
    ok = (d >= 0) & (d < limit)
    extra = jnp.zeros((1, n), F32)
    if union:
        cnt = jnp.zeros((1, n), F32)
        for window, dil in DIL_BRANCHES:
            cnt = cnt + jnp.where((d >= 0) & (d <= window) & (jnp.bitwise_and(d, dil - 1) == 0), 1.0, 0.0)
        ok = ok & (cnt > 0.5)
        extra = jnp.log(jnp.maximum(cnt, 1.0))
    dd = jnp.maximum(d, 0)
    for row, h in enumerate(heads):
        if h is None:
            o_ref[row:row + 1, :] = jnp.zeros((1, n), F32)
        else:
            o_ref[row:row + 1, :] = jnp.where(ok, _bias_of_distance(dd, tab_ref, h) + extra, NEG)


def _affine_bias(rel_bias, heads, n, d0, step, limit, union=False):
    return pl.pallas_call(
        functools.partial(_affine_bias_kernel, heads=tuple(heads), d0=d0, step=step, limit=limit, union=union),
        in_specs=[pl.BlockSpec(memory_space=pltpu.SMEM)],
        out_shape=jax.ShapeDtypeStruct((len(heads), n), F32),
        compiler_params=pltpu.CompilerParams(vmem_limit_bytes=VMEM_LIMIT_BYTES),
        name="bias_affine",
    )(rel_bias)


NSA_GROUPED_ROWS = tuple((NSA_GROUP * (row // 8) + row % 8) if row % 8 < NSA_GROUP else None
                         for row in range(8 * NSA_KV_HEADS))


def _ada_kernel(c_ref, w_ref, b_ref, o_ref):
    c = c_ref[...]
    s = (c * jax.nn.sigmoid(c)).astype(BF16)
    o_ref[0] = jnp.dot(s, w_ref[0].astype(BF16), preferred_element_type=F32) + b_ref[0]


def _ada_params(c_all, w_ada, b_ada):
    depth, d, d6 = w_ada.shape
    m = c_all.shape[0]
    return pl.pallas_call(
        _ada_kernel,
        grid=(depth, d6 // d),
        in_specs=[pl.BlockSpec((m, d), lambda l, j: (0, 0)),
                  pl.BlockSpec((1, d, d), lambda l, j: (l, 0, j)),
                  pl.BlockSpec((1, 1, d), lambda l, j: (l, 0, j))],
        out_specs=pl.BlockSpec((1, m, d), lambda l, j: (l, 0, j)),
        out_shape=jax.ShapeDtypeStruct((depth, m, d6), F32),
        compiler_params=_cparams("arbitrary", "arbitrary"),
        name="adaln",
    )(c_all, w_ada, b_ada.reshape(depth, 1, d6))


def _norm_mod(x, g, scale, shift):
    y = x * lax.rsqrt(jnp.mean(x * x, axis=-1, keepdims=True) + NORM_EPS)
    return (y * g) * (1.0 + scale) + shift


def _row_tile(s, want):
    return want if s % want == 0 else s


def _mod_spec(mod, tm):
    if mod.shape[1] == 1:
        return pl.BlockSpec((1, 1, mod.shape[2]), lambda b, i: (b, 0, 0))
    return pl.BlockSpec((1, tm, mod.shape[2]), lambda b, i: (b, i, 0))


def _proj_even_kernel(x_ref, sh_ref, sc_ref, g_ref, w_ref, sb_ref,
                      qn_ref, nsa4_ref, nsabf_ref, win_ref, qf_ref, fkv_ref, fkvbf_ref, small_ref):
    h = _norm_mod(x_ref[0], g_ref[...], sc_ref[0], sh_ref[0]).astype(BF16)
    z = jnp.dot(h, w_ref[...], preferred_element_type=F32)
    qn_ref[0] = (z[:, 0:512] * QK_SCALE).astype(BF16)
    nsa4_ref[0] = z[:, 512:1024]
    nsabf_ref[0] = z[:, 512:1280].astype(BF16)
    win_ref[0] = z[:, 1024:1280]
    qf_ref[0] = (z[:, 1280:1792] * QK_SCALE).astype(BF16)
    fkv_ref[0] = z[:, 1792:2816]
    fkvbf_ref[0] = z[:, 1792:2816].astype(BF16)
    zs = z[:, 2816:2944] + sb_ref[...]
    lane = lax.broadcasted_iota(jnp.int32, zs.shape, 1)
    sig = jax.nn.sigmoid(zs)
    lsg = jnp.minimum(zs, 0.0) - jnp.log1p(jnp.exp(-jnp.abs(zs)))
    small_ref[0] = jnp.where(lane < SMALL_GATES, sig, lsg)


def _proj_even(x, shift, scale, g, w, sb):
    bx, s, d = x.shape
    tm = _row_tile(s, 256)
    n = w.shape[1]
    widths = (512, 512, 768, 256, 512, 1024, 1024, 128)
    dtypes = (BF16, F32, BF16, F32, BF16, F32, BF16, F32)
    return pl.pallas_call(
        _proj_even_kernel,
        grid=(bx, s // tm),
        in_specs=[pl.BlockSpec((1, tm, d), lambda b, i: (b, i, 0)),
                  _mod_spec(shift, tm), _mod_spec(scale, tm),
                  pl.BlockSpec((1, d), lambda b, i: (0, 0)),
                  pl.BlockSpec((d, n), lambda b, i: (0, 0)),
                  pl.BlockSpec((1, LANES), lambda b, i: (0, 0))],
        out_specs=[pl.BlockSpec((1, tm, wd), lambda b, i: (b, i, 0)) for wd in widths],
        out_shape=[jax.ShapeDtypeStruct((bx, s, wd), dt) for wd, dt in zip(widths, dtypes)],
        compiler_params=_cparams("arbitrary", "arbitrary"),
        name="proj_even",
    )(x, shift, scale, g, w, sb)


def _proj_odd_kernel(x_ref, sh_ref, sc_ref, g_ref, w_ref, q_ref, kv_ref, kvbf_ref):
    h = _norm_mod(x_ref[0], g_ref[...], sc_ref[0], sh_ref[0]).astype(BF16)
    z = jnp.dot(h, w_ref[...], preferred_element_type=F32)
    q_ref[0] = (z[:, 0:1024] * QK_SCALE).astype(BF16)
    kv_ref[0] = z[:, 1024:3072]
    kvbf_ref[0] = z[:, 1024:3072].astype(BF16)


def _proj_odd(x, shift, scale, g, w):
    bx, s, d = x.shape
    tm = _row_tile(s, 256)
    n = w.shape[1]
    widths = (1024, 2048, 2048)
    dtypes = (BF16, F32, BF16)
    return pl.pallas_call(
        _proj_odd_kernel,
        grid=(bx, s // tm),
        in_specs=[pl.BlockSpec((1, tm, d), lambda b, i: (b, i, 0)),
                  _mod_spec(shift, tm), _mod_spec(scale, tm),
                  pl.BlockSpec((1, d), lambda b, i: (0, 0)),
                  pl.BlockSpec((d, n), lambda b, i: (0, 0))],
        out_specs=[pl.BlockSpec((1, tm, wd), lambda b, i: (b, i, 0)) for wd in widths],
        out_shape=[jax.ShapeDtypeStruct((bx, s, wd), dt) for wd, dt in zip(widths, dtypes)],
        compiler_params=_cparams("arbitrary", "arbitrary"),
        name="proj_odd",
    )(x, shift, scale, g, w)


def _post_kernel(oa_ref, ob_ref, w_ref, x_ref, gate_ref, g_ref, o_ref):
    half = oa_ref.shape[2]
    y = jnp.dot(oa_ref[0], w_ref[0:half, :], preferred_element_type=F32)
    y = y + jnp.dot(ob_ref[0], w_ref[half:, :], preferred_element_type=F32)
    yn = y * lax.rsqrt(jnp.mean(y * y, axis=-1, keepdims=True) + NORM_EPS) * g_ref[...]
    o_ref[0] = x_ref[0] + gate_ref[0] * yn


def _post(o_a, o_b, w_out, x, gate, g, col_a=0, col_b=0):
    bx, s, d = x.shape
    tm = _row_tile(s, 512)
    half = w_out.shape[0] // 2
    return pl.pallas_call(
        _post_kernel,
        grid=(bx, s // tm),
        in_specs=[pl.BlockSpec((1, tm, half), lambda b, i: (b, i, col_a)),
                  pl.BlockSpec((1, tm, half), lambda b, i: (b, i, col_b)),
                  pl.BlockSpec(w_out.shape, lambda b, i: (0, 0)),
                  pl.BlockSpec((1, tm, d), lambda b, i: (b, i, 0)),
                  _mod_spec(gate, tm),
                  pl.BlockSpec((1, d), lambda b, i: (0, 0))],
        out_specs=pl.BlockSpec((1, tm, d), lambda b, i: (b, i, 0)),
        out_shape=jax.ShapeDtypeStruct((bx, s, d), F32),
        compiler_params=_cparams("arbitrary", "arbitrary"),
        name="post",
    )(o_a, o_b, w_out, x, gate, g)


def _mlp_kernel(x_ref, sh_ref, sc_ref, gate_ref, g2_ref, g3_ref, w1_ref, w2_ref, o_ref, h_ref, acc_ref):
    j = pl.program_id(2)

    @pl.when(j == 0)
    def _():
        h_ref[...] = _norm_mod(x_ref[0], g2_ref[...], sc_ref[0], sh_ref[0]).astype(BF16)
        acc_ref[...] = jnp.zeros_like(acc_ref)

    a = jnp.maximum(jnp.dot(h_ref[...], w1_ref[...], preferred_element_type=F32), 0.0)
    acc_ref[...] += jnp.dot((a * a).astype(BF16), w2_ref[...], preferred_element_type=F32)

    @pl.when(j == pl.num_programs(2) - 1)
    def _():
        y = acc_ref[...]
        yn = y * lax.rsqrt(jnp.mean(y * y, axis=-1, keepdims=True) + NORM_EPS) * g3_ref[...]
        o_ref[0] = x_ref[0] + gate_ref[0] * yn


def _mlp(x, shift, scale, gate, g2, g3, w1, w2):
    bx, s, d = x.shape
    f = w1.shape[1]
    tm = _row_tile(s, 1024)
    tf = 1024

    def mod3(mod):
        if mod.shape[1] == 1:
            return pl.BlockSpec((1, 1, d), lambda b, i, j: (b, 0, 0))
        return pl.BlockSpec((1, tm, d), lambda b, i, j: (b, i, 0))

    return pl.pallas_call(
        _mlp_kernel,
        grid=(bx, s // tm, f // tf),
        in_specs=[pl.BlockSpec((1, tm, d), lambda b, i, j: (b, i, 0)),
                  mod3(shift), mod3(scale), mod3(gate),
                  pl.BlockSpec((1, d), lambda b, i, j: (0, 0)),
                  pl.BlockSpec((1, d), lambda b, i, j: (0, 0)),
                  pl.BlockSpec((d, tf), lambda b, i, j: (0, j)),
                  pl.BlockSpec((tf, d), lambda b, i, j: (j, 0))],
        out_specs=pl.BlockSpec((1, tm, d), lambda b, i, j: (b, i, 0)),
        out_shape=jax.ShapeDtypeStruct((bx, s, d), F32),
        scratch_shapes=[pltpu.VMEM((tm, d), BF16), pltpu.VMEM((tm, d), F32)],
        compiler_params=_cparams("arbitrary", "arbitrary", "arbitrary"),
        name="mlp",
    )(x, shift, scale, gate, g2, g3, w1, w2)


def _cumsum_kernel(x_ref, c_ref, ct_ref, carry_ref, *, tc):
    @pl.when(pl.program_id(1) == 0)
    def _():
        carry_ref[...] = jnp.zeros_like(carry_ref)

    r = lax.broadcasted_iota(jnp.int32, (tc, tc), 0)
    c = lax.broadcasted_iota(jnp.int32, (tc, tc), 1)
    tri = jnp.where(c <= r, 1.0, 0.0).astype(F32)
    cs = jnp.dot(tri, x_ref[0], preferred_element_type=F32, precision=lax.Precision.HIGHEST) + carry_ref[...]
    carry_ref[...] = cs[tc - 1:tc, :]
    c_ref[0] = cs
    ct_ref[0] = cs.T[SMALL_GATES:SMALL_GATES + FOX_HEADS, :]


def _cumsum(small):
    bx, s, _ = small.shape
    tc = _row_tile(s, 256)
    return pl.pallas_call(
        functools.partial(_cumsum_kernel, tc=tc),
        grid=(bx, s // tc),
        in_specs=[pl.BlockSpec((1, tc, LANES), lambda b, i: (b, i, 0))],
        out_specs=[pl.BlockSpec((1, tc, LANES), lambda b, i: (b, i, 0)),
                   pl.BlockSpec((1, FOX_HEADS, tc), lambda b, i: (b, 0, i))],
        out_shape=[jax.ShapeDtypeStruct((bx, s, LANES), F32),
                   jax.ShapeDtypeStruct((bx, FOX_HEADS, s), F32)],
        scratch_shapes=[pltpu.VMEM((1, LANES), F32)],
        compiler_params=_cparams("arbitrary", "arbitrary"),
        name="logf_cumsum",
    )(small)


def _nt_dot(a, b):
    return lax.dot_general(a, b, (((1,), (1,)), ((), ())), preferred_element_type=F32)


def _online_update(state, s, v):
    m, l, acc = state
    m_new = jnp.maximum(m, jnp.max(s, axis=1, keepdims=True))
    alpha = jnp.exp(m - m_new)
    p = jnp.exp(s - m_new)
    l = alpha * l + jnp.sum(p, axis=1, keepdims=True)
    acc = alpha * acc + jnp.dot(p.astype(BF16), v, preferred_element_type=F32)
    return m_new, l, acc


def _init_state(tq):
    return (jnp.full((tq, 1), NEG, F32), jnp.zeros((tq, 1), F32), jnp.zeros((tq, LANES), F32))


def _half_masks(tq):
    lane = lax.broadcasted_iota(jnp.int32, (tq, LANES), 1)
    return lane < HEAD_DIM


def _split_heads(q2, lo):
    zero = jnp.zeros_like(q2)
    return jnp.where(lo, q2, zero), jnp.where(lo, zero, q2)


def _fox_kernel(q_ref, k_ref, v_ref, c_ref, ct_ref, o_ref, *, tq, tk, n_pairs):
    grp = pl.program_id(1)
    i = pl.program_id(2)
    lo = _half_masks(tq)
    lane = lax.broadcasted_iota(jnp.int32, (tq, LANES), 1)
    cblk = c_ref[0]
    qs, cqs, heads = [], [], []
    for p in range(n_pairs):
        qs.extend(_split_heads(q_ref[0, :, p * LANES:(p + 1) * LANES], lo))
        for e in range(2):
            head = (grp * n_pairs + p) * 2 + e
            heads.append(head)
            cqs.append(jnp.sum(jnp.where(lane == SMALL_GATES + head, cblk, 0.0), axis=1, keepdims=True))

    def chunk(c, states, masked):
        off = pl.multiple_of(c * tk, tk)
        if masked:
            row = i * tq + lax.broadcasted_iota(jnp.int32, (tq, tk), 0)
            col = off + lax.broadcasted_iota(jnp.int32, (tq, tk), 1)
            ok = col <= row
        out = []
        for n in range(2 * n_pairs):
            p = n // 2
            k = k_ref[0, pl.ds(off, tk), p * LANES:(p + 1) * LANES]
            v = v_ref[0, pl.ds(off, tk), p * LANES:(p + 1) * LANES]
            ck = ct_ref[0, pl.ds(heads[n], 1), pl.ds(off, tk)]
            s = _nt_dot(qs[n], k) + cqs[n] - ck
            if masked:
                s = jnp.where(ok, s, NEG)
            out.append(_online_update(states[n], s, v))
        return tuple(out)

    n_full = (i * tq) // tk
    init = tuple(_init_state(tq) for _ in range(2 * n_pairs))
    states = lax.fori_loop(0, n_full, lambda c, st: chunk(c, st, False), init)
    states = chunk(n_full, states, True)
    outs = []
    for p in range(n_pairs):
        (_, la, acca), (_, lb, accb) = states[2 * p], states[2 * p + 1]
        outs.append(jnp.where(lo, acca / la, accb / lb))
    o_ref[0] = jnp.concatenate(outs, axis=1).astype(o_ref.dtype)


def _fox_prompt(qf, fkvbf, c, ct):
    bx, s, _ = qf.shape
    tq = _row_tile(s, 512)
    tk = _row_tile(s, 512)
    n_pairs = 1
    n_grp = FOX_HEADS // 2 // n_pairs
    wd = n_pairs * LANES
    return pl.pallas_call(
        functools.partial(_fox_kernel, tq=tq, tk=tk, n_pairs=n_pairs),
        grid=(bx, n_grp, s // tq),
        in_specs=[pl.BlockSpec((1, tq, wd), lambda b, p, i: (b, i, p)),
                  pl.BlockSpec((1, s, wd), lambda b, p, i: (b, 0, p)),
                  pl.BlockSpec((1, s, wd), lambda b, p, i: (b, 0, n_grp + p)),
                  pl.BlockSpec((1, tq, LANES), lambda b, p, i: (b, i, 0)),
                  pl.BlockSpec((1, FOX_HEADS, s), lambda b, p, i: (b, 0, 0))],
        out_specs=pl.BlockSpec((1, tq, wd), lambda b, p, i: (b, i, p)),
        out_shape=jax.ShapeDtypeStruct((bx, s, FOX_HEADS * HEAD_DIM), BF16),
        compiler_params=_cparams("arbitrary", "arbitrary", "arbitrary"),
        name="fox_prompt",
    )(qf, fkvbf, fkvbf, c, ct)


def _dil_branch_kernel(q_ref, kc_ref, kp_ref, vc_ref, vp_ref, tab_ref, o_ref, lse_ref, *, tq):
    i = pl.program_id(2)
    lo = _half_masks(TILE)
    lane = lax.broadcasted_iota(jnp.int32, (TILE, LANES), 1)
    first_prev = jnp.where(i == 0, 2, 1)
    for a in range(tq // TILE):
        rows = slice(a * TILE, (a + 1) * TILE)
        lse_tile = jnp.zeros((TILE, LANES), F32)
        outs = []
        for p in range(DIL_HEADS // 2):
            cols = slice(p * LANES, (p + 1) * LANES)
            if a == 0:
                k_prev, v_prev, prev_idx = kp_ref[0, :, cols], vp_ref[0, :, cols], first_prev
            else:
                prev = slice((a - 1) * TILE, a * TILE)
                k_prev, v_prev, prev_idx = kc_ref[0, prev, cols], vc_ref[0, prev, cols], 1
            k2 = jnp.concatenate([k_prev, kc_ref[0, rows, cols]], axis=0)
            v2 = jnp.concatenate([v_prev, vc_ref[0, rows, cols]], axis=0)
            pair = []
            for e, qh in enumerate(_split_heads(q_ref[0, rows, cols], lo)):
                h = 2 * p + e
                bias = jnp.concatenate([tab_ref[h, prev_idx], tab_ref[h, 0]], axis=1)
                s = _nt_dot(qh, k2) + bias
                m = jnp.max(s, axis=1, keepdims=True)
                e_s = jnp.exp(s - m)
                l = jnp.sum(e_s, axis=1, keepdims=True)
                pair.append(jnp.dot(e_s.astype(BF16), v2, preferred_element_type=F32) / l)
                lse_tile = jnp.where(lane == h, jnp.log(l) + m, lse_tile)
            outs.append(jnp.where(lo, pair[0], pair[1]))
        o_ref[0, rows, :] = jnp.concatenate(outs, axis=1).astype(o_ref.dtype)
        lse_ref[0, rows, :] = lse_tile


def _dil_branch_prompt(q, kvbf, table, dil):
    bx, s, width = q.shape
    n_rows = s // dil
    tq = _row_tile(n_rows, 256)
    sub = tq // TILE
    qv = q.reshape(bx, n_rows, dil * width)
    kvv = kvbf.reshape(bx, n_rows, dil * 2 * width)
    o, lse = pl.pallas_call(
        functools.partial(_dil_branch_kernel, tq=tq),
        grid=(bx, dil, n_rows // tq),
        in_specs=[pl.BlockSpec((1, tq, width), lambda b, r, i: (b, i, r)),
                  pl.BlockSpec((1, tq, width), lambda b, r, i: (b, i, 2 * r)),
                  pl.BlockSpec((1, TILE, width), lambda b, r, i: (b, jnp.maximum(sub * i - 1, 0), 2 * r)),
                  pl.BlockSpec((1, tq, width), lambda b, r, i: (b, i, 2 * r + 1)),
                  pl.BlockSpec((1, TILE, width), lambda b, r, i: (b, jnp.maximum(sub * i - 1, 0), 2 * r + 1)),
                  pl.BlockSpec(table.shape, lambda b, r, i: (0, 0, 0, 0))],
        out_specs=[pl.BlockSpec((1, tq, width), lambda b, r, i: (b, i, r)),
                   pl.BlockSpec((1, tq, LANES), lambda b, r, i: (b, i, r))],
        out_shape=[jax.ShapeDtypeStruct((bx, n_rows, dil * width), BF16),
                   jax.ShapeDtypeStruct((bx, n_rows, dil * LANES), F32)],
        compiler_params=_cparams("arbitrary", "arbitrary", "arbitrary"),
        name="dilated_branch_prompt",
    )(qv, kvv, kvv, kvv, kvv, table)
    return o.reshape(bx, s, width), lse.reshape(bx, s, LANES)


def _post_dil_kernel(o1_ref, o2_ref, o3_ref, l1_ref, l2_ref, l3_ref, w_ref, x_ref, gate_ref, g_ref, o_ref):
    lses = [r[0] for r in (l1_ref, l2_ref, l3_ref)]
    m = jnp.maximum(jnp.maximum(lses[0], lses[1]), lses[2])
    es = [jnp.exp(l - m) for l in lses]
    tot = es[0] + es[1] + es[2]
    width = o1_ref.shape[2]
    head_of_col = lax.shift_right_arithmetic(lax.broadcasted_iota(jnp.int32, (LANES, width), 1), SEL_SHIFT)
    expand = jnp.where(head_of_col == lax.broadcasted_iota(jnp.int32, (LANES, width), 0), 1.0, 0.0).astype(BF16)
    mix = jnp.zeros((o1_ref.shape[1], width), F32)
    for e, o_ref_j in zip(es, (o1_ref, o2_ref, o3_ref)):
        alpha = jnp.dot((e / tot).astype(BF16), expand, preferred_element_type=F32)
        mix = mix + alpha * o_ref_j[0].astype(F32)
    y = jnp.dot(mix.astype(BF16), w_ref[...], preferred_element_type=F32)
    yn = y * lax.rsqrt(jnp.mean(y * y, axis=-1, keepdims=True) + NORM_EPS) * g_ref[...]
    o_ref[0] = x_ref[0] + gate_ref[0] * yn


def _post_dil(outs, lses, w_out, x, gate, g):
    bx, s, d = x.shape
    tm = _row_tile(s, 256)
    width = w_out.shape[0]
    return pl.pallas_call(
        _post_dil_kernel,
        grid=(bx, s // tm),
        in_specs=([pl.BlockSpec((1, tm, width), lambda b, i: (b, i, 0))] * 3
                  + [pl.BlockSpec((1, tm, LANES), lambda b, i: (b, i, 0))] * 3
                  + [pl.BlockSpec(w_out.shape, lambda b, i: (0, 0)),
                     pl.BlockSpec((1, tm, d), lambda b, i: (b, i, 0)),
                     _mod_spec(gate, tm),
                     pl.BlockSpec((1, d), lambda b, i: (0, 0))]),
        out_specs=pl.BlockSpec((1, tm, d), lambda b, i: (b, i, 0)),
        out_shape=jax.ShapeDtypeStruct((bx, s, d), F32),
        compiler_params=_cparams("arbitrary", "arbitrary"),
        name="post_dilated",
    )(*outs, *lses, w_out, x, gate, g)


def _compress_kernel(pt_ref, *refs, n_pages, pages_per_step, feature_major):
    page_refs = refs[:pages_per_step]
    w1_ref, w2_ref, pe_ref, kc_ref, vc_ref, rows_ref, chunk_ref = refs[pages_per_step:]
    j = pl.program_id(1)
    for p, page_ref in enumerate(page_refs):
        row0 = pl.multiple_of((j * pages_per_step + p) * PAGE_SIZE, PAGE_SIZE)
        if feature_major:
            for kv in range(2):
                rows_ref[kv, pl.ds(row0, PAGE_SIZE), :] = jnp.concatenate(
                    [page_ref[0, 0, kv, g].T for g in range(NSA_KV_HEADS)], axis=1)
        else:
            rows_ref[0, pl.ds(row0, PAGE_SIZE), :] = page_ref[0, :, 0:LANES]
            rows_ref[1, pl.ds(row0, PAGE_SIZE), :] = page_ref[0, :, LANES:2 * LANES]

    @pl.when(j == n_pages // pages_per_step - 1)
    def _():
        n_chunks = n_pages * PAGE_SIZE // CMP_STRIDE
        half = CMP_STRIDE * HEAD_DIM
        for kv, out_ref in ((0, kc_ref), (1, vc_ref)):
            w1 = w1_ref[kv]
            pe_a = jnp.broadcast_to(pe_ref[kv, :, 0:half], (8, half)).astype(BF16)
            pe_b = jnp.broadcast_to(pe_ref[kv, :, half:], (8, half)).astype(BF16)
            pe_term = (jnp.dot(pe_a, w1, preferred_element_type=F32)[0:1, 0:CMP_HIDDEN]
                       + jnp.dot(pe_b, w1, preferred_element_type=F32)[0:1, CMP_HIDDEN:])
            for l in range(CMP_STRIDE):
                both = rows_ref[kv, pl.ds(l, n_chunks, stride=CMP_STRIDE), :].astype(BF16)
                for g in range(NSA_KV_HEADS):
                    chunk_ref[g, :, l * HEAD_DIM:(l + 1) * HEAD_DIM] = both[:, g * HEAD_DIM:(g + 1) * HEAD_DIM]
            outs = []
            for g in range(NSA_KV_HEADS):
                uv = jnp.dot(chunk_ref[g], w1, preferred_element_type=F32)
                pre = uv[:, 0:CMP_HIDDEN] + pltpu.roll(uv[:, CMP_HIDDEN:], n_chunks - 1, 0) + pe_term
                hid = jax.nn.gelu(pre).astype(BF16)
                outs.append(jnp.dot(hid, w2_ref[kv], preferred_element_type=F32))
            out_ref[0] = jnp.concatenate(outs, axis=1)


def _compress(pool, page_table, w1cat, w2, pe, layer=None):
    n_req, n_pages = page_table.shape
    n_chunks = n_pages * PAGE_SIZE // CMP_STRIDE
    width = 2 * NSA_KV_HEADS * HEAD_DIM
    pps = 8 if n_pages % 8 == 0 else 1

    def page_spec(p):
        if layer is None:
            return pl.BlockSpec((1, PAGE_SIZE, width), lambda r, j, pt: (pt[r, j * pps + p], 0, 0))
        return pl.BlockSpec((1, 1, 2, NSA_KV_HEADS, HEAD_DIM, PAGE_SIZE),
                            lambda r, j, pt: (layer, pt[r, j * pps + p], 0, 0, 0, 0))

    grid_spec = pltpu.PrefetchScalarGridSpec(
        num_scalar_prefetch=1,
        grid=(n_req, n_pages // pps),
        in_specs=[page_spec(p) for p in range(pps)] + [
                  pl.BlockSpec(w1cat.shape, lambda r, j, pt: (0, 0, 0)),
                  pl.BlockSpec(w2.shape, lambda r, j, pt: (0, 0, 0)),
                  pl.BlockSpec(pe.shape, lambda r, j, pt: (0, 0, 0))],
        out_specs=[pl.BlockSpec((1, n_chunks, LANES), lambda r, j, pt: (r, 0, 0)),
                   pl.BlockSpec((1, n_chunks, LANES), lambda r, j, pt: (r, 0, 0))],
        scratch_shapes=[pltpu.VMEM((2, n_pages * PAGE_SIZE, LANES), F32),
                        pltpu.VMEM((NSA_KV_HEADS, n_chunks, CMP_STRIDE * HEAD_DIM), BF16)],
    )
    return pl.pallas_call(
        functools.partial(_compress_kernel, n_pages=n_pages, pages_per_step=pps, feature_major=layer is not None),
        grid_spec=grid_spec,
        out_shape=[jax.ShapeDtypeStruct((n_req, n_chunks, LANES), F32)] * 2,
        compiler_params=_cparams("arbitrary", "arbitrary"),
        name="nsa_compress",
    )(page_table, *([pool] * pps), w1cat, w2, pe)


def _top_k_mask(imp, n_top):
    cand = lax.broadcasted_iota(jnp.int32, imp.shape, 0)
    height = imp.shape[0]

    def body(_, carry):
        imp, sel = carry
        m = jnp.max(imp, axis=0, keepdims=True)
        first = jnp.min(jnp.where(imp == m, cand, height), axis=0, keepdims=True)
        pick = (cand == first) & (m > -jnp.inf)
        return jnp.where(cand == first, -jnp.inf, imp), jnp.where(pick, 1.0, sel)

    _, sel = lax.fori_loop(0, n_top, body, (imp, jnp.zeros(imp.shape, F32)))
    return sel


def _nsa_kernel(qn_ref, kc_ref, vc_ref, bcmp_ref, sel_ref, win_ref, tsel_ref, twin_ref, small_ref, o_ref,
                *, tq, tk, n_sel_delta, n_win_tiles):
    i = pl.program_id(1)
    t0 = i * tq
    n_cmp_pad = kc_ref.shape[1]
    n_blk = LANES
    lo = _half_masks(tq)
    lane = lax.broadcasted_iota(jnp.int32, (tq, LANES), 1)
    gates = small_ref[0]
    q_all = qn_ref[0].astype(F32)

    sj = lax.broadcasted_iota(jnp.int32, (n_blk, n_cmp_pad), 0) * SEL_BLOCK
    ci = lax.broadcasted_iota(jnp.int32, (n_blk, n_cmp_pad), 1) * CMP_STRIDE
    cover_t = jnp.where((ci < sj + SEL_BLOCK) & (ci + CMP_LEN > sj), 1.0, 0.0).astype(F32)
    blk_t = lax.broadcasted_iota(jnp.int32, (n_blk, tq), 0)
    cur_t = lax.shift_right_arithmetic(t0 + lax.broadcasted_iota(jnp.int32, (n_blk, tq), 1), SEL_SHIFT)
    forced_t = (blk_t == 0) | (blk_t == cur_t) | (blk_t == cur_t - 1)
    blk_of_key = lax.shift_right_arithmetic(lax.broadcasted_iota(jnp.int32, (tk, n_blk), 0), SEL_SHIFT)
    blk_delta = lax.broadcasted_iota(jnp.int32, (tk, n_blk), 1) - blk_of_key
    sub = tk // TILE

    n_sub = tq // TILE
    group_of = [h // NSA_GROUP for h in range(NSA_HEADS)]

    qs = []
    for h in range(NSA_HEADS):
        g = group_of[h]
        blk = q_all[:, LANES * (h // 2):LANES * (h // 2 + 1)]
        if h % 2 != g:
            blk = pltpu.roll(blk, HEAD_DIM, 1)
        qs.append(jnp.where(lo if g == 0 else jnp.logical_not(lo), blk, 0.0).astype(BF16))

    kcb = kc_ref[0].astype(BF16)
    vcb = vc_ref[0].astype(BF16)
    o_cmp, imps = [], []
    for g in range(NSA_KV_HEADS):
        psum = jnp.zeros((tq, n_cmp_pad), F32)
        for r in range(NSA_GROUP):
            h = g * NSA_GROUP + r
            s = _nt_dot(qs[h], kcb) + bcmp_ref[h]
            m = jnp.max(s, axis=1, keepdims=True)
            e = jnp.exp(s - m)
            scale = jnp.where(m > 0.5 * NEG, 1.0 / jnp.maximum(jnp.sum(e, axis=1, keepdims=True), TINY), 0.0)
            p = e * scale
            psum = psum + p
            o_cmp.append(jnp.dot(p.astype(BF16), vcb, preferred_element_type=F32))
        imp = lax.dot_general(cover_t, psum, (((1,), (1,)), ((), ())), preferred_element_type=F32,
                              precision=lax.Precision.HIGHEST)
        imp = jnp.where(forced_t, FORCED_SCORE, imp)
        imps.append(jnp.where(blk_t <= cur_t, imp, -jnp.inf))

    sel_t = _top_k_mask(jnp.concatenate(imps, axis=1), min(SEL_TOPK, n_blk))
    not_sel = [(1.0 - sel_t[:, g * tq:(g + 1) * tq]).T.astype(BF16) for g in range(NSA_KV_HEADS)]

    q_ext = [jnp.concatenate([qs[h], not_sel[group_of[h]]], axis=1) for h in range(NSA_HEADS)]

    def sel_chunk(c, states):
        off = pl.multiple_of(c * tk, tk)
        k = sel_ref[0, pl.ds(off, tk), 0:LANES]
        v = sel_ref[0, pl.ds(off, tk), LANES:2 * LANES]
        k_ext = jnp.concatenate([k, jnp.where(blk_delta == c * (tk // SEL_BLOCK), NEG, 0.0).astype(BF16)], axis=1)
        out = []
        for h in range(NSA_HEADS):
            bias = jnp.concatenate([jnp.concatenate(
                [tsel_ref[h, jnp.clip(n_sub * i + a - (c * sub + u), -1, n_sel_delta - 1) + 1] for u in range(sub)],
                axis=1) for a in range(n_sub)], axis=0)
            out.append(_online_update(states[h], _nt_dot(q_ext[h], k_ext) + bias, v))
        return tuple(out)

    n_chunks = (t0 + tq - 1) // tk + 1
    sel_states = lax.fori_loop(0, n_chunks, sel_chunk, tuple(_init_state(tq) for _ in range(NSA_HEADS)))

    o_win = [[] for _ in range(NSA_HEADS)]
    for a in range(n_sub):
        j0 = n_sub * i + a
        ks, vs, idxs = [], [], []
        for u in range(n_win_tiles - 1, -1, -1):
            off = pl.multiple_of(jnp.maximum(j0 - u, 0) * TILE, TILE)
            ks.append(win_ref[0, pl.ds(off, TILE), 0:LANES])
            vs.append(win_ref[0, pl.ds(off, TILE), LANES:2 * LANES])
            idxs.append(jnp.where(j0 - u >= 0, u + 1, 0))
        k = jnp.concatenate(ks, axis=0)
        v = jnp.concatenate(vs, axis=0)
        for h in range(NSA_HEADS):
            bias = jnp.concatenate([twin_ref[h, ix] for ix in idxs], axis=1)
            s = _nt_dot(qs[h][a * TILE:(a + 1) * TILE], k) + bias
            e = jnp.exp(s - jnp.max(s, axis=1, keepdims=True))
            l = jnp.sum(e, axis=1, keepdims=True)
            o_win[h].append(jnp.dot(e.astype(BF16), v, preferred_element_type=F32) / l)

    pair_out = [None] * (NSA_HEADS // 2)
    for h in range(NSA_HEADS):
        gc, gs, gw = (jnp.sum(jnp.where(lane == 3 * h + b, gates, 0.0), axis=1, keepdims=True) for b in range(3))
        _, l_s, acc_s = sel_states[h]
        o = gc * o_cmp[h] + gs * (acc_s / l_s) + gw * jnp.concatenate(o_win[h], axis=0)
        if h % 2 != group_of[h]:
            o = pltpu.roll(o, HEAD_DIM, 1)
        prev = pair_out[h // 2]
        keep = lo if h % 2 == 0 else jnp.logical_not(lo)
        pair_out[h // 2] = jnp.where(keep, o, 0.0 if prev is None else prev)

    o_ref[0] = jnp.concatenate(pair_out, axis=1).astype(o_ref.dtype)


def _nsa_prompt(qn, kc, vc, bcmp, nsabf, tsel, twin, small):
    bx, s, _ = qn.shape
    tq = _row_tile(s, 256)
    tk = _row_tile(s, 1024)
    n_cmp_pad = kc.shape[1]
    once = pl.Buffered(1)
    return pl.pallas_call(
        functools.partial(_nsa_kernel, tq=tq, tk=tk, n_sel_delta=tsel.shape[1] - 1, n_win_tiles=twin.shape[1] - 1),
        grid=(bx, s // tq),
        in_specs=[pl.BlockSpec((1, tq, NSA_HEADS * HEAD_DIM), lambda b, i: (b, i, 0)),
                  pl.BlockSpec((1, n_cmp_pad, LANES), lambda b, i: (b, 0, 0)),
                  pl.BlockSpec((1, n_cmp_pad, LANES), lambda b, i: (b, 0, 0)),
                  pl.BlockSpec((NSA_HEADS, tq, n_cmp_pad), lambda b, i: (0, i, 0)),
                  pl.BlockSpec((1, s, 2 * LANES), lambda b, i: (b, 0, 1)),
                  pl.BlockSpec((1, s, 2 * LANES), lambda b, i: (b, 0, 2)),
                  pl.BlockSpec(tsel.shape, lambda b, i: (0, 0, 0, 0), pipeline_mode=once),
                  pl.BlockSpec(twin.shape, lambda b, i: (0, 0, 0, 0), pipeline_mode=once),
                  pl.BlockSpec((1, tq, LANES), lambda b, i: (b, i, 0))],
        out_specs=pl.BlockSpec((1, tq, NSA_HEADS * HEAD_DIM), lambda b, i: (b, i, 0)),
        out_shape=jax.ShapeDtypeStruct((bx, s, NSA_HEADS * HEAD_DIM), BF16),
        compiler_params=_cparams("arbitrary", "arbitrary"),
        name="nsa_prompt",
    )(qn, kc, vc, bcmp, nsabf, nsabf, tsel, twin, small)


def _bf16_round(x):
    return x.astype(BF16).astype(F32)


def _dot3(z, w):
    hi = z.astype(BF16)
    rest = z - hi.astype(F32)
    mid = rest.astype(BF16)
    lo = (rest - mid.astype(F32)).astype(BF16)
    return sum(jnp.dot(part, w, preferred_element_type=F32) for part in (hi, mid, lo))


def _feature_major(cache):
    n = cache.ndim
    return jnp.transpose(cache, tuple(range(n - 4)) + (n - 3, n - 2, n - 1, n - 4))


def _nt_dot_bf16(p, vt):
    return lax.dot_general(p.astype(BF16), vt, (((1,), (1,)), ((), ())), preferred_element_type=F32)


def _fox_sample_kernel(pt_ref, *refs, n_steps, pages_per_step):
    page_refs, lf_refs = refs[:pages_per_step], refs[pages_per_step:2 * pages_per_step]
    q_ref, new_ref, lfn_ref, o_ref, m_ref, l_ref, acc_ref, carry_ref = refs[2 * pages_per_step:]
    j = pl.program_id(1)
    width = FOX_HEADS * HEAD_DIM
    qbd = q_ref[0]

    @pl.when(j == 0)
    def _():
        s_new = jnp.sum(qbd.astype(F32) * _bf16_round(new_ref[0, 0:1]), axis=1, keepdims=True)
        m_ref[...] = jnp.broadcast_to(s_new, m_ref.shape)
        l_ref[...] = jnp.ones_like(l_ref)
        acc_ref[...] = jnp.broadcast_to(_bf16_round(new_ref[0, 1:2]), acc_ref.shape)
        carry_ref[...] = lfn_ref[0]

    u = lax.broadcasted_iota(jnp.int32, (PAGE_SIZE, PAGE_SIZE), 0)
    c = lax.broadcasted_iota(jnp.int32, (PAGE_SIZE, PAGE_SIZE), 1)
    later = jnp.where(u > c, 1.0, 0.0).astype(BF16)
    carry = carry_ref[:, 0:1]
    biases = []
    for lf_ref in lf_refs:
        lf = lf_ref[0, 0]
        biases.append(carry + _dot3(lf, later))
        carry = carry + jnp.sum(lf, axis=1, keepdims=True)
    kt = jnp.concatenate([r[0, 0, 0].reshape(width, PAGE_SIZE).astype(BF16) for r in page_refs], axis=1)
    vt = jnp.concatenate([r[0, 0, 1].reshape(width, PAGE_SIZE).astype(BF16) for r in page_refs], axis=1)
    s = jnp.dot(qbd, kt, preferred_element_type=F32) + jnp.concatenate(biases, axis=1)
    m_old = m_ref[:, 0:1]
    m_new = jnp.maximum(m_old, jnp.max(s, axis=1, keepdims=True))
    alpha = jnp.exp(m_old - m_new)
    p = jnp.exp(s - m_new)
    l_ref[...] = jnp.broadcast_to(alpha * l_ref[:, 0:1] + jnp.sum(p, axis=1, keepdims=True), l_ref.shape)
    acc_ref[...] = alpha * acc_ref[...] + _nt_dot_bf16(p, vt)
    m_ref[...] = jnp.broadcast_to(m_new, m_ref.shape)
    carry_ref[...] = jnp.broadcast_to(carry, carry_ref.shape)

    @pl.when(j == n_steps - 1)
    def _():
        o_ref[0] = acc_ref[...] / l_ref[:, 0:1]


def _block_diag_queries(q, n_heads):
    width = n_heads * HEAD_DIM
    keep = (jnp.arange(width) // HEAD_DIM)[None, :] == jnp.arange(n_heads)[:, None]
    return jnp.where(keep[None], q[:, None, :], jnp.zeros((), q.dtype))


def _diag_blocks(o, n_heads):
    n = o.shape[0]
    o5 = o.reshape(n, n_heads, n_heads, HEAD_DIM)
    return jnp.stack([o5[:, h, h] for h in range(n_heads)], axis=1).reshape(n, n_heads * HEAD_DIM)


def _fox_sample(cache_t, layer, logf_t, page_table, qbd, new_kv, lf_new):
    n_req, n_pages = page_table.shape
    pps = 8 if n_pages % 8 == 0 else 1
    n_steps = n_pages // pps
    width = FOX_HEADS * HEAD_DIM

    def page_spec(shape, p):
        zeros = (0,) * (len(shape) - 2)
        return pl.BlockSpec(shape, lambda r, j, pt: (layer, pt[r, n_pages - 1 - (j * pps + p)]) + zeros)

    grid_spec = pltpu.PrefetchScalarGridSpec(
        num_scalar_prefetch=1,
        grid=(n_req, n_steps),
        in_specs=([page_spec((1, 1, 2, FOX_HEADS, HEAD_DIM, PAGE_SIZE), p) for p in range(pps)]
                  + [page_spec((1, 1, FOX_HEADS, PAGE_SIZE), p) for p in range(pps)]
                  + [pl.BlockSpec((1, FOX_HEADS, width), lambda r, j, pt: (r, 0, 0)),
                     pl.BlockSpec((1, 2, width), lambda r, j, pt: (r, 0, 0)),
                     pl.BlockSpec((1, FOX_HEADS, LANES), lambda r, j, pt: (r, 0, 0))]),
        out_specs=pl.BlockSpec((1, FOX_HEADS, width), lambda r, j, pt: (r, 0, 0)),
        scratch_shapes=[pltpu.VMEM((FOX_HEADS, LANES), F32), pltpu.VMEM((FOX_HEADS, LANES), F32),
                        pltpu.VMEM((FOX_HEADS, width), F32), pltpu.VMEM((FOX_HEADS, LANES), F32)],
    )
    return pl.pallas_call(
        functools.partial(_fox_sample_kernel, n_steps=n_steps, pages_per_step=pps),
        grid_spec=grid_spec,
        out_shape=jax.ShapeDtypeStruct((n_req, FOX_HEADS, width), F32),
        compiler_params=_cparams("arbitrary", "arbitrary"),
        name="fox_sample",
    )(page_table, *([cache_t] * pps), *([logf_t] * pps), qbd, new_kv, lf_new)


def _dil_sample_kernel(q_ref, new_ref, kt_ref, vt_ref, tab_ref, o_ref, *, n_buf):
    heads = q_ref.shape[2]
    qbd = q_ref[0, 0]
    kt = kt_ref[0, 0, 0].reshape(heads * HEAD_DIM, n_buf).astype(BF16)
    vt = vt_ref[0, 0, 0].reshape(heads * HEAD_DIM, n_buf).astype(BF16)
    tab = tab_ref[0]
    s = jnp.dot(qbd, kt, preferred_element_type=F32) + tab[:, 0:n_buf]
    s_new = (jnp.sum(qbd.astype(F32) * _bf16_round(new_ref[0, 0, 0:1]), axis=1, keepdims=True)
             + tab[:, n_buf:n_buf + 1])
    m = jnp.maximum(jnp.max(s, axis=1, keepdims=True), s_new)
    p = jnp.exp(s - m)
    p_new = jnp.exp(s_new - m)
    l = jnp.sum(p, axis=1, keepdims=True) + p_new
    o_ref[0, 0] = (_nt_dot_bf16(p, vt) + _bf16_round(p_new) * _bf16_round(new_ref[0, 0, 1:2])) / l


def _dil_sample(state_t, layer, qbd, new_kv, table):
    n_req = state_t.shape[1]
    n_buf = state_t.shape[-1]
    half = DIL_HEADS // 2
    width = half * HEAD_DIM
    return pl.pallas_call(
        functools.partial(_dil_sample_kernel, n_buf=n_buf),
        grid=(n_req, 2),
        in_specs=[pl.BlockSpec((1, 1, half, width), lambda r, hh: (r, hh, 0, 0)),
                  pl.BlockSpec((1, 1, 2, width), lambda r, hh: (r, hh, 0, 0)),
                  pl.BlockSpec((1, 1, 1, half, HEAD_DIM, n_buf), lambda r, hh: (layer, r, 0, hh, 0, 0)),
                  pl.BlockSpec((1, 1, 1, half, HEAD_DIM, n_buf), lambda r, hh: (layer, r, 1, hh, 0, 0)),
                  pl.BlockSpec((1, half, table.shape[2]), lambda r, hh: (hh, 0, 0))],
        out_specs=pl.BlockSpec((1, 1, half, width), lambda r, hh: (r, hh, 0, 0)),
        out_shape=jax.ShapeDtypeStruct((n_req, 2, half, width), F32),
        compiler_params=_cparams("arbitrary", "arbitrary"),
        name="dilated_sample",
    )(qbd, new_kv, state_t, state_t, table)


def _group_queries(q_row):
    row8 = lax.broadcasted_iota(jnp.int32, (8, LANES), 0)
    lo = lax.broadcasted_iota(jnp.int32, (8, LANES), 1) < HEAD_DIM
    out = []
    for g in range(NSA_KV_HEADS):
        qg = jnp.zeros((8, LANES), F32)
        for r in range(NSA_GROUP):
            h = g * NSA_GROUP + r
            blk = jnp.broadcast_to(q_row[:, LANES * (h // 2):LANES * (h // 2 + 1)], (8, LANES))
            if h % 2 != g:
                blk = pltpu.roll(blk, HEAD_DIM, 1)
            qg = jnp.where((row8 == r) & (lo if g == 0 else jnp.logical_not(lo)), blk, qg)
        out.append(qg)
    return out


def _nsa_cmp_kernel(q_ref, kc_ref, vc_ref, bcmp_ref, oc_ref, psum_ref):
    n_cmp_pad = kc_ref.shape[1]
    kcb = kc_ref[0].astype(BF16)
    vcb = vc_ref[0].astype(BF16)
    row8 = lax.broadcasted_iota(jnp.int32, (8, n_cmp_pad), 0)
    for g, qg in enumerate(_group_queries(q_ref[0].astype(F32))):
        bias = bcmp_ref[8 * g:8 * g + 8, :]
        ok = (bias > 0.5 * NEG) & (row8 < NSA_GROUP)
        s = jnp.where(ok, _nt_dot(qg.astype(BF16), kcb) + bias, NEG)
        m = jnp.max(s, axis=1, keepdims=True)
        e = jnp.where(ok, jnp.exp(s - m), 0.0)
        p = e / jnp.maximum(jnp.sum(e, axis=1, keepdims=True), TINY)
        oc_ref[0, g] = jnp.dot(p.astype(BF16), vcb, preferred_element_type=F32)
        psum_ref[0, g] = jnp.broadcast_to(jnp.sum(p, axis=0, keepdims=True), (8, n_cmp_pad))


def _nsa_select(q, kc, vc, bcmp_row, n_blk_pad, cur):
    n_req = q.shape[0]
    n_cmp_pad = kc.shape[1]
    o_cmp, psum = pl.pallas_call(
        _nsa_cmp_kernel,
        grid=(n_req,),
        in_specs=[pl.BlockSpec((1, 1, NSA_HEADS * HEAD_DIM), lambda r: (r, 0, 0)),
                  pl.BlockSpec((1, n_cmp_pad, LANES), lambda r: (r, 0, 0)),
                  pl.BlockSpec((1, n_cmp_pad, LANES), lambda r: (r, 0, 0)),
                  pl.BlockSpec(bcmp_row.shape, lambda r: (0, 0))],
        out_specs=[pl.BlockSpec((1, NSA_KV_HEADS, 8, LANES), lambda r: (r, 0, 0, 0)),
                   pl.BlockSpec((1, NSA_KV_HEADS, 8, n_cmp_pad), lambda r: (r, 0, 0, 0))],
        out_shape=[jax.ShapeDtypeStruct((n_req, NSA_KV_HEADS, 8, LANES), F32),
                   jax.ShapeDtypeStruct((n_req, NSA_KV_HEADS, 8, n_cmp_pad), F32)],
        compiler_params=_cparams("arbitrary"),
        name="nsa_sample_cmp",
    )(q, kc, vc, bcmp_row)
    n_pairs = n_req * NSA_KV_HEADS
    assert n_pairs <= LANES
    pairs = jnp.pad(psum[:, :, 0, :].reshape(n_pairs, n_cmp_pad), ((0, LANES - n_pairs), (0, 0)))
    idx = pl.pallas_call(
        functools.partial(_nsa_rank_kernel, n_blk_pad=n_blk_pad, cur=cur),
        out_shape=jax.ShapeDtypeStruct((SEL_TOPK, LANES), jnp.int32),
        compiler_params=pltpu.CompilerParams(vmem_limit_bytes=VMEM_LIMIT_BYTES),
        name="nsa_sample_topk",
    )(pairs)
    return o_cmp, idx[:, :n_pairs].T.reshape(n_req, NSA_KV_HEADS * SEL_TOPK)


def _nsa_rank_kernel(psum_ref, idx_ref, *, n_blk_pad, cur):
    n_cmp_pad = psum_ref.shape[1]
    sj = lax.broadcasted_iota(jnp.int32, (n_blk_pad, n_cmp_pad), 0) * SEL_BLOCK
    ci = lax.broadcasted_iota(jnp.int32, (n_blk_pad, n_cmp_pad), 1) * CMP_STRIDE
    cover_t = jnp.where((ci < sj + SEL_BLOCK) & (ci + CMP_LEN > sj), 1.0, 0.0).astype(F32)
    imp = lax.dot_general(cover_t, psum_ref[...], (((1,), (1,)), ((), ())), preferred_element_type=F32,
                          precision=lax.Precision.HIGHEST)
    blk = lax.broadcasted_iota(jnp.int32, imp.shape, 0)
    imp = jnp.where((blk == 0) | (blk == cur) | (blk == cur - 1), FORCED_SCORE, imp)
    imp = jnp.where(blk <= cur, imp, -jnp.inf)
    pick_row = lax.broadcasted_iota(jnp.int32, idx_ref.shape, 0)

    def body(it, carry):
        imp, idx = carry
        top = jnp.max(imp, axis=0, keepdims=True)
        first = jnp.min(jnp.where(imp == top, blk, n_blk_pad), axis=0, keepdims=True)
        idx = jnp.where(pick_row == it, jnp.where(top > -jnp.inf, first, -1), idx)
        return jnp.where(blk == first, -jnp.inf, imp), idx

    _, idx = lax.fori_loop(0, SEL_TOPK, body, (imp, jnp.full(idx_ref.shape, -1, jnp.int32)))
    idx_ref[...] = idx


def _nsa_attend_kernel(idx_ref, pt_ref, q_ref, *refs, n_past_blk, new_lane, n_win):
    n_picks = NSA_KV_HEADS * SEL_TOPK
    blk_refs = refs[:n_picks]
    fsel_ref, wbuf_ref, fwin_ref, nsel_ref, nwin_ref, oc_ref, small_ref, o_ref = refs[n_picks:]
    r_idx = pl.program_id(0)
    qgs = [qg[:, g * HEAD_DIM:(g + 1) * HEAD_DIM] for g, qg in enumerate(_group_queries(q_ref[0].astype(F32)))]
    row8 = lax.broadcasted_iota(jnp.int32, (8, LANES), 0)
    lane = lax.broadcasted_iota(jnp.int32, (8, LANES), 1)
    n_pages_past = n_past_blk // 2
    gates = small_ref[0]

    for g, qg in enumerate(qgs):
        grp = slice(g * HEAD_DIM, (g + 1) * HEAD_DIM)
        rows = slice(8 * g, 8 * g + 8)
        qb = qg.astype(BF16)
        ksel_new = _bf16_round(nsel_ref[0][:, 2 * LANES:3 * LANES][:, grp])
        vsel_new = _bf16_round(nsel_ref[0][:, 3 * LANES:4 * LANES][:, grp])
        kwin_new = _bf16_round(nwin_ref[0][:, 0:LANES][:, grp])
        vwin_new = _bf16_round(nwin_ref[0][:, LANES:2 * LANES][:, grp])

        kts, vts, biases = [], [], []
        for k in range(SEL_TOPK):
            blk_ref = blk_refs[g * SEL_TOPK + k]
            b = idx_ref[r_idx, g * SEL_TOPK + k]
            valid = (b >= 0) & (b < n_past_blk)
            page = jnp.clip(lax.shift_right_arithmetic(b, 1), 0, n_pages_past - 1)
            in_blk = lax.shift_right_arithmetic(lane, SEL_SHIFT) == jnp.bitwise_and(b, 1)
            kts.append(blk_ref[0, 0, 0, g].astype(BF16))
            vts.append(blk_ref[0, 0, 1, g].astype(BF16))
            biases.append(jnp.where(valid & in_blk, fsel_ref[page, rows, :], NEG))
        s = jnp.dot(qb, jnp.concatenate(kts, axis=1), preferred_element_type=F32) + jnp.concatenate(biases, axis=1)
        s_new = jnp.sum(qg * ksel_new, axis=1, keepdims=True) + fsel_ref[n_pages_past, rows, new_lane:new_lane + 1]
        m = jnp.maximum(jnp.max(s, axis=1, keepdims=True), s_new)
        p = jnp.exp(s - m)
        p_new = jnp.exp(s_new - m)
        l = jnp.sum(p, axis=1, keepdims=True) + p_new
        o_sel = (_nt_dot_bf16(p, jnp.concatenate(vts, axis=1)) + _bf16_round(p_new) * vsel_new) / l

        kw_t = wbuf_ref[0, 0, 0, g].astype(BF16)
        vw_t = wbuf_ref[0, 0, 1, g].astype(BF16)
        s = jnp.dot(qb, kw_t, preferred_element_type=F32) + fwin_ref[rows, 0:n_win]
        s_wn = jnp.sum(qg * kwin_new, axis=1, keepdims=True) + fwin_ref[rows, n_win:n_win + 1]
        m = jnp.maximum(jnp.max(s, axis=1, keepdims=True), s_wn)
        p = jnp.exp(s - m)
        p_new = jnp.exp(s_wn - m)
        l = jnp.sum(p, axis=1, keepdims=True) + p_new
        o_win = (_nt_dot_bf16(p, vw_t) + _bf16_round(p_new) * vwin_new) / l

        gate = []
        for b in range(3):
            col = jnp.zeros((8, 1), F32)
            for r in range(NSA_GROUP):
                lane_i = 3 * (g * NSA_GROUP + r) + b
                col = jnp.where(row8[:, 0:1] == r, gates[:, lane_i:lane_i + 1], col)
            gate.append(col)
        o_cmp = oc_ref[0, g][:, g * HEAD_DIM:(g + 1) * HEAD_DIM]
        o_ref[0, g] = gate[0] * o_cmp + gate[1] * o_sel + gate[2] * o_win


def _nsa_attend(idx, page_table, q, cache_t, layer, fsel, win_t, fwin, new_sel, new_win, o_cmp, small, n_past_blk,
                new_lane):
    n_req = q.shape[0]
    n_win = win_t.shape[-1]

    def blk_map(pick):
        def index(r, idx_ref, pt_ref):
            b = idx_ref[r, pick]
            b = jnp.where((b >= 0) & (b < n_past_blk), b, 0)
            return layer, pt_ref[r, lax.shift_right_arithmetic(b, 1)], 1, 0, 0, 0
        return index

    n_picks = NSA_KV_HEADS * SEL_TOPK
    const = lambda *shape: (lambda r, idx_ref, pt_ref: shape)
    per_req3 = lambda r, idx_ref, pt_ref: (r, 0, 0)
    per_req4 = lambda r, idx_ref, pt_ref: (r, 0, 0, 0)
    page_blk = (1, 1, 2, NSA_KV_HEADS, HEAD_DIM, PAGE_SIZE)
    grid_spec = pltpu.PrefetchScalarGridSpec(
        num_scalar_prefetch=2,
        grid=(n_req,),
        in_specs=([pl.BlockSpec((1, 1, NSA_HEADS * HEAD_DIM), per_req3)]
                  + [pl.BlockSpec(page_blk, blk_map(pick)) for pick in range(n_picks)]
                  + [pl.BlockSpec(fsel.shape, const(0, 0, 0)),
                     pl.BlockSpec((1, 1, 2, NSA_KV_HEADS, HEAD_DIM, n_win),
                                  lambda r, idx_ref, pt_ref: (layer, r, 0, 0, 0, 0)),
                     pl.BlockSpec(fwin.shape, const(0, 0)),
                     pl.BlockSpec((1, 1, 4 * LANES), per_req3),
                     pl.BlockSpec((1, 1, 2 * LANES), per_req3),
                     pl.BlockSpec((1, NSA_KV_HEADS, 8, LANES), per_req4),
                     pl.BlockSpec((1, 1, LANES), per_req3)]),
        out_specs=pl.BlockSpec((1, NSA_KV_HEADS, 8, HEAD_DIM), per_req4),
    )
    return pl.pallas_call(
        functools.partial(_nsa_attend_kernel, n_past_blk=n_past_blk, new_lane=new_lane, n_win=n_win),
        grid_spec=grid_spec,
        out_shape=jax.ShapeDtypeStruct((n_req, NSA_KV_HEADS, 8, HEAD_DIM), F32),
        compiler_params=_cparams("arbitrary"),
        name="nsa_sample_attend",
    )(idx, page_table, q, *([cache_t] * n_picks), fsel, win_t, fwin, new_sel, new_win, o_cmp, small)


def kernel(x_prompt, x_sample, cache_nsa_kv, cache_fox_kv, cache_fox_logf, state_nsa_win_kv, state_dil_kv, page_table,
           c_prompt, c_sample, rel_bias, norm_g, w_ada, b_ada, w_in_a, nsa_gate_b, fox_f_b, nsa_cmp_w1, nsa_cmp_w2,
           nsa_cmp_pe, w_out_a, w_in_c, w_out_c, w_mlp1, w_mlp2):
    bp, s, d = x_prompt.shape
    bd = x_sample.shape[0]
    depth = w_ada.shape[0]
    n_pages = s // PAGE_SIZE
    n_cmp_pad = s // CMP_STRIDE

    mods = _ada_params(jnp.concatenate([c_prompt, c_sample], axis=0), w_ada, b_ada).reshape(depth, bp + bd, 6, d)
    tab_sel = _toeplitz_table(rel_bias, NSA_HEADS, 15, "causal")
    tab_win = _toeplitz_table(rel_bias, NSA_HEADS, NSA_WINDOW // TILE + 2, "window")
    assert all(window // dil == TILE for window, dil in DIL_BRANCHES)
    tab_dil = [_toeplitz_table(rel_bias, DIL_HEADS, 3, "branch", dil) for _, dil in DIL_BRANCHES]
    bcmp = _cmp_bias_table(rel_bias, s, TILE, n_cmp_pad, 0)

    past_len = page_table.shape[1] * PAGE_SIZE
    assert past_len % SEL_BLOCK == 0
    n_past_blk = past_len // SEL_BLOCK
    n_blk_pad = -(-(n_past_blk + 1) // LANES) * LANES
    far = 1 << 30
    bcmp_s = _affine_bias(rel_bias, NSA_GROUPED_ROWS, past_len // CMP_STRIDE, past_len - (CMP_LEN - 1), -CMP_STRIDE, far)
    assert past_len % PAGE_SIZE == 0
    n_pages_s = past_len // PAGE_SIZE
    fsel = _affine_bias(rel_bias, NSA_GROUPED_ROWS, (n_pages_s + 1) * PAGE_SIZE, past_len, -1, far)
    fsel = jnp.swapaxes(fsel.reshape(len(NSA_GROUPED_ROWS), n_pages_s + 1, PAGE_SIZE), 0, 1)
    n_win_buf = state_nsa_win_kv.shape[2]
    fwin = _affine_bias(rel_bias, NSA_GROUPED_ROWS, n_win_buf + LANES, n_win_buf, -1, NSA_WINDOW)
    n_dil_buf = state_dil_kv.shape[2]
    tab_dil_s = _affine_bias(rel_bias, range(DIL_HEADS), n_dil_buf + LANES, n_dil_buf, -1, far, union=True)
    tab_dil_s = tab_dil_s.reshape(2, DIL_HEADS // 2, n_dil_buf + LANES)
    cache_nsa_t = _feature_major(cache_nsa_kv)
    cache_fox_t = _feature_major(cache_fox_kv)
    logf_t = jnp.swapaxes(cache_fox_logf, 2, 3)
    win_t = _feature_major(state_nsa_win_kv)
    dil_t = _feature_major(state_dil_kv)

    xp = x_prompt
    xs = x_sample.reshape(1, bd, d)
    per_req = lambda a: a.reshape(bd, 1, a.shape[-1])
    prompt_pages = jnp.arange(bp * n_pages, dtype=jnp.int32).reshape(bp, n_pages)
    nsa_p, nsa_s, fkv_p, fkv_s, lf_p, lf_s, win_p, win_s, dil_p, dil_s = [], [], [], [], [], [], [], [], [], []
    for layer in range(depth):
        mp = [mods[layer, :bp, k].reshape(bp, 1, d) for k in range(6)]
        ms = [mods[layer, bp:, k].reshape(1, bd, d) for k in range(6)]
        g = [norm_g[layer, k].reshape(1, d) for k in range(4)]
        i = layer // 2
        if layer % 2 == 0:
            wa = w_in_a[i]
            w_in = jnp.concatenate([wa[:, 0:1280], wa[:, 1304:2840], wa[:, 1280:1304], wa[:, 2840:2848],
                                    jnp.zeros((d, LANES - SMALL_GATES - FOX_HEADS), F32)], axis=1).astype(BF16)
            sb = jnp.concatenate([nsa_gate_b[i].reshape(-1), fox_f_b[i],
                                  jnp.zeros((LANES - SMALL_GATES - FOX_HEADS,), F32)]).reshape(1, LANES)
            half = CMP_STRIDE * HEAD_DIM
            w1cat = jnp.concatenate([nsa_cmp_w1[i][:, :half], nsa_cmp_w1[i][:, half:]], axis=2).astype(BF16)
            w2 = nsa_cmp_w2[i].astype(BF16)
            pe = nsa_cmp_pe[i].reshape(2, 1, CMP_LEN * HEAD_DIM)
            w_out = w_out_a[i].astype(BF16)

            qn, nsa4, nsabf, win, qf, fkv, fkvbf, small = _proj_even(xp, mp[0], mp[1], g[0], w_in, sb)
            c, ct = _cumsum(small)
            o_f = _fox_prompt(qf, fkvbf, c, ct)
            kc, vc = _compress(nsa4.reshape(bp * n_pages, PAGE_SIZE, 512), prompt_pages, w1cat, w2, pe)
            o_n = _nsa_prompt(qn, kc, vc, bcmp, nsabf, tab_sel, tab_win, small)
            op_a, op_b = o_n, o_f
            nsa_p.append(nsa4.reshape(bp, s, 4, NSA_KV_HEADS, HEAD_DIM))
            fkv_p.append(fkv.reshape(bp, s, 2, FOX_HEADS, HEAD_DIM))
            lf_p.append(small[:, :, SMALL_GATES:SMALL_GATES + FOX_HEADS])
            n_win = min(NSA_WINDOW, s)
            win_p.append(win[:, s - n_win:].reshape(bp, n_win, 2, NSA_KV_HEADS, HEAD_DIM))

            qn_s, nsa4_s, _, win_new, qf_s, fkv_s_, _, small_s = _proj_even(xs, ms[0], ms[1], g[0], w_in, sb)
            kc_s, vc_s = _compress(cache_nsa_t, page_table, w1cat, w2, pe, layer=i)
            o_cmp, idx = _nsa_select(per_req(qn_s), kc_s, vc_s, bcmp_s, n_blk_pad, n_past_blk)
            o_nsa = _nsa_attend(idx, page_table, per_req(qn_s), cache_nsa_t, i, fsel, win_t, fwin, per_req(nsa4_s),
                                per_req(win_new), o_cmp, per_req(small_s), n_past_blk, 0)
            os_a = o_nsa[:, :, :NSA_GROUP].reshape(1, bd, NSA_HEADS * HEAD_DIM).astype(BF16)
            lf_new = small_s[0, :, SMALL_GATES:SMALL_GATES + FOX_HEADS]
            o_fox = _fox_sample(cache_fox_t, i, logf_t, page_table, _block_diag_queries(qf_s[0], FOX_HEADS),
                                fkv_s_.reshape(bd, 2, FOX_HEADS * HEAD_DIM),
                                jnp.broadcast_to(lf_new[:, :, None], (bd, FOX_HEADS, LANES)))
            os_b = _diag_blocks(o_fox, FOX_HEADS).reshape(1, bd, -1).astype(BF16)
            nsa_s.append(nsa4_s.reshape(bd, 1, 4, NSA_KV_HEADS, HEAD_DIM))
            fkv_s.append(fkv_s_.reshape(bd, 1, 2, FOX_HEADS, HEAD_DIM))
            lf_s.append(small_s[0, :, SMALL_GATES:SMALL_GATES + FOX_HEADS].reshape(bd, 1, FOX_HEADS))
            win_s.append(win_new.reshape(bd, 1, 2, NSA_KV_HEADS, HEAD_DIM))
        else:
            w_in = w_in_c[i].astype(BF16)
            w_out = w_out_c[i].astype(BF16)
            q, kv, kvbf = _proj_odd(xp, mp[0], mp[1], g[0], w_in)
            branches = [_dil_branch_prompt(q, kvbf, tab, dil) for tab, (_, dil) in zip(tab_dil, DIL_BRANCHES)]
            n_dil = min(DIL_BRANCHES[-1][0], s)
            dil_p.append(kv[:, s - n_dil:].reshape(bp, n_dil, 2, DIL_HEADS, HEAD_DIM))
            q_s, kv_s, _ = _proj_odd(xs, ms[0], ms[1], g[0], w_in)
            half = DIL_HEADS // 2
            qbd = _block_diag_queries(q_s.reshape(bd * 2, half * HEAD_DIM), half).reshape(bd, 2, half, half * HEAD_DIM)
            new_kv = jnp.swapaxes(kv_s.reshape(bd, 2, 2, half * HEAD_DIM), 1, 2)
            o_dil = _dil_sample(dil_t, i, qbd, new_kv, tab_dil_s)
            os_a = os_b = _diag_blocks(o_dil.reshape(bd * 2, half, half * HEAD_DIM), half).reshape(1, bd, -1).astype(BF16)
            dil_s.append(kv_s.reshape(bd, 1, 2, DIL_HEADS, HEAD_DIM))
        if layer % 2 == 0:
            xp = _post(op_a, op_b, w_out, xp, mp[2], g[1])
            xs = _post(os_a, os_b, w_out, xs, ms[2], g[1])
        else:
            xp = _post_dil([o for o, _ in branches], [l for _, l in branches], w_out, xp, mp[2], g[1])
            xs = _post(os_a, os_b, w_out, xs, ms[2], g[1], 0, 1)
        w1 = w_mlp1[layer].astype(BF16)
        w2m = w_mlp2[layer].astype(BF16)
        xp = _mlp(xp, mp[3], mp[4], mp[5], g[2], g[3], w1, w2m)
        xs = _mlp(xs, ms[3], ms[4], ms[5], g[2], g[3], w1, w2m)
    return (xp, xs.reshape(bd, 1, d), jnp.stack(nsa_p), jnp.stack(nsa_s), jnp.stack(fkv_p), jnp.stack(fkv_s),
            jnp.stack(lf_p), jnp.stack(lf_s), jnp.stack(win_p), jnp.stack(win_s), jnp.stack(dil_p), jnp.stack(dil_s))
```

```python
import functools
import math

import numpy as np
import jax
import jax.numpy as jnp
from jax import lax
from jax.experimental import pallas as pl
from jax.experimental.pallas import tpu as pltpu

F32 = jnp.float32
BF16 = jnp.bfloat16

HEAD_DIM = 64
NSA_HEADS = 8
NSA_KV_HEADS = 2
NSA_GROUP = NSA_HEADS // NSA_KV_HEADS
FOX_HEADS = 8
DIL_HEADS = 16
CMP_LEN = 32
CMP_STRIDE = 16
CMP_HIDDEN = 4 * HEAD_DIM
SEL_BLOCK = 64
SEL_TOPK = 16
NSA_WINDOW = 512
FORCED_SCORE = 1e9
DIL_BRANCHES = ((128, 1), (512, 4), (2048, 16))
N_BUCKETS = 32
BUCKET_EXACT = 16
BUCKET_MAX_DIST = 2048
NORM_EPS = 1e-6
TINY = 1e-30
PAGE_SIZE = 128

LANES = 128
VMEM_LIMIT_BYTES = 56 * 1024 * 1024

NEG = -1e30
QK_SCALE = HEAD_DIM ** -0.5
TILE = 128
SEL_SHIFT = 6
W_NSA_Q = NSA_HEADS * HEAD_DIM
W_NSA_KV = NSA_KV_HEADS * HEAD_DIM
W_FOX_Q = FOX_HEADS * HEAD_DIM
W_DIL = DIL_HEADS * HEAD_DIM
SMALL_GATES = 3 * NSA_HEADS


def _cparams(*sem):
    return pltpu.CompilerParams(dimension_semantics=sem, vmem_limit_bytes=VMEM_LIMIT_BYTES)


def _bucket_thresholds():
    d = np.arange(0, 2 * BUCKET_MAX_DIST + 1)
    df = np.maximum(d, 1).astype(np.float64)
    ratio = math.log(BUCKET_MAX_DIST / BUCKET_EXACT)
    log_b = BUCKET_EXACT + (np.log(df / BUCKET_EXACT) / ratio * (N_BUCKETS - BUCKET_EXACT)).astype(np.int64)
    bucket = np.where(d < BUCKET_EXACT, d, np.clip(log_b, BUCKET_EXACT, N_BUCKETS - 1))
    return [int(np.argmax(bucket >= b)) for b in range(1, N_BUCKETS)]


BUCKET_THR = _bucket_thresholds()


def _bias_of_distance(d, tab_ref, h):
    val = jnp.full(d.shape, tab_ref[0, h], F32)
    for b in range(1, N_BUCKETS):
        val = jnp.where(d >= BUCKET_THR[b - 1], tab_ref[b, h], val)
    return val


def _toeplitz_kernel(tab_ref, o_ref, *, n_heads, mode, dil):
    r = lax.broadcasted_iota(jnp.int32, (TILE, TILE), 0)
    c = lax.broadcasted_iota(jnp.int32, (TILE, TILE), 1)
    idx = pl.program_id(0)
    if mode == "branch":
        d = idx * TILE + r - c
        ok = (idx < 2) & (d >= 0) & (d <= TILE)
    else:
        d = (idx - 1) * TILE + r - c
        ok = (idx > 0) & (d >= 0)
        if mode == "window":
            ok = ok & (d < NSA_WINDOW)
    dd = jnp.maximum(d, 0) * dil
    for h in range(n_heads):
        o_ref[h, 0] = jnp.where(ok, _bias_of_distance(dd, tab_ref, h), NEG)


def _toeplitz_table(rel_bias, n_heads, n_idx, mode, dil=1):
    return pl.pallas_call(
        functools.partial(_toeplitz_kernel, n_heads=n_heads, mode=mode, dil=dil),
        grid=(n_idx,),
        in_specs=[pl.BlockSpec(memory_space=pltpu.SMEM)],
        out_specs=pl.BlockSpec((n_heads, 1, TILE, TILE), lambda i: (0, i, 0, 0)),
        out_shape=jax.ShapeDtypeStruct((n_heads, n_idx, TILE, TILE), F32),
        compiler_params=_cparams("arbitrary"),
        name="bias_toeplitz_%s%d" % (mode, dil),
    )(rel_bias)


def _cmp_bias_kernel(tab_ref, o_ref, *, tq, n_cmp_pad, t_base):
    t_lo = t_base + pl.program_id(0) * tq
    last = N_BUCKETS - 1
    for c in range(n_cmp_pad // LANES):
        cols = slice(c * LANES, (c + 1) * LANES)
        end_lo = c * LANES * CMP_STRIDE + CMP_LEN - 1
        end_hi = end_lo + (LANES - 1) * CMP_STRIDE
        all_future = t_lo + tq - 1 < end_lo
        all_far = t_lo - end_hi >= BUCKET_THR[last - 1]

        @pl.when(all_future)
        def _():
            for h in range(NSA_HEADS):
                o_ref[h, :, cols] = jnp.full((tq, LANES), NEG, F32)

        @pl.when(all_far)
        def _():
            for h in range(NSA_HEADS):
                o_ref[h, :, cols] = jnp.full((tq, LANES), tab_ref[last, h], F32)

        @pl.when(jnp.logical_not(all_future | all_far))
        def _():
            t = t_lo + lax.broadcasted_iota(jnp.int32, (tq, LANES), 0)
            n = c * LANES + lax.broadcasted_iota(jnp.int32, (tq, LANES), 1)
            d = t - (n * CMP_STRIDE + CMP_LEN - 1)
            dd = jnp.maximum(d, 0)
            for h in range(NSA_HEADS):
                o_ref[h, :, cols] = jnp.where(d >= 0, _bias_of_distance(dd, tab_ref, h), NEG)


def _cmp_bias_table(rel_bias, n_rows, tq, n_cmp_pad, t_base):
    assert n_cmp_pad % LANES == 0
    return pl.pallas_call(
        functools.partial(_cmp_bias_kernel, tq=tq, n_cmp_pad=n_cmp_pad, t_base=t_base),
        grid=(n_rows // tq,),
        in_specs=[pl.BlockSpec(memory_space=pltpu.SMEM)],
        out_specs=pl.BlockSpec((NSA_HEADS, tq, n_cmp_pad), lambda i: (0, i, 0)),
        out_shape=jax.ShapeDtypeStruct((NSA_HEADS, n_rows, n_cmp_pad), F32),
        compiler_params=_cparams("arbitrary"),
        name="bias_cmp",
    )(rel_bias)


def _affine_bias_kernel(tab_ref, o_ref, *, heads, d0, step, limit, union):
    n = o_ref.shape[1]
    d = d0 + step * lax.broadcasted_iota(jnp.int32, (1, n), 1)
    ok = (d >= 0) & (d < limit)
    extra = jnp.zeros((1, n), F32)
    if union:
        cnt = jnp.zeros((1, n), F32)
        for window, dil in DIL_BRANCHES:
            cnt = cnt + jnp.where((d >= 0) & (d <= window) & (jnp.bitwise_and(d, dil - 1) == 0), 1.0, 0.0)
        ok = ok & (cnt > 0.5)
        extra = jnp.log(jnp.maximum(cnt, 1.0))
    dd = jnp.maximum(d, 0)
    for row, h in enumerate(heads):
        if h is None:
            o_ref[row:row + 1, :] = jnp.zeros((1, n), F32)
        else:
            o_ref[row:row + 1, :] = jnp.where(ok, _bias_of_distance(dd, tab_ref, h) + extra, NEG)


def _affine_bias(rel_bias, heads, n, d0, step, limit, union=False):
    return pl.pallas_call(
        functools.partial(_affine_bias_kernel, heads=tuple(heads), d0=d0, step=step, limit=limit, union=union),
        in_specs=[pl.BlockSpec(memory_space=pltpu.SMEM)],
        out_shape=jax.ShapeDtypeStruct((len(heads), n), F32),
        compiler_params=pltpu.CompilerParams(vmem_limit_bytes=VMEM_LIMIT_BYTES),
        name="bias_affine",
    )(rel_bias)


NSA_GROUPED_ROWS = tuple((NSA_GROUP * (row // 8) + row % 8) if row % 8 < NSA_GROUP else None
                         for row in range(8 * NSA_KV_HEADS))


def _ada_kernel(c_ref, w_ref, b_ref, o_ref):
    c = c_ref[...]
    s = (c * jax.nn.sigmoid(c)).astype(BF16)
    o_ref[0] = jnp.dot(s, w_ref[0].astype(BF16), preferred_element_type=F32) + b_ref[0]


def _ada_params(c_all, w_ada, b_ada):
    depth, d, d6 = w_ada.shape
    m = c_all.shape[0]
    return pl.pallas_call(
        _ada_kernel,
        grid=(depth, d6 // d),
        in_specs=[pl.BlockSpec((m, d), lambda l, j: (0, 0)),
                  pl.BlockSpec((1, d, d), lambda l, j: (l, 0, j)),
                  pl.BlockSpec((1, 1, d), lambda l, j: (l, 0, j))],
        out_specs=pl.BlockSpec((1, m, d), lambda l, j: (l, 0, j)),
        out_shape=jax.ShapeDtypeStruct((depth, m, d6), F32),
        compiler_params=_cparams("arbitrary", "arbitrary"),
        name="adaln",
    )(c_all, w_ada, b_ada.reshape(depth, 1, d6))


def _norm_mod(x, g, scale, shift):
    y = x * lax.rsqrt(jnp.mean(x * x, axis=-1, keepdims=True) + NORM_EPS)
    return (y * g) * (1.0 + scale) + shift


def _row_tile(s, want):
    return want if s % want == 0 else s


def _mod_spec(mod, tm):
    if mod.shape[1] == 1:
        return pl.BlockSpec((1, 1, mod.shape[2]), lambda b, i: (b, 0, 0))
    return pl.BlockSpec((1, tm, mod.shape[2]), lambda b, i: (b, i, 0))


def _proj_even_kernel(x_ref, sh_ref, sc_ref, g_ref, w_ref, sb_ref,
                      qn_ref, nsa4_ref, nsabf_ref, win_ref, qf_ref, fkv_ref, fkvbf_ref, small_ref):
    h = _norm_mod(x_ref[0], g_ref[...], sc_ref[0], sh_ref[0]).astype(BF16)
    z = jnp.dot(h, w_ref[...], preferred_element_type=F32)
    c_kv = W_NSA_Q
    c_win = c_kv + 4 * W_NSA_KV
    c_qf = c_kv + 6 * W_NSA_KV
    c_fkv = c_qf + W_FOX_Q
    c_small = c_fkv + 2 * W_FOX_Q
    qn_ref[0] = (z[:, 0:c_kv] * QK_SCALE).astype(BF16)
    nsa4_ref[0] = z[:, c_kv:c_win]
    nsabf_ref[0] = z[:, c_kv:c_qf].astype(BF16)
    win_ref[0] = z[:, c_win:c_qf]
    qf_ref[0] = (z[:, c_qf:c_fkv] * QK_SCALE).astype(BF16)
    fkv_ref[0] = z[:, c_fkv:c_small]
    fkvbf_ref[0] = z[:, c_fkv:c_small].astype(BF16)
    zs = z[:, c_small:c_small + LANES] + sb_ref[...]
    lane = lax.broadcasted_iota(jnp.int32, zs.shape, 1)
    sig = jax.nn.sigmoid(zs)
    lsg = jnp.minimum(zs, 0.0) - jnp.log1p(jnp.exp(-jnp.abs(zs)))
    small_ref[0] = jnp.where(lane < SMALL_GATES, sig, lsg)


def _proj_even(x, shift, scale, g, w, sb):
    bx, s, d = x.shape
    tm = _row_tile(s, 256)
    n = w.shape[1]
    widths = (W_NSA_Q, 4 * W_NSA_KV, 6 * W_NSA_KV, 2 * W_NSA_KV, W_FOX_Q, 2 * W_FOX_Q, 2 * W_FOX_Q, LANES)
    dtypes = (BF16, F32, BF16, F32, BF16, F32, BF16, F32)
    return pl.pallas_call(
        _proj_even_kernel,
        grid=(bx, s // tm),
        in_specs=[pl.BlockSpec((1, tm, d), lambda b, i: (b, i, 0)),
                  _mod_spec(shift, tm), _mod_spec(scale, tm),
                  pl.BlockSpec((1, d), lambda b, i: (0, 0)),
                  pl.BlockSpec((d, n), lambda b, i: (0, 0)),
                  pl.BlockSpec((1, LANES), lambda b, i: (0, 0))],
        out_specs=[pl.BlockSpec((1, tm, wd), lambda b, i: (b, i, 0)) for wd in widths],
        out_shape=[jax.ShapeDtypeStruct((bx, s, wd), dt) for wd, dt in zip(widths, dtypes)],
        compiler_params=_cparams("arbitrary", "arbitrary"),
        name="proj_even",
    )(x, shift, scale, g, w, sb)


def _proj_odd_kernel(x_ref, sh_ref, sc_ref, g_ref, w_ref, q_ref, kv_ref, kvbf_ref):
    h = _norm_mod(x_ref[0], g_ref[...], sc_ref[0], sh_ref[0]).astype(BF16)
    z = jnp.dot(h, w_ref[...], preferred_element_type=F32)
    q_ref[0] = (z[:, 0:W_DIL] * QK_SCALE).astype(BF16)
    kv_ref[0] = z[:, W_DIL:3 * W_DIL]
    kvbf_ref[0] = z[:, W_DIL:3 * W_DIL].astype(BF16)


def _proj_odd(x, shift, scale, g, w):
    bx, s, d = x.shape
    tm = _row_tile(s, 256)
    n = w.shape[1]
    widths = (W_DIL, 2 * W_DIL, 2 * W_DIL)
    dtypes = (BF16, F32, BF16)
    return pl.pallas_call(
        _proj_odd_kernel,
        grid=(bx, s // tm),
        in_specs=[pl.BlockSpec((1, tm, d), lambda b, i: (b, i, 0)),
                  _mod_spec(shift, tm), _mod_spec(scale, tm),
                  pl.BlockSpec((1, d), lambda b, i: (0, 0)),
                  pl.BlockSpec((d, n), lambda b, i: (0, 0))],
        out_specs=[pl.BlockSpec((1, tm, wd), lambda b, i: (b, i, 0)) for wd in widths],
        out_shape=[jax.ShapeDtypeStruct((bx, s, wd), dt) for wd, dt in zip(widths, dtypes)],
        compiler_params=_cparams("arbitrary", "arbitrary"),
        name="proj_odd",
    )(x, shift, scale, g, w)


def _post_kernel(oa_ref, ob_ref, w_ref, x_ref, gate_ref, g_ref, o_ref):
    half = oa_ref.shape[2]
    y = jnp.dot(oa_ref[0], w_ref[0:half, :], preferred_element_type=F32)
    y = y + jnp.dot(ob_ref[0], w_ref[half:, :], preferred_element_type=F32)
    yn = y * lax.rsqrt(jnp.mean(y * y, axis=-1, keepdims=True) + NORM_EPS) * g_ref[...]
    o_ref[0] = x_ref[0] + gate_ref[0] * yn


def _post(o_a, o_b, w_out, x, gate, g, col_a=0, col_b=0):
    bx, s, d = x.shape
    tm = _row_tile(s, 512)
    half = w_out.shape[0] // 2
    return pl.pallas_call(
        _post_kernel,
        grid=(bx, s // tm),
        in_specs=[pl.BlockSpec((1, tm, half), lambda b, i: (b, i, col_a)),
                  pl.BlockSpec((1, tm, half), lambda b, i: (b, i, col_b)),
                  pl.BlockSpec(w_out.shape, lambda b, i: (0, 0)),
                  pl.BlockSpec((1, tm, d), lambda b, i: (b, i, 0)),
                  _mod_spec(gate, tm),
                  pl.BlockSpec((1, d), lambda b, i: (0, 0))],
        out_specs=pl.BlockSpec((1, tm, d), lambda b, i: (b, i, 0)),
        out_shape=jax.ShapeDtypeStruct((bx, s, d), F32),
        compiler_params=_cparams("arbitrary", "arbitrary"),
        name="post",
    )(o_a, o_b, w_out, x, gate, g)


def _mlp_kernel(x_ref, sh_ref, sc_ref, gate_ref, g2_ref, g3_ref, w1_ref, w2_ref, o_ref, h_ref, acc_ref):
    j = pl.program_id(2)

    @pl.when(j == 0)
    def _():
        h_ref[...] = _norm_mod(x_ref[0], g2_ref[...], sc_ref[0], sh_ref[0]).astype(BF16)
        acc_ref[...] = jnp.zeros_like(acc_ref)

    a = jnp.maximum(jnp.dot(h_ref[...], w1_ref[...], preferred_element_type=F32), 0.0)
    acc_ref[...] += jnp.dot((a * a).astype(BF16), w2_ref[...], preferred_element_type=F32)

    @pl.when(j == pl.num_programs(2) - 1)
    def _():
        y = acc_ref[...]
        yn = y * lax.rsqrt(jnp.mean(y * y, axis=-1, keepdims=True) + NORM_EPS) * g3_ref[...]
        o_ref[0] = x_ref[0] + gate_ref[0] * yn


def _mlp(x, shift, scale, gate, g2, g3, w1, w2):
    bx, s, d = x.shape
    f = w1.shape[1]
    tm = _row_tile(s, 1024)
    tf = 1024

    def mod3(mod):
        if mod.shape[1] == 1:
            return pl.BlockSpec((1, 1, d), lambda b, i, j: (b, 0, 0))
        return pl.BlockSpec((1, tm, d), lambda b, i, j: (b, i, 0))

    return pl.pallas_call(
        _mlp_kernel,
        grid=(bx, s // tm, f // tf),
        in_specs=[pl.BlockSpec((1, tm, d), lambda b, i, j: (b, i, 0)),
                  mod3(shift), mod3(scale), mod3(gate),
                  pl.BlockSpec((1, d), lambda b, i, j: (0, 0)),
                  pl.BlockSpec((1, d), lambda b, i, j: (0, 0)),
                  pl.BlockSpec((d, tf), lambda b, i, j: (0, j)),
                  pl.BlockSpec((tf, d), lambda b, i, j: (j, 0))],
        out_specs=pl.BlockSpec((1, tm, d), lambda b, i, j: (b, i, 0)),
        out_shape=jax.ShapeDtypeStruct((bx, s, d), F32),
        scratch_shapes=[pltpu.VMEM((tm, d), BF16), pltpu.VMEM((tm, d), F32)],
        compiler_params=_cparams("arbitrary", "arbitrary", "arbitrary"),
        name="mlp",
    )(x, shift, scale, gate, g2, g3, w1, w2)


def _cumsum_kernel(x_ref, c_ref, ct_ref, carry_ref, *, tc):
    @pl.when(pl.program_id(1) == 0)
    def _():
        carry_ref[...] = jnp.zeros_like(carry_ref)

    r = lax.broadcasted_iota(jnp.int32, (tc, tc), 0)
    c = lax.broadcasted_iota(jnp.int32, (tc, tc), 1)
    tri = jnp.where(c <= r, 1.0, 0.0).astype(F32)
    cs = jnp.dot(tri, x_ref[0], preferred_element_type=F32, precision=lax.Precision.HIGHEST) + carry_ref[...]
    carry_ref[...] = cs[tc - 1:tc, :]
    c_ref[0] = cs
    ct_ref[0] = cs.T[SMALL_GATES:SMALL_GATES + FOX_HEADS, :]


def _cumsum(small):
    bx, s, _ = small.shape
    tc = _row_tile(s, 256)
    return pl.pallas_call(
        functools.partial(_cumsum_kernel, tc=tc),
        grid=(bx, s // tc),
        in_specs=[pl.BlockSpec((1, tc, LANES), lambda b, i: (b, i, 0))],
        out_specs=[pl.BlockSpec((1, tc, LANES), lambda b, i: (b, i, 0)),
                   pl.BlockSpec((1, FOX_HEADS, tc), lambda b, i: (b, 0, i))],
        out_shape=[jax.ShapeDtypeStruct((bx, s, LANES), F32),
                   jax.ShapeDtypeStruct((bx, FOX_HEADS, s), F32)],
        scratch_shapes=[pltpu.VMEM((1, LANES), F32)],
        compiler_params=_cparams("arbitrary", "arbitrary"),
        name="logf_cumsum",
    )(small)


def _nt_dot(a, b):
    return lax.dot_general(a, b, (((1,), (1,)), ((), ())), preferred_element_type=F32)


def _online_update(state, s, v):
    m, l, acc = state
    m_new = jnp.maximum(m, jnp.max(s, axis=1, keepdims=True))
    alpha = jnp.exp(m - m_new)
    p = jnp.exp(s - m_new)
    l = alpha * l + jnp.sum(p, axis=1, keepdims=True)
    acc = alpha * acc + jnp.dot(p.astype(BF16), v, preferred_element_type=F32)
    return m_new, l, acc


def _init_state(tq):
    return (jnp.full((tq, 1), NEG, F32), jnp.zeros((tq, 1), F32), jnp.zeros((tq, LANES), F32))


def _half_masks(tq):
    lane = lax.broadcasted_iota(jnp.int32, (tq, LANES), 1)
    return lane < HEAD_DIM


def _split_heads(q2, lo):
    zero = jnp.zeros_like(q2)
    return jnp.where(lo, q2, zero), jnp.where(lo, zero, q2)


def _fox_kernel(q_ref, k_ref, v_ref, c_ref, ct_ref, o_ref, *, tq, tk, n_pairs):
    grp = pl.program_id(1)
    i = pl.program_id(2)
    lo = _half_masks(tq)
    lane = lax.broadcasted_iota(jnp.int32, (tq, LANES), 1)
    cblk = c_ref[0]
    qs, cqs, heads = [], [], []
    for p in range(n_pairs):
        qs.extend(_split_heads(q_ref[0, :, p * LANES:(p + 1) * LANES], lo))
        for e in range(2):
            head = (grp * n_pairs + p) * 2 + e
            heads.append(head)
            cqs.append(jnp.sum(jnp.where(lane == SMALL_GATES + head, cblk, 0.0), axis=1, keepdims=True))

    def chunk(c, states, masked):
        off = pl.multiple_of(c * tk, tk)
        if masked:
            row = i * tq + lax.broadcasted_iota(jnp.int32, (tq, tk), 0)
            col = off + lax.broadcasted_iota(jnp.int32, (tq, tk), 1)
            ok = col <= row
        out = []
        for n in range(2 * n_pairs):
            p = n // 2
            k = k_ref[0, pl.ds(off, tk), p * LANES:(p + 1) * LANES]
            v = v_ref[0, pl.ds(off, tk), p * LANES:(p + 1) * LANES]
            ck = ct_ref[0, pl.ds(heads[n], 1), pl.ds(off, tk)]
            s = _nt_dot(qs[n], k) + cqs[n] - ck
            if masked:
                s = jnp.where(ok, s, NEG)
            out.append(_online_update(states[n], s, v))
        return tuple(out)

    n_full = (i * tq) // tk
    init = tuple(_init_state(tq) for _ in range(2 * n_pairs))
    states = lax.fori_loop(0, n_full, lambda c, st: chunk(c, st, False), init)
    states = chunk(n_full, states, True)
    outs = []
    for p in range(n_pairs):
        (_, la, acca), (_, lb, accb) = states[2 * p], states[2 * p + 1]
        outs.append(jnp.where(lo, acca / la, accb / lb))
    o_ref[0] = jnp.concatenate(outs, axis=1).astype(o_ref.dtype)


def _fox_prompt(qf, fkvbf, c, ct):
    bx, s, _ = qf.shape
    tq = _row_tile(s, 1024)
    tk = _row_tile(s, 1024)
    n_pairs = 1
    n_grp = FOX_HEADS // 2 // n_pairs
    wd = n_pairs * LANES
    return pl.pallas_call(
        functools.partial(_fox_kernel, tq=tq, tk=tk, n_pairs=n_pairs),
        grid=(bx, n_grp, s // tq),
        in_specs=[pl.BlockSpec((1, tq, wd), lambda b, p, i: (b, i, p)),
                  pl.BlockSpec((1, s, wd), lambda b, p, i: (b, 0, p)),
                  pl.BlockSpec((1, s, wd), lambda b, p, i: (b, 0, n_grp + p)),
                  pl.BlockSpec((1, tq, LANES), lambda b, p, i: (b, i, 0)),
                  pl.BlockSpec((1, FOX_HEADS, s), lambda b, p, i: (b, 0, 0))],
        out_specs=pl.BlockSpec((1, tq, wd), lambda b, p, i: (b, i, p)),
        out_shape=jax.ShapeDtypeStruct((bx, s, FOX_HEADS * HEAD_DIM), BF16),
        compiler_params=_cparams("arbitrary", "arbitrary", "arbitrary"),
        name="fox_prompt",
    )(qf, fkvbf, fkvbf, c, ct)


def _dil_branch_kernel(q_ref, kc_ref, kp_ref, vc_ref, vp_ref, tab_ref, o_ref, lse_ref, *, tq):
    i = pl.program_id(2)
    lo = _half_masks(TILE)
    lane = lax.broadcasted_iota(jnp.int32, (TILE, LANES), 1)
    first_prev = jnp.where(i == 0, 2, 1)
    for a in range(tq // TILE):
        rows = slice(a * TILE, (a + 1) * TILE)
        lse_tile = jnp.zeros((TILE, LANES), F32)
        outs = []
        for p in range(DIL_HEADS // 2):
            cols = slice(p * LANES, (p + 1) * LANES)
            if a == 0:
                k_prev, v_prev, prev_idx = kp_ref[0, :, cols], vp_ref[0, :, cols], first_prev
            else:
                prev = slice((a - 1) * TILE, a * TILE)
                k_prev, v_prev, prev_idx = kc_ref[0, prev, cols], vc_ref[0, prev, cols], 1
            k2 = jnp.concatenate([k_prev, kc_ref[0, rows, cols]], axis=0)
            v2 = jnp.concatenate([v_prev, vc_ref[0, rows, cols]], axis=0)
            pair = []
            for e, qh in enumerate(_split_heads(q_ref[0, rows, cols], lo)):
                h = 2 * p + e
                bias = jnp.concatenate([tab_ref[h, prev_idx], tab_ref[h, 0]], axis=1)
                s = _nt_dot(qh, k2) + bias
                m = jnp.max(s, axis=1, keepdims=True)
                e_s = jnp.exp(s - m)
                l = jnp.sum(e_s, axis=1, keepdims=True)
                pair.append(jnp.dot(e_s.astype(BF16), v2, preferred_element_type=F32) / l)
                lse_tile = jnp.where(lane == h, jnp.log(l) + m, lse_tile)
            outs.append(jnp.where(lo, pair[0], pair[1]))
        o_ref[0, rows, :] = jnp.concatenate(outs, axis=1).astype(o_ref.dtype)
        lse_ref[0, rows, :] = lse_tile


def _dil_branch_prompt(q, kvbf, table, dil):
    bx, s, width = q.shape
    n_rows = s // dil
    tq = _row_tile(n_rows, 256)
    sub = tq // TILE
    qv = q.reshape(bx, n_rows, dil * width)
    kvv = kvbf.reshape(bx, n_rows, dil * 2 * width)
    o, lse = pl.pallas_call(
        functools.partial(_dil_branch_kernel, tq=tq),
        grid=(bx, dil, n_rows // tq),
        in_specs=[pl.BlockSpec((1, tq, width), lambda b, r, i: (b, i, r)),
                  pl.BlockSpec((1, tq, width), lambda b, r, i: (b, i, 2 * r)),
                  pl.BlockSpec((1, TILE, width), lambda b, r, i: (b, jnp.maximum(sub * i - 1, 0), 2 * r)),
                  pl.BlockSpec((1, tq, width), lambda b, r, i: (b, i, 2 * r + 1)),
                  pl.BlockSpec((1, TILE, width), lambda b, r, i: (b, jnp.maximum(sub * i - 1, 0), 2 * r + 1)),
                  pl.BlockSpec(table.shape, lambda b, r, i: (0, 0, 0, 0))],
        out_specs=[pl.BlockSpec((1, tq, width), lambda b, r, i: (b, i, r)),
                   pl.BlockSpec((1, tq, LANES), lambda b, r, i: (b, i, r))],
        out_shape=[jax.ShapeDtypeStruct((bx, n_rows, dil * width), BF16),
                   jax.ShapeDtypeStruct((bx, n_rows, dil * LANES), F32)],
        compiler_params=_cparams("arbitrary", "arbitrary", "arbitrary"),
        name="dilated_branch_prompt",
    )(qv, kvv, kvv, kvv, kvv, table)
    return o.reshape(bx, s, width), lse.reshape(bx, s, LANES)


def _post_dil_kernel(o1_ref, o2_ref, o3_ref, l1_ref, l2_ref, l3_ref, w_ref, x_ref, gate_ref, g_ref, o_ref):
    lses = [r[0] for r in (l1_ref, l2_ref, l3_ref)]
    m = jnp.maximum(jnp.maximum(lses[0], lses[1]), lses[2])
    es = [jnp.exp(l - m) for l in lses]
    tot = es[0] + es[1] + es[2]
    width = o1_ref.shape[2]
    head_of_col = lax.shift_right_arithmetic(lax.broadcasted_iota(jnp.int32, (LANES, width), 1), SEL_SHIFT)
    expand = jnp.where(head_of_col == lax.broadcasted_iota(jnp.int32, (LANES, width), 0), 1.0, 0.0).astype(BF16)
    mix = jnp.zeros((o1_ref.shape[1], width), F32)
    for e, o_ref_j in zip(es, (o1_ref, o2_ref, o3_ref)):
        alpha = jnp.dot((e / tot).astype(BF16), expand, preferred_element_type=F32)
        mix = mix + alpha * o_ref_j[0].astype(F32)
    y = jnp.dot(mix.astype(BF16), w_ref[...], preferred_element_type=F32)
    yn = y * lax.rsqrt(jnp.mean(y * y, axis=-1, keepdims=True) + NORM_EPS) * g_ref[...]
    o_ref[0] = x_ref[0] + gate_ref[0] * yn


def _post_dil(outs, lses, w_out, x, gate, g):
    bx, s, d = x.shape
    tm = _row_tile(s, 256)
    width = w_out.shape[0]
    return pl.pallas_call(
        _post_dil_kernel,
        grid=(bx, s // tm),
        in_specs=([pl.BlockSpec((1, tm, width), lambda b, i: (b, i, 0))] * 3
                  + [pl.BlockSpec((1, tm, LANES), lambda b, i: (b, i, 0))] * 3
                  + [pl.BlockSpec(w_out.shape, lambda b, i: (0, 0)),
                     pl.BlockSpec((1, tm, d), lambda b, i: (b, i, 0)),
                     _mod_spec(gate, tm),
                     pl.BlockSpec((1, d), lambda b, i: (0, 0))]),
        out_specs=pl.BlockSpec((1, tm, d), lambda b, i: (b, i, 0)),
        out_shape=jax.ShapeDtypeStruct((bx, s, d), F32),
        compiler_params=_cparams("arbitrary", "arbitrary"),
        name="post_dilated",
    )(*outs, *lses, w_out, x, gate, g)


def _compress_kernel(pt_ref, *refs, n_pages, pages_per_step, feature_major):
    page_refs = refs[:pages_per_step]
    w1_ref, w2_ref, pe_ref, kc_ref, vc_ref, rows_ref, chunk_ref = refs[pages_per_step:]
    j = pl.program_id(1)
    for p, page_ref in enumerate(page_refs):
        row0 = pl.multiple_of((j * pages_per_step + p) * PAGE_SIZE, PAGE_SIZE)
        if feature_major:
            for kv in range(2):
                rows_ref[kv, pl.ds(row0, PAGE_SIZE), :] = jnp.concatenate(
                    [page_ref[0, 0, kv, g].T for g in range(NSA_KV_HEADS)], axis=1)
        else:
            rows_ref[0, pl.ds(row0, PAGE_SIZE), :] = page_ref[0, :, 0:LANES]
            rows_ref[1, pl.ds(row0, PAGE_SIZE), :] = page_ref[0, :, LANES:2 * LANES]

    @pl.when(j == n_pages // pages_per_step - 1)
    def _():
        n_chunks = n_pages * PAGE_SIZE // CMP_STRIDE
        half = CMP_STRIDE * HEAD_DIM
        for kv, out_ref in ((0, kc_ref), (1, vc_ref)):
            w1 = w1_ref[kv]
            pe_a = jnp.broadcast_to(pe_ref[kv, :, 0:half], (8, half)).astype(BF16)
            pe_b = jnp.broadcast_to(pe_ref[kv, :, half:], (8, half)).astype(BF16)
            pe_term = (jnp.dot(pe_a, w1, preferred_element_type=F32)[0:1, 0:CMP_HIDDEN]
                       + jnp.dot(pe_b, w1, preferred_element_type=F32)[0:1, CMP_HIDDEN:])
            for l in range(CMP_STRIDE):
                both = rows_ref[kv, pl.ds(l, n_chunks, stride=CMP_STRIDE), :].astype(BF16)
                for g in range(NSA_KV_HEADS):
                    chunk_ref[g, :, l * HEAD_DIM:(l + 1) * HEAD_DIM] = both[:, g * HEAD_DIM:(g + 1) * HEAD_DIM]
            outs = []
            for g in range(NSA_KV_HEADS):
                uv = jnp.dot(chunk_ref[g], w1, preferred_element_type=F32)
                pre = uv[:, 0:CMP_HIDDEN] + pltpu.roll(uv[:, CMP_HIDDEN:], n_chunks - 1, 0) + pe_term
                hid = jax.nn.gelu(pre).astype(BF16)
                outs.append(jnp.dot(hid, w2_ref[kv], preferred_element_type=F32))
            out_ref[0] = jnp.concatenate(outs, axis=1)


def _compress(pool, page_table, w1cat, w2, pe, layer=None):
    n_req, n_pages = page_table.shape
    n_chunks = n_pages * PAGE_SIZE // CMP_STRIDE
    width = 2 * NSA_KV_HEADS * HEAD_DIM
    pps = 8 if n_pages % 8 == 0 else 1

    def page_spec(p):
        if layer is None:
            return pl.BlockSpec((1, PAGE_SIZE, width), lambda r, j, pt: (pt[r, j * pps + p], 0, 0))
        return pl.BlockSpec((1, 1, 2, NSA_KV_HEADS, HEAD_DIM, PAGE_SIZE),
                            lambda r, j, pt: (layer, pt[r, j * pps + p], 0, 0, 0, 0))

    grid_spec = pltpu.PrefetchScalarGridSpec(
        num_scalar_prefetch=1,
        grid=(n_req, n_pages // pps),
        in_specs=[page_spec(p) for p in range(pps)] + [
                  pl.BlockSpec(w1cat.shape, lambda r, j, pt: (0, 0, 0)),
                  pl.BlockSpec(w2.shape, lambda r, j, pt: (0, 0, 0)),
                  pl.BlockSpec(pe.shape, lambda r, j, pt: (0, 0, 0))],
        out_specs=[pl.BlockSpec((1, n_chunks, LANES), lambda r, j, pt: (r, 0, 0)),
                   pl.BlockSpec((1, n_chunks, LANES), lambda r, j, pt: (r, 0, 0))],
        scratch_shapes=[pltpu.VMEM((2, n_pages * PAGE_SIZE, LANES), F32),
                        pltpu.VMEM((NSA_KV_HEADS, n_chunks, CMP_STRIDE * HEAD_DIM), BF16)],
    )
    return pl.pallas_call(
        functools.partial(_compress_kernel, n_pages=n_pages, pages_per_step=pps, feature_major=layer is not None),
        grid_spec=grid_spec,
        out_shape=[jax.ShapeDtypeStruct((n_req, n_chunks, LANES), F32)] * 2,
        compiler_params=_cparams("arbitrary", "arbitrary"),
        name="nsa_compress",
    )(page_table, *([pool] * pps), w1cat, w2, pe)


def _top_k_mask(imp, n_top):
    cand = lax.broadcasted_iota(jnp.int32, imp.shape, 0)
    height = imp.shape[0]

    def body(_, carry):
        imp, sel = carry
        m = jnp.max(imp, axis=0, keepdims=True)
        first = jnp.min(jnp.where(imp == m, cand, height), axis=0, keepdims=True)
        pick = (cand == first) & (m > -jnp.inf)
        return jnp.where(cand == first, -jnp.inf, imp), jnp.where(pick, 1.0, sel)

    _, sel = lax.fori_loop(0, n_top, body, (imp, jnp.zeros(imp.shape, F32)))
    return sel


def _nsa_kernel(qn_ref, kc_ref, vc_ref, bcmp_ref, sel_ref, win_ref, tsel_ref, twin_ref, small_ref, o_ref,
                *, tq, tk, n_sel_delta, n_win_tiles):
    i = pl.program_id(1)
    t0 = i * tq
    n_cmp_pad = kc_ref.shape[1]
    n_blk = LANES
    lo = _half_masks(tq)
    lane = lax.broadcasted_iota(jnp.int32, (tq, LANES), 1)
    gates = small_ref[0]
    q_all = qn_ref[0].astype(F32)

    sj = lax.broadcasted_iota(jnp.int32, (n_blk, n_cmp_pad), 0) * SEL_BLOCK
    ci = lax.broadcasted_iota(jnp.int32, (n_blk, n_cmp_pad), 1) * CMP_STRIDE
    cover_t = jnp.where((ci < sj + SEL_BLOCK) & (ci + CMP_LEN > sj), 1.0, 0.0).astype(F32)
    blk_t = lax.broadcasted_iota(jnp.int32, (n_blk, tq), 0)
    cur_t = lax.shift_right_arithmetic(t0 + lax.broadcasted_iota(jnp.int32, (n_blk, tq), 1), SEL_SHIFT)
    forced_t = (blk_t == 0) | (blk_t == cur_t) | (blk_t == cur_t - 1)
    blk_of_key = lax.shift_right_arithmetic(lax.broadcasted_iota(jnp.int32, (tk, n_blk), 0), SEL_SHIFT)
    blk_delta = lax.broadcasted_iota(jnp.int32, (tk, n_blk), 1) - blk_of_key
    sub = tk // TILE

    n_sub = tq // TILE
    group_of = [h // NSA_GROUP for h in range(NSA_HEADS)]

    qs = []
    for h in range(NSA_HEADS):
        g = group_of[h]
        blk = q_all[:, LANES * (h // 2):LANES * (h // 2 + 1)]
        if h % 2 != g:
            blk = pltpu.roll(blk, HEAD_DIM, 1)
        qs.append(jnp.where(lo if g == 0 else jnp.logical_not(lo), blk, 0.0).astype(BF16))

    kcb = kc_ref[0].astype(BF16)
    vcb = vc_ref[0].astype(BF16)
    o_cmp, imps = [], []
    for g in range(NSA_KV_HEADS):
        psum = jnp.zeros((tq, n_cmp_pad), F32)
        for r in range(NSA_GROUP):
            h = g * NSA_GROUP + r
            s = _nt_dot(qs[h], kcb) + bcmp_ref[h]
            m = jnp.max(s, axis=1, keepdims=True)
            e = jnp.exp(s - m)
            scale = jnp.where(m > 0.5 * NEG, 1.0 / jnp.maximum(jnp.sum(e, axis=1, keepdims=True), TINY), 0.0)
            p = e * scale
            psum = psum + p
            o_cmp.append(jnp.dot(p.astype(BF16), vcb, preferred_element_type=F32))
        imp = lax.dot_general(cover_t, psum, (((1,), (1,)), ((), ())), preferred_element_type=F32,
                              precision=lax.Precision.HIGHEST)
        imp = jnp.where(forced_t, FORCED_SCORE, imp)
        imps.append(jnp.where(blk_t <= cur_t, imp, -jnp.inf))

    sel_t = _top_k_mask(jnp.concatenate(imps, axis=1), min(SEL_TOPK, n_blk))
    not_sel = [(1.0 - sel_t[:, g * tq:(g + 1) * tq]).T.astype(BF16) for g in range(NSA_KV_HEADS)]

    q_ext = [jnp.concatenate([qs[h], not_sel[group_of[h]]], axis=1) for h in range(NSA_HEADS)]

    def sel_chunk(c, states):
        off = pl.multiple_of(c * tk, tk)
        k = sel_ref[0, pl.ds(off, tk), 0:LANES]
        v = sel_ref[0, pl.ds(off, tk), LANES:2 * LANES]
        k_ext = jnp.concatenate([k, jnp.where(blk_delta == c * (tk // SEL_BLOCK), NEG, 0.0).astype(BF16)], axis=1)
        out = []
        for h in range(NSA_HEADS):
            bias = jnp.concatenate([jnp.concatenate(
                [tsel_ref[h, jnp.clip(n_sub * i + a - (c * sub + u), -1, n_sel_delta - 1) + 1] for u in range(sub)],
                axis=1) for a in range(n_sub)], axis=0)
            out.append(_online_update(states[h], _nt_dot(q_ext[h], k_ext) + bias, v))
        return tuple(out)

    n_chunks = (t0 + tq - 1) // tk + 1
    sel_states = lax.fori_loop(0, n_chunks, sel_chunk, tuple(_init_state(tq) for _ in range(NSA_HEADS)))

    o_win = [[] for _ in range(NSA_HEADS)]
    for a in range(n_sub):
        j0 = n_sub * i + a
        ks, vs, idxs = [], [], []
        for u in range(n_win_tiles - 1, -1, -1):
            off = pl.multiple_of(jnp.maximum(j0 - u, 0) * TILE, TILE)
            ks.append(win_ref[0, pl.ds(off, TILE), 0:LANES])
            vs.append(win_ref[0, pl.ds(off, TILE), LANES:2 * LANES])
            idxs.append(jnp.where(j0 - u >= 0, u + 1, 0))
        k = jnp.concatenate(ks, axis=0)
        v = jnp.concatenate(vs, axis=0)
        for h in range(NSA_HEADS):
            bias = jnp.concatenate([twin_ref[h, ix] for ix in idxs], axis=1)
            s = _nt_dot(qs[h][a * TILE:(a + 1) * TILE], k) + bias
            e = jnp.exp(s - jnp.max(s, axis=1, keepdims=True))
            l = jnp.sum(e, axis=1, keepdims=True)
            o_win[h].append(jnp.dot(e.astype(BF16), v, preferred_element_type=F32) / l)

    pair_out = [None] * (NSA_HEADS // 2)
    for h in range(NSA_HEADS):
        gc, gs, gw = (jnp.sum(jnp.where(lane == 3 * h + b, gates, 0.0), axis=1, keepdims=True) for b in range(3))
        _, l_s, acc_s = sel_states[h]
        o = gc * o_cmp[h] + gs * (acc_s / l_s) + gw * jnp.concatenate(o_win[h], axis=0)
        if h % 2 != group_of[h]:
            o = pltpu.roll(o, HEAD_DIM, 1)
        prev = pair_out[h // 2]
        keep = lo if h % 2 == 0 else jnp.logical_not(lo)
        pair_out[h // 2] = jnp.where(keep, o, 0.0 if prev is None else prev)

    o_ref[0] = jnp.concatenate(pair_out, axis=1).astype(o_ref.dtype)


def _nsa_prompt(qn, kc, vc, bcmp, nsabf, tsel, twin, small):
    bx, s, _ = qn.shape
    tq = _row_tile(s, 256)
    tk = _row_tile(s, 1024)
    n_cmp_pad = kc.shape[1]
    once = pl.Buffered(1)
    return pl.pallas_call(
        functools.partial(_nsa_kernel, tq=tq, tk=tk, n_sel_delta=tsel.shape[1] - 1, n_win_tiles=twin.shape[1] - 1),
        grid=(bx, s // tq),
        in_specs=[pl.BlockSpec((1, tq, NSA_HEADS * HEAD_DIM), lambda b, i: (b, i, 0)),
                  pl.BlockSpec((1, n_cmp_pad, LANES), lambda b, i: (b, 0, 0)),
                  pl.BlockSpec((1, n_cmp_pad, LANES), lambda b, i: (b, 0, 0)),
                  pl.BlockSpec((NSA_HEADS, tq, n_cmp_pad), lambda b, i: (0, i, 0)),
                  pl.BlockSpec((1, s, 2 * LANES), lambda b, i: (b, 0, 1)),
                  pl.BlockSpec((1, s, 2 * LANES), lambda b, i: (b, 0, 2)),
                  pl.BlockSpec(tsel.shape, lambda b, i: (0, 0, 0, 0), pipeline_mode=once),
                  pl.BlockSpec(twin.shape, lambda b, i: (0, 0, 0, 0), pipeline_mode=once),
                  pl.BlockSpec((1, tq, LANES), lambda b, i: (b, i, 0))],
        out_specs=pl.BlockSpec((1, tq, NSA_HEADS * HEAD_DIM), lambda b, i: (b, i, 0)),
        out_shape=jax.ShapeDtypeStruct((bx, s, NSA_HEADS * HEAD_DIM), BF16),
        compiler_params=_cparams("arbitrary", "arbitrary"),
        name="nsa_prompt",
    )(qn, kc, vc, bcmp, nsabf, nsabf, tsel, twin, small)


def _bf16_round(x):
    return x.astype(BF16).astype(F32)


def _dot3(z, w):
    hi = z.astype(BF16)
    rest = z - hi.astype(F32)
    mid = rest.astype(BF16)
    lo = (rest - mid.astype(F32)).astype(BF16)
    return sum(jnp.dot(part, w, preferred_element_type=F32) for part in (hi, mid, lo))


def _feature_major(cache):
    n = cache.ndim
    return jnp.transpose(cache, tuple(range(n - 4)) + (n - 3, n - 2, n - 1, n - 4))


def _nt_dot_bf16(p, vt):
    return lax.dot_general(p.astype(BF16), vt, (((1,), (1,)), ((), ())), preferred_element_type=F32)


def _fox_sample_kernel(pt_ref, *refs, n_steps, pages_per_step):
    page_refs, lf_refs = refs[:pages_per_step], refs[pages_per_step:2 * pages_per_step]
    q_ref, new_ref, lfn_ref, o_ref, m_ref, l_ref, acc_ref, carry_ref = refs[2 * pages_per_step:]
    j = pl.program_id(1)
    width = FOX_HEADS * HEAD_DIM
    qbd = q_ref[0]

    @pl.when(j == 0)
    def _():
        s_new = jnp.sum(qbd.astype(F32) * _bf16_round(new_ref[0, 0:1]), axis=1, keepdims=True)
        m_ref[...] = jnp.broadcast_to(s_new, m_ref.shape)
        l_ref[...] = jnp.ones_like(l_ref)
        acc_ref[...] = jnp.broadcast_to(_bf16_round(new_ref[0, 1:2]), acc_ref.shape)
        carry_ref[...] = lfn_ref[0]

    u = lax.broadcasted_iota(jnp.int32, (PAGE_SIZE, PAGE_SIZE), 0)
    c = lax.broadcasted_iota(jnp.int32, (PAGE_SIZE, PAGE_SIZE), 1)
    later = jnp.where(u > c, 1.0, 0.0).astype(BF16)
    carry = carry_ref[:, 0:1]
    biases = []
    for lf_ref in lf_refs:
        lf = lf_ref[0, 0]
        biases.append(carry + _dot3(lf, later))
        carry = carry + jnp.sum(lf, axis=1, keepdims=True)
    kt = jnp.concatenate([r[0, 0, 0].reshape(width, PAGE_SIZE).astype(BF16) for r in page_refs], axis=1)
    vt = jnp.concatenate([r[0, 0, 1].reshape(width, PAGE_SIZE).astype(BF16) for r in page_refs], axis=1)
    s = jnp.dot(qbd, kt, preferred_element_type=F32) + jnp.concatenate(biases, axis=1)
    m_old = m_ref[:, 0:1]
    m_new = jnp.maximum(m_old, jnp.max(s, axis=1, keepdims=True))
    alpha = jnp.exp(m_old - m_new)
    p = jnp.exp(s - m_new)
    l_ref[...] = jnp.broadcast_to(alpha * l_ref[:, 0:1] + jnp.sum(p, axis=1, keepdims=True), l_ref.shape)
    acc_ref[...] = alpha * acc_ref[...] + _nt_dot_bf16(p, vt)
    m_ref[...] = jnp.broadcast_to(m_new, m_ref.shape)
    carry_ref[...] = jnp.broadcast_to(carry, carry_ref.shape)

    @pl.when(j == n_steps - 1)
    def _():
        o_ref[0] = acc_ref[...] / l_ref[:, 0:1]


def _block_diag_queries(q, n_heads):
    width = n_heads * HEAD_DIM
    keep = (jnp.arange(width) // HEAD_DIM)[None, :] == jnp.arange(n_heads)[:, None]
    return jnp.where(keep[None], q[:, None, :], jnp.zeros((), q.dtype))


def _diag_blocks(o, n_heads):
    n = o.shape[0]
    o5 = o.reshape(n, n_heads, n_heads, HEAD_DIM)
    return jnp.stack([o5[:, h, h] for h in range(n_heads)], axis=1).reshape(n, n_heads * HEAD_DIM)


def _fox_sample(cache_t, layer, logf_t, page_table, qbd, new_kv, lf_new):
    n_req, n_pages = page_table.shape
    pps = 8 if n_pages % 8 == 0 else 1
    n_steps = n_pages // pps
    width = FOX_HEADS * HEAD_DIM

    def page_spec(shape, p):
        zeros = (0,) * (len(shape) - 2)
        return pl.BlockSpec(shape, lambda r, j, pt: (layer, pt[r, n_pages - 1 - (j * pps + p)]) + zeros)

    grid_spec = pltpu.PrefetchScalarGridSpec(
        num_scalar_prefetch=1,
        grid=(n_req, n_steps),
        in_specs=([page_spec((1, 1, 2, FOX_HEADS, HEAD_DIM, PAGE_SIZE), p) for p in range(pps)]
                  + [page_spec((1, 1, FOX_HEADS, PAGE_SIZE), p) for p in range(pps)]
                  + [pl.BlockSpec((1, FOX_HEADS, width), lambda r, j, pt: (r, 0, 0)),
                     pl.BlockSpec((1, 2, width), lambda r, j, pt: (r, 0, 0)),
                     pl.BlockSpec((1, FOX_HEADS, LANES), lambda r, j, pt: (r, 0, 0))]),
        out_specs=pl.BlockSpec((1, FOX_HEADS, width), lambda r, j, pt: (r, 0, 0)),
        scratch_shapes=[pltpu.VMEM((FOX_HEADS, LANES), F32), pltpu.VMEM((FOX_HEADS, LANES), F32),
                        pltpu.VMEM((FOX_HEADS, width), F32), pltpu.VMEM((FOX_HEADS, LANES), F32)],
    )
    return pl.pallas_call(
        functools.partial(_fox_sample_kernel, n_steps=n_steps, pages_per_step=pps),
        grid_spec=grid_spec,
        out_shape=jax.ShapeDtypeStruct((n_req, FOX_HEADS, width), F32),
        compiler_params=_cparams("arbitrary", "arbitrary"),
        name="fox_sample",
    )(page_table, *([cache_t] * pps), *([logf_t] * pps), qbd, new_kv, lf_new)


def _dil_sample_kernel(q_ref, new_ref, kt_ref, vt_ref, tab_ref, o_ref, *, n_buf):
    heads = q_ref.shape[2]
    qbd = q_ref[0, 0]
    kt = kt_ref[0, 0, 0].reshape(heads * HEAD_DIM, n_buf).astype(BF16)
    vt = vt_ref[0, 0, 0].reshape(heads * HEAD_DIM, n_buf).astype(BF16)
    tab = tab_ref[0]
    s = jnp.dot(qbd, kt, preferred_element_type=F32) + tab[:, 0:n_buf]
    s_new = (jnp.sum(qbd.astype(F32) * _bf16_round(new_ref[0, 0, 0:1]), axis=1, keepdims=True)
             + tab[:, n_buf:n_buf + 1])
    m = jnp.maximum(jnp.max(s, axis=1, keepdims=True), s_new)
    p = jnp.exp(s - m)
    p_new = jnp.exp(s_new - m)
    l = jnp.sum(p, axis=1, keepdims=True) + p_new
    o_ref[0, 0] = (_nt_dot_bf16(p, vt) + _bf16_round(p_new) * _bf16_round(new_ref[0, 0, 1:2])) / l


def _dil_sample(state_t, layer, qbd, new_kv, table):
    n_req = state_t.shape[1]
    n_buf = state_t.shape[-1]
    half = DIL_HEADS // 2
    width = half * HEAD_DIM
    return pl.pallas_call(
        functools.partial(_dil_sample_kernel, n_buf=n_buf),
        grid=(n_req, 2),
        in_specs=[pl.BlockSpec((1, 1, half, width), lambda r, hh: (r, hh, 0, 0)),
                  pl.BlockSpec((1, 1, 2, width), lambda r, hh: (r, hh, 0, 0)),
                  pl.BlockSpec((1, 1, 1, half, HEAD_DIM, n_buf), lambda r, hh: (layer, r, 0, hh, 0, 0)),
                  pl.BlockSpec((1, 1, 1, half, HEAD_DIM, n_buf), lambda r, hh: (layer, r, 1, hh, 0, 0)),
                  pl.BlockSpec((1, half, table.shape[2]), lambda r, hh: (hh, 0, 0))],
        out_specs=pl.BlockSpec((1, 1, half, width), lambda r, hh: (r, hh, 0, 0)),
        out_shape=jax.ShapeDtypeStruct((n_req, 2, half, width), F32),
        compiler_params=_cparams("arbitrary", "arbitrary"),
        name="dilated_sample",
    )(qbd, new_kv, state_t, state_t, table)


def _group_queries(q_row):
    row8 = lax.broadcasted_iota(jnp.int32, (8, LANES), 0)
    lo = lax.broadcasted_iota(jnp.int32, (8, LANES), 1) < HEAD_DIM
    out = []
    for g in range(NSA_KV_HEADS):
        qg = jnp.zeros((8, LANES), F32)
        for r in range(NSA_GROUP):
            h = g * NSA_GROUP + r
            blk = jnp.broadcast_to(q_row[:, LANES * (h // 2):LANES * (h // 2 + 1)], (8, LANES))
            if h % 2 != g:
                blk = pltpu.roll(blk, HEAD_DIM, 1)
            qg = jnp.where((row8 == r) & (lo if g == 0 else jnp.logical_not(lo)), blk, qg)
        out.append(qg)
    return out


def _nsa_cmp_kernel(q_ref, kc_ref, vc_ref, bcmp_ref, oc_ref, psum_ref):
    n_cmp_pad = kc_ref.shape[1]
    kcb = kc_ref[0].astype(BF16)
    vcb = vc_ref[0].astype(BF16)
    row8 = lax.broadcasted_iota(jnp.int32, (8, n_cmp_pad), 0)
    for g, qg in enumerate(_group_queries(q_ref[0].astype(F32))):
        bias = bcmp_ref[8 * g:8 * g + 8, :]
        ok = (bias > 0.5 * NEG) & (row8 < NSA_GROUP)
        s = jnp.where(ok, _nt_dot(qg.astype(BF16), kcb) + bias, NEG)
        m = jnp.max(s, axis=1, keepdims=True)
        e = jnp.where(ok, jnp.exp(s - m), 0.0)
        p = e / jnp.maximum(jnp.sum(e, axis=1, keepdims=True), TINY)
        oc_ref[0, g] = jnp.dot(p.astype(BF16), vcb, preferred_element_type=F32)
        psum_ref[0, g] = jnp.broadcast_to(jnp.sum(p, axis=0, keepdims=True), (8, n_cmp_pad))


def _nsa_select(q, kc, vc, bcmp_row, n_blk_pad, cur):
    n_req = q.shape[0]
    n_cmp_pad = kc.shape[1]
    o_cmp, psum = pl.pallas_call(
        _nsa_cmp_kernel,
        grid=(n_req,),
        in_specs=[pl.BlockSpec((1, 1, NSA_HEADS * HEAD_DIM), lambda r: (r, 0, 0)),
                  pl.BlockSpec((1, n_cmp_pad, LANES), lambda r: (r, 0, 0)),
                  pl.BlockSpec((1, n_cmp_pad, LANES), lambda r: (r, 0, 0)),
                  pl.BlockSpec(bcmp_row.shape, lambda r: (0, 0))],
        out_specs=[pl.BlockSpec((1, NSA_KV_HEADS, 8, LANES), lambda r: (r, 0, 0, 0)),
                   pl.BlockSpec((1, NSA_KV_HEADS, 8, n_cmp_pad), lambda r: (r, 0, 0, 0))],
        out_shape=[jax.ShapeDtypeStruct((n_req, NSA_KV_HEADS, 8, LANES), F32),
                   jax.ShapeDtypeStruct((n_req, NSA_KV_HEADS, 8, n_cmp_pad), F32)],
        compiler_params=_cparams("arbitrary"),
        name="nsa_sample_cmp",
    )(q, kc, vc, bcmp_row)
    n_pairs = n_req * NSA_KV_HEADS
    assert n_pairs <= LANES
    pairs = jnp.pad(psum[:, :, 0, :].reshape(n_pairs, n_cmp_pad), ((0, LANES - n_pairs), (0, 0)))
    idx = pl.pallas_call(
        functools.partial(_nsa_rank_kernel, n_blk_pad=n_blk_pad, cur=cur),
        out_shape=jax.ShapeDtypeStruct((SEL_TOPK, LANES), jnp.int32),
        compiler_params=pltpu.CompilerParams(vmem_limit_bytes=VMEM_LIMIT_BYTES),
        name="nsa_sample_topk",
    )(pairs)
    return o_cmp, idx[:, :n_pairs].T.reshape(n_req, NSA_KV_HEADS * SEL_TOPK)


def _nsa_rank_kernel(psum_ref, idx_ref, *, n_blk_pad, cur):
    n_cmp_pad = psum_ref.shape[1]
    sj = lax.broadcasted_iota(jnp.int32, (n_blk_pad, n_cmp_pad), 0) * SEL_BLOCK
    ci = lax.broadcasted_iota(jnp.int32, (n_blk_pad, n_cmp_pad), 1) * CMP_STRIDE
    cover_t = jnp.where((ci < sj + SEL_BLOCK) & (ci + CMP_LEN > sj), 1.0, 0.0).astype(F32)
    imp = lax.dot_general(cover_t, psum_ref[...], (((1,), (1,)), ((), ())), preferred_element_type=F32,
                          precision=lax.Precision.HIGHEST)
    blk = lax.broadcasted_iota(jnp.int32, imp.shape, 0)
    imp = jnp.where((blk == 0) | (blk == cur) | (blk == cur - 1), FORCED_SCORE, imp)
    imp = jnp.where(blk <= cur, imp, -jnp.inf)
    pick_row = lax.broadcasted_iota(jnp.int32, idx_ref.shape, 0)

    def body(it, carry):
        imp, idx = carry
        top = jnp.max(imp, axis=0, keepdims=True)
        first = jnp.min(jnp.where(imp == top, blk, n_blk_pad), axis=0, keepdims=True)
        idx = jnp.where(pick_row == it, jnp.where(top > -jnp.inf, first, -1), idx)
        return jnp.where(blk == first, -jnp.inf, imp), idx

    _, idx = lax.fori_loop(0, SEL_TOPK, body, (imp, jnp.full(idx_ref.shape, -1, jnp.int32)))
    idx_ref[...] = idx


def _nsa_attend_kernel(idx_ref, pt_ref, q_ref, *refs, n_past_blk, new_lane, n_win):
    n_picks = NSA_KV_HEADS * SEL_TOPK
    blk_refs = refs[:n_picks]
    fsel_ref, wbuf_ref, fwin_ref, nsel_ref, nwin_ref, oc_ref, small_ref, o_ref = refs[n_picks:]
    r_idx = pl.program_id(0)
    qgs = [qg[:, g * HEAD_DIM:(g + 1) * HEAD_DIM] for g, qg in enumerate(_group_queries(q_ref[0].astype(F32)))]
    row8 = lax.broadcasted_iota(jnp.int32, (8, LANES), 0)
    lane = lax.broadcasted_iota(jnp.int32, (8, LANES), 1)
    n_pages_past = n_past_blk // 2
    gates = small_ref[0]

    for g, qg in enumerate(qgs):
        grp = slice(g * HEAD_DIM, (g + 1) * HEAD_DIM)
        rows = slice(8 * g, 8 * g + 8)
        qb = qg.astype(BF16)
        ksel_new = _bf16_round(nsel_ref[0][:, 2 * LANES:3 * LANES][:, grp])
        vsel_new = _bf16_round(nsel_ref[0][:, 3 * LANES:4 * LANES][:, grp])
        kwin_new = _bf16_round(nwin_ref[0][:, 0:LANES][:, grp])
        vwin_new = _bf16_round(nwin_ref[0][:, LANES:2 * LANES][:, grp])

        kts, vts, biases = [], [], []
        for k in range(SEL_TOPK):
            blk_ref = blk_refs[g * SEL_TOPK + k]
            b = idx_ref[r_idx, g * SEL_TOPK + k]
            valid = (b >= 0) & (b < n_past_blk)
            page = jnp.clip(lax.shift_right_arithmetic(b, 1), 0, n_pages_past - 1)
            in_blk = lax.shift_right_arithmetic(lane, SEL_SHIFT) == jnp.bitwise_and(b, 1)
            kts.append(blk_ref[0, 0, 0, g].astype(BF16))
            vts.append(blk_ref[0, 0, 1, g].astype(BF16))
            biases.append(jnp.where(valid & in_blk, fsel_ref[page, rows, :], NEG))
        s = jnp.dot(qb, jnp.concatenate(kts, axis=1), preferred_element_type=F32) + jnp.concatenate(biases, axis=1)
        s_new = jnp.sum(qg * ksel_new, axis=1, keepdims=True) + fsel_ref[n_pages_past, rows, new_lane:new_lane + 1]
        m = jnp.maximum(jnp.max(s, axis=1, keepdims=True), s_new)
        p = jnp.exp(s - m)
        p_new = jnp.exp(s_new - m)
        l = jnp.sum(p, axis=1, keepdims=True) + p_new
        o_sel = (_nt_dot_bf16(p, jnp.concatenate(vts, axis=1)) + _bf16_round(p_new) * vsel_new) / l

        kw_t = wbuf_ref[0, 0, 0, g].astype(BF16)
        vw_t = wbuf_ref[0, 0, 1, g].astype(BF16)
        s = jnp.dot(qb, kw_t, preferred_element_type=F32) + fwin_ref[rows, 0:n_win]
        s_wn = jnp.sum(qg * kwin_new, axis=1, keepdims=True) + fwin_ref[rows, n_win:n_win + 1]
        m = jnp.maximum(jnp.max(s, axis=1, keepdims=True), s_wn)
        p = jnp.exp(s - m)
        p_new = jnp.exp(s_wn - m)
        l = jnp.sum(p, axis=1, keepdims=True) + p_new
        o_win = (_nt_dot_bf16(p, vw_t) + _bf16_round(p_new) * vwin_new) / l

        gate = []
        for b in range(3):
            col = jnp.zeros((8, 1), F32)
            for r in range(NSA_GROUP):
                lane_i = 3 * (g * NSA_GROUP + r) + b
                col = jnp.where(row8[:, 0:1] == r, gates[:, lane_i:lane_i + 1], col)
            gate.append(col)
        o_cmp = oc_ref[0, g][:, g * HEAD_DIM:(g + 1) * HEAD_DIM]
        o_ref[0, g] = gate[0] * o_cmp + gate[1] * o_sel + gate[2] * o_win


def _nsa_attend(idx, page_table, q, cache_t, layer, fsel, win_t, fwin, new_sel, new_win, o_cmp, small, n_past_blk,
                new_lane):
    n_req = q.shape[0]
    n_win = win_t.shape[-1]

    def blk_map(pick):
        def index(r, idx_ref, pt_ref):
            b = idx_ref[r, pick]
            b = jnp.where((b >= 0) & (b < n_past_blk), b, 0)
            return layer, pt_ref[r, lax.shift_right_arithmetic(b, 1)], 1, 0, 0, 0
        return index

    n_picks = NSA_KV_HEADS * SEL_TOPK
    const = lambda *shape: (lambda r, idx_ref, pt_ref: shape)
    per_req3 = lambda r, idx_ref, pt_ref: (r, 0, 0)
    per_req4 = lambda r, idx_ref, pt_ref: (r, 0, 0, 0)
    page_blk = (1, 1, 2, NSA_KV_HEADS, HEAD_DIM, PAGE_SIZE)
    grid_spec = pltpu.PrefetchScalarGridSpec(
        num_scalar_prefetch=2,
        grid=(n_req,),
        in_specs=([pl.BlockSpec((1, 1, NSA_HEADS * HEAD_DIM), per_req3)]
                  + [pl.BlockSpec(page_blk, blk_map(pick)) for pick in range(n_picks)]
                  + [pl.BlockSpec(fsel.shape, const(0, 0, 0)),
                     pl.BlockSpec((1, 1, 2, NSA_KV_HEADS, HEAD_DIM, n_win),
                                  lambda r, idx_ref, pt_ref: (layer, r, 0, 0, 0, 0)),
                     pl.BlockSpec(fwin.shape, const(0, 0)),
                     pl.BlockSpec((1, 1, 4 * LANES), per_req3),
                     pl.BlockSpec((1, 1, 2 * LANES), per_req3),
                     pl.BlockSpec((1, NSA_KV_HEADS, 8, LANES), per_req4),
                     pl.BlockSpec((1, 1, LANES), per_req3)]),
        out_specs=pl.BlockSpec((1, NSA_KV_HEADS, 8, HEAD_DIM), per_req4),
    )
    return pl.pallas_call(
        functools.partial(_nsa_attend_kernel, n_past_blk=n_past_blk, new_lane=new_lane, n_win=n_win),
        grid_spec=grid_spec,
        out_shape=jax.ShapeDtypeStruct((n_req, NSA_KV_HEADS, 8, HEAD_DIM), F32),
        compiler_params=_cparams("arbitrary"),
        name="nsa_sample_attend",
    )(idx, page_table, q, *([cache_t] * n_picks), fsel, win_t, fwin, new_sel, new_win, o_cmp, small)


def kernel(x_prompt, x_sample, cache_nsa_kv, cache_fox_kv, cache_fox_logf, state_nsa_win_kv, state_dil_kv, page_table,
           c_prompt, c_sample, rel_bias, norm_g, w_ada, b_ada, w_in_a, nsa_gate_b, fox_f_b, nsa_cmp_w1, nsa_cmp_w2,
           nsa_cmp_pe, w_out_a, w_in_c, w_out_c, w_mlp1, w_mlp2):
    bp, s, d = x_prompt.shape
    bd = x_sample.shape[0]
    depth = w_ada.shape[0]
    n_pages = s // PAGE_SIZE
    n_cmp_pad = s // CMP_STRIDE

    mods = _ada_params(jnp.concatenate([c_prompt, c_sample], axis=0), w_ada, b_ada).reshape(depth, bp + bd, 6, d)
    tab_sel = _toeplitz_table(rel_bias, NSA_HEADS, 15, "causal")
    tab_win = _toeplitz_table(rel_bias, NSA_HEADS, NSA_WINDOW // TILE + 2, "window")
    assert all(window // dil == TILE for window, dil in DIL_BRANCHES)
    tab_dil = [_toeplitz_table(rel_bias, DIL_HEADS, 3, "branch", dil) for _, dil in DIL_BRANCHES]
    bcmp = _cmp_bias_table(rel_bias, s, TILE, n_cmp_pad, 0)

    past_len = page_table.shape[1] * PAGE_SIZE
    assert past_len % SEL_BLOCK == 0
    n_past_blk = past_len // SEL_BLOCK
    n_blk_pad = -(-(n_past_blk + 1) // LANES) * LANES
    far = 1 << 30
    bcmp_s = _affine_bias(rel_bias, NSA_GROUPED_ROWS, past_len // CMP_STRIDE, past_len - (CMP_LEN - 1), -CMP_STRIDE, far)
    assert past_len % PAGE_SIZE == 0
    n_pages_s = past_len // PAGE_SIZE
    fsel = _affine_bias(rel_bias, NSA_GROUPED_ROWS, (n_pages_s + 1) * PAGE_SIZE, past_len, -1, far)
    fsel = jnp.swapaxes(fsel.reshape(len(NSA_GROUPED_ROWS), n_pages_s + 1, PAGE_SIZE), 0, 1)
    n_win_buf = state_nsa_win_kv.shape[2]
    fwin = _affine_bias(rel_bias, NSA_GROUPED_ROWS, n_win_buf + LANES, n_win_buf, -1, NSA_WINDOW)
    n_dil_buf = state_dil_kv.shape[2]
    tab_dil_s = _affine_bias(rel_bias, range(DIL_HEADS), n_dil_buf + LANES, n_dil_buf, -1, far, union=True)
    tab_dil_s = tab_dil_s.reshape(2, DIL_HEADS // 2, n_dil_buf + LANES)
    cache_nsa_t = _feature_major(cache_nsa_kv)
    cache_fox_t = _feature_major(cache_fox_kv)
    logf_t = jnp.swapaxes(cache_fox_logf, 2, 3)
    win_t = _feature_major(state_nsa_win_kv)
    dil_t = _feature_major(state_dil_kv)

    xp = x_prompt
    xs = x_sample.reshape(1, bd, d)
    per_req = lambda a: a.reshape(bd, 1, a.shape[-1])
    prompt_pages = jnp.arange(bp * n_pages, dtype=jnp.int32).reshape(bp, n_pages)
    nsa_p, nsa_s, fkv_p, fkv_s, lf_p, lf_s, win_p, win_s, dil_p, dil_s = [], [], [], [], [], [], [], [], [], []
    for layer in range(depth):
        mp = [mods[layer, :bp, k].reshape(bp, 1, d) for k in range(6)]
        ms = [mods[layer, bp:, k].reshape(1, bd, d) for k in range(6)]
        g = [norm_g[layer, k].reshape(1, d) for k in range(4)]
        i = layer // 2
        if layer % 2 == 0:
            wa = w_in_a[i]
            c_gate = W_NSA_Q + 6 * W_NSA_KV
            c_fox = c_gate + SMALL_GATES
            c_forget = c_fox + 3 * W_FOX_Q
            w_in = jnp.concatenate([wa[:, 0:c_gate], wa[:, c_fox:c_forget], wa[:, c_gate:c_fox], wa[:, c_forget:],
                                    jnp.zeros((d, LANES - SMALL_GATES - FOX_HEADS), F32)], axis=1).astype(BF16)
            sb = jnp.concatenate([nsa_gate_b[i].reshape(-1), fox_f_b[i],
                                  jnp.zeros((LANES - SMALL_GATES - FOX_HEADS,), F32)]).reshape(1, LANES)
            half = CMP_STRIDE * HEAD_DIM
            w1cat = jnp.concatenate([nsa_cmp_w1[i][:, :half], nsa_cmp_w1[i][:, half:]], axis=2).astype(BF16)
            w2 = nsa_cmp_w2[i].astype(BF16)
            pe = nsa_cmp_pe[i].reshape(2, 1, CMP_LEN * HEAD_DIM)
            w_out = w_out_a[i].astype(BF16)

            qn, nsa4, nsabf, win, qf, fkv, fkvbf, small = _proj_even(xp, mp[0], mp[1], g[0], w_in, sb)
            c, ct = _cumsum(small)
            o_f = _fox_prompt(qf, fkvbf, c, ct)
            kc, vc = _compress(nsa4.reshape(bp * n_pages, PAGE_SIZE, 4 * W_NSA_KV), prompt_pages, w1cat, w2, pe)
            o_n = _nsa_prompt(qn, kc, vc, bcmp, nsabf, tab_sel, tab_win, small)
            op_a, op_b = o_n, o_f
            nsa_p.append(nsa4.reshape(bp, s, 4, NSA_KV_HEADS, HEAD_DIM))
            fkv_p.append(fkv.reshape(bp, s, 2, FOX_HEADS, HEAD_DIM))
            lf_p.append(small[:, :, SMALL_GATES:SMALL_GATES + FOX_HEADS])
            n_win = min(NSA_WINDOW, s)
            win_p.append(win[:, s - n_win:].reshape(bp, n_win, 2, NSA_KV_HEADS, HEAD_DIM))

            qn_s, nsa4_s, _, win_new, qf_s, fkv_s_, _, small_s = _proj_even(xs, ms[0], ms[1], g[0], w_in, sb)
            kc_s, vc_s = _compress(cache_nsa_t, page_table, w1cat, w2, pe, layer=i)
            o_cmp, idx = _nsa_select(per_req(qn_s), kc_s, vc_s, bcmp_s, n_blk_pad, n_past_blk)
            o_nsa = _nsa_attend(idx, page_table, per_req(qn_s), cache_nsa_t, i, fsel, win_t, fwin, per_req(nsa4_s),
                                per_req(win_new), o_cmp, per_req(small_s), n_past_blk, 0)
            os_a = o_nsa[:, :, :NSA_GROUP].reshape(1, bd, NSA_HEADS * HEAD_DIM).astype(BF16)
            lf_new = small_s[0, :, SMALL_GATES:SMALL_GATES + FOX_HEADS]
            o_fox = _fox_sample(cache_fox_t, i, logf_t, page_table, _block_diag_queries(qf_s[0], FOX_HEADS),
                                fkv_s_.reshape(bd, 2, FOX_HEADS * HEAD_DIM),
                                jnp.broadcast_to(lf_new[:, :, None], (bd, FOX_HEADS, LANES)))
            os_b = _diag_blocks(o_fox, FOX_HEADS).reshape(1, bd, -1).astype(BF16)
            nsa_s.append(nsa4_s.reshape(bd, 1, 4, NSA_KV_HEADS, HEAD_DIM))
            fkv_s.append(fkv_s_.reshape(bd, 1, 2, FOX_HEADS, HEAD_DIM))
            lf_s.append(small_s[0, :, SMALL_GATES:SMALL_GATES + FOX_HEADS].reshape(bd, 1, FOX_HEADS))
            win_s.append(win_new.reshape(bd, 1, 2, NSA_KV_HEADS, HEAD_DIM))
        else:
            w_in = w_in_c[i].astype(BF16)
            w_out = w_out_c[i].astype(BF16)
            q, kv, kvbf = _proj_odd(xp, mp[0], mp[1], g[0], w_in)
            branches = [_dil_branch_prompt(q, kvbf, tab, dil) for tab, (_, dil) in zip(tab_dil, DIL_BRANCHES)]
            n_dil = min(DIL_BRANCHES[-1][0], s)
            dil_p.append(kv[:, s - n_dil:].reshape(bp, n_dil, 2, DIL_HEADS, HEAD_DIM))
            q_s, kv_s, _ = _proj_odd(xs, ms[0], ms[1], g[0], w_in)
            half = DIL_HEADS // 2
            qbd = _block_diag_queries(q_s.reshape(bd * 2, half * HEAD_DIM), half).reshape(bd, 2, half, half * HEAD_DIM)
            new_kv = jnp.swapaxes(kv_s.reshape(bd, 2, 2, half * HEAD_DIM), 1, 2)
            o_dil = _dil_sample(dil_t, i, qbd, new_kv, tab_dil_s)
            os_a = os_b = _diag_blocks(o_dil.reshape(bd * 2, half, half * HEAD_DIM), half).reshape(1, bd, -1).astype(BF16)
            dil_s.append(kv_s.reshape(bd, 1, 2, DIL_HEADS, HEAD_DIM))
        if layer % 2 == 0:
            xp = _post(op_a, op_b, w_out, xp, mp[2], g[1])
            xs = _post(os_a, os_b, w_out, xs, ms[2], g[1])
        else:
            xp = _post_dil([o for o, _ in branches], [l for _, l in branches], w_out, xp, mp[2], g[1])
            xs = _post(os_a, os_b, w_out, xs, ms[2], g[1], 0, 1)
        w1 = w_mlp1[layer].astype(BF16)
        w2m = w_mlp2[layer].astype(BF16)
        xp = _mlp(xp, mp[3], mp[4], mp[5], g[2], g[3], w1, w2m)
        xs = _mlp(xs, ms[3], ms[4], ms[5], g[2], g[3], w1, w2m)
    return (xp, xs.reshape(bd, 1, d), jnp.stack(nsa_p), jnp.stack(nsa_s), jnp.stack(fkv_p), jnp.stack(fkv_s),
            jnp.stack(lf_p), jnp.stack(lf_s), jnp.stack(win_p), jnp.stack(win_s), jnp.stack(dil_p), jnp.stack(dil_s))
```

```python
import functools
import math

import numpy as np
import jax
import jax.numpy as jnp
from jax import lax
from jax.experimental import pallas as pl
from jax.experimental.pallas import tpu as pltpu

F32 = jnp.float32
BF16 = jnp.bfloat16

HEAD_DIM = 64
NSA_HEADS = 8
NSA_KV_HEADS = 2
NSA_GROUP = NSA_HEADS // NSA_KV_HEADS
FOX_HEADS = 8
DIL_HEADS = 16
CMP_LEN = 32
CMP_STRIDE = 16
CMP_HIDDEN = 4 * HEAD_DIM
SEL_BLOCK = 64
SEL_TOPK = 16
NSA_WINDOW = 512
FORCED_SCORE = 1e9
DIL_BRANCHES = ((128, 1), (512, 4), (2048, 16))
N_BUCKETS = 32
BUCKET_EXACT = 16
BUCKET_MAX_DIST = 2048
NORM_EPS = 1e-6
TINY = 1e-30
PAGE_SIZE = 128

LANES = 128
VMEM_LIMIT_BYTES = 56 * 1024 * 1024

NEG = -1e30
QK_SCALE = HEAD_DIM ** -0.5
TILE = 128
SEL_SHIFT = 6
W_NSA_Q = NSA_HEADS * HEAD_DIM
W_NSA_KV = NSA_KV_HEADS * HEAD_DIM
W_FOX_Q = FOX_HEADS * HEAD_DIM
W_DIL = DIL_HEADS * HEAD_DIM
SMALL_GATES = 3 * NSA_HEADS


def _cparams(*sem):
    return pltpu.CompilerParams(dimension_semantics=sem, vmem_limit_bytes=VMEM_LIMIT_BYTES)


def _bucket_thresholds():
    d = np.arange(0, 2 * BUCKET_MAX_DIST + 1)
    df = np.maximum(d, 1).astype(np.float64)
    ratio = math.log(BUCKET_MAX_DIST / BUCKET_EXACT)
    log_b = BUCKET_EXACT + (np.log(df / BUCKET_EXACT) / ratio * (N_BUCKETS - BUCKET_EXACT)).astype(np.int64)
    bucket = np.where(d < BUCKET_EXACT, d, np.clip(log_b, BUCKET_EXACT, N_BUCKETS - 1))
    return [int(np.argmax(bucket >= b)) for b in range(1, N_BUCKETS)]


BUCKET_THR = _bucket_thresholds()


def _bias_of_distance(d, tab_ref, h):
    val = jnp.full(d.shape, tab_ref[0, h], F32)
    for b in range(1, N_BUCKETS):
        val = jnp.where(d >= BUCKET_THR[b - 1], tab_ref[b, h], val)
    return val


def _toeplitz_kernel(tab_ref, o_ref, *, n_heads, mode, dil):
    r = lax.broadcasted_iota(jnp.int32, (TILE, TILE), 0)
    c = lax.broadcasted_iota(jnp.int32, (TILE, TILE), 1)
    idx = pl.program_id(0)
    if mode == "branch":
        d = idx * TILE + r - c
        ok = (idx < 2) & (d >= 0) & (d <= TILE)
    else:
        d = (idx - 1) * TILE + r - c
        ok = (idx > 0) & (d >= 0)
        if mode == "window":
            ok = ok & (d < NSA_WINDOW)
    dd = jnp.maximum(d, 0) * dil
    for h in range(n_heads):
        o_ref[h, 0] = jnp.where(ok, _bias_of_distance(dd, tab_ref, h), NEG)


def _toeplitz_table(rel_bias, n_heads, n_idx, mode, dil=1):
    return pl.pallas_call(
        functools.partial(_toeplitz_kernel, n_heads=n_heads, mode=mode, dil=dil),
        grid=(n_idx,),
        in_specs=[pl.BlockSpec(memory_space=pltpu.SMEM)],
        out_specs=pl.BlockSpec((n_heads, 1, TILE, TILE), lambda i: (0, i, 0, 0)),
        out_shape=jax.ShapeDtypeStruct((n_heads, n_idx, TILE, TILE), F32),
        compiler_params=_cparams("arbitrary"),
        name="bias_toeplitz_%s%d" % (mode, dil),
    )(rel_bias)


def _cmp_bias_kernel(tab_ref, o_ref, *, tq, n_cmp_pad, t_base):
    t_lo = t_base + pl.program_id(0) * tq
    last = N_BUCKETS - 1
    for c in range(n_cmp_pad // LANES):
        cols = slice(c * LANES, (c + 1) * LANES)
        end_lo = c * LANES * CMP_STRIDE + CMP_LEN - 1
        end_hi = end_lo + (LANES - 1) * CMP_STRIDE
        all_future = t_lo + tq - 1 < end_lo
        all_far = t_lo - end_hi >= BUCKET_THR[last - 1]

        @pl.when(all_future)
        def _():
            for h in range(NSA_HEADS):
                o_ref[h, :, cols] = jnp.full((tq, LANES), NEG, F32)

        @pl.when(all_far)
        def _():
            for h in range(NSA_HEADS):
                o_ref[h, :, cols] = jnp.full((tq, LANES), tab_ref[last, h], F32)

        @pl.when(jnp.logical_not(all_future | all_far))
        def _():
            t = t_lo + lax.broadcasted_iota(jnp.int32, (tq, LANES), 0)
            n = c * LANES + lax.broadcasted_iota(jnp.int32, (tq, LANES), 1)
            d = t - (n * CMP_STRIDE + CMP_LEN - 1)
            dd = jnp.maximum(d, 0)
            for h in range(NSA_HEADS):
                o_ref[h, :, cols] = jnp.where(d >= 0, _bias_of_distance(dd, tab_ref, h), NEG)


def _cmp_bias_table(rel_bias, n_rows, tq, n_cmp_pad, t_base):
    assert n_cmp_pad % LANES == 0
    return pl.pallas_call(
        functools.partial(_cmp_bias_kernel, tq=tq, n_cmp_pad=n_cmp_pad, t_base=t_base),
        grid=(n_rows // tq,),
        in_specs=[pl.BlockSpec(memory_space=pltpu.SMEM)],
        out_specs=pl.BlockSpec((NSA_HEADS, tq, n_cmp_pad), lambda i: (0, i, 0)),
        out_shape=jax.ShapeDtypeStruct((NSA_HEADS, n_rows, n_cmp_pad), F32),
        compiler_params=_cparams("arbitrary"),
        name="bias_cmp",
    )(rel_bias)


def _affine_bias_kernel(tab_ref, o_ref, *, heads, d0, step, limit, union):
    n = o_ref.shape[1]
    d = d0 + step * lax.broadcasted_iota(jnp.int32, (1, n), 1)
    ok = (d >= 0) & (d < limit)
    extra = jnp.zeros((1, n), F32)
    if union:
        cnt = jnp.zeros((1, n), F32)
        for window, dil in DIL_BRANCHES:
            cnt = cnt + jnp.where((d >= 0) & (d <= window) & (jnp.bitwise_and(d, dil - 1) == 0), 1.0, 0.0)
        ok = ok & (cnt > 0.5)
        extra = jnp.log(jnp.maximum(cnt, 1.0))
    dd = jnp.maximum(d, 0)
    for row, h in enumerate(heads):
        if h is None:
            o_ref[row:row + 1, :] = jnp.zeros((1, n), F32)
        else:
            o_ref[row:row + 1, :] = jnp.where(ok, _bias_of_distance(dd, tab_ref, h) + extra, NEG)


def _affine_bias(rel_bias, heads, n, d0, step, limit, union=False):
    return pl.pallas_call(
        functools.partial(_affine_bias_kernel, heads=tuple(heads), d0=d0, step=step, limit=limit, union=union),
        in_specs=[pl.BlockSpec(memory_space=pltpu.SMEM)],
        out_shape=jax.ShapeDtypeStruct((len(heads), n), F32),
        compiler_params=pltpu.CompilerParams(vmem_limit_bytes=VMEM_LIMIT_BYTES),
        name="bias_affine",
    )(rel_bias)


NSA_GROUPED_ROWS = tuple((NSA_GROUP * (row // 8) + row % 8) if row % 8 < NSA_GROUP else None
                         for row in range(8 * NSA_KV_HEADS))


def _ada_kernel(c_ref, w_ref, b_ref, o_ref):
    c = c_ref[...]
    s = (c * jax.nn.sigmoid(c)).astype(BF16)
    o_ref[0] = jnp.dot(s, w_ref[0].astype(BF16), preferred_element_type=F32) + b_ref[0]


def _ada_params(c_all, w_ada, b_ada):
    depth, d, d6 = w_ada.shape
    m = c_all.shape[0]
    return pl.pallas_call(
        _ada_kernel,
        grid=(depth, d6 // d),
        in_specs=[pl.BlockSpec((m, d), lambda l, j: (0, 0)),
                  pl.BlockSpec((1, d, d), lambda l, j: (l, 0, j)),
                  pl.BlockSpec((1, 1, d), lambda l, j: (l, 0, j))],
        out_specs=pl.BlockSpec((1, m, d), lambda l, j: (l, 0, j)),
        out_shape=jax.ShapeDtypeStruct((depth, m, d6), F32),
        compiler_params=_cparams("arbitrary", "arbitrary"),
        name="adaln",
    )(c_all, w_ada, b_ada.reshape(depth, 1, d6))


def _norm_mod(x, g, scale, shift):
    y = x * lax.rsqrt(jnp.mean(x * x, axis=-1, keepdims=True) + NORM_EPS)
    return (y * g) * (1.0 + scale) + shift


def _row_tile(s, want):
    return want if s % want == 0 else s


def _mod_spec(mod, tm):
    if mod.shape[1] == 1:
        return pl.BlockSpec((1, 1, mod.shape[2]), lambda b, i: (b, 0, 0))
    return pl.BlockSpec((1, tm, mod.shape[2]), lambda b, i: (b, i, 0))


def _proj_even_kernel(x_ref, sh_ref, sc_ref, g_ref, w_ref, sb_ref,
                      qn_ref, nsa4_ref, nsabf_ref, win_ref, qf_ref, fkv_ref, fkvbf_ref, small_ref):
    h = _norm_mod(x_ref[0], g_ref[...], sc_ref[0], sh_ref[0]).astype(BF16)
    z = jnp.dot(h, w_ref[...], preferred_element_type=F32)
    c_kv = W_NSA_Q
    c_win = c_kv + 4 * W_NSA_KV
    c_qf = c_kv + 6 * W_NSA_KV
    c_fkv = c_qf + W_FOX_Q
    c_small = c_fkv + 2 * W_FOX_Q
    qn_ref[0] = (z[:, 0:c_kv] * QK_SCALE).astype(BF16)
    nsa4_ref[0] = z[:, c_kv:c_win]
    nsabf_ref[0] = z[:, c_kv:c_qf].astype(BF16)
    win_ref[0] = z[:, c_win:c_qf]
    qf_ref[0] = (z[:, c_qf:c_fkv] * QK_SCALE).astype(BF16)
    fkv_ref[0] = z[:, c_fkv:c_small]
    fkvbf_ref[0] = z[:, c_fkv:c_small].astype(BF16)
    zs = z[:, c_small:c_small + LANES] + sb_ref[...]
    lane = lax.broadcasted_iota(jnp.int32, zs.shape, 1)
    sig = jax.nn.sigmoid(zs)
    lsg = jnp.minimum(zs, 0.0) - jnp.log1p(jnp.exp(-jnp.abs(zs)))
    small_ref[0] = jnp.where(lane < SMALL_GATES, sig, lsg)


def _proj_even(x, shift, scale, g, w, sb):
    bx, s, d = x.shape
    tm = _row_tile(s, 256)
    n = w.shape[1]
    widths = (W_NSA_Q, 4 * W_NSA_KV, 6 * W_NSA_KV, 2 * W_NSA_KV, W_FOX_Q, 2 * W_FOX_Q, 2 * W_FOX_Q, LANES)
    dtypes = (BF16, F32, BF16, F32, BF16, F32, BF16, F32)
    return pl.pallas_call(
        _proj_even_kernel,
        grid=(bx, s // tm),
        in_specs=[pl.BlockSpec((1, tm, d), lambda b, i: (b, i, 0)),
                  _mod_spec(shift, tm), _mod_spec(scale, tm),
                  pl.BlockSpec((1, d), lambda b, i: (0, 0)),
                  pl.BlockSpec((d, n), lambda b, i: (0, 0)),
                  pl.BlockSpec((1, LANES), lambda b, i: (0, 0))],
        out_specs=[pl.BlockSpec((1, tm, wd), lambda b, i: (b, i, 0)) for wd in widths],
        out_shape=[jax.ShapeDtypeStruct((bx, s, wd), dt) for wd, dt in zip(widths, dtypes)],
        compiler_params=_cparams("arbitrary", "arbitrary"),
        name="proj_even",
    )(x, shift, scale, g, w, sb)


def _proj_odd_kernel(x_ref, sh_ref, sc_ref, g_ref, w_ref, q_ref, kv_ref, kvbf_ref):
    h = _norm_mod(x_ref[0], g_ref[...], sc_ref[0], sh_ref[0]).astype(BF16)
    z = jnp.dot(h, w_ref[...], preferred_element_type=F32)
    q_ref[0] = (z[:, 0:W_DIL] * QK_SCALE).astype(BF16)
    kv_ref[0] = z[:, W_DIL:3 * W_DIL]
    kvbf_ref[0] = z[:, W_DIL:3 * W_DIL].astype(BF16)


def _proj_odd(x, shift, scale, g, w):
    bx, s, d = x.shape
    tm = _row_tile(s, 256)
    n = w.shape[1]
    widths = (W_DIL, 2 * W_DIL, 2 * W_DIL)
    dtypes = (BF16, F32, BF16)
    return pl.pallas_call(
        _proj_odd_kernel,
        grid=(bx, s // tm),
        in_specs=[pl.BlockSpec((1, tm, d), lambda b, i: (b, i, 0)),
                  _mod_spec(shift, tm), _mod_spec(scale, tm),
                  pl.BlockSpec((1, d), lambda b, i: (0, 0)),
                  pl.BlockSpec((d, n), lambda b, i: (0, 0))],
        out_specs=[pl.BlockSpec((1, tm, wd), lambda b, i: (b, i, 0)) for wd in widths],
        out_shape=[jax.ShapeDtypeStruct((bx, s, wd), dt) for wd, dt in zip(widths, dtypes)],
        compiler_params=_cparams("arbitrary", "arbitrary"),
        name="proj_odd",
    )(x, shift, scale, g, w)


def _post_kernel(oa_ref, ob_ref, w_ref, x_ref, gate_ref, g_ref, o_ref):
    half = oa_ref.shape[2]
    y = jnp.dot(oa_ref[0], w_ref[0:half, :], preferred_element_type=F32)
    y = y + jnp.dot(ob_ref[0], w_ref[half:, :], preferred_element_type=F32)
    yn = y * lax.rsqrt(jnp.mean(y * y, axis=-1, keepdims=True) + NORM_EPS) * g_ref[...]
    o_ref[0] = x_ref[0] + gate_ref[0] * yn


def _post(o_a, o_b, w_out, x, gate, g, col_a=0, col_b=0):
    bx, s, d = x.shape
    tm = _row_tile(s, 512)
    half = w_out.shape[0] // 2
    return pl.pallas_call(
        _post_kernel,
        grid=(bx, s // tm),
        in_specs=[pl.BlockSpec((1, tm, half), lambda b, i: (b, i, col_a)),
                  pl.BlockSpec((1, tm, half), lambda b, i: (b, i, col_b)),
                  pl.BlockSpec(w_out.shape, lambda b, i: (0, 0)),
                  pl.BlockSpec((1, tm, d), lambda b, i: (b, i, 0)),
                  _mod_spec(gate, tm),
                  pl.BlockSpec((1, d), lambda b, i: (0, 0))],
        out_specs=pl.BlockSpec((1, tm, d), lambda b, i: (b, i, 0)),
        out_shape=jax.ShapeDtypeStruct((bx, s, d), F32),
        compiler_params=_cparams("arbitrary", "arbitrary"),
        name="post",
    )(o_a, o_b, w_out, x, gate, g)


def _mlp_kernel(x_ref, sh_ref, sc_ref, gate_ref, g2_ref, g3_ref, w1_ref, w2_ref, o_ref, h_ref, acc_ref):
    j = pl.program_id(2)

    @pl.when(j == 0)
    def _():
        h_ref[...] = _norm_mod(x_ref[0], g2_ref[...], sc_ref[0], sh_ref[0]).astype(BF16)
        acc_ref[...] = jnp.zeros_like(acc_ref)

    a = jnp.maximum(jnp.dot(h_ref[...], w1_ref[...], preferred_element_type=F32), 0.0)
    acc_ref[...] += jnp.dot((a * a).astype(BF16), w2_ref[...], preferred_element_type=F32)

    @pl.when(j == pl.num_programs(2) - 1)
    def _():
        y = acc_ref[...]
        yn = y * lax.rsqrt(jnp.mean(y * y, axis=-1, keepdims=True) + NORM_EPS) * g3_ref[...]
        o_ref[0] = x_ref[0] + gate_ref[0] * yn


def _mlp(x, shift, scale, gate, g2, g3, w1, w2):
    bx, s, d = x.shape
    f = w1.shape[1]
    tm = _row_tile(s, 1024)
    tf = 1024

    def mod3(mod):
        if mod.shape[1] == 1:
            return pl.BlockSpec((1, 1, d), lambda b, i, j: (b, 0, 0))
        return pl.BlockSpec((1, tm, d), lambda b, i, j: (b, i, 0))

    return pl.pallas_call(
        _mlp_kernel,
        grid=(bx, s // tm, f // tf),
        in_specs=[pl.BlockSpec((1, tm, d), lambda b, i, j: (b, i, 0)),
                  mod3(shift), mod3(scale), mod3(gate),
                  pl.BlockSpec((1, d), lambda b, i, j: (0, 0)),
                  pl.BlockSpec((1, d), lambda b, i, j: (0, 0)),
                  pl.BlockSpec((d, tf), lambda b, i, j: (0, j)),
                  pl.BlockSpec((tf, d), lambda b, i, j: (j, 0))],
        out_specs=pl.BlockSpec((1, tm, d), lambda b, i, j: (b, i, 0)),
        out_shape=jax.ShapeDtypeStruct((bx, s, d), F32),
        scratch_shapes=[pltpu.VMEM((tm, d), BF16), pltpu.VMEM((tm, d), F32)],
        compiler_params=_cparams("arbitrary", "arbitrary", "arbitrary"),
        name="mlp",
    )(x, shift, scale, gate, g2, g3, w1, w2)


def _cumsum_kernel(x_ref, c_ref, ct_ref, carry_ref, *, tc):
    @pl.when(pl.program_id(1) == 0)
    def _():
        carry_ref[...] = jnp.zeros_like(carry_ref)

    r = lax.broadcasted_iota(jnp.int32, (tc, tc), 0)
    c = lax.broadcasted_iota(jnp.int32, (tc, tc), 1)
    tri = jnp.where(c <= r, 1.0, 0.0).astype(F32)
    cs = jnp.dot(tri, x_ref[0], preferred_element_type=F32, precision=lax.Precision.HIGHEST) + carry_ref[...]
    carry_ref[...] = cs[tc - 1:tc, :]
    c_ref[0] = cs
    ct_ref[0] = cs.T[SMALL_GATES:SMALL_GATES + FOX_HEADS, :]


def _cumsum(small):
    bx, s, _ = small.shape
    tc = _row_tile(s, 256)
    return pl.pallas_call(
        functools.partial(_cumsum_kernel, tc=tc),
        grid=(bx, s // tc),
        in_specs=[pl.BlockSpec((1, tc, LANES), lambda b, i: (b, i, 0))],
        out_specs=[pl.BlockSpec((1, tc, LANES), lambda b, i: (b, i, 0)),
                   pl.BlockSpec((1, FOX_HEADS, tc), lambda b, i: (b, 0, i))],
        out_shape=[jax.ShapeDtypeStruct((bx, s, LANES), F32),
                   jax.ShapeDtypeStruct((bx, FOX_HEADS, s), F32)],
        scratch_shapes=[pltpu.VMEM((1, LANES), F32)],
        compiler_params=_cparams("arbitrary", "arbitrary"),
        name="logf_cumsum",
    )(small)


def _nt_dot(a, b):
    return lax.dot_general(a, b, (((1,), (1,)), ((), ())), preferred_element_type=F32)


def _online_update(state, s, v):
    m, l, acc = state
    m_new = jnp.maximum(m, jnp.max(s, axis=1, keepdims=True))
    alpha = jnp.exp(m - m_new)
    p = jnp.exp(s - m_new)
    l = alpha * l + jnp.sum(p, axis=1, keepdims=True)
    acc = alpha * acc + jnp.dot(p.astype(BF16), v, preferred_element_type=F32)
    return m_new, l, acc


def _init_state(tq):
    return (jnp.full((tq, 1), NEG, F32), jnp.zeros((tq, 1), F32), jnp.zeros((tq, LANES), F32))


def _half_masks(tq):
    lane = lax.broadcasted_iota(jnp.int32, (tq, LANES), 1)
    return lane < HEAD_DIM


def _split_heads(q2, lo):
    zero = jnp.zeros_like(q2)
    return jnp.where(lo, q2, zero), jnp.where(lo, zero, q2)


def _fox_kernel(q_ref, k_ref, v_ref, c_ref, ct_ref, o_ref, *, tq, tk, n_pairs):
    grp = pl.program_id(1)
    i = pl.program_id(2)
    lo = _half_masks(tq)
    lane = lax.broadcasted_iota(jnp.int32, (tq, LANES), 1)
    cblk = c_ref[0]
    qs, cqs, heads = [], [], []
    for p in range(n_pairs):
        qs.extend(_split_heads(q_ref[0, :, p * LANES:(p + 1) * LANES], lo))
        for e in range(2):
            head = (grp * n_pairs + p) * 2 + e
            heads.append(head)
            cqs.append(jnp.sum(jnp.where(lane == SMALL_GATES + head, cblk, 0.0), axis=1, keepdims=True))

    def chunk(c, states, masked):
        off = pl.multiple_of(c * tk, tk)
        if masked:
            row = i * tq + lax.broadcasted_iota(jnp.int32, (tq, tk), 0)
            col = off + lax.broadcasted_iota(jnp.int32, (tq, tk), 1)
            ok = col <= row
        out = []
        for n in range(2 * n_pairs):
            p = n // 2
            k = k_ref[0, pl.ds(off, tk), p * LANES:(p + 1) * LANES]
            v = v_ref[0, pl.ds(off, tk), p * LANES:(p + 1) * LANES]
            ck = ct_ref[0, pl.ds(heads[n], 1), pl.ds(off, tk)]
            s = _nt_dot(qs[n], k) + cqs[n] - ck
            if masked:
                s = jnp.where(ok, s, NEG)
            out.append(_online_update(states[n], s, v))
        return tuple(out)

    n_full = (i * tq) // tk
    init = tuple(_init_state(tq) for _ in range(2 * n_pairs))
    states = lax.fori_loop(0, n_full, lambda c, st: chunk(c, st, False), init)
    states = chunk(n_full, states, True)
    outs = []
    for p in range(n_pairs):
        (_, la, acca), (_, lb, accb) = states[2 * p], states[2 * p + 1]
        outs.append(jnp.where(lo, acca / la, accb / lb))
    o_ref[0] = jnp.concatenate(outs, axis=1).astype(o_ref.dtype)


def _fox_prompt(qf, fkvbf, c, ct):
    bx, s, _ = qf.shape
    tq = _row_tile(s, 1024)
    tk = _row_tile(s, 1024)
    n_pairs = 1
    n_grp = FOX_HEADS // 2 // n_pairs
    wd = n_pairs * LANES
    return pl.pallas_call(
        functools.partial(_fox_kernel, tq=tq, tk=tk, n_pairs=n_pairs),
        grid=(bx, n_grp, s // tq),
        in_specs=[pl.BlockSpec((1, tq, wd), lambda b, p, i: (b, i, p)),
                  pl.BlockSpec((1, s, wd), lambda b, p, i: (b, 0, p)),
                  pl.BlockSpec((1, s, wd), lambda b, p, i: (b, 0, n_grp + p)),
                  pl.BlockSpec((1, tq, LANES), lambda b, p, i: (b, i, 0)),
                  pl.BlockSpec((1, FOX_HEADS, s), lambda b, p, i: (b, 0, 0))],
        out_specs=pl.BlockSpec((1, tq, wd), lambda b, p, i: (b, i, p)),
        out_shape=jax.ShapeDtypeStruct((bx, s, FOX_HEADS * HEAD_DIM), BF16),
        compiler_params=_cparams("arbitrary", "arbitrary", "arbitrary"),
        name="fox_prompt",
    )(qf, fkvbf, fkvbf, c, ct)


def _dil_branch_kernel(q_ref, kc_ref, kp_ref, vc_ref, vp_ref, tab_ref, o_ref, lse_ref, *, tq):
    i = pl.program_id(2)
    lo = _half_masks(TILE)
    lane = lax.broadcasted_iota(jnp.int32, (TILE, LANES), 1)
    first_prev = jnp.where(i == 0, 2, 1)
    for a in range(tq // TILE):
        rows = slice(a * TILE, (a + 1) * TILE)
        lse_tile = jnp.zeros((TILE, LANES), F32)
        outs = []
        for p in range(DIL_HEADS // 2):
            cols = slice(p * LANES, (p + 1) * LANES)
            if a == 0:
                k_prev, v_prev, prev_idx = kp_ref[0, :, cols], vp_ref[0, :, cols], first_prev
            else:
                prev = slice((a - 1) * TILE, a * TILE)
                k_prev, v_prev, prev_idx = kc_ref[0, prev, cols], vc_ref[0, prev, cols], 1
            k2 = jnp.concatenate([k_prev, kc_ref[0, rows, cols]], axis=0)
            v2 = jnp.concatenate([v_prev, vc_ref[0, rows, cols]], axis=0)
            pair = []
            for e, qh in enumerate(_split_heads(q_ref[0, rows, cols], lo)):
                h = 2 * p + e
                bias = jnp.concatenate([tab_ref[h, prev_idx], tab_ref[h, 0]], axis=1)
                s = _nt_dot(qh, k2) + bias
                m = jnp.max(s, axis=1, keepdims=True)
                e_s = jnp.exp(s - m)
                l = jnp.sum(e_s, axis=1, keepdims=True)
                pair.append(jnp.dot(e_s.astype(BF16), v2, preferred_element_type=F32) / l)
                lse_tile = jnp.where(lane == h, jnp.log(l) + m, lse_tile)
            outs.append(jnp.where(lo, pair[0], pair[1]))
        o_ref[0, rows, :] = jnp.concatenate(outs, axis=1).astype(o_ref.dtype)
        lse_ref[0, rows, :] = lse_tile


def _dil_branch_prompt(q, kvbf, table, dil):
    bx, s, width = q.shape
    n_rows = s // dil
    tq = _row_tile(n_rows, 256)
    sub = tq // TILE
    qv = q.reshape(bx, n_rows, dil * width)
    kvv = kvbf.reshape(bx, n_rows, dil * 2 * width)
    o, lse = pl.pallas_call(
        functools.partial(_dil_branch_kernel, tq=tq),
        grid=(bx, dil, n_rows // tq),
        in_specs=[pl.BlockSpec((1, tq, width), lambda b, r, i: (b, i, r)),
                  pl.BlockSpec((1, tq, width), lambda b, r, i: (b, i, 2 * r)),
                  pl.BlockSpec((1, TILE, width), lambda b, r, i: (b, jnp.maximum(sub * i - 1, 0), 2 * r)),
                  pl.BlockSpec((1, tq, width), lambda b, r, i: (b, i, 2 * r + 1)),
                  pl.BlockSpec((1, TILE, width), lambda b, r, i: (b, jnp.maximum(sub * i - 1, 0), 2 * r + 1)),
                  pl.BlockSpec(table.shape, lambda b, r, i: (0, 0, 0, 0))],
        out_specs=[pl.BlockSpec((1, tq, width), lambda b, r, i: (b, i, r)),
                   pl.BlockSpec((1, tq, LANES), lambda b, r, i: (b, i, r))],
        out_shape=[jax.ShapeDtypeStruct((bx, n_rows, dil * width), BF16),
                   jax.ShapeDtypeStruct((bx, n_rows, dil * LANES), F32)],
        compiler_params=_cparams("arbitrary", "arbitrary", "arbitrary"),
        name="dilated_branch_prompt",
    )(qv, kvv, kvv, kvv, kvv, table)
    return o.reshape(bx, s, width), lse.reshape(bx, s, LANES)


def _post_dil_kernel(o1_ref, o2_ref, o3_ref, l1_ref, l2_ref, l3_ref, w_ref, x_ref, gate_ref, g_ref, o_ref):
    lses = [r[0] for r in (l1_ref, l2_ref, l3_ref)]
    m = jnp.maximum(jnp.maximum(lses[0], lses[1]), lses[2])
    es = [jnp.exp(l - m) for l in lses]
    tot = es[0] + es[1] + es[2]
    width = o1_ref.shape[2]
    head_of_col = lax.shift_right_arithmetic(lax.broadcasted_iota(jnp.int32, (LANES, width), 1), SEL_SHIFT)
    expand = jnp.where(head_of_col == lax.broadcasted_iota(jnp.int32, (LANES, width), 0), 1.0, 0.0).astype(BF16)
    mix = jnp.zeros((o1_ref.shape[1], width), F32)
    for e, o_ref_j in zip(es, (o1_ref, o2_ref, o3_ref)):
        alpha = jnp.dot((e / tot).astype(BF16), expand, preferred_element_type=F32)
        mix = mix + alpha * o_ref_j[0].astype(F32)
    y = jnp.dot(mix.astype(BF16), w_ref[...], preferred_element_type=F32)
    yn = y * lax.rsqrt(jnp.mean(y * y, axis=-1, keepdims=True) + NORM_EPS) * g_ref[...]
    o_ref[0] = x_ref[0] + gate_ref[0] * yn


def _post_dil(outs, lses, w_out, x, gate, g):
    bx, s, d = x.shape
    tm = _row_tile(s, 256)
    width = w_out.shape[0]
    return pl.pallas_call(
        _post_dil_kernel,
        grid=(bx, s // tm),
        in_specs=([pl.BlockSpec((1, tm, width), lambda b, i: (b, i, 0))] * 3
                  + [pl.BlockSpec((1, tm, LANES), lambda b, i: (b, i, 0))] * 3
                  + [pl.BlockSpec(w_out.shape, lambda b, i: (0, 0)),
                     pl.BlockSpec((1, tm, d), lambda b, i: (b, i, 0)),
                     _mod_spec(gate, tm),
                     pl.BlockSpec((1, d), lambda b, i: (0, 0))]),
        out_specs=pl.BlockSpec((1, tm, d), lambda b, i: (b, i, 0)),
        out_shape=jax.ShapeDtypeStruct((bx, s, d), F32),
        compiler_params=_cparams("arbitrary", "arbitrary"),
        name="post_dilated",
    )(*outs, *lses, w_out, x, gate, g)


def _compress_kernel(pt_ref, *refs, n_pages, pages_per_step, feature_major):
    page_refs = refs[:pages_per_step]
    w1_ref, w2_ref, pe_ref, kc_ref, vc_ref, rows_ref, chunk_ref = refs[pages_per_step:]
    j = pl.program_id(1)
    for p, page_ref in enumerate(page_refs):
        row0 = pl.multiple_of((j * pages_per_step + p) * PAGE_SIZE, PAGE_SIZE)
        if feature_major:
            for kv in range(2):
                rows_ref[kv, pl.ds(row0, PAGE_SIZE), :] = jnp.concatenate(
                    [page_ref[0, 0, kv, g].T for g in range(NSA_KV_HEADS)], axis=1)
        else:
            rows_ref[0, pl.ds(row0, PAGE_SIZE), :] = page_ref[0, :, 0:LANES]
            rows_ref[1, pl.ds(row0, PAGE_SIZE), :] = page_ref[0, :, LANES:2 * LANES]

    @pl.when(j == n_pages // pages_per_step - 1)
    def _():
        n_chunks = n_pages * PAGE_SIZE // CMP_STRIDE
        half = CMP_STRIDE * HEAD_DIM
        for kv, out_ref in ((0, kc_ref), (1, vc_ref)):
            w1 = w1_ref[kv]
            pe_a = jnp.broadcast_to(pe_ref[kv, :, 0:half], (8, half)).astype(BF16)
            pe_b = jnp.broadcast_to(pe_ref[kv, :, half:], (8, half)).astype(BF16)
            pe_term = (jnp.dot(pe_a, w1, preferred_element_type=F32)[0:1, 0:CMP_HIDDEN]
                       + jnp.dot(pe_b, w1, preferred_element_type=F32)[0:1, CMP_HIDDEN:])
            for l in range(CMP_STRIDE):
                both = rows_ref[kv, pl.ds(l, n_chunks, stride=CMP_STRIDE), :].astype(BF16)
                for g in range(NSA_KV_HEADS):
                    chunk_ref[g, :, l * HEAD_DIM:(l + 1) * HEAD_DIM] = both[:, g * HEAD_DIM:(g + 1) * HEAD_DIM]
            outs = []
            for g in range(NSA_KV_HEADS):
                uv = jnp.dot(chunk_ref[g], w1, preferred_element_type=F32)
                pre = uv[:, 0:CMP_HIDDEN] + pltpu.roll(uv[:, CMP_HIDDEN:], n_chunks - 1, 0) + pe_term
                hid = jax.nn.gelu(pre).astype(BF16)
                outs.append(jnp.dot(hid, w2_ref[kv], preferred_element_type=F32))
            out_ref[0] = jnp.concatenate(outs, axis=1)


def _compress(pool, page_table, w1cat, w2, pe, layer=None):
    n_req, n_pages = page_table.shape
    n_chunks = n_pages * PAGE_SIZE // CMP_STRIDE
    width = 2 * NSA_KV_HEADS * HEAD_DIM
    pps = 8 if n_pages % 8 == 0 else 1

    def page_spec(p):
        if layer is None:
            return pl.BlockSpec((1, PAGE_SIZE, width), lambda r, j, pt: (pt[r, j * pps + p], 0, 0))
        return pl.BlockSpec((1, 1, 2, NSA_KV_HEADS, HEAD_DIM, PAGE_SIZE),
                            lambda r, j, pt: (layer, pt[r, j * pps + p], 0, 0, 0, 0))

    grid_spec = pltpu.PrefetchScalarGridSpec(
        num_scalar_prefetch=1,
        grid=(n_req, n_pages // pps),
        in_specs=[page_spec(p) for p in range(pps)] + [
                  pl.BlockSpec(w1cat.shape, lambda r, j, pt: (0, 0, 0)),
                  pl.BlockSpec(w2.shape, lambda r, j, pt: (0, 0, 0)),
                  pl.BlockSpec(pe.shape, lambda r, j, pt: (0, 0, 0))],
        out_specs=[pl.BlockSpec((1, n_chunks, LANES), lambda r, j, pt: (r, 0, 0)),
                   pl.BlockSpec((1, n_chunks, LANES), lambda r, j, pt: (r, 0, 0))],
        scratch_shapes=[pltpu.VMEM((2, n_pages * PAGE_SIZE, LANES), F32),
                        pltpu.VMEM((NSA_KV_HEADS, n_chunks, CMP_STRIDE * HEAD_DIM), BF16)],
    )
    return pl.pallas_call(
        functools.partial(_compress_kernel, n_pages=n_pages, pages_per_step=pps, feature_major=layer is not None),
        grid_spec=grid_spec,
        out_shape=[jax.ShapeDtypeStruct((n_req, n_chunks, LANES), F32)] * 2,
        compiler_params=_cparams("arbitrary", "arbitrary"),
        name="nsa_compress",
    )(page_table, *([pool] * pps), w1cat, w2, pe)


def _top_k_mask(imp, n_top):
    cand = lax.broadcasted_iota(jnp.int32, imp.shape, 0)
    height = imp.shape[0]

    def body(_, carry):
        imp, sel = carry
        m = jnp.max(imp, axis=0, keepdims=True)
        first = jnp.min(jnp.where(imp == m, cand, height), axis=0, keepdims=True)
        pick = (cand == first) & (m > -jnp.inf)
        return jnp.where(cand == first, -jnp.inf, imp), jnp.where(pick, 1.0, sel)

    _, sel = lax.fori_loop(0, n_top, body, (imp, jnp.zeros(imp.shape, F32)))
    return sel


def _nsa_kernel(qn_ref, kc_ref, vc_ref, bcmp_ref, sel_ref, win_ref, tsel_ref, twin_ref, small_ref, o_ref,
                *, tq, tk, n_sel_delta, n_win_tiles):
    i = pl.program_id(1)
    t0 = i * tq
    n_cmp_pad = kc_ref.shape[1]
    n_blk = LANES
    lo = _half_masks(tq)
    lane = lax.broadcasted_iota(jnp.int32, (tq, LANES), 1)
    gates = small_ref[0]
    q_all = qn_ref[0].astype(F32)

    sj = lax.broadcasted_iota(jnp.int32, (n_blk, n_cmp_pad), 0) * SEL_BLOCK
    ci = lax.broadcasted_iota(jnp.int32, (n_blk, n_cmp_pad), 1) * CMP_STRIDE
    cover_t = jnp.where((ci < sj + SEL_BLOCK) & (ci + CMP_LEN > sj), 1.0, 0.0).astype(F32)
    blk_t = lax.broadcasted_iota(jnp.int32, (n_blk, tq), 0)
    cur_t = lax.shift_right_arithmetic(t0 + lax.broadcasted_iota(jnp.int32, (n_blk, tq), 1), SEL_SHIFT)
    forced_t = (blk_t == 0) | (blk_t == cur_t) | (blk_t == cur_t - 1)
    blk_of_key = lax.shift_right_arithmetic(lax.broadcasted_iota(jnp.int32, (tk, n_blk), 0), SEL_SHIFT)
    blk_delta = lax.broadcasted_iota(jnp.int32, (tk, n_blk), 1) - blk_of_key
    sub = tk // TILE

    n_sub = tq // TILE
    group_of = [h // NSA_GROUP for h in range(NSA_HEADS)]

    qs = []
    for h in range(NSA_HEADS):
        g = group_of[h]
        blk = q_all[:, LANES * (h // 2):LANES * (h // 2 + 1)]
        if h % 2 != g:
            blk = pltpu.roll(blk, HEAD_DIM, 1)
        qs.append(jnp.where(lo if g == 0 else jnp.logical_not(lo), blk, 0.0).astype(BF16))

    kcb = kc_ref[0].astype(BF16)
    vcb = vc_ref[0].astype(BF16)
    o_cmp, imps = [], []
    for g in range(NSA_KV_HEADS):
        psum = jnp.zeros((tq, n_cmp_pad), F32)
        for r in range(NSA_GROUP):
            h = g * NSA_GROUP + r
            s = _nt_dot(qs[h], kcb) + bcmp_ref[h]
            m = jnp.max(s, axis=1, keepdims=True)
            e = jnp.exp(s - m)
            scale = jnp.where(m > 0.5 * NEG, 1.0 / jnp.maximum(jnp.sum(e, axis=1, keepdims=True), TINY), 0.0)
            p = e * scale
            psum = psum + p
            o_cmp.append(jnp.dot(p.astype(BF16), vcb, preferred_element_type=F32))
        imp = lax.dot_general(cover_t, psum, (((1,), (1,)), ((), ())), preferred_element_type=F32,
                              precision=lax.Precision.HIGHEST)
        imp = jnp.where(forced_t, FORCED_SCORE, imp)
        imps.append(jnp.where(blk_t <= cur_t, imp, -jnp.inf))

    sel_t = _top_k_mask(jnp.concatenate(imps, axis=1), min(SEL_TOPK, n_blk))
    not_sel = [(1.0 - sel_t[:, g * tq:(g + 1) * tq]).T.astype(BF16) for g in range(NSA_KV_HEADS)]

    q_ext = [jnp.concatenate([qs[h], not_sel[group_of[h]]], axis=1) for h in range(NSA_HEADS)]

    def sel_chunk(c, states):
        off = pl.multiple_of(c * tk, tk)
        k = sel_ref[0, pl.ds(off, tk), 0:LANES]
        v = sel_ref[0, pl.ds(off, tk), LANES:2 * LANES]
        k_ext = jnp.concatenate([k, jnp.where(blk_delta == c * (tk // SEL_BLOCK), NEG, 0.0).astype(BF16)], axis=1)
        out = []
        for h in range(NSA_HEADS):
            bias = jnp.concatenate([jnp.concatenate(
                [tsel_ref[h, jnp.clip(n_sub * i + a - (c * sub + u), -1, n_sel_delta - 1) + 1] for u in range(sub)],
                axis=1) for a in range(n_sub)], axis=0)
            out.append(_online_update(states[h], _nt_dot(q_ext[h], k_ext) + bias, v))
        return tuple(out)

    n_chunks = (t0 + tq - 1) // tk + 1
    sel_states = lax.fori_loop(0, n_chunks, sel_chunk, tuple(_init_state(tq) for _ in range(NSA_HEADS)))

    o_win = [[] for _ in range(NSA_HEADS)]
    for a in range(n_sub):
        j0 = n_sub * i + a
        ks, vs, idxs = [], [], []
        for u in range(n_win_tiles - 1, -1, -1):
            off = pl.multiple_of(jnp.maximum(j0 - u, 0) * TILE, TILE)
            ks.append(win_ref[0, pl.ds(off, TILE), 0:LANES])
            vs.append(win_ref[0, pl.ds(off, TILE), LANES:2 * LANES])
            idxs.append(jnp.where(j0 - u >= 0, u + 1, 0))
        k = jnp.concatenate(ks, axis=0)
        v = jnp.concatenate(vs, axis=0)
        for h in range(NSA_HEADS):
            bias = jnp.concatenate([twin_ref[h, ix] for ix in idxs], axis=1)
            s = _nt_dot(qs[h][a * TILE:(a + 1) * TILE], k) + bias
            e = jnp.exp(s - jnp.max(s, axis=1, keepdims=True))
            l = jnp.sum(e, axis=1, keepdims=True)
            o_win[h].append(jnp.dot(e.astype(BF16), v, preferred_element_type=F32) / l)

    pair_out = [None] * (NSA_HEADS // 2)
    for h in range(NSA_HEADS):
        gc, gs, gw = (jnp.sum(jnp.where(lane == 3 * h + b, gates, 0.0), axis=1, keepdims=True) for b in range(3))
        _, l_s, acc_s = sel_states[h]
        o = gc * o_cmp[h] + gs * (acc_s / l_s) + gw * jnp.concatenate(o_win[h], axis=0)
        if h % 2 != group_of[h]:
            o = pltpu.roll(o, HEAD_DIM, 1)
        prev = pair_out[h // 2]
        keep = lo if h % 2 == 0 else jnp.logical_not(lo)
        pair_out[h // 2] = jnp.where(keep, o, 0.0 if prev is None else prev)

    o_ref[0] = jnp.concatenate(pair_out, axis=1).astype(o_ref.dtype)


def _nsa_prompt(qn, kc, vc, bcmp, nsabf, tsel, twin, small):
    bx, s, _ = qn.shape
    tq = _row_tile(s, 256)
    tk = _row_tile(s, 1024)
    n_cmp_pad = kc.shape[1]
    once = pl.Buffered(1)
    return pl.pallas_call(
        functools.partial(_nsa_kernel, tq=tq, tk=tk, n_sel_delta=tsel.shape[1] - 1, n_win_tiles=twin.shape[1] - 1),
        grid=(bx, s // tq),
        in_specs=[pl.BlockSpec((1, tq, NSA_HEADS * HEAD_DIM), lambda b, i: (b, i, 0)),
                  pl.BlockSpec((1, n_cmp_pad, LANES), lambda b, i: (b, 0, 0)),
                  pl.BlockSpec((1, n_cmp_pad, LANES), lambda b, i: (b, 0, 0)),
                  pl.BlockSpec((NSA_HEADS, tq, n_cmp_pad), lambda b, i: (0, i, 0)),
                  pl.BlockSpec((1, s, 2 * LANES), lambda b, i: (b, 0, 1)),
                  pl.BlockSpec((1, s, 2 * LANES), lambda b, i: (b, 0, 2)),
                  pl.BlockSpec(tsel.shape, lambda b, i: (0, 0, 0, 0), pipeline_mode=once),
                  pl.BlockSpec(twin.shape, lambda b, i: (0, 0, 0, 0), pipeline_mode=once),
                  pl.BlockSpec((1, tq, LANES), lambda b, i: (b, i, 0))],
        out_specs=pl.BlockSpec((1, tq, NSA_HEADS * HEAD_DIM), lambda b, i: (b, i, 0)),
        out_shape=jax.ShapeDtypeStruct((bx, s, NSA_HEADS * HEAD_DIM), BF16),
        compiler_params=_cparams("arbitrary", "arbitrary"),
        name="nsa_prompt",
    )(qn, kc, vc, bcmp, nsabf, nsabf, tsel, twin, small)


def _bf16_round(x):
    return x.astype(BF16).astype(F32)


def _dot3(z, w):
    hi = z.astype(BF16)
    rest = z - hi.astype(F32)
    mid = rest.astype(BF16)
    lo = (rest - mid.astype(F32)).astype(BF16)
    return sum(jnp.dot(part, w, preferred_element_type=F32) for part in (hi, mid, lo))


def _feature_major(cache):
    n = cache.ndim
    return jnp.transpose(cache, tuple(range(n - 4)) + (n - 3, n - 2, n - 1, n - 4))


def _nt_dot_bf16(p, vt):
    return lax.dot_general(p.astype(BF16), vt, (((1,), (1,)), ((), ())), preferred_element_type=F32)


def _fox_sample_kernel(pt_ref, *refs, n_steps, pages_per_step):
    page_refs, lf_refs = refs[:pages_per_step], refs[pages_per_step:2 * pages_per_step]
    q_ref, new_ref, lfn_ref, o_ref, m_ref, l_ref, acc_ref, carry_ref = refs[2 * pages_per_step:]
    j = pl.program_id(1)
    width = FOX_HEADS * HEAD_DIM
    qbd = q_ref[0]

    @pl.when(j == 0)
    def _():
        s_new = jnp.sum(qbd.astype(F32) * _bf16_round(new_ref[0, 0:1]), axis=1, keepdims=True)
        m_ref[...] = jnp.broadcast_to(s_new, m_ref.shape)
        l_ref[...] = jnp.ones_like(l_ref)
        acc_ref[...] = jnp.broadcast_to(_bf16_round(new_ref[0, 1:2]), acc_ref.shape)
        carry_ref[...] = lfn_ref[0]

    u = lax.broadcasted_iota(jnp.int32, (PAGE_SIZE, PAGE_SIZE), 0)
    c = lax.broadcasted_iota(jnp.int32, (PAGE_SIZE, PAGE_SIZE), 1)
    later = jnp.where(u > c, 1.0, 0.0).astype(BF16)
    carry = carry_ref[:, 0:1]
    biases = []
    for lf_ref in lf_refs:
        lf = lf_ref[0, 0]
        biases.append(carry + _dot3(lf, later))
        carry = carry + jnp.sum(lf, axis=1, keepdims=True)
    kt = jnp.concatenate([r[0, 0, 0].reshape(width, PAGE_SIZE).astype(BF16) for r in page_refs], axis=1)
    vt = jnp.concatenate([r[0, 0, 1].reshape(width, PAGE_SIZE).astype(BF16) for r in page_refs], axis=1)
    s = jnp.dot(qbd, kt, preferred_element_type=F32) + jnp.concatenate(biases, axis=1)
    m_old = m_ref[:, 0:1]
    m_new = jnp.maximum(m_old, jnp.max(s, axis=1, keepdims=True))
    alpha = jnp.exp(m_old - m_new)
    p = jnp.exp(s - m_new)
    l_ref[...] = jnp.broadcast_to(alpha * l_ref[:, 0:1] + jnp.sum(p, axis=1, keepdims=True), l_ref.shape)
    acc_ref[...] = alpha * acc_ref[...] + _nt_dot_bf16(p, vt)
    m_ref[...] = jnp.broadcast_to(m_new, m_ref.shape)
    carry_ref[...] = jnp.broadcast_to(carry, carry_ref.shape)

    @pl.when(j == n_steps - 1)
    def _():
        o_ref[0] = acc_ref[...] / l_ref[:, 0:1]


def _block_diag_queries(q, n_heads):
    width = n_heads * HEAD_DIM
    keep = (jnp.arange(width) // HEAD_DIM)[None, :] == jnp.arange(n_heads)[:, None]
    return jnp.where(keep[None], q[:, None, :], jnp.zeros((), q.dtype))


def _diag_blocks(o, n_heads):
    n = o.shape[0]
    o5 = o.reshape(n, n_heads, n_heads, HEAD_DIM)
    return jnp.stack([o5[:, h, h] for h in range(n_heads)], axis=1).reshape(n, n_heads * HEAD_DIM)


def _fox_sample(cache_t, layer, logf_t, page_table, qbd, new_kv, lf_new):
    n_req, n_pages = page_table.shape
    pps = 16 if n_pages % 16 == 0 else 1
    n_steps = n_pages // pps
    width = FOX_HEADS * HEAD_DIM

    def page_spec(shape, p):
        zeros = (0,) * (len(shape) - 2)
        return pl.BlockSpec(shape, lambda r, j, pt: (layer, pt[r, n_pages - 1 - (j * pps + p)]) + zeros)

    grid_spec = pltpu.PrefetchScalarGridSpec(
        num_scalar_prefetch=1,
        grid=(n_req, n_steps),
        in_specs=([page_spec((1, 1, 2, FOX_HEADS, HEAD_DIM, PAGE_SIZE), p) for p in range(pps)]
                  + [page_spec((1, 1, FOX_HEADS, PAGE_SIZE), p) for p in range(pps)]
                  + [pl.BlockSpec((1, FOX_HEADS, width), lambda r, j, pt: (r, 0, 0)),
                     pl.BlockSpec((1, 2, width), lambda r, j, pt: (r, 0, 0)),
                     pl.BlockSpec((1, FOX_HEADS, LANES), lambda r, j, pt: (r, 0, 0))]),
        out_specs=pl.BlockSpec((1, FOX_HEADS, width), lambda r, j, pt: (r, 0, 0)),
        scratch_shapes=[pltpu.VMEM((FOX_HEADS, LANES), F32), pltpu.VMEM((FOX_HEADS, LANES), F32),
                        pltpu.VMEM((FOX_HEADS, width), F32), pltpu.VMEM((FOX_HEADS, LANES), F32)],
    )
    return pl.pallas_call(
        functools.partial(_fox_sample_kernel, n_steps=n_steps, pages_per_step=pps),
        grid_spec=grid_spec,
        out_shape=jax.ShapeDtypeStruct((n_req, FOX_HEADS, width), F32),
        compiler_params=_cparams("arbitrary", "arbitrary"),
        name="fox_sample",
    )(page_table, *([cache_t] * pps), *([logf_t] * pps), qbd, new_kv, lf_new)


def _dil_sample_kernel(q_ref, new_ref, kt_ref, vt_ref, tab_ref, o_ref, *, n_buf):
    heads = q_ref.shape[2]
    qbd = q_ref[0, 0]
    kt = kt_ref[0, 0, 0].reshape(heads * HEAD_DIM, n_buf).astype(BF16)
    vt = vt_ref[0, 0, 0].reshape(heads * HEAD_DIM, n_buf).astype(BF16)
    tab = tab_ref[0]
    s = jnp.dot(qbd, kt, preferred_element_type=F32) + tab[:, 0:n_buf]
    s_new = (jnp.sum(qbd.astype(F32) * _bf16_round(new_ref[0, 0, 0:1]), axis=1, keepdims=True)
             + tab[:, n_buf:n_buf + 1])
    m = jnp.maximum(jnp.max(s, axis=1, keepdims=True), s_new)
    p = jnp.exp(s - m)
    p_new = jnp.exp(s_new - m)
    l = jnp.sum(p, axis=1, keepdims=True) + p_new
    o_ref[0, 0] = (_nt_dot_bf16(p, vt) + _bf16_round(p_new) * _bf16_round(new_ref[0, 0, 1:2])) / l


def _dil_sample(state_t, layer, qbd, new_kv, table):
    n_req = state_t.shape[1]
    n_buf = state_t.shape[-1]
    half = DIL_HEADS // 2
    width = half * HEAD_DIM
    return pl.pallas_call(
        functools.partial(_dil_sample_kernel, n_buf=n_buf),
        grid=(n_req, 2),
        in_specs=[pl.BlockSpec((1, 1, half, width), lambda r, hh: (r, hh, 0, 0)),
                  pl.BlockSpec((1, 1, 2, width), lambda r, hh: (r, hh, 0, 0)),
                  pl.BlockSpec((1, 1, 1, half, HEAD_DIM, n_buf), lambda r, hh: (layer, r, 0, hh, 0, 0)),
                  pl.BlockSpec((1, 1, 1, half, HEAD_DIM, n_buf), lambda r, hh: (layer, r, 1, hh, 0, 0)),
                  pl.BlockSpec((1, half, table.shape[2]), lambda r, hh: (hh, 0, 0))],
        out_specs=pl.BlockSpec((1, 1, half, width), lambda r, hh: (r, hh, 0, 0)),
        out_shape=jax.ShapeDtypeStruct((n_req, 2, half, width), F32),
        compiler_params=_cparams("arbitrary", "arbitrary"),
        name="dilated_sample",
    )(qbd, new_kv, state_t, state_t, table)


def _group_queries(q_row):
    row8 = lax.broadcasted_iota(jnp.int32, (8, LANES), 0)
    lo = lax.broadcasted_iota(jnp.int32, (8, LANES), 1) < HEAD_DIM
    out = []
    for g in range(NSA_KV_HEADS):
        qg = jnp.zeros((8, LANES), F32)
        for r in range(NSA_GROUP):
            h = g * NSA_GROUP + r
            blk = jnp.broadcast_to(q_row[:, LANES * (h // 2):LANES * (h // 2 + 1)], (8, LANES))
            if h % 2 != g:
                blk = pltpu.roll(blk, HEAD_DIM, 1)
            qg = jnp.where((row8 == r) & (lo if g == 0 else jnp.logical_not(lo)), blk, qg)
        out.append(qg)
    return out


def _nsa_cmp_kernel(q_ref, kc_ref, vc_ref, bcmp_ref, oc_ref, psum_ref):
    n_cmp_pad = kc_ref.shape[1]
    kcb = kc_ref[0].astype(BF16)
    vcb = vc_ref[0].astype(BF16)
    row8 = lax.broadcasted_iota(jnp.int32, (8, n_cmp_pad), 0)
    for g, qg in enumerate(_group_queries(q_ref[0].astype(F32))):
        bias = bcmp_ref[8 * g:8 * g + 8, :]
        ok = (bias > 0.5 * NEG) & (row8 < NSA_GROUP)
        s = jnp.where(ok, _nt_dot(qg.astype(BF16), kcb) + bias, NEG)
        m = jnp.max(s, axis=1, keepdims=True)
        e = jnp.where(ok, jnp.exp(s - m), 0.0)
        p = e / jnp.maximum(jnp.sum(e, axis=1, keepdims=True), TINY)
        oc_ref[0, g] = jnp.dot(p.astype(BF16), vcb, preferred_element_type=F32)
        psum_ref[0, g] = jnp.broadcast_to(jnp.sum(p, axis=0, keepdims=True), (8, n_cmp_pad))


def _nsa_select(q, kc, vc, bcmp_row, n_blk_pad, cur):
    n_req = q.shape[0]
    n_cmp_pad = kc.shape[1]
    o_cmp, psum = pl.pallas_call(
        _nsa_cmp_kernel,
        grid=(n_req,),
        in_specs=[pl.BlockSpec((1, 1, NSA_HEADS * HEAD_DIM), lambda r: (r, 0, 0)),
                  pl.BlockSpec((1, n_cmp_pad, LANES), lambda r: (r, 0, 0)),
                  pl.BlockSpec((1, n_cmp_pad, LANES), lambda r: (r, 0, 0)),
                  pl.BlockSpec(bcmp_row.shape, lambda r: (0, 0))],
        out_specs=[pl.BlockSpec((1, NSA_KV_HEADS, 8, LANES), lambda r: (r, 0, 0, 0)),
                   pl.BlockSpec((1, NSA_KV_HEADS, 8, n_cmp_pad), lambda r: (r, 0, 0, 0))],
        out_shape=[jax.ShapeDtypeStruct((n_req, NSA_KV_HEADS, 8, LANES), F32),
                   jax.ShapeDtypeStruct((n_req, NSA_KV_HEADS, 8, n_cmp_pad), F32)],
        compiler_params=_cparams("arbitrary"),
        name="nsa_sample_cmp",
    )(q, kc, vc, bcmp_row)
    n_pairs = n_req * NSA_KV_HEADS
    assert n_pairs <= LANES
    pairs = jnp.pad(psum[:, :, 0, :].reshape(n_pairs, n_cmp_pad), ((0, LANES - n_pairs), (0, 0)))
    idx = pl.pallas_call(
        functools.partial(_nsa_rank_kernel, n_blk_pad=n_blk_pad, cur=cur),
        out_shape=jax.ShapeDtypeStruct((SEL_TOPK, LANES), jnp.int32),
        compiler_params=pltpu.CompilerParams(vmem_limit_bytes=VMEM_LIMIT_BYTES),
        name="nsa_sample_topk",
    )(pairs)
    return o_cmp, idx[:, :n_pairs].T.reshape(n_req, NSA_KV_HEADS * SEL_TOPK)


def _nsa_rank_kernel(psum_ref, idx_ref, *, n_blk_pad, cur):
    n_cmp_pad = psum_ref.shape[1]
    sj = lax.broadcasted_iota(jnp.int32, (n_blk_pad, n_cmp_pad), 0) * SEL_BLOCK
    ci = lax.broadcasted_iota(jnp.int32, (n_blk_pad, n_cmp_pad), 1) * CMP_STRIDE
    cover_t = jnp.where((ci < sj + SEL_BLOCK) & (ci + CMP_LEN > sj), 1.0, 0.0).astype(F32)
    imp = lax.dot_general(cover_t, psum_ref[...], (((1,), (1,)), ((), ())), preferred_element_type=F32,
                          precision=lax.Precision.HIGHEST)
    blk = lax.broadcasted_iota(jnp.int32, imp.shape, 0)
    imp = jnp.where((blk == 0) | (blk == cur) | (blk == cur - 1), FORCED_SCORE, imp)
    imp = jnp.where(blk <= cur, imp, -jnp.inf)
    pick_row = lax.broadcasted_iota(jnp.int32, idx_ref.shape, 0)

    def body(it, carry):
        imp, idx = carry
        top = jnp.max(imp, axis=0, keepdims=True)
        first = jnp.min(jnp.where(imp == top, blk, n_blk_pad), axis=0, keepdims=True)
        idx = jnp.where(pick_row == it, jnp.where(top > -jnp.inf, first, -1), idx)
        return jnp.where(blk == first, -jnp.inf, imp), idx

    _, idx = lax.fori_loop(0, SEL_TOPK, body, (imp, jnp.full(idx_ref.shape, -1, jnp.int32)))
    idx_ref[...] = idx


def _nsa_attend_kernel(idx_ref, pt_ref, q_ref, *refs, n_past_blk, new_lane, n_win):
    n_picks = NSA_KV_HEADS * SEL_TOPK
    blk_refs = refs[:n_picks]
    fsel_ref, wbuf_ref, fwin_ref, nsel_ref, nwin_ref, oc_ref, small_ref, o_ref = refs[n_picks:]
    r_idx = pl.program_id(0)
    qgs = [qg[:, g * HEAD_DIM:(g + 1) * HEAD_DIM] for g, qg in enumerate(_group_queries(q_ref[0].astype(F32)))]
    row8 = lax.broadcasted_iota(jnp.int32, (8, LANES), 0)
    lane = lax.broadcasted_iota(jnp.int32, (8, LANES), 1)
    n_pages_past = n_past_blk // 2
    gates = small_ref[0]

    for g, qg in enumerate(qgs):
        grp = slice(g * HEAD_DIM, (g + 1) * HEAD_DIM)
        rows = slice(8 * g, 8 * g + 8)
        qb = qg.astype(BF16)
        ksel_new = _bf16_round(nsel_ref[0][:, 2 * LANES:3 * LANES][:, grp])
        vsel_new = _bf16_round(nsel_ref[0][:, 3 * LANES:4 * LANES][:, grp])
        kwin_new = _bf16_round(nwin_ref[0][:, 0:LANES][:, grp])
        vwin_new = _bf16_round(nwin_ref[0][:, LANES:2 * LANES][:, grp])

        kts, vts, biases = [], [], []
        for k in range(SEL_TOPK):
            blk_ref = blk_refs[g * SEL_TOPK + k]
            b = idx_ref[r_idx, g * SEL_TOPK + k]
            valid = (b >= 0) & (b < n_past_blk)
            page = jnp.clip(lax.shift_right_arithmetic(b, 1), 0, n_pages_past - 1)
            in_blk = lax.shift_right_arithmetic(lane, SEL_SHIFT) == jnp.bitwise_and(b, 1)
            kts.append(blk_ref[0, 0, 0, g].astype(BF16))
            vts.append(blk_ref[0, 0, 1, g].astype(BF16))
            biases.append(jnp.where(valid & in_blk, fsel_ref[page, rows, :], NEG))
        s = jnp.dot(qb, jnp.concatenate(kts, axis=1), preferred_element_type=F32) + jnp.concatenate(biases, axis=1)
        s_new = jnp.sum(qg * ksel_new, axis=1, keepdims=True) + fsel_ref[n_pages_past, rows, new_lane:new_lane + 1]
        m = jnp.maximum(jnp.max(s, axis=1, keepdims=True), s_new)
        p = jnp.exp(s - m)
        p_new = jnp.exp(s_new - m)
        l = jnp.sum(p, axis=1, keepdims=True) + p_new
        o_sel = (_nt_dot_bf16(p, jnp.concatenate(vts, axis=1)) + _bf16_round(p_new) * vsel_new) / l

        kw_t = wbuf_ref[0, 0, 0, g].astype(BF16)
        vw_t = wbuf_ref[0, 0, 1, g].astype(BF16)
        s = jnp.dot(qb, kw_t, preferred_element_type=F32) + fwin_ref[rows, 0:n_win]
        s_wn = jnp.sum(qg * kwin_new, axis=1, keepdims=True) + fwin_ref[rows, n_win:n_win + 1]
        m = jnp.maximum(jnp.max(s, axis=1, keepdims=True), s_wn)
        p = jnp.exp(s - m)
        p_new = jnp.exp(s_wn - m)
        l = jnp.sum(p, axis=1, keepdims=True) + p_new
        o_win = (_nt_dot_bf16(p, vw_t) + _bf16_round(p_new) * vwin_new) / l

        gate = []
        for b in range(3):
            col = jnp.zeros((8, 1), F32)
            for r in range(NSA_GROUP):
                lane_i = 3 * (g * NSA_GROUP + r) + b
                col = jnp.where(row8[:, 0:1] == r, gates[:, lane_i:lane_i + 1], col)
            gate.append(col)
        o_cmp = oc_ref[0, g][:, g * HEAD_DIM:(g + 1) * HEAD_DIM]
        o_ref[0, g] = gate[0] * o_cmp + gate[1] * o_sel + gate[2] * o_win


def _nsa_attend(idx, page_table, q, cache_t, layer, fsel, win_t, fwin, new_sel, new_win, o_cmp, small, n_past_blk,
                new_lane):
    n_req = q.shape[0]
    n_win = win_t.shape[-1]

    def blk_map(pick):
        def index(r, idx_ref, pt_ref):
            b = idx_ref[r, pick]
            b = jnp.where((b >= 0) & (b < n_past_blk), b, 0)
            return layer, pt_ref[r, lax.shift_right_arithmetic(b, 1)], 1, 0, 0, 0
        return index

    n_picks = NSA_KV_HEADS * SEL_TOPK
    const = lambda *shape: (lambda r, idx_ref, pt_ref: shape)
    per_req3 = lambda r, idx_ref, pt_ref: (r, 0, 0)
    per_req4 = lambda r, idx_ref, pt_ref: (r, 0, 0, 0)
    page_blk = (1, 1, 2, NSA_KV_HEADS, HEAD_DIM, PAGE_SIZE)
    grid_spec = pltpu.PrefetchScalarGridSpec(
        num_scalar_prefetch=2,
        grid=(n_req,),
        in_specs=([pl.BlockSpec((1, 1, NSA_HEADS * HEAD_DIM), per_req3)]
                  + [pl.BlockSpec(page_blk, blk_map(pick)) for pick in range(n_picks)]
                  + [pl.BlockSpec(fsel.shape, const(0, 0, 0)),
                     pl.BlockSpec((1, 1, 2, NSA_KV_HEADS, HEAD_DIM, n_win),
                                  lambda r, idx_ref, pt_ref: (layer, r, 0, 0, 0, 0)),
                     pl.BlockSpec(fwin.shape, const(0, 0)),
                     pl.BlockSpec((1, 1, 4 * LANES), per_req3),
                     pl.BlockSpec((1, 1, 2 * LANES), per_req3),
                     pl.BlockSpec((1, NSA_KV_HEADS, 8, LANES), per_req4),
                     pl.BlockSpec((1, 1, LANES), per_req3)]),
        out_specs=pl.BlockSpec((1, NSA_KV_HEADS, 8, HEAD_DIM), per_req4),
    )
    return pl.pallas_call(
        functools.partial(_nsa_attend_kernel, n_past_blk=n_past_blk, new_lane=new_lane, n_win=n_win),
        grid_spec=grid_spec,
        out_shape=jax.ShapeDtypeStruct((n_req, NSA_KV_HEADS, 8, HEAD_DIM), F32),
        compiler_params=_cparams("arbitrary"),
        name="nsa_sample_attend",
    )(idx, page_table, q, *([cache_t] * n_picks), fsel, win_t, fwin, new_sel, new_win, o_cmp, small)


def kernel(x_prompt, x_sample, cache_nsa_kv, cache_fox_kv, cache_fox_logf, state_nsa_win_kv, state_dil_kv, page_table,
           c_prompt, c_sample, rel_bias, norm_g, w_ada, b_ada, w_in_a, nsa_gate_b, fox_f_b, nsa_cmp_w1, nsa_cmp_w2,
           nsa_cmp_pe, w_out_a, w_in_c, w_out_c, w_mlp1, w_mlp2):
    bp, s, d = x_prompt.shape
    bd = x_sample.shape[0]
    depth = w_ada.shape[0]
    n_pages = s // PAGE_SIZE
    n_cmp_pad = s // CMP_STRIDE

    mods = _ada_params(jnp.concatenate([c_prompt, c_sample], axis=0), w_ada, b_ada).reshape(depth, bp + bd, 6, d)
    tab_sel = _toeplitz_table(rel_bias, NSA_HEADS, 15, "causal")
    tab_win = _toeplitz_table(rel_bias, NSA_HEADS, NSA_WINDOW // TILE + 2, "window")
    assert all(window // dil == TILE for window, dil in DIL_BRANCHES)
    tab_dil = [_toeplitz_table(rel_bias, DIL_HEADS, 3, "branch", dil) for _, dil in DIL_BRANCHES]
    bcmp = _cmp_bias_table(rel_bias, s, TILE, n_cmp_pad, 0)

    past_len = page_table.shape[1] * PAGE_SIZE
    assert past_len % SEL_BLOCK == 0
    n_past_blk = past_len // SEL_BLOCK
    n_blk_pad = -(-(n_past_blk + 1) // LANES) * LANES
    far = 1 << 30
    bcmp_s = _affine_bias(rel_bias, NSA_GROUPED_ROWS, past_len // CMP_STRIDE, past_len - (CMP_LEN - 1), -CMP_STRIDE, far)
    assert past_len % PAGE_SIZE == 0
    n_pages_s = past_len // PAGE_SIZE
    fsel = _affine_bias(rel_bias, NSA_GROUPED_ROWS, (n_pages_s + 1) * PAGE_SIZE, past_len, -1, far)
    fsel = jnp.swapaxes(fsel.reshape(len(NSA_GROUPED_ROWS), n_pages_s + 1, PAGE_SIZE), 0, 1)
    n_win_buf = state_nsa_win_kv.shape[2]
    fwin = _affine_bias(rel_bias, NSA_GROUPED_ROWS, n_win_buf + LANES, n_win_buf, -1, NSA_WINDOW)
    n_dil_buf = state_dil_kv.shape[2]
    tab_dil_s = _affine_bias(rel_bias, range(DIL_HEADS), n_dil_buf + LANES, n_dil_buf, -1, far, union=True)
    tab_dil_s = tab_dil_s.reshape(2, DIL_HEADS // 2, n_dil_buf + LANES)
    cache_nsa_t = _feature_major(cache_nsa_kv)
    cache_fox_t = _feature_major(cache_fox_kv)
    logf_t = jnp.swapaxes(cache_fox_logf, 2, 3)
    win_t = _feature_major(state_nsa_win_kv)
    dil_t = _feature_major(state_dil_kv)

    xp = x_prompt
    xs = x_sample.reshape(1, bd, d)
    per_req = lambda a: a.reshape(bd, 1, a.shape[-1])
    prompt_pages = jnp.arange(bp * n_pages, dtype=jnp.int32).reshape(bp, n_pages)
    nsa_p, nsa_s, fkv_p, fkv_s, lf_p, lf_s, win_p, win_s, dil_p, dil_s = [], [], [], [], [], [], [], [], [], []
    for layer in range(depth):
        mp = [mods[layer, :bp, k].reshape(bp, 1, d) for k in range(6)]
        ms = [mods[layer, bp:, k].reshape(1, bd, d) for k in range(6)]
        g = [norm_g[layer, k].reshape(1, d) for k in range(4)]
        i = layer // 2
        if layer % 2 == 0:
            wa = w_in_a[i]
            c_gate = W_NSA_Q + 6 * W_NSA_KV
            c_fox = c_gate + SMALL_GATES
            c_forget = c_fox + 3 * W_FOX_Q
            w_in = jnp.concatenate([wa[:, 0:c_gate], wa[:, c_fox:c_forget], wa[:, c_gate:c_fox], wa[:, c_forget:],
                                    jnp.zeros((d, LANES - SMALL_GATES - FOX_HEADS), F32)], axis=1).astype(BF16)
            sb = jnp.concatenate([nsa_gate_b[i].reshape(-1), fox_f_b[i],
                                  jnp.zeros((LANES - SMALL_GATES - FOX_HEADS,), F32)]).reshape(1, LANES)
            half = CMP_STRIDE * HEAD_DIM
            w1cat = jnp.concatenate([nsa_cmp_w1[i][:, :half], nsa_cmp_w1[i][:, half:]], axis=2).astype(BF16)
            w2 = nsa_cmp_w2[i].astype(BF16)
            pe = nsa_cmp_pe[i].reshape(2, 1, CMP_LEN * HEAD_DIM)
            w_out = w_out_a[i].astype(BF16)

            qn, nsa4, nsabf, win, qf, fkv, fkvbf, small = _proj_even(xp, mp[0], mp[1], g[0], w_in, sb)
            c, ct = _cumsum(small)
            o_f = _fox_prompt(qf, fkvbf, c, ct)
            kc, vc = _compress(nsa4.reshape(bp * n_pages, PAGE_SIZE, 4 * W_NSA_KV), prompt_pages, w1cat, w2, pe)
            o_n = _nsa_prompt(qn, kc, vc, bcmp, nsabf, tab_sel, tab_win, small)
            op_a, op_b = o_n, o_f
            nsa_p.append(nsa4.reshape(bp, s, 4, NSA_KV_HEADS, HEAD_DIM))
            fkv_p.append(fkv.reshape(bp, s, 2, FOX_HEADS, HEAD_DIM))
            lf_p.append(small[:, :, SMALL_GATES:SMALL_GATES + FOX_HEADS])
            n_win = min(NSA_WINDOW, s)
            win_p.append(win[:, s - n_win:].reshape(bp, n_win, 2, NSA_KV_HEADS, HEAD_DIM))

            qn_s, nsa4_s, _, win_new, qf_s, fkv_s_, _, small_s = _proj_even(xs, ms[0], ms[1], g[0], w_in, sb)
            kc_s, vc_s = _compress(cache_nsa_t, page_table, w1cat, w2, pe, layer=i)
            o_cmp, idx = _nsa_select(per_req(qn_s), kc_s, vc_s, bcmp_s, n_blk_pad, n_past_blk)
            o_nsa = _nsa_attend(idx, page_table, per_req(qn_s), cache_nsa_t, i, fsel, win_t, fwin, per_req(nsa4_s),
                                per_req(win_new), o_cmp, per_req(small_s), n_past_blk, 0)
            os_a = o_nsa[:, :, :NSA_GROUP].reshape(1, bd, NSA_HEADS * HEAD_DIM).astype(BF16)
            lf_new = small_s[0, :, SMALL_GATES:SMALL_GATES + FOX_HEADS]
            o_fox = _fox_sample(cache_fox_t, i, logf_t, page_table, _block_diag_queries(qf_s[0], FOX_HEADS),
                                fkv_s_.reshape(bd, 2, FOX_HEADS * HEAD_DIM),
                                jnp.broadcast_to(lf_new[:, :, None], (bd, FOX_HEADS, LANES)))
            os_b = _diag_blocks(o_fox, FOX_HEADS).reshape(1, bd, -1).astype(BF16)
            nsa_s.append(nsa4_s.reshape(bd, 1, 4, NSA_KV_HEADS, HEAD_DIM))
            fkv_s.append(fkv_s_.reshape(bd, 1, 2, FOX_HEADS, HEAD_DIM))
            lf_s.append(small_s[0, :, SMALL_GATES:SMALL_GATES + FOX_HEADS].reshape(bd, 1, FOX_HEADS))
            win_s.append(win_new.reshape(bd, 1, 2, NSA_KV_HEADS, HEAD_DIM))
        else:
            w_in = w_in_c[i].astype(BF16)
            w_out = w_out_c[i].astype(BF16)
            q, kv, kvbf = _proj_odd(xp, mp[0], mp[1], g[0], w_in)
            branches = [_dil_branch_prompt(q, kvbf, tab, dil) for tab, (_, dil) in zip(tab_dil, DIL_BRANCHES)]
            n_dil = min(DIL_BRANCHES[-1][0], s)
            dil_p.append(kv[:, s - n_dil:].reshape(bp, n_dil, 2, DIL_HEADS, HEAD_DIM))
            q_s, kv_s, _ = _proj_odd(xs, ms[0], ms[1], g[0], w_in)
            half = DIL_HEADS // 2
            qbd = _block_diag_queries(q_s.reshape(bd * 2, half * HEAD_DIM), half).reshape(bd, 2, half, half * HEAD_DIM)
            new_kv = jnp.swapaxes(kv_s.reshape(bd, 2, 2, half * HEAD_DIM), 1, 2)
            o_dil = _dil_sample(dil_t, i, qbd, new_kv, tab_dil_s)
            os_a = os_b = _diag_blocks(o_dil.reshape(bd * 2, half, half * HEAD_DIM), half).reshape(1, bd, -1).astype(BF16)
            dil_s.append(kv_s.reshape(bd, 1, 2, DIL_HEADS, HEAD_DIM))
        if layer % 2 == 0:
            xp = _post(op_a, op_b, w_out, xp, mp[2], g[1])
            xs = _post(os_a, os_b, w_out, xs, ms[2], g[1])
        else:
            xp = _post_dil([o for o, _ in branches], [l for _, l in branches], w_out, xp, mp[2], g[1])
            xs = _post(os_a, os_b, w_out, xs, ms[2], g[1], 0, 1)
        w1 = w_mlp1[layer].astype(BF16)
        w2m = w_mlp2[layer].astype(BF16)
        xp = _mlp(xp, mp[3], mp[4], mp[5], g[2], g[3], w1, w2m)
        xs = _mlp(xs, ms[3], ms[4], ms[5], g[2], g[3], w1, w2m)
    return (xp, xs.reshape(bd, 1, d), jnp.stack(nsa_p), jnp.stack(nsa_s), jnp.stack(fkv_p), jnp.stack(fkv_s),
            jnp.stack(lf_p), jnp.stack(lf_s), jnp.stack(win_p), jnp.stack(win_s), jnp.stack(dil_p), jnp.stack(dil_s))
```

```python
import functools
import math

import numpy as np
import jax
import jax.numpy as jnp
from jax import lax
from jax.experimental import pallas as pl
from jax.experimental.pallas import tpu as pltpu

F32 = jnp.float32
BF16 = jnp.bfloat16

HEAD_DIM = 64
NSA_HEADS = 8
NSA_KV_HEADS = 2
NSA_GROUP = NSA_HEADS // NSA_KV_HEADS
FOX_HEADS = 8
DIL_HEADS = 16
CMP_LEN = 32
CMP_STRIDE = 16
CMP_HIDDEN = 4 * HEAD_DIM
SEL_BLOCK = 64
SEL_TOPK = 16
NSA_WINDOW = 512
FORCED_SCORE = 1e9
DIL_BRANCHES = ((128, 1), (512, 4), (2048, 16))
N_BUCKETS = 32
BUCKET_EXACT = 16
BUCKET_MAX_DIST = 2048
NORM_EPS = 1e-6
TINY = 1e-30
PAGE_SIZE = 128

LANES = 128
VMEM_LIMIT_BYTES = 56 * 1024 * 1024

NEG = -1e30
QK_SCALE = HEAD_DIM ** -0.5
TILE = 128
SEL_SHIFT = 6
W_NSA_Q = NSA_HEADS * HEAD_DIM
W_NSA_KV = NSA_KV_HEADS * HEAD_DIM
W_FOX_Q = FOX_HEADS * HEAD_DIM
W_DIL = DIL_HEADS * HEAD_DIM
SMALL_GATES = 3 * NSA_HEADS


def _cparams(*sem):
    return pltpu.CompilerParams(dimension_semantics=sem, vmem_limit_bytes=VMEM_LIMIT_BYTES)


def _bucket_thresholds():
    d = np.arange(0, 2 * BUCKET_MAX_DIST + 1)
    df = np.maximum(d, 1).astype(np.float64)
    ratio = math.log(BUCKET_MAX_DIST / BUCKET_EXACT)
    log_b = BUCKET_EXACT + (np.log(df / BUCKET_EXACT) / ratio * (N_BUCKETS - BUCKET_EXACT)).astype(np.int64)
    bucket = np.where(d < BUCKET_EXACT, d, np.clip(log_b, BUCKET_EXACT, N_BUCKETS - 1))
    return [int(np.argmax(bucket >= b)) for b in range(1, N_BUCKETS)]


BUCKET_THR = _bucket_thresholds()


def _bias_of_distance(d, tab_ref, h):
    val = jnp.full(d.shape, tab_ref[0, h], F32)
    for b in range(1, N_BUCKETS):
        val = jnp.where(d >= BUCKET_THR[b - 1], tab_ref[b, h], val)
    return val


def _toeplitz_kernel(tab_ref, o_ref, *, n_heads, mode, dil):
    r = lax.broadcasted_iota(jnp.int32, (TILE, TILE), 0)
    c = lax.broadcasted_iota(jnp.int32, (TILE, TILE), 1)
    idx = pl.program_id(0)
    if mode == "branch":
        d = idx * TILE + r - c
        ok = (idx < 2) & (d >= 0) & (d <= TILE)
    else:
        d = (idx - 1) * TILE + r - c
        ok = (idx > 0) & (d >= 0)
        if mode == "window":
            ok = ok & (d < NSA_WINDOW)
    dd = jnp.maximum(d, 0) * dil
    for h in range(n_heads):
        o_ref[h, 0] = jnp.where(ok, _bias_of_distance(dd, tab_ref, h), NEG)


def _toeplitz_table(rel_bias, n_heads, n_idx, mode, dil=1):
    return pl.pallas_call(
        functools.partial(_toeplitz_kernel, n_heads=n_heads, mode=mode, dil=dil),
        grid=(n_idx,),
        in_specs=[pl.BlockSpec(memory_space=pltpu.SMEM)],
        out_specs=pl.BlockSpec((n_heads, 1, TILE, TILE), lambda i: (0, i, 0, 0)),
        out_shape=jax.ShapeDtypeStruct((n_heads, n_idx, TILE, TILE), F32),
        compiler_params=_cparams("arbitrary"),
        name="bias_toeplitz_%s%d" % (mode, dil),
    )(rel_bias)


def _cmp_bias_kernel(tab_ref, o_ref, *, tq, n_cmp_pad, t_base):
    t_lo = t_base + pl.program_id(0) * tq
    last = N_BUCKETS - 1
    for c in range(n_cmp_pad // LANES):
        cols = slice(c * LANES, (c + 1) * LANES)
        end_lo = c * LANES * CMP_STRIDE + CMP_LEN - 1
        end_hi = end_lo + (LANES - 1) * CMP_STRIDE
        all_future = t_lo + tq - 1 < end_lo
        all_far = t_lo - end_hi >= BUCKET_THR[last - 1]

        @pl.when(all_future)
        def _():
            for h in range(NSA_HEADS):
                o_ref[h, :, cols] = jnp.full((tq, LANES), NEG, F32)

        @pl.when(all_far)
        def _():
            for h in range(NSA_HEADS):
                o_ref[h, :, cols] = jnp.full((tq, LANES), tab_ref[last, h], F32)

        @pl.when(jnp.logical_not(all_future | all_far))
        def _():
            t = t_lo + lax.broadcasted_iota(jnp.int32, (tq, LANES), 0)
            n = c * LANES + lax.broadcasted_iota(jnp.int32, (tq, LANES), 1)
            d = t - (n * CMP_STRIDE + CMP_LEN - 1)
            dd = jnp.maximum(d, 0)
            for h in range(NSA_HEADS):
                o_ref[h, :, cols] = jnp.where(d >= 0, _bias_of_distance(dd, tab_ref, h), NEG)


def _cmp_bias_table(rel_bias, n_rows, tq, n_cmp_pad, t_base):
    assert n_cmp_pad % LANES == 0
    return pl.pallas_call(
        functools.partial(_cmp_bias_kernel, tq=tq, n_cmp_pad=n_cmp_pad, t_base=t_base),
        grid=(n_rows // tq,),
        in_specs=[pl.BlockSpec(memory_space=pltpu.SMEM)],
        out_specs=pl.BlockSpec((NSA_HEADS, tq, n_cmp_pad), lambda i: (0, i, 0)),
        out_shape=jax.ShapeDtypeStruct((NSA_HEADS, n_rows, n_cmp_pad), F32),
        compiler_params=_cparams("arbitrary"),
        name="bias_cmp",
    )(rel_bias)


def _affine_bias_kernel(tab_ref, o_ref, *, heads, d0, step, limit, union):
    n = o_ref.shape[1]
    d = d0 + step * lax.broadcasted_iota(jnp.int32, (1, n), 1)
    ok = (d >= 0) & (d < limit)
    extra = jnp.zeros((1, n), F32)
    if union:
        cnt = jnp.zeros((1, n), F32)
        for window, dil in DIL_BRANCHES:
            cnt = cnt + jnp.where((d >= 0) & (d <= window) & (jnp.bitwise_and(d, dil - 1) == 0), 1.0, 0.0)
        ok = ok & (cnt > 0.5)
        extra = jnp.log(jnp.maximum(cnt, 1.0))
    dd = jnp.maximum(d, 0)
    for row, h in enumerate(heads):
        if h is None:
            o_ref[row:row + 1, :] = jnp.zeros((1, n), F32)
        else:
            o_ref[row:row + 1, :] = jnp.where(ok, _bias_of_distance(dd, tab_ref, h) + extra, NEG)


def _affine_bias(rel_bias, heads, n, d0, step, limit, union=False):
    return pl.pallas_call(
        functools.partial(_affine_bias_kernel, heads=tuple(heads), d0=d0, step=step, limit=limit, union=union),
        in_specs=[pl.BlockSpec(memory_space=pltpu.SMEM)],
        out_shape=jax.ShapeDtypeStruct((len(heads), n), F32),
        compiler_params=pltpu.CompilerParams(vmem_limit_bytes=VMEM_LIMIT_BYTES),
        name="bias_affine",
    )(rel_bias)


NSA_GROUPED_ROWS = tuple((NSA_GROUP * (row // 8) + row % 8) if row % 8 < NSA_GROUP else None
                         for row in range(8 * NSA_KV_HEADS))


def _ada_kernel(c_ref, w_ref, b_ref, o_ref):
    c = c_ref[...]
    s = (c * jax.nn.sigmoid(c)).astype(BF16)
    o_ref[0] = jnp.dot(s, w_ref[0].astype(BF16), preferred_element_type=F32) + b_ref[0]


def _ada_params(c_all, w_ada, b_ada):
    depth, d, d6 = w_ada.shape
    m = c_all.shape[0]
    return pl.pallas_call(
        _ada_kernel,
        grid=(depth, d6 // d),
        in_specs=[pl.BlockSpec((m, d), lambda l, j: (0, 0)),
                  pl.BlockSpec((1, d, d), lambda l, j: (l, 0, j)),
                  pl.BlockSpec((1, 1, d), lambda l, j: (l, 0, j))],
        out_specs=pl.BlockSpec((1, m, d), lambda l, j: (l, 0, j)),
        out_shape=jax.ShapeDtypeStruct((depth, m, d6), F32),
        compiler_params=_cparams("arbitrary", "arbitrary"),
        name="adaln",
    )(c_all, w_ada, b_ada.reshape(depth, 1, d6))


def _norm_mod(x, g, scale, shift):
    y = x * lax.rsqrt(jnp.mean(x * x, axis=-1, keepdims=True) + NORM_EPS)
    return (y * g) * (1.0 + scale) + shift


def _row_tile(s, want):
    return want if s % want == 0 else s


def _mod_spec(mod, tm):
    if mod.shape[1] == 1:
        return pl.BlockSpec((1, 1, mod.shape[2]), lambda b, i: (b, 0, 0))
    return pl.BlockSpec((1, tm, mod.shape[2]), lambda b, i: (b, i, 0))


def _proj_even_kernel(x_ref, sh_ref, sc_ref, g_ref, w_ref, sb_ref,
                      qn_ref, nsa4_ref, nsabf_ref, win_ref, qf_ref, fkv_ref, fkvbf_ref, small_ref):
    h = _norm_mod(x_ref[0], g_ref[...], sc_ref[0], sh_ref[0]).astype(BF16)
    z = jnp.dot(h, w_ref[...], preferred_element_type=F32)
    c_kv = W_NSA_Q
    c_win = c_kv + 4 * W_NSA_KV
    c_qf = c_kv + 6 * W_NSA_KV
    c_fkv = c_qf + W_FOX_Q
    c_small = c_fkv + 2 * W_FOX_Q
    qn_ref[0] = (z[:, 0:c_kv] * QK_SCALE).astype(BF16)
    nsa4_ref[0] = z[:, c_kv:c_win]
    nsabf_ref[0] = z[:, c_kv:c_qf].astype(BF16)
    win_ref[0] = z[:, c_win:c_qf]
    qf_ref[0] = (z[:, c_qf:c_fkv] * QK_SCALE).astype(BF16)
    fkv_ref[0] = z[:, c_fkv:c_small]
    fkvbf_ref[0] = z[:, c_fkv:c_small].astype(BF16)
    zs = z[:, c_small:c_small + LANES] + sb_ref[...]
    lane = lax.broadcasted_iota(jnp.int32, zs.shape, 1)
    sig = jax.nn.sigmoid(zs)
    lsg = jnp.minimum(zs, 0.0) - jnp.log1p(jnp.exp(-jnp.abs(zs)))
    small_ref[0] = jnp.where(lane < SMALL_GATES, sig, lsg)


def _proj_even(x, shift, scale, g, w, sb):
    bx, s, d = x.shape
    tm = _row_tile(s, 256)
    n = w.shape[1]
    widths = (W_NSA_Q, 4 * W_NSA_KV, 6 * W_NSA_KV, 2 * W_NSA_KV, W_FOX_Q, 2 * W_FOX_Q, 2 * W_FOX_Q, LANES)
    dtypes = (BF16, F32, BF16, F32, BF16, F32, BF16, F32)
    return pl.pallas_call(
        _proj_even_kernel,
        grid=(bx, s // tm),
        in_specs=[pl.BlockSpec((1, tm, d), lambda b, i: (b, i, 0)),
                  _mod_spec(shift, tm), _mod_spec(scale, tm),
                  pl.BlockSpec((1, d), lambda b, i: (0, 0)),
                  pl.BlockSpec((d, n), lambda b, i: (0, 0)),
                  pl.BlockSpec((1, LANES), lambda b, i: (0, 0))],
        out_specs=[pl.BlockSpec((1, tm, wd), lambda b, i: (b, i, 0)) for wd in widths],
        out_shape=[jax.ShapeDtypeStruct((bx, s, wd), dt) for wd, dt in zip(widths, dtypes)],
        compiler_params=_cparams("arbitrary", "arbitrary"),
        name="proj_even",
    )(x, shift, scale, g, w, sb)


def _proj_odd_kernel(x_ref, sh_ref, sc_ref, g_ref, w_ref, q_ref, kv_ref, kvbf_ref):
    h = _norm_mod(x_ref[0], g_ref[...], sc_ref[0], sh_ref[0]).astype(BF16)
    z = jnp.dot(h, w_ref[...], preferred_element_type=F32)
    q_ref[0] = (z[:, 0:W_DIL] * QK_SCALE).astype(BF16)
    kv_ref[0] = z[:, W_DIL:3 * W_DIL]
    kvbf_ref[0] = z[:, W_DIL:3 * W_DIL].astype(BF16)


def _proj_odd(x, shift, scale, g, w):
    bx, s, d = x.shape
    tm = _row_tile(s, 256)
    n = w.shape[1]
    widths = (W_DIL, 2 * W_DIL, 2 * W_DIL)
    dtypes = (BF16, F32, BF16)
    return pl.pallas_call(
        _proj_odd_kernel,
        grid=(bx, s // tm),
        in_specs=[pl.BlockSpec((1, tm, d), lambda b, i: (b, i, 0)),
                  _mod_spec(shift, tm), _mod_spec(scale, tm),
                  pl.BlockSpec((1, d), lambda b, i: (0, 0)),
                  pl.BlockSpec((d, n), lambda b, i: (0, 0))],
        out_specs=[pl.BlockSpec((1, tm, wd), lambda b, i: (b, i, 0)) for wd in widths],
        out_shape=[jax.ShapeDtypeStruct((bx, s, wd), dt) for wd, dt in zip(widths, dtypes)],
        compiler_params=_cparams("arbitrary", "arbitrary"),
        name="proj_odd",
    )(x, shift, scale, g, w)


def _post_kernel(oa_ref, ob_ref, w_ref, x_ref, gate_ref, g_ref, o_ref):
    half = oa_ref.shape[2]
    y = jnp.dot(oa_ref[0], w_ref[0:half, :], preferred_element_type=F32)
    y = y + jnp.dot(ob_ref[0], w_ref[half:, :], preferred_element_type=F32)
    yn = y * lax.rsqrt(jnp.mean(y * y, axis=-1, keepdims=True) + NORM_EPS) * g_ref[...]
    o_ref[0] = x_ref[0] + gate_ref[0] * yn


def _post(o_a, o_b, w_out, x, gate, g, col_a=0, col_b=0):
    bx, s, d = x.shape
    tm = _row_tile(s, 512)
    half = w_out.shape[0] // 2
    return pl.pallas_call(
        _post_kernel,
        grid=(bx, s // tm),
        in_specs=[pl.BlockSpec((1, tm, half), lambda b, i: (b, i, col_a)),
                  pl.BlockSpec((1, tm, half), lambda b, i: (b, i, col_b)),
                  pl.BlockSpec(w_out.shape, lambda b, i: (0, 0)),
                  pl.BlockSpec((1, tm, d), lambda b, i: (b, i, 0)),
                  _mod_spec(gate, tm),
                  pl.BlockSpec((1, d), lambda b, i: (0, 0))],
        out_specs=pl.BlockSpec((1, tm, d), lambda b, i: (b, i, 0)),
        out_shape=jax.ShapeDtypeStruct((bx, s, d), F32),
        compiler_params=_cparams("arbitrary", "arbitrary"),
        name="post",
    )(o_a, o_b, w_out, x, gate, g)


def _mlp_kernel(x_ref, sh_ref, sc_ref, gate_ref, g2_ref, g3_ref, w1_ref, w2_ref, o_ref, h_ref, acc_ref):
    j = pl.program_id(2)

    @pl.when(j == 0)
    def _():
        h_ref[...] = _norm_mod(x_ref[0], g2_ref[...], sc_ref[0], sh_ref[0]).astype(BF16)
        acc_ref[...] = jnp.zeros_like(acc_ref)

    a = jnp.maximum(jnp.dot(h_ref[...], w1_ref[...], preferred_element_type=F32), 0.0)
    acc_ref[...] += jnp.dot((a * a).astype(BF16), w2_ref[...], preferred_element_type=F32)

    @pl.when(j == pl.num_programs(2) - 1)
    def _():
        y = acc_ref[...]
        yn = y * lax.rsqrt(jnp.mean(y * y, axis=-1, keepdims=True) + NORM_EPS) * g3_ref[...]
        o_ref[0] = x_ref[0] + gate_ref[0] * yn


def _mlp(x, shift, scale, gate, g2, g3, w1, w2):
    bx, s, d = x.shape
    f = w1.shape[1]
    tm = _row_tile(s, 1024)
    tf = 1024

    def mod3(mod):
        if mod.shape[1] == 1:
            return pl.BlockSpec((1, 1, d), lambda b, i, j: (b, 0, 0))
        return pl.BlockSpec((1, tm, d), lambda b, i, j: (b, i, 0))

    return pl.pallas_call(
        _mlp_kernel,
        grid=(bx, s // tm, f // tf),
        in_specs=[pl.BlockSpec((1, tm, d), lambda b, i, j: (b, i, 0)),
                  mod3(shift), mod3(scale), mod3(gate),
                  pl.BlockSpec((1, d), lambda b, i, j: (0, 0)),
                  pl.BlockSpec((1, d), lambda b, i, j: (0, 0)),
                  pl.BlockSpec((d, tf), lambda b, i, j: (0, j)),
                  pl.BlockSpec((tf, d), lambda b, i, j: (j, 0))],
        out_specs=pl.BlockSpec((1, tm, d), lambda b, i, j: (b, i, 0)),
        out_shape=jax.ShapeDtypeStruct((bx, s, d), F32),
        scratch_shapes=[pltpu.VMEM((tm, d), BF16), pltpu.VMEM((tm, d), F32)],
        compiler_params=_cparams("parallel", "parallel", "arbitrary"),
        name="mlp",
    )(x, shift, scale, gate, g2, g3, w1, w2)


def _cumsum_kernel(x_ref, c_ref, ct_ref, carry_ref, *, tc):
    @pl.when(pl.program_id(1) == 0)
    def _():
        carry_ref[...] = jnp.zeros_like(carry_ref)

    r = lax.broadcasted_iota(jnp.int32, (tc, tc), 0)
    c = lax.broadcasted_iota(jnp.int32, (tc, tc), 1)
    tri = jnp.where(c <= r, 1.0, 0.0).astype(F32)
    cs = jnp.dot(tri, x_ref[0], preferred_element_type=F32, precision=lax.Precision.HIGHEST) + carry_ref[...]
    carry_ref[...] = cs[tc - 1:tc, :]
    c_ref[0] = cs
    ct_ref[0] = cs.T[SMALL_GATES:SMALL_GATES + FOX_HEADS, :]


def _cumsum(small):
    bx, s, _ = small.shape
    tc = _row_tile(s, 256)
    return pl.pallas_call(
        functools.partial(_cumsum_kernel, tc=tc),
        grid=(bx, s // tc),
        in_specs=[pl.BlockSpec((1, tc, LANES), lambda b, i: (b, i, 0))],
        out_specs=[pl.BlockSpec((1, tc, LANES), lambda b, i: (b, i, 0)),
                   pl.BlockSpec((1, FOX_HEADS, tc), lambda b, i: (b, 0, i))],
        out_shape=[jax.ShapeDtypeStruct((bx, s, LANES), F32),
                   jax.ShapeDtypeStruct((bx, FOX_HEADS, s), F32)],
        scratch_shapes=[pltpu.VMEM((1, LANES), F32)],
        compiler_params=_cparams("arbitrary", "arbitrary"),
        name="logf_cumsum",
    )(small)


def _nt_dot(a, b):
    return lax.dot_general(a, b, (((1,), (1,)), ((), ())), preferred_element_type=F32)


def _online_update(state, s, v):
    m, l, acc = state
    m_new = jnp.maximum(m, jnp.max(s, axis=1, keepdims=True))
    alpha = jnp.exp(m - m_new)
    p = jnp.exp(s - m_new)
    l = alpha * l + jnp.sum(p, axis=1, keepdims=True)
    acc = alpha * acc + jnp.dot(p.astype(BF16), v, preferred_element_type=F32)
    return m_new, l, acc


def _init_state(tq):
    return (jnp.full((tq, 1), NEG, F32), jnp.zeros((tq, 1), F32), jnp.zeros((tq, LANES), F32))


def _half_masks(tq):
    lane = lax.broadcasted_iota(jnp.int32, (tq, LANES), 1)
    return lane < HEAD_DIM


def _split_heads(q2, lo):
    zero = jnp.zeros_like(q2)
    return jnp.where(lo, q2, zero), jnp.where(lo, zero, q2)


def _fox_kernel(q_ref, k_ref, v_ref, c_ref, ct_ref, o_ref, *, tq, tk, n_pairs):
    grp = pl.program_id(1)
    i = pl.program_id(2)
    lo = _half_masks(tq)
    lane = lax.broadcasted_iota(jnp.int32, (tq, LANES), 1)
    cblk = c_ref[0]
    qs, cqs, heads = [], [], []
    for p in range(n_pairs):
        qs.extend(_split_heads(q_ref[0, :, p * LANES:(p + 1) * LANES], lo))
        for e in range(2):
            head = (grp * n_pairs + p) * 2 + e
            heads.append(head)
            cqs.append(jnp.sum(jnp.where(lane == SMALL_GATES + head, cblk, 0.0), axis=1, keepdims=True))

    def chunk(c, states, masked):
        off = pl.multiple_of(c * tk, tk)
        if masked:
            row = i * tq + lax.broadcasted_iota(jnp.int32, (tq, tk), 0)
            col = off + lax.broadcasted_iota(jnp.int32, (tq, tk), 1)
            ok = col <= row
        out = []
        for n in range(2 * n_pairs):
            p = n // 2
            k = k_ref[0, pl.ds(off, tk), p * LANES:(p + 1) * LANES]
            v = v_ref[0, pl.ds(off, tk), p * LANES:(p + 1) * LANES]
            ck = ct_ref[0, pl.ds(heads[n], 1), pl.ds(off, tk)]
            s = _nt_dot(qs[n], k) + cqs[n] - ck
            if masked:
                s = jnp.where(ok, s, NEG)
            out.append(_online_update(states[n], s, v))
        return tuple(out)

    n_full = (i * tq) // tk
    init = tuple(_init_state(tq) for _ in range(2 * n_pairs))
    states = lax.fori_loop(0, n_full, lambda c, st: chunk(c, st, False), init)
    states = chunk(n_full, states, True)
    outs = []
    for p in range(n_pairs):
        (_, la, acca), (_, lb, accb) = states[2 * p], states[2 * p + 1]
        outs.append(jnp.where(lo, acca / la, accb / lb))
    o_ref[0] = jnp.concatenate(outs, axis=1).astype(o_ref.dtype)


def _fox_prompt(qf, fkvbf, c, ct):
    bx, s, _ = qf.shape
    tq = _row_tile(s, 1024)
    tk = _row_tile(s, 1024)
    n_pairs = 1
    n_grp = FOX_HEADS // 2 // n_pairs
    wd = n_pairs * LANES
    return pl.pallas_call(
        functools.partial(_fox_kernel, tq=tq, tk=tk, n_pairs=n_pairs),
        grid=(bx, n_grp, s // tq),
        in_specs=[pl.BlockSpec((1, tq, wd), lambda b, p, i: (b, i, p)),
                  pl.BlockSpec((1, s, wd), lambda b, p, i: (b, 0, p)),
                  pl.BlockSpec((1, s, wd), lambda b, p, i: (b, 0, n_grp + p)),
                  pl.BlockSpec((1, tq, LANES), lambda b, p, i: (b, i, 0)),
                  pl.BlockSpec((1, FOX_HEADS, s), lambda b, p, i: (b, 0, 0))],
        out_specs=pl.BlockSpec((1, tq, wd), lambda b, p, i: (b, i, p)),
        out_shape=jax.ShapeDtypeStruct((bx, s, FOX_HEADS * HEAD_DIM), BF16),
        compiler_params=_cparams("parallel", "parallel", "parallel"),
        name="fox_prompt",
    )(qf, fkvbf, fkvbf, c, ct)


def _dil_branch_kernel(q_ref, kc_ref, kp_ref, vc_ref, vp_ref, tab_ref, o_ref, lse_ref, *, tq):
    i = pl.program_id(2)
    lo = _half_masks(TILE)
    lane = lax.broadcasted_iota(jnp.int32, (TILE, LANES), 1)
    first_prev = jnp.where(i == 0, 2, 1)
    for a in range(tq // TILE):
        rows = slice(a * TILE, (a + 1) * TILE)
        lse_tile = jnp.zeros((TILE, LANES), F32)
        outs = []
        for p in range(DIL_HEADS // 2):
            cols = slice(p * LANES, (p + 1) * LANES)
            if a == 0:
                k_prev, v_prev, prev_idx = kp_ref[0, :, cols], vp_ref[0, :, cols], first_prev
            else:
                prev = slice((a - 1) * TILE, a * TILE)
                k_prev, v_prev, prev_idx = kc_ref[0, prev, cols], vc_ref[0, prev, cols], 1
            k2 = jnp.concatenate([k_prev, kc_ref[0, rows, cols]], axis=0)
            v2 = jnp.concatenate([v_prev, vc_ref[0, rows, cols]], axis=0)
            pair = []
            for e, qh in enumerate(_split_heads(q_ref[0, rows, cols], lo)):
                h = 2 * p + e
                bias = jnp.concatenate([tab_ref[h, prev_idx], tab_ref[h, 0]], axis=1)
                s = _nt_dot(qh, k2) + bias
                m = jnp.max(s, axis=1, keepdims=True)
                e_s = jnp.exp(s - m)
                l = jnp.sum(e_s, axis=1, keepdims=True)
                pair.append(jnp.dot(e_s.astype(BF16), v2, preferred_element_type=F32) / l)
                lse_tile = jnp.where(lane == h, jnp.log(l) + m, lse_tile)
            outs.append(jnp.where(lo, pair[0], pair[1]))
        o_ref[0, rows, :] = jnp.concatenate(outs, axis=1).astype(o_ref.dtype)
        lse_ref[0, rows, :] = lse_tile


def _dil_branch_prompt(q, kvbf, table, dil):
    bx, s, width = q.shape
    n_rows = s // dil
    tq = _row_tile(n_rows, 256)
    sub = tq // TILE
    qv = q.reshape(bx, n_rows, dil * width)
    kvv = kvbf.reshape(bx, n_rows, dil * 2 * width)
    o, lse = pl.pallas_call(
        functools.partial(_dil_branch_kernel, tq=tq),
        grid=(bx, dil, n_rows // tq),
        in_specs=[pl.BlockSpec((1, tq, width), lambda b, r, i: (b, i, r)),
                  pl.BlockSpec((1, tq, width), lambda b, r, i: (b, i, 2 * r)),
                  pl.BlockSpec((1, TILE, width), lambda b, r, i: (b, jnp.maximum(sub * i - 1, 0), 2 * r)),
                  pl.BlockSpec((1, tq, width), lambda b, r, i: (b, i, 2 * r + 1)),
                  pl.BlockSpec((1, TILE, width), lambda b, r, i: (b, jnp.maximum(sub * i - 1, 0), 2 * r + 1)),
                  pl.BlockSpec(table.shape, lambda b, r, i: (0, 0, 0, 0))],
        out_specs=[pl.BlockSpec((1, tq, width), lambda b, r, i: (b, i, r)),
                   pl.BlockSpec((1, tq, LANES), lambda b, r, i: (b, i, r))],
        out_shape=[jax.ShapeDtypeStruct((bx, n_rows, dil * width), BF16),
                   jax.ShapeDtypeStruct((bx, n_rows, dil * LANES), F32)],
        compiler_params=_cparams("parallel", "parallel", "parallel"),
        name="dilated_branch_prompt",
    )(qv, kvv, kvv, kvv, kvv, table)
    return o.reshape(bx, s, width), lse.reshape(bx, s, LANES)


def _post_dil_kernel(o1_ref, o2_ref, o3_ref, l1_ref, l2_ref, l3_ref, w_ref, x_ref, gate_ref, g_ref, o_ref):
    lses = [r[0] for r in (l1_ref, l2_ref, l3_ref)]
    m = jnp.maximum(jnp.maximum(lses[0], lses[1]), lses[2])
    es = [jnp.exp(l - m) for l in lses]
    tot = es[0] + es[1] + es[2]
    width = o1_ref.shape[2]
    head_of_col = lax.shift_right_arithmetic(lax.broadcasted_iota(jnp.int32, (LANES, width), 1), SEL_SHIFT)
    expand = jnp.where(head_of_col == lax.broadcasted_iota(jnp.int32, (LANES, width), 0), 1.0, 0.0).astype(BF16)
    mix = jnp.zeros((o1_ref.shape[1], width), F32)
    for e, o_ref_j in zip(es, (o1_ref, o2_ref, o3_ref)):
        alpha = jnp.dot((e / tot).astype(BF16), expand, preferred_element_type=F32)
        mix = mix + alpha * o_ref_j[0].astype(F32)
    y = jnp.dot(mix.astype(BF16), w_ref[...], preferred_element_type=F32)
    yn = y * lax.rsqrt(jnp.mean(y * y, axis=-1, keepdims=True) + NORM_EPS) * g_ref[...]
    o_ref[0] = x_ref[0] + gate_ref[0] * yn


def _post_dil(outs, lses, w_out, x, gate, g):
    bx, s, d = x.shape
    tm = _row_tile(s, 256)
    width = w_out.shape[0]
    return pl.pallas_call(
        _post_dil_kernel,
        grid=(bx, s // tm),
        in_specs=([pl.BlockSpec((1, tm, width), lambda b, i: (b, i, 0))] * 3
                  + [pl.BlockSpec((1, tm, LANES), lambda b, i: (b, i, 0))] * 3
                  + [pl.BlockSpec(w_out.shape, lambda b, i: (0, 0)),
                     pl.BlockSpec((1, tm, d), lambda b, i: (b, i, 0)),
                     _mod_spec(gate, tm),
                     pl.BlockSpec((1, d), lambda b, i: (0, 0))]),
        out_specs=pl.BlockSpec((1, tm, d), lambda b, i: (b, i, 0)),
        out_shape=jax.ShapeDtypeStruct((bx, s, d), F32),
        compiler_params=_cparams("arbitrary", "arbitrary"),
        name="post_dilated",
    )(*outs, *lses, w_out, x, gate, g)


def _compress_kernel(pt_ref, *refs, n_pages, pages_per_step, feature_major):
    page_refs = refs[:pages_per_step]
    w1_ref, w2_ref, pe_ref, kc_ref, vc_ref, rows_ref, chunk_ref = refs[pages_per_step:]
    j = pl.program_id(1)
    for p, page_ref in enumerate(page_refs):
        row0 = pl.multiple_of((j * pages_per_step + p) * PAGE_SIZE, PAGE_SIZE)
        if feature_major:
            for kv in range(2):
                rows_ref[kv, pl.ds(row0, PAGE_SIZE), :] = jnp.concatenate(
                    [page_ref[0, 0, kv, g].T for g in range(NSA_KV_HEADS)], axis=1)
        else:
            rows_ref[0, pl.ds(row0, PAGE_SIZE), :] = page_ref[0, :, 0:LANES]
            rows_ref[1, pl.ds(row0, PAGE_SIZE), :] = page_ref[0, :, LANES:2 * LANES]

    @pl.when(j == n_pages // pages_per_step - 1)
    def _():
        n_chunks = n_pages * PAGE_SIZE // CMP_STRIDE
        half = CMP_STRIDE * HEAD_DIM
        for kv, out_ref in ((0, kc_ref), (1, vc_ref)):
            w1 = w1_ref[kv]
            pe_a = jnp.broadcast_to(pe_ref[kv, :, 0:half], (8, half)).astype(BF16)
            pe_b = jnp.broadcast_to(pe_ref[kv, :, half:], (8, half)).astype(BF16)
            pe_term = (jnp.dot(pe_a, w1, preferred_element_type=F32)[0:1, 0:CMP_HIDDEN]
                       + jnp.dot(pe_b, w1, preferred_element_type=F32)[0:1, CMP_HIDDEN:])
            for l in range(CMP_STRIDE):
                both = rows_ref[kv, pl.ds(l, n_chunks, stride=CMP_STRIDE), :].astype(BF16)
                for g in range(NSA_KV_HEADS):
                    chunk_ref[g, :, l * HEAD_DIM:(l + 1) * HEAD_DIM] = both[:, g * HEAD_DIM:(g + 1) * HEAD_DIM]
            outs = []
            for g in range(NSA_KV_HEADS):
                uv = jnp.dot(chunk_ref[g], w1, preferred_element_type=F32)
                pre = uv[:, 0:CMP_HIDDEN] + pltpu.roll(uv[:, CMP_HIDDEN:], n_chunks - 1, 0) + pe_term
                hid = jax.nn.gelu(pre).astype(BF16)
                outs.append(jnp.dot(hid, w2_ref[kv], preferred_element_type=F32))
            out_ref[0] = jnp.concatenate(outs, axis=1)


def _compress(pool, page_table, w1cat, w2, pe, layer=None):
    n_req, n_pages = page_table.shape
    n_chunks = n_pages * PAGE_SIZE // CMP_STRIDE
    width = 2 * NSA_KV_HEADS * HEAD_DIM
    pps = 8 if n_pages % 8 == 0 else 1

    def page_spec(p):
        if layer is None:
            return pl.BlockSpec((1, PAGE_SIZE, width), lambda r, j, pt: (pt[r, j * pps + p], 0, 0))
        return pl.BlockSpec((1, 1, 2, NSA_KV_HEADS, HEAD_DIM, PAGE_SIZE),
                            lambda r, j, pt: (layer, pt[r, j * pps + p], 0, 0, 0, 0))

    grid_spec = pltpu.PrefetchScalarGridSpec(
        num_scalar_prefetch=1,
        grid=(n_req, n_pages // pps),
        in_specs=[page_spec(p) for p in range(pps)] + [
                  pl.BlockSpec(w1cat.shape, lambda r, j, pt: (0, 0, 0)),
                  pl.BlockSpec(w2.shape, lambda r, j, pt: (0, 0, 0)),
                  pl.BlockSpec(pe.shape, lambda r, j, pt: (0, 0, 0))],
        out_specs=[pl.BlockSpec((1, n_chunks, LANES), lambda r, j, pt: (r, 0, 0)),
                   pl.BlockSpec((1, n_chunks, LANES), lambda r, j, pt: (r, 0, 0))],
        scratch_shapes=[pltpu.VMEM((2, n_pages * PAGE_SIZE, LANES), F32),
                        pltpu.VMEM((NSA_KV_HEADS, n_chunks, CMP_STRIDE * HEAD_DIM), BF16)],
    )
    return pl.pallas_call(
        functools.partial(_compress_kernel, n_pages=n_pages, pages_per_step=pps, feature_major=layer is not None),
        grid_spec=grid_spec,
        out_shape=[jax.ShapeDtypeStruct((n_req, n_chunks, LANES), F32)] * 2,
        compiler_params=_cparams("arbitrary", "arbitrary"),
        name="nsa_compress",
    )(page_table, *([pool] * pps), w1cat, w2, pe)


def _top_k_mask(imp, n_top):
    cand = lax.broadcasted_iota(jnp.int32, imp.shape, 0)
    height = imp.shape[0]

    def body(_, carry):
        imp, sel = carry
        m = jnp.max(imp, axis=0, keepdims=True)
        first = jnp.min(jnp.where(imp == m, cand, height), axis=0, keepdims=True)
        pick = (cand == first) & (m > -jnp.inf)
        return jnp.where(cand == first, -jnp.inf, imp), jnp.where(pick, 1.0, sel)

    _, sel = lax.fori_loop(0, n_top, body, (imp, jnp.zeros(imp.shape, F32)))
    return sel


def _nsa_kernel(qn_ref, kc_ref, vc_ref, bcmp_ref, sel_ref, win_ref, tsel_ref, twin_ref, small_ref, o_ref,
                *, tq, tk, n_sel_delta, n_win_tiles):
    i = pl.program_id(1)
    t0 = i * tq
    n_cmp_pad = kc_ref.shape[1]
    n_blk = LANES
    lo = _half_masks(tq)
    lane = lax.broadcasted_iota(jnp.int32, (tq, LANES), 1)
    gates = small_ref[0]
    q_all = qn_ref[0].astype(F32)

    sj = lax.broadcasted_iota(jnp.int32, (n_blk, n_cmp_pad), 0) * SEL_BLOCK
    ci = lax.broadcasted_iota(jnp.int32, (n_blk, n_cmp_pad), 1) * CMP_STRIDE
    cover_t = jnp.where((ci < sj + SEL_BLOCK) & (ci + CMP_LEN > sj), 1.0, 0.0).astype(F32)
    blk_t = lax.broadcasted_iota(jnp.int32, (n_blk, tq), 0)
    cur_t = lax.shift_right_arithmetic(t0 + lax.broadcasted_iota(jnp.int32, (n_blk, tq), 1), SEL_SHIFT)
    forced_t = (blk_t == 0) | (blk_t == cur_t) | (blk_t == cur_t - 1)
    blk_of_key = lax.shift_right_arithmetic(lax.broadcasted_iota(jnp.int32, (tk, n_blk), 0), SEL_SHIFT)
    blk_delta = lax.broadcasted_iota(jnp.int32, (tk, n_blk), 1) - blk_of_key
    sub = tk // TILE

    n_sub = tq // TILE
    group_of = [h // NSA_GROUP for h in range(NSA_HEADS)]

    qs = []
    for h in range(NSA_HEADS):
        g = group_of[h]
        blk = q_all[:, LANES * (h // 2):LANES * (h // 2 + 1)]
        if h % 2 != g:
            blk = pltpu.roll(blk, HEAD_DIM, 1)
        qs.append(jnp.where(lo if g == 0 else jnp.logical_not(lo), blk, 0.0).astype(BF16))

    kcb = kc_ref[0].astype(BF16)
    vcb = vc_ref[0].astype(BF16)
    o_cmp, imps = [], []
    for g in range(NSA_KV_HEADS):
        psum = jnp.zeros((tq, n_cmp_pad), F32)
        for r in range(NSA_GROUP):
            h = g * NSA_GROUP + r
            s = _nt_dot(qs[h], kcb) + bcmp_ref[h]
            m = jnp.max(s, axis=1, keepdims=True)
            e = jnp.exp(s - m)
            scale = jnp.where(m > 0.5 * NEG, 1.0 / jnp.maximum(jnp.sum(e, axis=1, keepdims=True), TINY), 0.0)
            p = e * scale
            psum = psum + p
            o_cmp.append(jnp.dot(p.astype(BF16), vcb, preferred_element_type=F32))
        imp = lax.dot_general(cover_t, psum, (((1,), (1,)), ((), ())), preferred_element_type=F32,
                              precision=lax.Precision.HIGHEST)
        imp = jnp.where(forced_t, FORCED_SCORE, imp)
        imps.append(jnp.where(blk_t <= cur_t, imp, -jnp.inf))

    sel_t = _top_k_mask(jnp.concatenate(imps, axis=1), min(SEL_TOPK, n_blk))
    not_sel = [(1.0 - sel_t[:, g * tq:(g + 1) * tq]).T.astype(BF16) for g in range(NSA_KV_HEADS)]

    q_ext = [jnp.concatenate([qs[h], not_sel[group_of[h]]], axis=1) for h in range(NSA_HEADS)]

    def sel_chunk(c, states):
        off = pl.multiple_of(c * tk, tk)
        k = sel_ref[0, pl.ds(off, tk), 0:LANES]
        v = sel_ref[0, pl.ds(off, tk), LANES:2 * LANES]
        k_ext = jnp.concatenate([k, jnp.where(blk_delta == c * (tk // SEL_BLOCK), NEG, 0.0).astype(BF16)], axis=1)
        out = []
        for h in range(NSA_HEADS):
            bias = jnp.concatenate([jnp.concatenate(
                [tsel_ref[h, jnp.clip(n_sub * i + a - (c * sub + u), -1, n_sel_delta - 1) + 1] for u in range(sub)],
                axis=1) for a in range(n_sub)], axis=0)
            out.append(_online_update(states[h], _nt_dot(q_ext[h], k_ext) + bias, v))
        return tuple(out)

    n_chunks = (t0 + tq - 1) // tk + 1
    sel_states = lax.fori_loop(0, n_chunks, sel_chunk, tuple(_init_state(tq) for _ in range(NSA_HEADS)))

    o_win = [[] for _ in range(NSA_HEADS)]
    for a in range(n_sub):
        j0 = n_sub * i + a
        ks, vs, idxs = [], [], []
        for u in range(n_win_tiles - 1, -1, -1):
            off = pl.multiple_of(jnp.maximum(j0 - u, 0) * TILE, TILE)
            ks.append(win_ref[0, pl.ds(off, TILE), 0:LANES])
            vs.append(win_ref[0, pl.ds(off, TILE), LANES:2 * LANES])
            idxs.append(jnp.where(j0 - u >= 0, u + 1, 0))
        k = jnp.concatenate(ks, axis=0)
        v = jnp.concatenate(vs, axis=0)
        for h in range(NSA_HEADS):
            bias = jnp.concatenate([twin_ref[h, ix] for ix in idxs], axis=1)
            s = _nt_dot(qs[h][a * TILE:(a + 1) * TILE], k) + bias
            e = jnp.exp(s - jnp.max(s, axis=1, keepdims=True))
            l = jnp.sum(e, axis=1, keepdims=True)
            o_win[h].append(jnp.dot(e.astype(BF16), v, preferred_element_type=F32) / l)

    pair_out = [None] * (NSA_HEADS // 2)
    for h in range(NSA_HEADS):
        gc, gs, gw = (jnp.sum(jnp.where(lane == 3 * h + b, gates, 0.0), axis=1, keepdims=True) for b in range(3))
        _, l_s, acc_s = sel_states[h]
        o = gc * o_cmp[h] + gs * (acc_s / l_s) + gw * jnp.concatenate(o_win[h], axis=0)
        if h % 2 != group_of[h]:
            o = pltpu.roll(o, HEAD_DIM, 1)
        prev = pair_out[h // 2]
        keep = lo if h % 2 == 0 else jnp.logical_not(lo)
        pair_out[h // 2] = jnp.where(keep, o, 0.0 if prev is None else prev)

    o_ref[0] = jnp.concatenate(pair_out, axis=1).astype(o_ref.dtype)


def _nsa_prompt(qn, kc, vc, bcmp, nsabf, tsel, twin, small):
    bx, s, _ = qn.shape
    tq = _row_tile(s, 256)
    tk = _row_tile(s, 1024)
    n_cmp_pad = kc.shape[1]
    once = pl.Buffered(1)
    return pl.pallas_call(
        functools.partial(_nsa_kernel, tq=tq, tk=tk, n_sel_delta=tsel.shape[1] - 1, n_win_tiles=twin.shape[1] - 1),
        grid=(bx, s // tq),
        in_specs=[pl.BlockSpec((1, tq, NSA_HEADS * HEAD_DIM), lambda b, i: (b, i, 0)),
                  pl.BlockSpec((1, n_cmp_pad, LANES), lambda b, i: (b, 0, 0)),
                  pl.BlockSpec((1, n_cmp_pad, LANES), lambda b, i: (b, 0, 0)),
                  pl.BlockSpec((NSA_HEADS, tq, n_cmp_pad), lambda b, i: (0, i, 0)),
                  pl.BlockSpec((1, s, 2 * LANES), lambda b, i: (b, 0, 1)),
                  pl.BlockSpec((1, s, 2 * LANES), lambda b, i: (b, 0, 2)),
                  pl.BlockSpec(tsel.shape, lambda b, i: (0, 0, 0, 0), pipeline_mode=once),
                  pl.BlockSpec(twin.shape, lambda b, i: (0, 0, 0, 0), pipeline_mode=once),
                  pl.BlockSpec((1, tq, LANES), lambda b, i: (b, i, 0))],
        out_specs=pl.BlockSpec((1, tq, NSA_HEADS * HEAD_DIM), lambda b, i: (b, i, 0)),
        out_shape=jax.ShapeDtypeStruct((bx, s, NSA_HEADS * HEAD_DIM), BF16),
        compiler_params=_cparams("parallel", "parallel"),
        name="nsa_prompt",
    )(qn, kc, vc, bcmp, nsabf, nsabf, tsel, twin, small)


def _bf16_round(x):
    return x.astype(BF16).astype(F32)


def _dot3(z, w):
    hi = z.astype(BF16)
    rest = z - hi.astype(F32)
    mid = rest.astype(BF16)
    lo = (rest - mid.astype(F32)).astype(BF16)
    return sum(jnp.dot(part, w, preferred_element_type=F32) for part in (hi, mid, lo))


def _feature_major(cache):
    n = cache.ndim
    return jnp.transpose(cache, tuple(range(n - 4)) + (n - 3, n - 2, n - 1, n - 4))


def _nt_dot_bf16(p, vt):
    return lax.dot_general(p.astype(BF16), vt, (((1,), (1,)), ((), ())), preferred_element_type=F32)


def _fox_sample_kernel(pt_ref, *refs, n_steps, pages_per_step):
    page_refs, lf_refs = refs[:pages_per_step], refs[pages_per_step:2 * pages_per_step]
    q_ref, new_ref, lfn_ref, o_ref, m_ref, l_ref, acc_ref, carry_ref = refs[2 * pages_per_step:]
    j = pl.program_id(1)
    width = FOX_HEADS * HEAD_DIM
    qbd = q_ref[0]

    @pl.when(j == 0)
    def _():
        s_new = jnp.sum(qbd.astype(F32) * _bf16_round(new_ref[0, 0:1]), axis=1, keepdims=True)
        m_ref[...] = jnp.broadcast_to(s_new, m_ref.shape)
        l_ref[...] = jnp.ones_like(l_ref)
        acc_ref[...] = jnp.broadcast_to(_bf16_round(new_ref[0, 1:2]), acc_ref.shape)
        carry_ref[...] = lfn_ref[0]

    u = lax.broadcasted_iota(jnp.int32, (PAGE_SIZE, PAGE_SIZE), 0)
    c = lax.broadcasted_iota(jnp.int32, (PAGE_SIZE, PAGE_SIZE), 1)
    later = jnp.where(u > c, 1.0, 0.0).astype(BF16)
    carry = carry_ref[:, 0:1]
    biases = []
    for lf_ref in lf_refs:
        lf = lf_ref[0, 0]
        biases.append(carry + _dot3(lf, later))
        carry = carry + jnp.sum(lf, axis=1, keepdims=True)
    kt = jnp.concatenate([r[0, 0, 0].reshape(width, PAGE_SIZE).astype(BF16) for r in page_refs], axis=1)
    vt = jnp.concatenate([r[0, 0, 1].reshape(width, PAGE_SIZE).astype(BF16) for r in page_refs], axis=1)
    s = jnp.dot(qbd, kt, preferred_element_type=F32) + jnp.concatenate(biases, axis=1)
    m_old = m_ref[:, 0:1]
    m_new = jnp.maximum(m_old, jnp.max(s, axis=1, keepdims=True))
    alpha = jnp.exp(m_old - m_new)
    p = jnp.exp(s - m_new)
    l_ref[...] = jnp.broadcast_to(alpha * l_ref[:, 0:1] + jnp.sum(p, axis=1, keepdims=True), l_ref.shape)
    acc_ref[...] = alpha * acc_ref[...] + _nt_dot_bf16(p, vt)
    m_ref[...] = jnp.broadcast_to(m_new, m_ref.shape)
    carry_ref[...] = jnp.broadcast_to(carry, carry_ref.shape)

    @pl.when(j == n_steps - 1)
    def _():
        o_ref[0] = acc_ref[...] / l_ref[:, 0:1]


def _block_diag_queries(q, n_heads):
    width = n_heads * HEAD_DIM
    keep = (jnp.arange(width) // HEAD_DIM)[None, :] == jnp.arange(n_heads)[:, None]
    return jnp.where(keep[None], q[:, None, :], jnp.zeros((), q.dtype))


def _diag_blocks(o, n_heads):
    n = o.shape[0]
    o5 = o.reshape(n, n_heads, n_heads, HEAD_DIM)
    return jnp.stack([o5[:, h, h] for h in range(n_heads)], axis=1).reshape(n, n_heads * HEAD_DIM)


def _fox_sample(cache_t, layer, logf_t, page_table, qbd, new_kv, lf_new):
    n_req, n_pages = page_table.shape
    pps = 16 if n_pages % 16 == 0 else 1
    n_steps = n_pages // pps
    width = FOX_HEADS * HEAD_DIM

    def page_spec(shape, p):
        zeros = (0,) * (len(shape) - 2)
        return pl.BlockSpec(shape, lambda r, j, pt: (layer, pt[r, n_pages - 1 - (j * pps + p)]) + zeros)

    grid_spec = pltpu.PrefetchScalarGridSpec(
        num_scalar_prefetch=1,
        grid=(n_req, n_steps),
        in_specs=([page_spec((1, 1, 2, FOX_HEADS, HEAD_DIM, PAGE_SIZE), p) for p in range(pps)]
                  + [page_spec((1, 1, FOX_HEADS, PAGE_SIZE), p) for p in range(pps)]
                  + [pl.BlockSpec((1, FOX_HEADS, width), lambda r, j, pt: (r, 0, 0)),
                     pl.BlockSpec((1, 2, width), lambda r, j, pt: (r, 0, 0)),
                     pl.BlockSpec((1, FOX_HEADS, LANES), lambda r, j, pt: (r, 0, 0))]),
        out_specs=pl.BlockSpec((1, FOX_HEADS, width), lambda r, j, pt: (r, 0, 0)),
        scratch_shapes=[pltpu.VMEM((FOX_HEADS, LANES), F32), pltpu.VMEM((FOX_HEADS, LANES), F32),
                        pltpu.VMEM((FOX_HEADS, width), F32), pltpu.VMEM((FOX_HEADS, LANES), F32)],
    )
    return pl.pallas_call(
        functools.partial(_fox_sample_kernel, n_steps=n_steps, pages_per_step=pps),
        grid_spec=grid_spec,
        out_shape=jax.ShapeDtypeStruct((n_req, FOX_HEADS, width), F32),
        compiler_params=_cparams("arbitrary", "arbitrary"),
        name="fox_sample",
    )(page_table, *([cache_t] * pps), *([logf_t] * pps), qbd, new_kv, lf_new)


def _dil_sample_kernel(q_ref, new_ref, kt_ref, vt_ref, tab_ref, o_ref, *, n_buf):
    heads = q_ref.shape[2]
    qbd = q_ref[0, 0]
    kt = kt_ref[0, 0, 0].reshape(heads * HEAD_DIM, n_buf).astype(BF16)
    vt = vt_ref[0, 0, 0].reshape(heads * HEAD_DIM, n_buf).astype(BF16)
    tab = tab_ref[0]
    s = jnp.dot(qbd, kt, preferred_element_type=F32) + tab[:, 0:n_buf]
    s_new = (jnp.sum(qbd.astype(F32) * _bf16_round(new_ref[0, 0, 0:1]), axis=1, keepdims=True)
             + tab[:, n_buf:n_buf + 1])
    m = jnp.maximum(jnp.max(s, axis=1, keepdims=True), s_new)
    p = jnp.exp(s - m)
    p_new = jnp.exp(s_new - m)
    l = jnp.sum(p, axis=1, keepdims=True) + p_new
    o_ref[0, 0] = (_nt_dot_bf16(p, vt) + _bf16_round(p_new) * _bf16_round(new_ref[0, 0, 1:2])) / l


def _dil_sample(state_t, layer, qbd, new_kv, table):
    n_req = state_t.shape[1]
    n_buf = state_t.shape[-1]
    half = DIL_HEADS // 2
    width = half * HEAD_DIM
    return pl.pallas_call(
        functools.partial(_dil_sample_kernel, n_buf=n_buf),
        grid=(n_req, 2),
        in_specs=[pl.BlockSpec((1, 1, half, width), lambda r, hh: (r, hh, 0, 0)),
                  pl.BlockSpec((1, 1, 2, width), lambda r, hh: (r, hh, 0, 0)),
                  pl.BlockSpec((1, 1, 1, half, HEAD_DIM, n_buf), lambda r, hh: (layer, r, 0, hh, 0, 0)),
                  pl.BlockSpec((1, 1, 1, half, HEAD_DIM, n_buf), lambda r, hh: (layer, r, 1, hh, 0, 0)),
                  pl.BlockSpec((1, half, table.shape[2]), lambda r, hh: (hh, 0, 0))],
        out_specs=pl.BlockSpec((1, 1, half, width), lambda r, hh: (r, hh, 0, 0)),
        out_shape=jax.ShapeDtypeStruct((n_req, 2, half, width), F32),
        compiler_params=_cparams("arbitrary", "arbitrary"),
        name="dilated_sample",
    )(qbd, new_kv, state_t, state_t, table)


def _group_queries(q_row):
    row8 = lax.broadcasted_iota(jnp.int32, (8, LANES), 0)
    lo = lax.broadcasted_iota(jnp.int32, (8, LANES), 1) < HEAD_DIM
    out = []
    for g in range(NSA_KV_HEADS):
        qg = jnp.zeros((8, LANES), F32)
        for r in range(NSA_GROUP):
            h = g * NSA_GROUP + r
            blk = jnp.broadcast_to(q_row[:, LANES * (h // 2):LANES * (h // 2 + 1)], (8, LANES))
            if h % 2 != g:
                blk = pltpu.roll(blk, HEAD_DIM, 1)
            qg = jnp.where((row8 == r) & (lo if g == 0 else jnp.logical_not(lo)), blk, qg)
        out.append(qg)
    return out


def _nsa_cmp_kernel(q_ref, kc_ref, vc_ref, bcmp_ref, oc_ref, psum_ref):
    n_cmp_pad = kc_ref.shape[1]
    kcb = kc_ref[0].astype(BF16)
    vcb = vc_ref[0].astype(BF16)
    row8 = lax.broadcasted_iota(jnp.int32, (8, n_cmp_pad), 0)
    for g, qg in enumerate(_group_queries(q_ref[0].astype(F32))):
        bias = bcmp_ref[8 * g:8 * g + 8, :]
        ok = (bias > 0.5 * NEG) & (row8 < NSA_GROUP)
        s = jnp.where(ok, _nt_dot(qg.astype(BF16), kcb) + bias, NEG)
        m = jnp.max(s, axis=1, keepdims=True)
        e = jnp.where(ok, jnp.exp(s - m), 0.0)
        p = e / jnp.maximum(jnp.sum(e, axis=1, keepdims=True), TINY)
        oc_ref[0, g] = jnp.dot(p.astype(BF16), vcb, preferred_element_type=F32)
        psum_ref[0, g] = jnp.broadcast_to(jnp.sum(p, axis=0, keepdims=True), (8, n_cmp_pad))


def _nsa_select(q, kc, vc, bcmp_row, n_blk_pad, cur):
    n_req = q.shape[0]
    n_cmp_pad = kc.shape[1]
    o_cmp, psum = pl.pallas_call(
        _nsa_cmp_kernel,
        grid=(n_req,),
        in_specs=[pl.BlockSpec((1, 1, NSA_HEADS * HEAD_DIM), lambda r: (r, 0, 0)),
                  pl.BlockSpec((1, n_cmp_pad, LANES), lambda r: (r, 0, 0)),
                  pl.BlockSpec((1, n_cmp_pad, LANES), lambda r: (r, 0, 0)),
                  pl.BlockSpec(bcmp_row.shape, lambda r: (0, 0))],
        out_specs=[pl.BlockSpec((1, NSA_KV_HEADS, 8, LANES), lambda r: (r, 0, 0, 0)),
                   pl.BlockSpec((1, NSA_KV_HEADS, 8, n_cmp_pad), lambda r: (r, 0, 0, 0))],
        out_shape=[jax.ShapeDtypeStruct((n_req, NSA_KV_HEADS, 8, LANES), F32),
                   jax.ShapeDtypeStruct((n_req, NSA_KV_HEADS, 8, n_cmp_pad), F32)],
        compiler_params=_cparams("arbitrary"),
        name="nsa_sample_cmp",
    )(q, kc, vc, bcmp_row)
    n_pairs = n_req * NSA_KV_HEADS
    assert n_pairs <= LANES
    pairs = jnp.pad(psum[:, :, 0, :].reshape(n_pairs, n_cmp_pad), ((0, LANES - n_pairs), (0, 0)))
    idx = pl.pallas_call(
        functools.partial(_nsa_rank_kernel, n_blk_pad=n_blk_pad, cur=cur),
        out_shape=jax.ShapeDtypeStruct((SEL_TOPK, LANES), jnp.int32),
        compiler_params=pltpu.CompilerParams(vmem_limit_bytes=VMEM_LIMIT_BYTES),
        name="nsa_sample_topk",
    )(pairs)
    return o_cmp, idx[:, :n_pairs].T.reshape(n_req, NSA_KV_HEADS * SEL_TOPK)


def _nsa_rank_kernel(psum_ref, idx_ref, *, n_blk_pad, cur):
    n_cmp_pad = psum_ref.shape[1]
    sj = lax.broadcasted_iota(jnp.int32, (n_blk_pad, n_cmp_pad), 0) * SEL_BLOCK
    ci = lax.broadcasted_iota(jnp.int32, (n_blk_pad, n_cmp_pad), 1) * CMP_STRIDE
    cover_t = jnp.where((ci < sj + SEL_BLOCK) & (ci + CMP_LEN > sj), 1.0, 0.0).astype(F32)
    imp = lax.dot_general(cover_t, psum_ref[...], (((1,), (1,)), ((), ())), preferred_element_type=F32,
                          precision=lax.Precision.HIGHEST)
    blk = lax.broadcasted_iota(jnp.int32, imp.shape, 0)
    imp = jnp.where((blk == 0) | (blk == cur) | (blk == cur - 1), FORCED_SCORE, imp)
    imp = jnp.where(blk <= cur, imp, -jnp.inf)
    pick_row = lax.broadcasted_iota(jnp.int32, idx_ref.shape, 0)

    def body(it, carry):
        imp, idx = carry
        top = jnp.max(imp, axis=0, keepdims=True)
        first = jnp.min(jnp.where(imp == top, blk, n_blk_pad), axis=0, keepdims=True)
        idx = jnp.where(pick_row == it, jnp.where(top > -jnp.inf, first, -1), idx)
        return jnp.where(blk == first, -jnp.inf, imp), idx

    _, idx = lax.fori_loop(0, SEL_TOPK, body, (imp, jnp.full(idx_ref.shape, -1, jnp.int32)))
    idx_ref[...] = idx


def _nsa_attend_kernel(idx_ref, pt_ref, q_ref, *refs, n_past_blk, new_lane, n_win):
    n_picks = NSA_KV_HEADS * SEL_TOPK
    blk_refs = refs[:n_picks]
    fsel_ref, wbuf_ref, fwin_ref, nsel_ref, nwin_ref, oc_ref, small_ref, o_ref = refs[n_picks:]
    r_idx = pl.program_id(0)
    qgs = [qg[:, g * HEAD_DIM:(g + 1) * HEAD_DIM] for g, qg in enumerate(_group_queries(q_ref[0].astype(F32)))]
    row8 = lax.broadcasted_iota(jnp.int32, (8, LANES), 0)
    lane = lax.broadcasted_iota(jnp.int32, (8, LANES), 1)
    n_pages_past = n_past_blk // 2
    gates = small_ref[0]

    for g, qg in enumerate(qgs):
        grp = slice(g * HEAD_DIM, (g + 1) * HEAD_DIM)
        rows = slice(8 * g, 8 * g + 8)
        qb = qg.astype(BF16)
        ksel_new = _bf16_round(nsel_ref[0][:, 2 * LANES:3 * LANES][:, grp])
        vsel_new = _bf16_round(nsel_ref[0][:, 3 * LANES:4 * LANES][:, grp])
        kwin_new = _bf16_round(nwin_ref[0][:, 0:LANES][:, grp])
        vwin_new = _bf16_round(nwin_ref[0][:, LANES:2 * LANES][:, grp])

        kts, vts, biases = [], [], []
        for k in range(SEL_TOPK):
            blk_ref = blk_refs[g * SEL_TOPK + k]
            b = idx_ref[r_idx, g * SEL_TOPK + k]
            valid = (b >= 0) & (b < n_past_blk)
            page = jnp.clip(lax.shift_right_arithmetic(b, 1), 0, n_pages_past - 1)
            in_blk = lax.shift_right_arithmetic(lane, SEL_SHIFT) == jnp.bitwise_and(b, 1)
            kts.append(blk_ref[0, 0, 0, g].astype(BF16))
            vts.append(blk_ref[0, 0, 1, g].astype(BF16))
            biases.append(jnp.where(valid & in_blk, fsel_ref[page, rows, :], NEG))
        s = jnp.dot(qb, jnp.concatenate(kts, axis=1), preferred_element_type=F32) + jnp.concatenate(biases, axis=1)
        s_new = jnp.sum(qg * ksel_new, axis=1, keepdims=True) + fsel_ref[n_pages_past, rows, new_lane:new_lane + 1]
        m = jnp.maximum(jnp.max(s, axis=1, keepdims=True), s_new)
        p = jnp.exp(s - m)
        p_new = jnp.exp(s_new - m)
        l = jnp.sum(p, axis=1, keepdims=True) + p_new
        o_sel = (_nt_dot_bf16(p, jnp.concatenate(vts, axis=1)) + _bf16_round(p_new) * vsel_new) / l

        kw_t = wbuf_ref[0, 0, 0, g].astype(BF16)
        vw_t = wbuf_ref[0, 0, 1, g].astype(BF16)
        s = jnp.dot(qb, kw_t, preferred_element_type=F32) + fwin_ref[rows, 0:n_win]
        s_wn = jnp.sum(qg * kwin_new, axis=1, keepdims=True) + fwin_ref[rows, n_win:n_win + 1]
        m = jnp.maximum(jnp.max(s, axis=1, keepdims=True), s_wn)
        p = jnp.exp(s - m)
        p_new = jnp.exp(s_wn - m)
        l = jnp.sum(p, axis=1, keepdims=True) + p_new
        o_win = (_nt_dot_bf16(p, vw_t) + _bf16_round(p_new) * vwin_new) / l

        gate = []
        for b in range(3):
            col = jnp.zeros((8, 1), F32)
            for r in range(NSA_GROUP):
                lane_i = 3 * (g * NSA_GROUP + r) + b
                col = jnp.where(row8[:, 0:1] == r, gates[:, lane_i:lane_i + 1], col)
            gate.append(col)
        o_cmp = oc_ref[0, g][:, g * HEAD_DIM:(g + 1) * HEAD_DIM]
        o_ref[0, g] = gate[0] * o_cmp + gate[1] * o_sel + gate[2] * o_win


def _nsa_attend(idx, page_table, q, cache_t, layer, fsel, win_t, fwin, new_sel, new_win, o_cmp, small, n_past_blk,
                new_lane):
    n_req = q.shape[0]
    n_win = win_t.shape[-1]

    def blk_map(pick):
        def index(r, idx_ref, pt_ref):
            b = idx_ref[r, pick]
            b = jnp.where((b >= 0) & (b < n_past_blk), b, 0)
            return layer, pt_ref[r, lax.shift_right_arithmetic(b, 1)], 1, 0, 0, 0
        return index

    n_picks = NSA_KV_HEADS * SEL_TOPK
    const = lambda *shape: (lambda r, idx_ref, pt_ref: shape)
    per_req3 = lambda r, idx_ref, pt_ref: (r, 0, 0)
    per_req4 = lambda r, idx_ref, pt_ref: (r, 0, 0, 0)
    page_blk = (1, 1, 2, NSA_KV_HEADS, HEAD_DIM, PAGE_SIZE)
    grid_spec = pltpu.PrefetchScalarGridSpec(
        num_scalar_prefetch=2,
        grid=(n_req,),
        in_specs=([pl.BlockSpec((1, 1, NSA_HEADS * HEAD_DIM), per_req3)]
                  + [pl.BlockSpec(page_blk, blk_map(pick)) for pick in range(n_picks)]
                  + [pl.BlockSpec(fsel.shape, const(0, 0, 0)),
                     pl.BlockSpec((1, 1, 2, NSA_KV_HEADS, HEAD_DIM, n_win),
                                  lambda r, idx_ref, pt_ref: (layer, r, 0, 0, 0, 0)),
                     pl.BlockSpec(fwin.shape, const(0, 0)),
                     pl.BlockSpec((1, 1, 4 * LANES), per_req3),
                     pl.BlockSpec((1, 1, 2 * LANES), per_req3),
                     pl.BlockSpec((1, NSA_KV_HEADS, 8, LANES), per_req4),
                     pl.BlockSpec((1, 1, LANES), per_req3)]),
        out_specs=pl.BlockSpec((1, NSA_KV_HEADS, 8, HEAD_DIM), per_req4),
    )
    return pl.pallas_call(
        functools.partial(_nsa_attend_kernel, n_past_blk=n_past_blk, new_lane=new_lane, n_win=n_win),
        grid_spec=grid_spec,
        out_shape=jax.ShapeDtypeStruct((n_req, NSA_KV_HEADS, 8, HEAD_DIM), F32),
        compiler_params=_cparams("arbitrary"),
        name="nsa_sample_attend",
    )(idx, page_table, q, *([cache_t] * n_picks), fsel, win_t, fwin, new_sel, new_win, o_cmp, small)


def kernel(x_prompt, x_sample, cache_nsa_kv, cache_fox_kv, cache_fox_logf, state_nsa_win_kv, state_dil_kv, page_table,
           c_prompt, c_sample, rel_bias, norm_g, w_ada, b_ada, w_in_a, nsa_gate_b, fox_f_b, nsa_cmp_w1, nsa_cmp_w2,
           nsa_cmp_pe, w_out_a, w_in_c, w_out_c, w_mlp1, w_mlp2):
    bp, s, d = x_prompt.shape
    bd = x_sample.shape[0]
    depth = w_ada.shape[0]
    n_pages = s // PAGE_SIZE
    n_cmp_pad = s // CMP_STRIDE

    mods = _ada_params(jnp.concatenate([c_prompt, c_sample], axis=0), w_ada, b_ada).reshape(depth, bp + bd, 6, d)
    tab_sel = _toeplitz_table(rel_bias, NSA_HEADS, 15, "causal")
    tab_win = _toeplitz_table(rel_bias, NSA_HEADS, NSA_WINDOW // TILE + 2, "window")
    assert all(window // dil == TILE for window, dil in DIL_BRANCHES)
    tab_dil = [_toeplitz_table(rel_bias, DIL_HEADS, 3, "branch", dil) for _, dil in DIL_BRANCHES]
    bcmp = _cmp_bias_table(rel_bias, s, TILE, n_cmp_pad, 0)

    past_len = page_table.shape[1] * PAGE_SIZE
    assert past_len % SEL_BLOCK == 0
    n_past_blk = past_len // SEL_BLOCK
    n_blk_pad = -(-(n_past_blk + 1) // LANES) * LANES
    far = 1 << 30
    bcmp_s = _affine_bias(rel_bias, NSA_GROUPED_ROWS, past_len // CMP_STRIDE, past_len - (CMP_LEN - 1), -CMP_STRIDE, far)
    assert past_len % PAGE_SIZE == 0
    n_pages_s = past_len // PAGE_SIZE
    fsel = _affine_bias(rel_bias, NSA_GROUPED_ROWS, (n_pages_s + 1) * PAGE_SIZE, past_len, -1, far)
    fsel = jnp.swapaxes(fsel.reshape(len(NSA_GROUPED_ROWS), n_pages_s + 1, PAGE_SIZE), 0, 1)
    n_win_buf = state_nsa_win_kv.shape[2]
    fwin = _affine_bias(rel_bias, NSA_GROUPED_ROWS, n_win_buf + LANES, n_win_buf, -1, NSA_WINDOW)
    n_dil_buf = state_dil_kv.shape[2]
    tab_dil_s = _affine_bias(rel_bias, range(DIL_HEADS), n_dil_buf + LANES, n_dil_buf, -1, far, union=True)
    tab_dil_s = tab_dil_s.reshape(2, DIL_HEADS // 2, n_dil_buf + LANES)
    cache_nsa_t = _feature_major(cache_nsa_kv)
    cache_fox_t = _feature_major(cache_fox_kv)
    logf_t = jnp.swapaxes(cache_fox_logf, 2, 3)
    win_t = _feature_major(state_nsa_win_kv)
    dil_t = _feature_major(state_dil_kv)

    xp = x_prompt
    xs = x_sample.reshape(1, bd, d)
    per_req = lambda a: a.reshape(bd, 1, a.shape[-1])
    prompt_pages = jnp.arange(bp * n_pages, dtype=jnp.int32).reshape(bp, n_pages)
    nsa_p, nsa_s, fkv_p, fkv_s, lf_p, lf_s, win_p, win_s, dil_p, dil_s = [], [], [], [], [], [], [], [], [], []
    for layer in range(depth):
        mp = [mods[layer, :bp, k].reshape(bp, 1, d) for k in range(6)]
        ms = [mods[layer, bp:, k].reshape(1, bd, d) for k in range(6)]
        g = [norm_g[layer, k].reshape(1, d) for k in range(4)]
        i = layer // 2
        if layer % 2 == 0:
            wa = w_in_a[i]
            c_gate = W_NSA_Q + 6 * W_NSA_KV
            c_fox = c_gate + SMALL_GATES
            c_forget = c_fox + 3 * W_FOX_Q
            w_in = jnp.concatenate([wa[:, 0:c_gate], wa[:, c_fox:c_forget], wa[:, c_gate:c_fox], wa[:, c_forget:],
                                    jnp.zeros((d, LANES - SMALL_GATES - FOX_HEADS), F32)], axis=1).astype(BF16)
            sb = jnp.concatenate([nsa_gate_b[i].reshape(-1), fox_f_b[i],
                                  jnp.zeros((LANES - SMALL_GATES - FOX_HEADS,), F32)]).reshape(1, LANES)
            half = CMP_STRIDE * HEAD_DIM
            w1cat = jnp.concatenate([nsa_cmp_w1[i][:, :half], nsa_cmp_w1[i][:, half:]], axis=2).astype(BF16)
            w2 = nsa_cmp_w2[i].astype(BF16)
            pe = nsa_cmp_pe[i].reshape(2, 1, CMP_LEN * HEAD_DIM)
            w_out = w_out_a[i].astype(BF16)

            qn, nsa4, nsabf, win, qf, fkv, fkvbf, small = _proj_even(xp, mp[0], mp[1], g[0], w_in, sb)
            c, ct = _cumsum(small)
            o_f = _fox_prompt(qf, fkvbf, c, ct)
            kc, vc = _compress(nsa4.reshape(bp * n_pages, PAGE_SIZE, 4 * W_NSA_KV), prompt_pages, w1cat, w2, pe)
            o_n = _nsa_prompt(qn, kc, vc, bcmp, nsabf, tab_sel, tab_win, small)
            op_a, op_b = o_n, o_f
            nsa_p.append(nsa4.reshape(bp, s, 4, NSA_KV_HEADS, HEAD_DIM))
            fkv_p.append(fkv.reshape(bp, s, 2, FOX_HEADS, HEAD_DIM))
            lf_p.append(small[:, :, SMALL_GATES:SMALL_GATES + FOX_HEADS])
            n_win = min(NSA_WINDOW, s)
            win_p.append(win[:, s - n_win:].reshape(bp, n_win, 2, NSA_KV_HEADS, HEAD_DIM))

            qn_s, nsa4_s, _, win_new, qf_s, fkv_s_, _, small_s = _proj_even(xs, ms[0], ms[1], g[0], w_in, sb)
            kc_s, vc_s = _compress(cache_nsa_t, page_table, w1cat, w2, pe, layer=i)
            o_cmp, idx = _nsa_select(per_req(qn_s), kc_s, vc_s, bcmp_s, n_blk_pad, n_past_blk)
            o_nsa = _nsa_attend(idx, page_table, per_req(qn_s), cache_nsa_t, i, fsel, win_t, fwin, per_req(nsa4_s),
                                per_req(win_new), o_cmp, per_req(small_s), n_past_blk, 0)
            os_a = o_nsa[:, :, :NSA_GROUP].reshape(1, bd, NSA_HEADS * HEAD_DIM).astype(BF16)
            lf_new = small_s[0, :, SMALL_GATES:SMALL_GATES + FOX_HEADS]
            o_fox = _fox_sample(cache_fox_t, i, logf_t, page_table, _block_diag_queries(qf_s[0], FOX_HEADS),
                                fkv_s_.reshape(bd, 2, FOX_HEADS * HEAD_DIM),
                                jnp.broadcast_to(lf_new[:, :, None], (bd, FOX_HEADS, LANES)))
            os_b = _diag_blocks(o_fox, FOX_HEADS).reshape(1, bd, -1).astype(BF16)
            nsa_s.append(nsa4_s.reshape(bd, 1, 4, NSA_KV_HEADS, HEAD_DIM))
            fkv_s.append(fkv_s_.reshape(bd, 1, 2, FOX_HEADS, HEAD_DIM))
            lf_s.append(small_s[0, :, SMALL_GATES:SMALL_GATES + FOX_HEADS].reshape(bd, 1, FOX_HEADS))
            win_s.append(win_new.reshape(bd, 1, 2, NSA_KV_HEADS, HEAD_DIM))
        else:
            w_in = w_in_c[i].astype(BF16)
            w_out = w_out_c[i].astype(BF16)
            q, kv, kvbf = _proj_odd(xp, mp[0], mp[1], g[0], w_in)
            branches = [_dil_branch_prompt(q, kvbf, tab, dil) for tab, (_, dil) in zip(tab_dil, DIL_BRANCHES)]
            n_dil = min(DIL_BRANCHES[-1][0], s)
            dil_p.append(kv[:, s - n_dil:].reshape(bp, n_dil, 2, DIL_HEADS, HEAD_DIM))
            q_s, kv_s, _ = _proj_odd(xs, ms[0], ms[1], g[0], w_in)
            half = DIL_HEADS // 2
            qbd = _block_diag_queries(q_s.reshape(bd * 2, half * HEAD_DIM), half).reshape(bd, 2, half, half * HEAD_DIM)
            new_kv = jnp.swapaxes(kv_s.reshape(bd, 2, 2, half * HEAD_DIM), 1, 2)
            o_dil = _dil_sample(dil_t, i, qbd, new_kv, tab_dil_s)
            os_a = os_b = _diag_blocks(o_dil.reshape(bd * 2, half, half * HEAD_DIM), half).reshape(1, bd, -1).astype(BF16)
            dil_s.append(kv_s.reshape(bd, 1, 2, DIL_HEADS, HEAD_DIM))
        if layer % 2 == 0:
            xp = _post(op_a, op_b, w_out, xp, mp[2], g[1])
            xs = _post(os_a, os_b, w_out, xs, ms[2], g[1])
        else:
            xp = _post_dil([o for o, _ in branches], [l for _, l in branches], w_out, xp, mp[2], g[1])
            xs = _post(os_a, os_b, w_out, xs, ms[2], g[1], 0, 1)
        w1 = w_mlp1[layer].astype(BF16)
        w2m = w_mlp2[layer].astype(BF16)
        xp = _mlp(xp, mp[3], mp[4], mp[5], g[2], g[3], w1, w2m)
        xs = _mlp(xs, ms[3], ms[4], ms[5], g[2], g[3], w1, w2m)
    return (xp, xs.reshape(bd, 1, d), jnp.stack(nsa_p), jnp.stack(nsa_s), jnp.stack(fkv_p), jnp.stack(fkv_s),
            jnp.stack(lf_p), jnp.stack(lf_s), jnp.stack(win_p), jnp.stack(win_s), jnp.stack(dil_p), jnp.stack(dil_s))
```
